```python
import jax
import jax.numpy as jnp
from jax import lax
import numpy as np

D_MODEL = 1024
BATCH = 8
SEQ = 2048
DEPTH = 1

RET_HEADS = 4
RET_DK = 128
RET_DV = 256
GLA_HEADS = 4
GLA_DK = 128
GLA_DV = 256
GLA_GATE_RANK = 16
GLA_GATE_NORMALIZER = 16.0
CHUNK = 128
N_BRANCHES = 2
N_EXPERTS = 16
EC_CAPACITY_FACTOR = 2
D_FF = 2816
ROPE_THETA = 10000.0
NORM_EPS = 1e-6

RET_QK = RET_HEADS * RET_DK
RET_V = RET_HEADS * RET_DV
GLA_QK = GLA_HEADS * GLA_DK
GLA_V = GLA_HEADS * GLA_DV
IN_SPLITS = (RET_QK, RET_QK, RET_V, RET_V,
             GLA_QK, GLA_QK, GLA_V, GLA_V, 2 * GLA_GATE_RANK,
             N_BRANCHES * D_MODEL)
D_IN = sum(IN_SPLITS)

kernel_name = 'hybrid_retention_gla_ec_moe_encoder'


def _rmsnorm(x, gain):
    xf = x.astype(jnp.float32)
    y = xf * lax.rsqrt(jnp.mean(xf * xf, axis=-1, keepdims=True) + NORM_EPS)
    return (y * gain.astype(jnp.float32)).astype(x.dtype)


def _rope(t, positions):
    d = t.shape[-1]
    inv_freq = ROPE_THETA ** (-jnp.arange(0, d, 2, dtype=jnp.float32) / d)
    ang = positions.astype(jnp.float32)[..., None] * inv_freq
    cos = jnp.cos(ang)[:, :, None, :]
    sin = jnp.sin(ang)[:, :, None, :]
    t1, t2 = t[..., : d // 2], t[..., d // 2:]
    return jnp.concatenate([t1 * cos - t2 * sin, t2 * cos + t1 * sin], axis=-1)


def _to_chunks(t):
    b, s, h, d = t.shape
    return t.reshape(b, s // CHUNK, CHUNK, h, d).transpose(0, 3, 1, 2, 4)


def _from_chunks(t):
    b, h, n, c, d = t.shape
    return t.transpose(0, 2, 3, 1, 4).reshape(b, n * c, h, d)


def _flip(t):
    return jnp.flip(t, axis=1)


def _causal_mask(c, strict):
    pos = jnp.arange(c)
    return pos[:, None] > pos[None, :] if strict else pos[:, None] >= pos[None, :]


def _prefix_states(decay, kv):
    def step(s, inp):
        dec, u = inp
        return dec * s + u, s
    s0 = jnp.zeros_like(kv[:, :, 0])
    _, s_prev = lax.scan(step, s0, (jnp.moveaxis(decay, 2, 0), jnp.moveaxis(kv, 2, 0)))
    return jnp.moveaxis(s_prev, 0, 2)


def _retention_scan(q, k, v, log_gamma, strict):
    b, h, n, c, _ = q.shape
    pos = jnp.arange(c, dtype=jnp.float32)
    rel = pos[:, None] - pos[None, :]
    mask = _causal_mask(c, strict)
    decay_mask = jnp.where(mask, jnp.exp(jnp.where(mask, rel, 0.0) * log_gamma[:, None, None]), 0.0)
    scores = jnp.einsum('bhncd,bhnsd->bhncs', q, k) * decay_mask[None, :, None]
    intra = jnp.einsum('bhncs,bhnse->bhnce', scores, v)
    k_w = k * jnp.exp((c - 1.0 - pos)[None, :] * log_gamma[:, None])[None, :, None, :, None]
    kv = jnp.einsum('bhncd,bhnce->bhnde', k_w, v)
    chunk_decay = jnp.broadcast_to(jnp.exp(c * log_gamma)[None, :, None, None, None], (b, h, n, 1, 1))
    s_prev = _prefix_states(chunk_decay, kv)
    q_w = q * jnp.exp((pos + 1.0)[None, :] * log_gamma[:, None])[None, :, None, :, None]
    inter = jnp.einsum('bhncd,bhnde->bhnce', q_w, s_prev)
    return intra + inter


def _gla_scan(q, k, v, log_alpha, strict):
    c = q.shape[3]
    cum = jnp.cumsum(log_alpha, axis=3)
    ref = cum[:, :, :, c // 2:c // 2 + 1]
    last = cum[:, :, :, -1:]
    scores = jnp.einsum('bhncd,bhnsd->bhncs', q * jnp.exp(cum - ref), k * jnp.exp(ref - cum))
    scores = jnp.where(_causal_mask(c, strict), scores, 0.0)
    intra = jnp.einsum('bhncs,bhnse->bhnce', scores, v)
    kv = jnp.einsum('bhncd,bhnce->bhnde', k * jnp.exp(last - cum), v)
    s_prev = _prefix_states(jnp.swapaxes(jnp.exp(last), -1, -2), kv)
    inter = jnp.einsum('bhncd,bhnde->bhnce', q * jnp.exp(cum), s_prev)
    return intra + inter


def _mixer(h, positions, w_in, ret_decay_fwd, ret_decay_bwd, ret_norm,
           gla_gate_w_fwd, gla_gate_b_fwd, gla_gate_w_bwd, gla_gate_b_bwd, gla_norm,
           w_branch_ret, w_branch_gla, w_out):
    b, s, _ = h.shape
    f32 = jnp.float32
    proj = h @ w_in
    split_at = [int(i) for i in np.cumsum(IN_SPLITS)[:-1]]
    rq, rk, rv, rg, gq, gk, gv, gr, ga, gate_logits = jnp.split(proj, split_at, axis=-1)

    rq = _rope(rq.astype(f32).reshape(b, s, RET_HEADS, RET_DK), positions) * RET_DK ** -0.5
    rk = _rope(rk.astype(f32).reshape(b, s, RET_HEADS, RET_DK), positions)
    rv = rv.astype(f32).reshape(b, s, RET_HEADS, RET_DV)
    lg_f = jax.nn.log_sigmoid(ret_decay_fwd.astype(f32))
    lg_b = jax.nn.log_sigmoid(ret_decay_bwd.astype(f32))
    ret = _from_chunks(_retention_scan(_to_chunks(rq), _to_chunks(rk), _to_chunks(rv), lg_f, False))
    ret = ret + _flip(_from_chunks(_retention_scan(
        _to_chunks(_flip(rq)), _to_chunks(_flip(rk)), _to_chunks(_flip(rv)), lg_b, True)))
    mu = jnp.mean(ret, axis=-1, keepdims=True)
    var = jnp.mean(jnp.square(ret - mu), axis=-1, keepdims=True)
    ret = ((ret - mu) * lax.rsqrt(var + NORM_EPS)).reshape(b, s, RET_V) * ret_norm.astype(f32)
    ret = (ret * jax.nn.silu(rg.astype(f32))).astype(h.dtype)

    gq = gq.astype(f32).reshape(b, s, GLA_HEADS, GLA_DK) * GLA_DK ** -0.5
    gk = gk.astype(f32).reshape(b, s, GLA_HEADS, GLA_DK)
    gv = gv.astype(f32).reshape(b, s, GLA_HEADS, GLA_DV)
    ga_f, ga_b = jnp.split(ga.astype(f32), 2, axis=-1)
    la_f = (jax.nn.log_sigmoid(ga_f @ gla_gate_w_fwd.astype(f32) + gla_gate_b_fwd.astype(f32))
            / GLA_GATE_NORMALIZER).reshape(b, s, GLA_HEADS, GLA_DK)
    la_b = (jax.nn.log_sigmoid(ga_b @ gla_gate_w_bwd.astype(f32) + gla_gate_b_bwd.astype(f32))
            / GLA_GATE_NORMALIZER).reshape(b, s, GLA_HEADS, GLA_DK)
    gla = _from_chunks(_gla_scan(_to_chunks(gq), _to_chunks(gk), _to_chunks(gv), _to_chunks(la_f), False))
    gla = gla + _flip(_from_chunks(_gla_scan(
        _to_chunks(_flip(gq)), _to_chunks(_flip(gk)), _to_chunks(_flip(gv)), _to_chunks(_flip(la_b)), True)))
    gla = gla * lax.rsqrt(jnp.mean(gla * gla, axis=-1, keepdims=True) + NORM_EPS)
    gla = gla.reshape(b, s, GLA_V) * gla_norm.astype(f32)
    gla = (gla * jax.nn.silu(gr.astype(f32))).astype(h.dtype)

    g_ret, g_gla = jnp.split(jax.nn.sigmoid(gate_logits.astype(f32)), N_BRANCHES, axis=-1)
    merged = g_ret * (ret @ w_branch_ret).astype(f32) + g_gla * (gla @ w_branch_gla).astype(f32)
    return merged.astype(h.dtype) @ w_out


def _expert_choice_moe(h, w_router, w_gate, w_up, w_down):
    b, s, d = h.shape
    capacity = EC_CAPACITY_FACTOR * s // N_EXPERTS
    affinity = jax.nn.softmax((h @ w_router).astype(jnp.float32), axis=-1)
    gate, idx = lax.top_k(jnp.swapaxes(affinity, 1, 2), capacity)
    xg = jax.vmap(lambda hb, ib: hb[ib])(h, idx)
    act = jax.nn.silu(jnp.einsum('becd,edf->becf', xg, w_gate)) * jnp.einsum('becd,edf->becf', xg, w_up)
    y = jnp.einsum('becf,efd->becd', act, w_down) * gate[..., None].astype(h.dtype)
    return jax.vmap(lambda ib, yb: jnp.zeros((s, d), yb.dtype).at[ib.reshape(-1)].add(yb.reshape(-1, d)))(idx, y)


def setup_inputs(seed: int = 0) -> dict:
    key = jax.random.key(seed)
    ks = jax.random.split(key, 24)
    L, D = DEPTH, D_MODEL
    f32 = jnp.float32

    def dense(k, shape, fan_in):
        return jax.random.normal(k, shape, f32) * fan_in ** -0.5

    def gain(k, shape):
        return 1.0 + 0.02 * jax.random.normal(k, shape, f32)

    gamma0 = 1.0 - 2.0 ** (-5.0 - np.arange(RET_HEADS))
    logit0 = jnp.asarray(np.log(gamma0) - np.log1p(-gamma0), f32)
    return {
        'x': jax.random.normal(ks[0], (BATCH, SEQ, D), f32),
        'positions': jnp.broadcast_to(jnp.arange(SEQ, dtype=jnp.int32), (BATCH, SEQ)),
        'norm_mix': gain(ks[1], (L, D)),
        'w_in': dense(ks[2], (L, D, D_IN), D),
        'ret_decay_fwd': logit0 + 0.05 * jax.random.normal(ks[3], (L, RET_HEADS), f32),
        'ret_decay_bwd': logit0 + 0.05 * jax.random.normal(ks[4], (L, RET_HEADS), f32),
        'ret_norm': gain(ks[5], (L, RET_V)),
        'gla_gate_w_fwd': dense(ks[6], (L, GLA_GATE_RANK, GLA_QK), GLA_GATE_RANK),
        'gla_gate_b_fwd': 0.1 * jax.random.normal(ks[7], (L, GLA_QK), f32),
        'gla_gate_w_bwd': dense(ks[8], (L, GLA_GATE_RANK, GLA_QK), GLA_GATE_RANK),
        'gla_gate_b_bwd': 0.1 * jax.random.normal(ks[9], (L, GLA_QK), f32),
        'gla_norm': gain(ks[10], (L, GLA_V)),
        'w_branch_ret': dense(ks[11], (L, RET_V, D), RET_V),
        'w_branch_gla': dense(ks[12], (L, GLA_V, D), GLA_V),
        'w_out': dense(ks[13], (L, D, D), D),
        'norm_ffn': gain(ks[14], (L, D)),
        'w_router': dense(ks[15], (L, D, N_EXPERTS), D),
        'w_gate': dense(ks[16], (L, N_EXPERTS, D, D_FF), D),
        'w_up': dense(ks[17], (L, N_EXPERTS, D, D_FF), D),
        'w_down': dense(ks[18], (L, N_EXPERTS, D_FF, D), D_FF),
        'norm_final': gain(ks[19], (D,)),
    }


def reference(x, positions, norm_mix, w_in, ret_decay_fwd, ret_decay_bwd, ret_norm,
              gla_gate_w_fwd, gla_gate_b_fwd, gla_gate_w_bwd, gla_gate_b_bwd, gla_norm,
              w_branch_ret, w_branch_gla, w_out, norm_ffn, w_router, w_gate, w_up, w_down,
              norm_final):
    for l in range(DEPTH):
        h = _rmsnorm(x, norm_mix[l])
        x = x + _mixer(h, positions, w_in[l], ret_decay_fwd[l], ret_decay_bwd[l], ret_norm[l],
                       gla_gate_w_fwd[l], gla_gate_b_fwd[l], gla_gate_w_bwd[l], gla_gate_b_bwd[l],
                       gla_norm[l], w_branch_ret[l], w_branch_gla[l], w_out[l])
        h = _rmsnorm(x, norm_ffn[l])
        x = x + _expert_choice_moe(h, w_router[l], w_gate[l], w_up[l], w_down[l])
    return _rmsnorm(x, norm_final)
```

```python
import functools

import jax
import jax.numpy as jnp
from jax import lax
from jax.experimental import pallas as pl
from jax.experimental.pallas import tpu as pltpu

F32 = jnp.float32
BF16 = jnp.bfloat16

RET_HEADS = 4
RET_DK = 128
RET_DV = 256
GLA_HEADS = 4
GLA_DK = 128
GLA_DV = 256
GLA_GATE_RANK = 16
GLA_GATE_NORMALIZER = 16.0
CHUNK = 128
EC_CAPACITY_FACTOR = 2
ROPE_THETA = 10000.0
NORM_EPS = 1e-6

LANES = 128
VMEM_LIMIT = 56 << 20


def _params(sem, vmem=VMEM_LIMIT):
    return pltpu.CompilerParams(dimension_semantics=sem, vmem_limit_bytes=vmem)


def _sigmoid(x):
    return 1.0 / (1.0 + jnp.exp(-x))


def _log_sigmoid(x):
    return jnp.minimum(x, 0.0) - jnp.log1p(jnp.exp(-jnp.abs(x)))


def _rms(x, gain):
    return x * lax.rsqrt(jnp.mean(x * x, axis=-1, keepdims=True) + NORM_EPS) * gain


def _dot(a, b):
    return jnp.dot(a, b, preferred_element_type=F32)


def _dot_nt(a, b):
    return lax.dot_general(a, b, (((1,), (1,)), ((), ())), preferred_element_type=F32)


def _dot_tn(a, b):
    return lax.dot_general(a, b, (((0,), (0,)), ((), ())), preferred_element_type=F32)


def _split3(x):
    hi = x.astype(BF16)
    r = x - hi.astype(F32)
    mid = r.astype(BF16)
    lo = (r - mid.astype(F32)).astype(BF16)
    return hi, mid, lo


def _dot_exact_lhs(tri, x):
    hi, mid, lo = _split3(x)
    return _dot(tri, hi) + _dot(tri, mid) + _dot(tri, lo)


def _dot_f32(a, b):
    a_hi = a.astype(BF16)
    a_lo = (a - a_hi.astype(F32)).astype(BF16)
    b_hi = b.astype(BF16)
    b_lo = (b - b_hi.astype(F32)).astype(BF16)
    return _dot(a_hi, b_hi) + _dot(a_hi, b_lo) + _dot(a_lo, b_hi)


def _rope_table_kernel(pos_ref, freq_ref, sign_ref, cos_ref, sin_ref):
    ang = pos_ref[0].astype(F32) * freq_ref[...]
    cos_ref[0] = jnp.cos(ang)
    sin_ref[0] = jnp.sin(ang) * sign_ref[...]


def _rope_table(positions, dk):
    b, s = positions.shape
    half = jnp.arange(0, dk, 2, dtype=F32) / dk
    inv_freq = ROPE_THETA ** (-half)
    freq = jnp.concatenate([inv_freq, inv_freq])[None, :]
    sign = jnp.concatenate([-jnp.ones(dk // 2, F32), jnp.ones(dk // 2, F32)])[None, :]
    out = jax.ShapeDtypeStruct((b, s, dk), F32)
    return pl.pallas_call(
        _rope_table_kernel,
        grid=(b,),
        in_specs=[pl.BlockSpec((1, s, 1), lambda i: (i, 0, 0)),
                  pl.BlockSpec((1, dk), lambda i: (0, 0)),
                  pl.BlockSpec((1, dk), lambda i: (0, 0))],
        out_specs=[pl.BlockSpec((1, s, dk), lambda i: (i, 0, 0))] * 2,
        out_shape=[out, out],
        compiler_params=_params(("parallel",)),
        name="rope_table",
    )(positions[:, :, None], freq, sign)


def _rope(t, cos, sin_signed):
    return t * cos + pltpu.roll(t, t.shape[-1] // 2, 1) * sin_signed


def _in_proj_kernel(x_ref, gain_ref, w_ref, wga_ref, proj_ref, ga_ref, h_ref):
    @pl.when(pl.program_id(1) == 0)
    def _():
        h = _rms(x_ref[...], gain_ref[...]).astype(BF16)
        h_ref[...] = h
        ga_ref[...] = _dot(h, wga_ref[...])

    proj_ref[...] = _dot(h_ref[...], w_ref[...])


def _in_proj(x2, gain, w_main, w_ga, tm, tn):
    t, d = x2.shape
    n = w_main.shape[1]
    return pl.pallas_call(
        _in_proj_kernel,
        grid=(t // tm, n // tn),
        in_specs=[pl.BlockSpec((tm, d), lambda i, j: (i, 0)),
                  pl.BlockSpec((1, d), lambda i, j: (0, 0)),
                  pl.BlockSpec((d, tn), lambda i, j: (0, j)),
                  pl.BlockSpec((d, LANES), lambda i, j: (0, 0))],
        out_specs=[pl.BlockSpec((tm, tn), lambda i, j: (i, j)),
                   pl.BlockSpec((tm, LANES), lambda i, j: (i, 0))],
        out_shape=[jax.ShapeDtypeStruct((t, n), F32),
                   jax.ShapeDtypeStruct((t, LANES), F32)],
        scratch_shapes=[pltpu.VMEM((tm, d), BF16)],
        compiler_params=_params(("parallel", "arbitrary")),
        name="in_proj",
    )(x2, gain, w_main, w_ga)


def _prefix_states(n_chunks, dk, dv, decay_f, decay_b, kv_ref, st_ref):
    def fwd(n, s):
        st_ref[n, 0:dk, :] = s.astype(BF16)
        return decay_f(n) * s + kv_ref[n, 0:dk, :]

    def bwd(i, s):
        n = n_chunks - 1 - i
        st_ref[n, dk:2 * dk, :] = s.astype(BF16)
        return decay_b(n) * s + kv_ref[n, dk:2 * dk, :]

    zero = jnp.zeros((dk, dv), F32)
    lax.fori_loop(0, n_chunks, fwd, zero)
    lax.fori_loop(0, n_chunks, bwd, zero)


def _chunk_rows(n):
    return pl.ds(pl.multiple_of(n * CHUNK, CHUNK), CHUNK)


def _retention_kernel(q_ref, k_ref, v_ref, g_ref, cos_ref, sin_ref, dec_ref, gain_ref, o_ref,
                      qr_ref, kr_ref, kv_ref, st_ref):
    c = CHUNK
    s_len, dk = q_ref.shape[1], q_ref.shape[2]
    dv = v_ref.shape[2]
    n_chunks = s_len // c
    scale = dk ** -0.5

    lg_f = _log_sigmoid(dec_ref[0, 0:1, :])
    lg_b = _log_sigmoid(dec_ref[0, 1:2, :])
    lgf_k, lgb_k = lg_f[:, :dk], lg_b[:, :dk]
    pos = lax.broadcasted_iota(jnp.int32, (c, dk), 0).astype(F32)
    wq_f = jnp.exp((pos + 1.0) * lgf_k)
    wk_f = jnp.exp((c - 1.0 - pos) * lgf_k)
    wq_b = jnp.exp((c - pos) * lgb_k)
    wk_b = jnp.exp(pos * lgb_k)
    ri = lax.broadcasted_iota(jnp.int32, (c, c), 0)
    ci = lax.broadcasted_iota(jnp.int32, (c, c), 1)
    lower = ri >= ci
    rel = (ri - ci).astype(F32)
    decay_mask = jnp.where(lower,
                           jnp.exp(jnp.where(lower, rel, 0.0) * lg_f[:, :c]),
                           jnp.exp(jnp.where(lower, 0.0, -rel) * lg_b[:, :c]))
    chunk_decay_f = jnp.exp(c * lg_f)
    chunk_decay_b = jnp.exp(c * lg_b)

    def phase1(n, carry):
        rows = _chunk_rows(n)
        cos, sin = cos_ref[0, rows, :], sin_ref[0, rows, :]
        qr = _rope(q_ref[0, rows, :], cos, sin) * scale
        kr = _rope(k_ref[0, rows, :], cos, sin)
        qr_ref[rows, :] = qr
        kr_ref[rows, :] = kr
        kw = jnp.concatenate([kr * wk_f, kr * wk_b], axis=1).astype(BF16)
        kv_ref[n] = _dot_tn(kw, v_ref[0, rows, :].astype(BF16))
        return carry

    lax.fori_loop(0, n_chunks, phase1, 0)
    _prefix_states(n_chunks, dk, dv, lambda n: chunk_decay_f, lambda n: chunk_decay_b, kv_ref, st_ref)

    def phase2(n, carry):
        rows = _chunk_rows(n)
        qr, kr = qr_ref[rows, :], kr_ref[rows, :]
        scores = _dot_nt(qr.astype(BF16), kr.astype(BF16)) * decay_mask
        intra = _dot(scores.astype(BF16), v_ref[0, rows, :].astype(BF16))
        qw = jnp.concatenate([qr * wq_f, qr * wq_b], axis=1).astype(BF16)
        o = intra + _dot(qw, st_ref[n])
        mu = jnp.mean(o, axis=-1, keepdims=True)
        d = o - mu
        var = jnp.mean(d * d, axis=-1, keepdims=True)
        y = d * lax.rsqrt(var + NORM_EPS) * gain_ref[...]
        g = g_ref[0, rows, :]
        o_ref[0, rows, :] = (y * (g * _sigmoid(g))).astype(o_ref.dtype)
        return carry

    lax.fori_loop(0, n_chunks, phase2, 0)


def _retention(proj, cos, sin, dec, gain, col0):
    b, s, _ = proj.shape
    h, dk, dv = RET_HEADS, RET_DK, RET_DV
    q0 = col0 // dk
    k0 = q0 + h
    v0 = (col0 + 2 * h * dk) // dv
    g0 = v0 + h
    n_chunks = s // CHUNK
    return pl.pallas_call(
        _retention_kernel,
        grid=(b, h),
        in_specs=[pl.BlockSpec((1, s, dk), lambda i, j: (i, 0, q0 + j)),
                  pl.BlockSpec((1, s, dk), lambda i, j: (i, 0, k0 + j)),
                  pl.BlockSpec((1, s, dv), lambda i, j: (i, 0, v0 + j)),
                  pl.BlockSpec((1, s, dv), lambda i, j: (i, 0, g0 + j)),
                  pl.BlockSpec((1, s, dk), lambda i, j: (i, 0, 0)),
                  pl.BlockSpec((1, s, dk), lambda i, j: (i, 0, 0)),
                  pl.BlockSpec((1, 8, dv), lambda i, j: (j, 0, 0)),
                  pl.BlockSpec((1, dv), lambda i, j: (0, j))],
        out_specs=pl.BlockSpec((1, s, dv), lambda i, j: (i, 0, j)),
        out_shape=jax.ShapeDtypeStruct((b, s, h * dv), BF16),
        scratch_shapes=[pltpu.VMEM((s, dk), F32),
                        pltpu.VMEM((s, dk), F32),
                        pltpu.VMEM((n_chunks, 2 * dk, dv), F32),
                        pltpu.VMEM((n_chunks, 2 * dk, dv), BF16)],
        compiler_params=_params(("parallel", "parallel")),
        name="retention",
    )(proj, proj, proj, proj, cos, sin, dec, gain)


def _gla_kernel(q_ref, k_ref, v_ref, g_ref, ga_ref, wf_ref, wb_ref, bf_ref, bb_ref, gain_ref, o_ref,
                cumf_ref, cumb_ref, kv_ref, dcol_ref, st_ref):
    c = CHUNK
    s_len, dk = q_ref.shape[1], q_ref.shape[2]
    dv = v_ref.shape[2]
    n_chunks = s_len // c
    scale = dk ** -0.5
    inv_norm = 1.0 / GLA_GATE_NORMALIZER

    ri = lax.broadcasted_iota(jnp.int32, (c, c), 0)
    ci = lax.broadcasted_iota(jnp.int32, (c, c), 1)
    lower = ri >= ci
    tri_lower = jnp.where(lower, 1.0, 0.0).astype(BF16)
    tri_upper = jnp.where(ci >= ri, 1.0, 0.0).astype(BF16)

    def phase1(n, carry):
        rows = _chunk_rows(n)
        ga = ga_ref[0, rows, :]
        la_f = _log_sigmoid(_dot_f32(ga, wf_ref[...]) + bf_ref[...]) * inv_norm
        la_b = _log_sigmoid(_dot_f32(ga, wb_ref[...]) + bb_ref[...]) * inv_norm
        cum_f = _dot_exact_lhs(tri_lower, la_f)
        cum_b = _dot_exact_lhs(tri_upper, la_b)
        cumf_ref[rows, :] = cum_f
        cumb_ref[rows, :] = cum_b
        last_f = cum_f[c - 1:c, :]
        last_b = cum_b[0:1, :]
        k = k_ref[0, rows, :]
        kw = jnp.concatenate([k * jnp.exp(last_f - cum_f), k * jnp.exp(last_b - cum_b)], axis=1).astype(BF16)
        kv_ref[n] = _dot_tn(kw, v_ref[0, rows, :].astype(BF16))
        dcol_ref[n, 0:dk, :] = jnp.broadcast_to(jnp.exp(last_f), (dk, dk)).T
        dcol_ref[n, dk:2 * dk, :] = jnp.broadcast_to(jnp.exp(last_b), (dk, dk)).T
        return carry

    lax.fori_loop(0, n_chunks, phase1, 0)

    def decay_f(n):
        d = dcol_ref[n, 0:dk, :]
        return jnp.concatenate([d] * (dv // dk), axis=1)

    def decay_b(n):
        d = dcol_ref[n, dk:2 * dk, :]
        return jnp.concatenate([d] * (dv // dk), axis=1)

    _prefix_states(n_chunks, dk, dv, decay_f, decay_b, kv_ref, st_ref)

    def phase2(n, carry):
        rows = _chunk_rows(n)
        cum_f, cum_b = cumf_ref[rows, :], cumb_ref[rows, :]
        ref_f = cum_f[c // 2:c // 2 + 1, :]
        ref_b = cum_b[c // 2 - 1:c // 2, :]
        q = q_ref[0, rows, :] * scale
        k = k_ref[0, rows, :]
        s_f = _dot_nt((q * jnp.exp(cum_f - ref_f)).astype(BF16), (k * jnp.exp(ref_f - cum_f)).astype(BF16))
        s_b = _dot_nt((q * jnp.exp(cum_b - ref_b)).astype(BF16), (k * jnp.exp(ref_b - cum_b)).astype(BF16))
        scores = jnp.where(lower, s_f, s_b)
        intra = _dot(scores.astype(BF16), v_ref[0, rows, :].astype(BF16))
        qw = jnp.concatenate([q * jnp.exp(cum_f), q * jnp.exp(cum_b)], axis=1).astype(BF16)
        o = intra + _dot(qw, st_ref[n])
        y = _rms(o, gain_ref[...])
        g = g_ref[0, rows, :]
        o_ref[0, rows, :] = (y * (g * _sigmoid(g))).astype(o_ref.dtype)
        return carry

    lax.fori_loop(0, n_chunks, phase2, 0)


def _gla(proj, ga, w_f, w_b, b_f, b_b, gain, col0):
    b, s, _ = proj.shape
    h, dk, dv = GLA_HEADS, GLA_DK, GLA_DV
    q0 = col0 // dk
    k0 = q0 + h
    v0 = (col0 + 2 * h * dk) // dv
    g0 = v0 + h
    n_chunks = s // CHUNK
    return pl.pallas_call(
        _gla_kernel,
        grid=(b, h),
        in_specs=[pl.BlockSpec((1, s, dk), lambda i, j: (i, 0, q0 + j)),
                  pl.BlockSpec((1, s, dk), lambda i, j: (i, 0, k0 + j)),
                  pl.BlockSpec((1, s, dv), lambda i, j: (i, 0, v0 + j)),
                  pl.BlockSpec((1, s, dv), lambda i, j: (i, 0, g0 + j)),
                  pl.BlockSpec((1, s, LANES), lambda i, j: (i, 0, 0)),
                  pl.BlockSpec((LANES, dk), lambda i, j: (0, j)),
                  pl.BlockSpec((LANES, dk), lambda i, j: (0, j)),
                  pl.BlockSpec((1, dk), lambda i, j: (0, j)),
                  pl.BlockSpec((1, dk), lambda i, j: (0, j)),
                  pl.BlockSpec((1, dv), lambda i, j: (0, j))],
        out_specs=pl.BlockSpec((1, s, dv), lambda i, j: (i, 0, j)),
        out_shape=jax.ShapeDtypeStruct((b, s, h * dv), BF16),
        scratch_shapes=[pltpu.VMEM((s, dk), F32),
                        pltpu.VMEM((s, dk), F32),
                        pltpu.VMEM((n_chunks, 2 * dk, dv), F32),
                        pltpu.VMEM((n_chunks, 2 * dk, dk), F32),
                        pltpu.VMEM((n_chunks, 2 * dk, dv), BF16)],
        compiler_params=_params(("parallel", "parallel")),
        name="gla",
    )(proj, proj, proj, proj, ga, w_f, w_b, b_f, b_b, gain)


def _merge_kernel(n_experts, ret_ref, gla_ref, gl_ref, x_ref, wr_ref, wg_ref, wo_ref, gain_ref,
                  wrh_ref, wrl_ref, x1_ref, h2_ref, aff_ref):
    d = x_ref.shape[1]
    a = _dot(ret_ref[...], wr_ref[...])
    b = _dot(gla_ref[...], wg_ref[...])
    merged = _sigmoid(gl_ref[:, 0:d]) * a + _sigmoid(gl_ref[:, d:2 * d]) * b
    x1 = x_ref[...] + _dot(merged.astype(BF16), wo_ref[...])
    x1_ref[...] = x1
    h2 = _rms(x1, gain_ref[...])
    h_hi = h2.astype(BF16)
    h2_ref[...] = h_hi
    h_lo = (h2 - h_hi.astype(F32)).astype(BF16)
    logits = _dot(h_hi, wrh_ref[...]) + _dot(h_hi, wrl_ref[...]) + _dot(h_lo, wrh_ref[...])
    lane = lax.broadcasted_iota(jnp.int32, logits.shape, 1)
    logits = jnp.where(lane < n_experts, logits, -jnp.inf)
    p = jnp.exp(logits - jnp.max(logits, axis=-1, keepdims=True))
    aff = p / jnp.sum(p, axis=-1, keepdims=True)
    aff_ref[0] = aff.T[0:n_experts, :]


def _merge(ret, gla, proj2, gl_col0, x2, w_ret, w_gla, w_out, gain, wr_hi, wr_lo, n_experts, batch, tm):
    t, d = x2.shape
    s = t // batch
    per_b = s // tm
    glb = gl_col0 // (2 * d)
    full = lambda shape: pl.BlockSpec(shape, lambda i: (0,) * len(shape))
    return pl.pallas_call(
        functools.partial(_merge_kernel, n_experts),
        grid=(t // tm,),
        in_specs=[pl.BlockSpec((tm, ret.shape[1]), lambda i: (i, 0)),
                  pl.BlockSpec((tm, gla.shape[1]), lambda i: (i, 0)),
                  pl.BlockSpec((tm, 2 * d), lambda i: (i, glb)),
                  pl.BlockSpec((tm, d), lambda i: (i, 0)),
                  full(w_ret.shape), full(w_gla.shape), full(w_out.shape), full((1, d)),
                  full(wr_hi.shape), full(wr_lo.shape)],
        out_specs=[pl.BlockSpec((tm, d), lambda i: (i, 0)),
                   pl.BlockSpec((tm, d), lambda i: (i, 0)),
                   pl.BlockSpec((1, n_experts, tm), lambda i: (i // per_b, 0, i % per_b))],
        out_shape=[jax.ShapeDtypeStruct((t, d), F32),
                   jax.ShapeDtypeStruct((t, d), BF16),
                   jax.ShapeDtypeStruct((batch, n_experts, s), F32)],
        compiler_params=_params(("parallel",)),
        name="merge",
    )(ret, gla, proj2, x2, w_ret, w_gla, w_out, gain, wr_hi, wr_lo)


def _prefix_count(mask):
    s = mask.shape[1]
    ri = lax.broadcasted_iota(jnp.int32, (LANES, LANES), 0)
    ci = lax.broadcasted_iota(jnp.int32, (LANES, LANES), 1)
    tri = jnp.where(ri <= ci, 1.0, 0.0).astype(BF16)
    off = jnp.zeros((mask.shape[0], 1), F32)
    parts = []
    for j in range(s // LANES):
        p = _dot(mask[:, j * LANES:(j + 1) * LANES].astype(BF16), tri) + off
        parts.append(p)
        off = p[:, LANES - 1:LANES]
    return jnp.concatenate(parts, axis=1)


def _route_kernel(capacity, aff_ref, slot_ref):
    a = aff_ref[0]
    bits = lax.bitcast_convert_type(a, jnp.int32)
    n_e = a.shape[0]
    cap = float(capacity)

    def count(pred):
        return jnp.sum(jnp.where(pred, 1.0, 0.0), axis=1, keepdims=True)

    def search(_, c):
        lo, hi = c
        mid = lo + lax.shift_right_logical(hi - lo, 1)
        ok = count(bits >= mid) >= cap
        return jnp.where(ok, mid, lo), jnp.where(ok, hi, mid)

    lo0 = jnp.zeros((n_e, 1), jnp.int32)
    hi0 = jnp.full((n_e, 1), 0x7F800000, jnp.int32)
    thr_bits, _ = lax.fori_loop(0, 31, search, (lo0, hi0))
    thr0 = jnp.max(jnp.where(bits <= thr_bits, a, -1.0), axis=1, keepdims=True)

    def counts(v):
        return count(a >= v), count(a > v)

    def unsettled(state):
        _, c_ge, c_gt = state
        bad = jnp.where(c_ge < cap, 1.0, jnp.where(c_gt >= cap, 1.0, 0.0))
        return jnp.max(bad, axis=0, keepdims=True)[0, 0] > 0.0

    def step(state):
        v, c_ge, c_gt = state
        below = jnp.max(jnp.where(a < v, a, -1.0), axis=1, keepdims=True)
        above = jnp.min(jnp.where(a > v, a, 2.0), axis=1, keepdims=True)
        v = jnp.where(c_ge < cap, below, jnp.where(c_gt >= cap, above, v))
        return (v,) + counts(v)

    thr, _, n_gt = lax.while_loop(unsettled, step, (thr0,) + counts(thr0))

    gt = a > thr
    eq = a == thr
    need = cap - n_gt
    eq_rank = _prefix_count(jnp.where(eq, 1.0, 0.0))
    sel = jnp.where(gt, 1.0, jnp.where(eq, jnp.where(eq_rank <= need, 1.0, 0.0), 0.0))
    pos = _prefix_count(sel)
    slot_ref[0] = jnp.where(sel > 0.0, pos - 1.0, -1.0)


def _route(aff_t, capacity):
    b, e, s = aff_t.shape
    return pl.pallas_call(
        functools.partial(_route_kernel, capacity),
        grid=(b,),
        in_specs=[pl.BlockSpec((1, e, s), lambda i: (i, 0, 0))],
        out_specs=pl.BlockSpec((1, e, s), lambda i: (i, 0, 0)),
        out_shape=jax.ShapeDtypeStruct((b, e, s), F32),
        compiler_params=_params(("parallel",)),
        name="route",
    )(aff_t)


def _dispatch_kernel(slot_ref, aff_ref, h_ref, xg_ref, gate_ref):
    cap = xg_ref.shape[2]
    slot = slot_ref[0, 0]
    hit = lax.broadcasted_iota(jnp.int32, (cap, slot.shape[1]), 0).astype(F32) == slot
    onehot = jnp.where(hit, 1.0, 0.0).astype(BF16)
    xg_ref[0, 0] = _dot(onehot, h_ref[0]).astype(xg_ref.dtype)
    gate = jnp.sum(jnp.where(hit, aff_ref[0, 0], 0.0), axis=1, keepdims=True)
    gate_ref[0, 0] = jnp.broadcast_to(gate, gate_ref.shape[2:])


def _dispatch(slot, aff_t, h2, capacity):
    b, e, s = slot.shape
    d = h2.shape[-1]
    return pl.pallas_call(
        _dispatch_kernel,
        grid=(b, e),
        in_specs=[pl.BlockSpec((1, 1, 1, s), lambda i, j: (i, j, 0, 0)),
                  pl.BlockSpec((1, 1, 1, s), lambda i, j: (i, j, 0, 0)),
                  pl.BlockSpec((1, s, d), lambda i, j: (i, 0, 0))],
        out_specs=[pl.BlockSpec((1, 1, capacity, d), lambda i, j: (j, i, 0, 0)),
                   pl.BlockSpec((1, 1, capacity, LANES), lambda i, j: (j, i, 0, 0))],
        out_shape=[jax.ShapeDtypeStruct((e, b, capacity, d), BF16),
                   jax.ShapeDtypeStruct((e, b, capacity, LANES), F32)],
        compiler_params=_params(("parallel", "parallel")),
        name="dispatch",
    )(slot.reshape(b, e, 1, s), aff_t.reshape(b, e, 1, s), h2)


def _ffn_kernel(tm, x_ref, gate_ref, wg_ref, wu_ref, wd_ref, y_ref, acc_ref):
    f = pl.program_id(1)
    w_gate = wg_ref[0].astype(BF16)
    w_up = wu_ref[0].astype(BF16)
    w_down = wd_ref[0].astype(BF16)
    m = x_ref.shape[1]
    d = x_ref.shape[2]

    @pl.when(f == 0)
    def _():
        acc_ref[...] = jnp.zeros_like(acc_ref)

    for i in range(m // tm):
        rows = pl.ds(i * tm, tm)
        x = x_ref[0, rows, :]
        g = _dot(x, w_gate)
        u = _dot(x, w_up)
        act = (g * _sigmoid(g) * u).astype(BF16)
        acc_ref[rows, :] += _dot(act, w_down)

    @pl.when(f == pl.num_programs(1) - 1)
    def _():
        gate = jnp.concatenate([gate_ref[0]] * (d // LANES), axis=1)
        y_ref[0] = (acc_ref[...] * gate).astype(y_ref.dtype)


def _ffn(xg, gate, w_gate, w_up, w_down, tf, tm):
    e, m, d = xg.shape
    f = w_gate.shape[2]
    return pl.pallas_call(
        functools.partial(_ffn_kernel, tm),
        grid=(e, f // tf),
        in_specs=[pl.BlockSpec((1, m, d), lambda i, j: (i, 0, 0)),
                  pl.BlockSpec((1, m, LANES), lambda i, j: (i, 0, 0)),
                  pl.BlockSpec((1, d, tf), lambda i, j: (i, 0, j)),
                  pl.BlockSpec((1, d, tf), lambda i, j: (i, 0, j)),
                  pl.BlockSpec((1, tf, d), lambda i, j: (i, j, 0))],
        out_specs=pl.BlockSpec((1, m, d), lambda i, j: (i, 0, 0)),
        out_shape=jax.ShapeDtypeStruct((e, m, d), BF16),
        scratch_shapes=[pltpu.VMEM((m, d), F32)],
        compiler_params=_params(("parallel", "arbitrary")),
        name="ffn",
    )(xg, gate, w_gate, w_up, w_down)


def _combine_kernel(slot_ref, y_ref, x1_ref, gain_ref, o_ref):
    n_e, cap = y_ref.shape[0], y_ref.shape[2]
    tt = x1_ref.shape[1]
    acc = x1_ref[0]
    row = lax.broadcasted_iota(jnp.int32, (cap, tt), 0).astype(F32)
    for e in range(n_e):
        onehot = jnp.where(row == slot_ref[0, e:e + 1, :], 1.0, 0.0).astype(BF16)
        acc = acc + _dot_tn(onehot, y_ref[e, 0])
    o_ref[0] = _rms(acc, gain_ref[...])


def _combine(slot, y, x1, gain, tt):
    b, e, s = slot.shape
    cap, d = y.shape[2], y.shape[3]
    return pl.pallas_call(
        _combine_kernel,
        grid=(b, s // tt),
        in_specs=[pl.BlockSpec((1, e, tt), lambda i, j: (i, 0, j)),
                  pl.BlockSpec((e, 1, cap, d), lambda i, j: (0, i, 0, 0)),
                  pl.BlockSpec((1, tt, d), lambda i, j: (i, j, 0)),
                  pl.BlockSpec((1, d), lambda i, j: (0, 0))],
        out_specs=pl.BlockSpec((1, tt, d), lambda i, j: (i, j, 0)),
        out_shape=jax.ShapeDtypeStruct((b, s, d), F32),
        compiler_params=_params(("parallel", "parallel")),
        name="combine",
    )(slot, y, x1, gain)


def _pick(n, prefs):
    for p in prefs:
        if n % p == 0:
            return p
    return n


def _layer(x, cos, sin, norm_mix, w_in, ret_decay_fwd, ret_decay_bwd, ret_norm,
           gla_gate_w_fwd, gla_gate_b_fwd, gla_gate_w_bwd, gla_gate_b_bwd, gla_norm,
           w_branch_ret, w_branch_gla, w_out, norm_ffn, w_router, w_gate, w_up, w_down, norm_out):
    b, s, d = x.shape
    t = b * s
    ret_qk, ret_v = RET_HEADS * RET_DK, RET_HEADS * RET_DV
    gla_qk, gla_v = GLA_HEADS * GLA_DK, GLA_HEADS * GLA_DV
    rank = GLA_GATE_RANK
    ga0 = 2 * ret_qk + 2 * ret_v + 2 * gla_qk + 2 * gla_v
    assert w_in.shape == (d, ga0 + 2 * rank + 2 * d)
    assert s % CHUNK == 0 and 2 * rank <= LANES

    w_main = jnp.concatenate([w_in[:, :ga0], w_in[:, ga0 + 2 * rank:]], axis=1).astype(BF16)
    w_ga = jnp.pad(w_in[:, ga0:ga0 + 2 * rank], ((0, 0), (0, LANES - 2 * rank))).astype(BF16)
    gate_w_f = jnp.pad(gla_gate_w_fwd, ((0, LANES - rank), (0, 0)))
    gate_w_b = jnp.pad(gla_gate_w_bwd, ((rank, LANES - 2 * rank), (0, 0)))
    dec = jnp.stack([ret_decay_fwd, ret_decay_bwd], axis=1)[:, :, None]
    dec = jnp.pad(jnp.broadcast_to(dec, (RET_HEADS, 2, RET_DV)), ((0, 0), (0, 6), (0, 0)))

    x2 = x.reshape(t, d)
    tm = _pick(t, (1024, 512, 256, 128))
    tn = _pick(w_main.shape[1], (1024, 512, 256, 128))
    proj, ga = _in_proj(x2, norm_mix[None, :], w_main, w_ga, tm, tn)
    proj3 = proj.reshape(b, s, -1)

    ret = _retention(proj3, cos, sin, dec, ret_norm[None, :], 0)
    gla = _gla(proj3, ga.reshape(b, s, LANES), gate_w_f, gate_w_b,
               gla_gate_b_fwd[None, :], gla_gate_b_bwd[None, :], gla_norm[None, :],
               2 * ret_qk + 2 * ret_v)

    n_e = w_router.shape[1]
    w_r = jnp.pad(w_router, ((0, 0), (0, LANES - n_e)))
    wr_hi = w_r.astype(BF16)
    wr_lo = (w_r - wr_hi.astype(F32)).astype(BF16)
    x1, h2, aff_t = _merge(ret.reshape(t, ret_v), gla.reshape(t, gla_v), proj, ga0, x2,
                           w_branch_ret.astype(BF16), w_branch_gla.astype(BF16), w_out.astype(BF16),
                           norm_ffn[None, :], wr_hi, wr_lo, n_e, b, _pick(s, (512, 256, 128)))

    capacity = EC_CAPACITY_FACTOR * s // n_e
    slot = _route(aff_t, capacity)
    xg, gate = _dispatch(slot, aff_t, h2.reshape(b, s, d), capacity)
    f = w_gate.shape[2]
    y = _ffn(xg.reshape(n_e, b * capacity, d), gate.reshape(n_e, b * capacity, LANES),
             w_gate, w_up, w_down, _pick(f, (256, 128)), _pick(b * capacity, (512, 256, 128)))
    return _combine(slot, y.reshape(n_e, b, capacity, d), x1.reshape(b, s, d), norm_out,
                    _pick(s, (512, 256, 128)))


def kernel(x, positions, norm_mix, w_in, ret_decay_fwd, ret_decay_bwd, ret_norm, gla_gate_w_fwd,
           gla_gate_b_fwd, gla_gate_w_bwd, gla_gate_b_bwd, gla_norm, w_branch_ret, w_branch_gla,
           w_out, norm_ffn, w_router, w_gate, w_up, w_down, norm_final):
    depth = norm_mix.shape[0]
    assert depth == 1, "the final RMSNorm is fused into the last layer's combine stage"
    cos, sin = _rope_table(positions, RET_DK)
    return _layer(x, cos, sin, norm_mix[0], w_in[0], ret_decay_fwd[0], ret_decay_bwd[0], ret_norm[0],
                  gla_gate_w_fwd[0], gla_gate_b_fwd[0], gla_gate_w_bwd[0], gla_gate_b_bwd[0], gla_norm[0],
                  w_branch_ret[0], w_branch_gla[0], w_out[0], norm_ffn[0], w_router[0],
                  w_gate[0], w_up[0], w_down[0], norm_final[None, :])
```

```python
import functools

import jax
import jax.numpy as jnp
from jax import lax
from jax.experimental import pallas as pl
from jax.experimental.pallas import tpu as pltpu

F32 = jnp.float32
BF16 = jnp.bfloat16

RET_HEADS = 4
RET_DK = 128
RET_DV = 256
GLA_HEADS = 4
GLA_DK = 128
GLA_DV = 256
GLA_GATE_RANK = 16
GLA_GATE_NORMALIZER = 16.0
CHUNK = 128
EC_CAPACITY_FACTOR = 2
ROPE_THETA = 10000.0
NORM_EPS = 1e-6

CHUNK_UNROLL = 4
LANES = 128
VMEM_LIMIT = 56 << 20


def _params(sem, vmem=VMEM_LIMIT):
    return pltpu.CompilerParams(dimension_semantics=sem, vmem_limit_bytes=vmem)


def _sigmoid(x):
    return 1.0 / (1.0 + jnp.exp(-x))


def _log_sigmoid(x):
    return jnp.minimum(x, 0.0) - jnp.log1p(jnp.exp(-jnp.abs(x)))


def _rms(x, gain):
    return x * lax.rsqrt(jnp.mean(x * x, axis=-1, keepdims=True) + NORM_EPS) * gain


def _dot(a, b):
    return jnp.dot(a, b, preferred_element_type=F32)


def _dot_nt(a, b):
    return lax.dot_general(a, b, (((1,), (1,)), ((), ())), preferred_element_type=F32)


def _dot_tn(a, b):
    return lax.dot_general(a, b, (((0,), (0,)), ((), ())), preferred_element_type=F32)


def _split3(x):
    hi = x.astype(BF16)
    r = x - hi.astype(F32)
    mid = r.astype(BF16)
    lo = (r - mid.astype(F32)).astype(BF16)
    return hi, mid, lo


def _dot_exact_lhs(tri, x):
    hi, mid, lo = _split3(x)
    return _dot(tri, hi) + _dot(tri, mid) + _dot(tri, lo)


def _dot_f32(a, b):
    a_hi = a.astype(BF16)
    a_lo = (a - a_hi.astype(F32)).astype(BF16)
    b_hi = b.astype(BF16)
    b_lo = (b - b_hi.astype(F32)).astype(BF16)
    return _dot(a_hi, b_hi) + _dot(a_hi, b_lo) + _dot(a_lo, b_hi)


def _rope_table_kernel(pos_ref, freq_ref, sign_ref, cos_ref, sin_ref):
    ang = pos_ref[0].astype(F32) * freq_ref[...]
    cos_ref[0] = jnp.cos(ang)
    sin_ref[0] = jnp.sin(ang) * sign_ref[...]


def _rope_table(positions, dk):
    b, s = positions.shape
    half = jnp.arange(0, dk, 2, dtype=F32) / dk
    inv_freq = ROPE_THETA ** (-half)
    freq = jnp.concatenate([inv_freq, inv_freq])[None, :]
    sign = jnp.concatenate([-jnp.ones(dk // 2, F32), jnp.ones(dk // 2, F32)])[None, :]
    out = jax.ShapeDtypeStruct((b, s, dk), F32)
    return pl.pallas_call(
        _rope_table_kernel,
        grid=(b,),
        in_specs=[pl.BlockSpec((1, s, 1), lambda i: (i, 0, 0)),
                  pl.BlockSpec((1, dk), lambda i: (0, 0)),
                  pl.BlockSpec((1, dk), lambda i: (0, 0))],
        out_specs=[pl.BlockSpec((1, s, dk), lambda i: (i, 0, 0))] * 2,
        out_shape=[out, out],
        compiler_params=_params(("parallel",)),
        name="rope_table",
    )(positions[:, :, None], freq, sign)


def _rope(t, cos, sin_signed):
    return t * cos + pltpu.roll(t, t.shape[-1] // 2, 1) * sin_signed


def _in_proj_kernel(x_ref, gain_ref, w_ref, wga_ref, proj_ref, ga_ref, h_ref):
    @pl.when(pl.program_id(1) == 0)
    def _():
        h = _rms(x_ref[...], gain_ref[...]).astype(BF16)
        h_ref[...] = h
        ga_ref[...] = _dot(h, wga_ref[...])

    proj_ref[...] = _dot(h_ref[...], w_ref[...])


def _in_proj(x2, gain, w_main, w_ga, tm, tn):
    t, d = x2.shape
    n = w_main.shape[1]
    return pl.pallas_call(
        _in_proj_kernel,
        grid=(t // tm, n // tn),
        in_specs=[pl.BlockSpec((tm, d), lambda i, j: (i, 0)),
                  pl.BlockSpec((1, d), lambda i, j: (0, 0)),
                  pl.BlockSpec((d, tn), lambda i, j: (0, j)),
                  pl.BlockSpec((d, LANES), lambda i, j: (0, 0))],
        out_specs=[pl.BlockSpec((tm, tn), lambda i, j: (i, j)),
                   pl.BlockSpec((tm, LANES), lambda i, j: (i, 0))],
        out_shape=[jax.ShapeDtypeStruct((t, n), F32),
                   jax.ShapeDtypeStruct((t, LANES), F32)],
        scratch_shapes=[pltpu.VMEM((tm, d), BF16)],
        compiler_params=_params(("parallel", "arbitrary")),
        name="in_proj",
    )(x2, gain, w_main, w_ga)


def _prefix_states(n_chunks, dk, dv, decay_f, decay_b, kv_ref, st_ref):
    def fwd(n, s):
        st_ref[n, 0:dk, :] = s.astype(BF16)
        return decay_f(n) * s + kv_ref[n, 0:dk, :]

    def bwd(i, s):
        n = n_chunks - 1 - i
        st_ref[n, dk:2 * dk, :] = s.astype(BF16)
        return decay_b(n) * s + kv_ref[n, dk:2 * dk, :]

    zero = jnp.zeros((dk, dv), F32)
    lax.fori_loop(0, n_chunks, fwd, zero)
    lax.fori_loop(0, n_chunks, bwd, zero)


def _chunk_rows(n):
    return pl.ds(pl.multiple_of(n * CHUNK, CHUNK), CHUNK)


def _retention_kernel(q_ref, k_ref, v_ref, g_ref, cos_ref, sin_ref, dec_ref, gain_ref, o_ref,
                      qr_ref, kr_ref, kv_ref, st_ref):
    c = CHUNK
    s_len, dk = q_ref.shape[1], q_ref.shape[2]
    dv = v_ref.shape[2]
    n_chunks = s_len // c
    scale = dk ** -0.5

    lg_f = _log_sigmoid(dec_ref[0, 0:1, :])
    lg_b = _log_sigmoid(dec_ref[0, 1:2, :])
    lgf_k, lgb_k = lg_f[:, :dk], lg_b[:, :dk]
    pos = lax.broadcasted_iota(jnp.int32, (c, dk), 0).astype(F32)
    wq_f = jnp.exp((pos + 1.0) * lgf_k)
    wk_f = jnp.exp((c - 1.0 - pos) * lgf_k)
    wq_b = jnp.exp((c - pos) * lgb_k)
    wk_b = jnp.exp(pos * lgb_k)
    ri = lax.broadcasted_iota(jnp.int32, (c, c), 0)
    ci = lax.broadcasted_iota(jnp.int32, (c, c), 1)
    lower = ri >= ci
    rel = (ri - ci).astype(F32)
    decay_mask = jnp.where(lower,
                           jnp.exp(jnp.where(lower, rel, 0.0) * lg_f[:, :c]),
                           jnp.exp(jnp.where(lower, 0.0, -rel) * lg_b[:, :c]))
    chunk_decay_f = jnp.exp(c * lg_f)
    chunk_decay_b = jnp.exp(c * lg_b)

    def phase1(n, carry):
        rows = _chunk_rows(n)
        cos, sin = cos_ref[0, rows, :], sin_ref[0, rows, :]
        qr = _rope(q_ref[0, rows, :], cos, sin) * scale
        kr = _rope(k_ref[0, rows, :], cos, sin)
        qr_ref[rows, :] = qr
        kr_ref[rows, :] = kr
        kw = jnp.concatenate([kr * wk_f, kr * wk_b], axis=1).astype(BF16)
        kv_ref[n] = _dot_tn(kw, v_ref[0, rows, :].astype(BF16))
        return carry

    lax.fori_loop(0, n_chunks, phase1, 0, unroll=CHUNK_UNROLL)
    _prefix_states(n_chunks, dk, dv, lambda n: chunk_decay_f, lambda n: chunk_decay_b, kv_ref, st_ref)

    def phase2(n, carry):
        rows = _chunk_rows(n)
        qr, kr = qr_ref[rows, :], kr_ref[rows, :]
        scores = _dot_nt(qr.astype(BF16), kr.astype(BF16)) * decay_mask
        intra = _dot(scores.astype(BF16), v_ref[0, rows, :].astype(BF16))
        qw = jnp.concatenate([qr * wq_f, qr * wq_b], axis=1).astype(BF16)
        o = intra + _dot(qw, st_ref[n])
        mu = jnp.mean(o, axis=-1, keepdims=True)
        d = o - mu
        var = jnp.mean(d * d, axis=-1, keepdims=True)
        y = d * lax.rsqrt(var + NORM_EPS) * gain_ref[...]
        g = g_ref[0, rows, :]
        o_ref[0, rows, :] = (y * (g * _sigmoid(g))).astype(o_ref.dtype)
        return carry

    lax.fori_loop(0, n_chunks, phase2, 0, unroll=CHUNK_UNROLL)


def _retention(proj, cos, sin, dec, gain, col0):
    b, s, _ = proj.shape
    h, dk, dv = RET_HEADS, RET_DK, RET_DV
    q0 = col0 // dk
    k0 = q0 + h
    v0 = (col0 + 2 * h * dk) // dv
    g0 = v0 + h
    n_chunks = s // CHUNK
    return pl.pallas_call(
        _retention_kernel,
        grid=(b, h),
        in_specs=[pl.BlockSpec((1, s, dk), lambda i, j: (i, 0, q0 + j)),
                  pl.BlockSpec((1, s, dk), lambda i, j: (i, 0, k0 + j)),
                  pl.BlockSpec((1, s, dv), lambda i, j: (i, 0, v0 + j)),
                  pl.BlockSpec((1, s, dv), lambda i, j: (i, 0, g0 + j)),
                  pl.BlockSpec((1, s, dk), lambda i, j: (i, 0, 0)),
                  pl.BlockSpec((1, s, dk), lambda i, j: (i, 0, 0)),
                  pl.BlockSpec((1, 8, dv), lambda i, j: (j, 0, 0)),
                  pl.BlockSpec((1, dv), lambda i, j: (0, j))],
        out_specs=pl.BlockSpec((1, s, dv), lambda i, j: (i, 0, j)),
        out_shape=jax.ShapeDtypeStruct((b, s, h * dv), BF16),
        scratch_shapes=[pltpu.VMEM((s, dk), F32),
                        pltpu.VMEM((s, dk), F32),
                        pltpu.VMEM((n_chunks, 2 * dk, dv), F32),
                        pltpu.VMEM((n_chunks, 2 * dk, dv), BF16)],
        compiler_params=_params(("parallel", "parallel")),
        name="retention",
    )(proj, proj, proj, proj, cos, sin, dec, gain)


def _gla_kernel(q_ref, k_ref, v_ref, g_ref, ga_ref, wf_ref, wb_ref, bf_ref, bb_ref, gain_ref, o_ref,
                cumf_ref, cumb_ref, kv_ref, dcol_ref, st_ref):
    c = CHUNK
    s_len, dk = q_ref.shape[1], q_ref.shape[2]
    dv = v_ref.shape[2]
    n_chunks = s_len // c
    scale = dk ** -0.5
    inv_norm = 1.0 / GLA_GATE_NORMALIZER

    ri = lax.broadcasted_iota(jnp.int32, (c, c), 0)
    ci = lax.broadcasted_iota(jnp.int32, (c, c), 1)
    lower = ri >= ci
    tri_lower = jnp.where(lower, 1.0, 0.0).astype(BF16)
    tri_upper = jnp.where(ci >= ri, 1.0, 0.0).astype(BF16)

    def phase1(n, carry):
        rows = _chunk_rows(n)
        ga = ga_ref[0, rows, :]
        la_f = _log_sigmoid(_dot_f32(ga, wf_ref[...]) + bf_ref[...]) * inv_norm
        la_b = _log_sigmoid(_dot_f32(ga, wb_ref[...]) + bb_ref[...]) * inv_norm
        cum_f = _dot_exact_lhs(tri_lower, la_f)
        cum_b = _dot_exact_lhs(tri_upper, la_b)
        cumf_ref[rows, :] = cum_f
        cumb_ref[rows, :] = cum_b
        last_f = cum_f[c - 1:c, :]
        last_b = cum_b[0:1, :]
        k = k_ref[0, rows, :]
        kw = jnp.concatenate([k * jnp.exp(last_f - cum_f), k * jnp.exp(last_b - cum_b)], axis=1).astype(BF16)
        kv_ref[n] = _dot_tn(kw, v_ref[0, rows, :].astype(BF16))
        dcol_ref[n, 0:dk, :] = jnp.broadcast_to(jnp.exp(last_f), (dk, dk)).T
        dcol_ref[n, dk:2 * dk, :] = jnp.broadcast_to(jnp.exp(last_b), (dk, dk)).T
        return carry

    lax.fori_loop(0, n_chunks, phase1, 0, unroll=CHUNK_UNROLL)

    def decay_f(n):
        d = dcol_ref[n, 0:dk, :]
        return jnp.concatenate([d] * (dv // dk), axis=1)

    def decay_b(n):
        d = dcol_ref[n, dk:2 * dk, :]
        return jnp.concatenate([d] * (dv // dk), axis=1)

    _prefix_states(n_chunks, dk, dv, decay_f, decay_b, kv_ref, st_ref)

    def phase2(n, carry):
        rows = _chunk_rows(n)
        cum_f, cum_b = cumf_ref[rows, :], cumb_ref[rows, :]
        ref_f = cum_f[c // 2:c // 2 + 1, :]
        ref_b = cum_b[c // 2 - 1:c // 2, :]
        q = q_ref[0, rows, :] * scale
        k = k_ref[0, rows, :]
        s_f = _dot_nt((q * jnp.exp(cum_f - ref_f)).astype(BF16), (k * jnp.exp(ref_f - cum_f)).astype(BF16))
        s_b = _dot_nt((q * jnp.exp(cum_b - ref_b)).astype(BF16), (k * jnp.exp(ref_b - cum_b)).astype(BF16))
        scores = jnp.where(lower, s_f, s_b)
        intra = _dot(scores.astype(BF16), v_ref[0, rows, :].astype(BF16))
        qw = jnp.concatenate([q * jnp.exp(cum_f), q * jnp.exp(cum_b)], axis=1).astype(BF16)
        o = intra + _dot(qw, st_ref[n])
        y = _rms(o, gain_ref[...])
        g = g_ref[0, rows, :]
        o_ref[0, rows, :] = (y * (g * _sigmoid(g))).astype(o_ref.dtype)
        return carry

    lax.fori_loop(0, n_chunks, phase2, 0, unroll=CHUNK_UNROLL)


def _gla(proj, ga, w_f, w_b, b_f, b_b, gain, col0):
    b, s, _ = proj.shape
    h, dk, dv = GLA_HEADS, GLA_DK, GLA_DV
    q0 = col0 // dk
    k0 = q0 + h
    v0 = (col0 + 2 * h * dk) // dv
    g0 = v0 + h
    n_chunks = s // CHUNK
    return pl.pallas_call(
        _gla_kernel,
        grid=(b, h),
        in_specs=[pl.BlockSpec((1, s, dk), lambda i, j: (i, 0, q0 + j)),
                  pl.BlockSpec((1, s, dk), lambda i, j: (i, 0, k0 + j)),
                  pl.BlockSpec((1, s, dv), lambda i, j: (i, 0, v0 + j)),
                  pl.BlockSpec((1, s, dv), lambda i, j: (i, 0, g0 + j)),
                  pl.BlockSpec((1, s, LANES), lambda i, j: (i, 0, 0)),
                  pl.BlockSpec((LANES, dk), lambda i, j: (0, j)),
                  pl.BlockSpec((LANES, dk), lambda i, j: (0, j)),
                  pl.BlockSpec((1, dk), lambda i, j: (0, j)),
                  pl.BlockSpec((1, dk), lambda i, j: (0, j)),
                  pl.BlockSpec((1, dv), lambda i, j: (0, j))],
        out_specs=pl.BlockSpec((1, s, dv), lambda i, j: (i, 0, j)),
        out_shape=jax.ShapeDtypeStruct((b, s, h * dv), BF16),
        scratch_shapes=[pltpu.VMEM((s, dk), F32),
                        pltpu.VMEM((s, dk), F32),
                        pltpu.VMEM((n_chunks, 2 * dk, dv), F32),
                        pltpu.VMEM((n_chunks, 2 * dk, dk), F32),
                        pltpu.VMEM((n_chunks, 2 * dk, dv), BF16)],
        compiler_params=_params(("parallel", "parallel")),
        name="gla",
    )(proj, proj, proj, proj, ga, w_f, w_b, b_f, b_b, gain)


def _merge_kernel(n_experts, ret_ref, gla_ref, gl_ref, x_ref, wr_ref, wg_ref, wo_ref, gain_ref,
                  wrh_ref, wrl_ref, x1_ref, h2_ref, aff_ref):
    d = x_ref.shape[1]
    a = _dot(ret_ref[...], wr_ref[...])
    b = _dot(gla_ref[...], wg_ref[...])
    merged = _sigmoid(gl_ref[:, 0:d]) * a + _sigmoid(gl_ref[:, d:2 * d]) * b
    x1 = x_ref[...] + _dot(merged.astype(BF16), wo_ref[...])
    x1_ref[...] = x1
    h2 = _rms(x1, gain_ref[...])
    h_hi = h2.astype(BF16)
    h2_ref[...] = h_hi
    h_lo = (h2 - h_hi.astype(F32)).astype(BF16)
    logits = _dot(h_hi, wrh_ref[...]) + _dot(h_hi, wrl_ref[...]) + _dot(h_lo, wrh_ref[...])
    lane = lax.broadcasted_iota(jnp.int32, logits.shape, 1)
    logits = jnp.where(lane < n_experts, logits, -jnp.inf)
    p = jnp.exp(logits - jnp.max(logits, axis=-1, keepdims=True))
    aff = p / jnp.sum(p, axis=-1, keepdims=True)
    aff_ref[0] = aff.T[0:n_experts, :]


def _merge(ret, gla, proj2, gl_col0, x2, w_ret, w_gla, w_out, gain, wr_hi, wr_lo, n_experts, batch, tm):
    t, d = x2.shape
    s = t // batch
    per_b = s // tm
    glb = gl_col0 // (2 * d)
    full = lambda shape: pl.BlockSpec(shape, lambda i: (0,) * len(shape))
    return pl.pallas_call(
        functools.partial(_merge_kernel, n_experts),
        grid=(t // tm,),
        in_specs=[pl.BlockSpec((tm, ret.shape[1]), lambda i: (i, 0)),
                  pl.BlockSpec((tm, gla.shape[1]), lambda i: (i, 0)),
                  pl.BlockSpec((tm, 2 * d), lambda i: (i, glb)),
                  pl.BlockSpec((tm, d), lambda i: (i, 0)),
                  full(w_ret.shape), full(w_gla.shape), full(w_out.shape), full((1, d)),
                  full(wr_hi.shape), full(wr_lo.shape)],
        out_specs=[pl.BlockSpec((tm, d), lambda i: (i, 0)),
                   pl.BlockSpec((tm, d), lambda i: (i, 0)),
                   pl.BlockSpec((1, n_experts, tm), lambda i: (i // per_b, 0, i % per_b))],
        out_shape=[jax.ShapeDtypeStruct((t, d), F32),
                   jax.ShapeDtypeStruct((t, d), BF16),
                   jax.ShapeDtypeStruct((batch, n_experts, s), F32)],
        compiler_params=_params(("parallel",)),
        name="merge",
    )(ret, gla, proj2, x2, w_ret, w_gla, w_out, gain, wr_hi, wr_lo)


def _prefix_count(mask):
    s = mask.shape[1]
    ri = lax.broadcasted_iota(jnp.int32, (LANES, LANES), 0)
    ci = lax.broadcasted_iota(jnp.int32, (LANES, LANES), 1)
    tri = jnp.where(ri <= ci, 1.0, 0.0).astype(BF16)
    off = jnp.zeros((mask.shape[0], 1), F32)
    parts = []
    for j in range(s // LANES):
        p = _dot(mask[:, j * LANES:(j + 1) * LANES].astype(BF16), tri) + off
        parts.append(p)
        off = p[:, LANES - 1:LANES]
    return jnp.concatenate(parts, axis=1)


def _route_kernel(capacity, aff_ref, slot_ref):
    a = aff_ref[0]
    bits = lax.bitcast_convert_type(a, jnp.int32)
    n_e = a.shape[0]
    cap = float(capacity)

    def count(pred):
        return jnp.sum(jnp.where(pred, 1.0, 0.0), axis=1, keepdims=True)

    def search(_, c):
        lo, hi = c
        mid = lo + lax.shift_right_logical(hi - lo, 1)
        ok = count(bits >= mid) >= cap
        return jnp.where(ok, mid, lo), jnp.where(ok, hi, mid)

    lo0 = jnp.zeros((n_e, 1), jnp.int32)
    hi0 = jnp.full((n_e, 1), 0x7F800000, jnp.int32)
    thr_bits, _ = lax.fori_loop(0, 31, search, (lo0, hi0))
    thr0 = jnp.max(jnp.where(bits <= thr_bits, a, -1.0), axis=1, keepdims=True)

    def counts(v):
        return count(a >= v), count(a > v)

    def unsettled(state):
        _, c_ge, c_gt = state
        bad = jnp.where(c_ge < cap, 1.0, jnp.where(c_gt >= cap, 1.0, 0.0))
        return jnp.max(bad, axis=0, keepdims=True)[0, 0] > 0.0

    def step(state):
        v, c_ge, c_gt = state
        below = jnp.max(jnp.where(a < v, a, -1.0), axis=1, keepdims=True)
        above = jnp.min(jnp.where(a > v, a, 2.0), axis=1, keepdims=True)
        v = jnp.where(c_ge < cap, below, jnp.where(c_gt >= cap, above, v))
        return (v,) + counts(v)

    thr, _, n_gt = lax.while_loop(unsettled, step, (thr0,) + counts(thr0))

    gt = a > thr
    eq = a == thr
    need = cap - n_gt
    eq_rank = _prefix_count(jnp.where(eq, 1.0, 0.0))
    sel = jnp.where(gt, 1.0, jnp.where(eq, jnp.where(eq_rank <= need, 1.0, 0.0), 0.0))
    pos = _prefix_count(sel)
    slot_ref[0] = jnp.where(sel > 0.0, pos - 1.0, -1.0)


def _route(aff_t, capacity):
    b, e, s = aff_t.shape
    return pl.pallas_call(
        functools.partial(_route_kernel, capacity),
        grid=(b,),
        in_specs=[pl.BlockSpec((1, e, s), lambda i: (i, 0, 0))],
        out_specs=pl.BlockSpec((1, e, s), lambda i: (i, 0, 0)),
        out_shape=jax.ShapeDtypeStruct((b, e, s), F32),
        compiler_params=_params(("parallel",)),
        name="route",
    )(aff_t)


def _dispatch_kernel(slot_ref, aff_ref, h_ref, xg_ref, gate_ref):
    cap = xg_ref.shape[2]
    slot = slot_ref[0, 0]
    hit = lax.broadcasted_iota(jnp.int32, (cap, slot.shape[1]), 0).astype(F32) == slot
    onehot = jnp.where(hit, 1.0, 0.0).astype(BF16)
    xg_ref[0, 0] = _dot(onehot, h_ref[0]).astype(xg_ref.dtype)
    gate = jnp.sum(jnp.where(hit, aff_ref[0, 0], 0.0), axis=1, keepdims=True)
    gate_ref[0, 0] = jnp.broadcast_to(gate, gate_ref.shape[2:])


def _dispatch(slot, aff_t, h2, capacity):
    b, e, s = slot.shape
    d = h2.shape[-1]
    return pl.pallas_call(
        _dispatch_kernel,
        grid=(b, e),
        in_specs=[pl.BlockSpec((1, 1, 1, s), lambda i, j: (i, j, 0, 0)),
                  pl.BlockSpec((1, 1, 1, s), lambda i, j: (i, j, 0, 0)),
                  pl.BlockSpec((1, s, d), lambda i, j: (i, 0, 0))],
        out_specs=[pl.BlockSpec((1, 1, capacity, d), lambda i, j: (j, i, 0, 0)),
                   pl.BlockSpec((1, 1, capacity, LANES), lambda i, j: (j, i, 0, 0))],
        out_shape=[jax.ShapeDtypeStruct((e, b, capacity, d), BF16),
                   jax.ShapeDtypeStruct((e, b, capacity, LANES), F32)],
        compiler_params=_params(("parallel", "parallel")),
        name="dispatch",
    )(slot.reshape(b, e, 1, s), aff_t.reshape(b, e, 1, s), h2)


def _ffn_kernel(tm, x_ref, gate_ref, wg_ref, wu_ref, wd_ref, y_ref, acc_ref):
    f = pl.program_id(1)
    w_gate = wg_ref[0].astype(BF16)
    w_up = wu_ref[0].astype(BF16)
    w_down = wd_ref[0].astype(BF16)
    m = x_ref.shape[1]
    d = x_ref.shape[2]

    @pl.when(f == 0)
    def _():
        acc_ref[...] = jnp.zeros_like(acc_ref)

    for i in range(m // tm):
        rows = pl.ds(i * tm, tm)
        x = x_ref[0, rows, :]
        g = _dot(x, w_gate)
        u = _dot(x, w_up)
        act = (g * _sigmoid(g) * u).astype(BF16)
        acc_ref[rows, :] += _dot(act, w_down)

    @pl.when(f == pl.num_programs(1) - 1)
    def _():
        gate = jnp.concatenate([gate_ref[0]] * (d // LANES), axis=1)
        y_ref[0] = (acc_ref[...] * gate).astype(y_ref.dtype)


def _ffn(xg, gate, w_gate, w_up, w_down, tf, tm):
    e, m, d = xg.shape
    f = w_gate.shape[2]
    return pl.pallas_call(
        functools.partial(_ffn_kernel, tm),
        grid=(e, f // tf),
        in_specs=[pl.BlockSpec((1, m, d), lambda i, j: (i, 0, 0)),
                  pl.BlockSpec((1, m, LANES), lambda i, j: (i, 0, 0)),
                  pl.BlockSpec((1, d, tf), lambda i, j: (i, 0, j)),
                  pl.BlockSpec((1, d, tf), lambda i, j: (i, 0, j)),
                  pl.BlockSpec((1, tf, d), lambda i, j: (i, j, 0))],
        out_specs=pl.BlockSpec((1, m, d), lambda i, j: (i, 0, 0)),
        out_shape=jax.ShapeDtypeStruct((e, m, d), BF16),
        scratch_shapes=[pltpu.VMEM((m, d), F32)],
        compiler_params=_params(("parallel", "arbitrary")),
        name="ffn",
    )(xg, gate, w_gate, w_up, w_down)


def _combine_kernel(slot_ref, y_ref, x1_ref, gain_ref, o_ref):
    n_e, cap = y_ref.shape[0], y_ref.shape[2]
    tt = x1_ref.shape[1]
    acc = x1_ref[0]
    row = lax.broadcasted_iota(jnp.int32, (cap, tt), 0).astype(F32)
    for e in range(n_e):
        onehot = jnp.where(row == slot_ref[0, e:e + 1, :], 1.0, 0.0).astype(BF16)
        acc = acc + _dot_tn(onehot, y_ref[e, 0])
    o_ref[0] = _rms(acc, gain_ref[...])


def _combine(slot, y, x1, gain, tt):
    b, e, s = slot.shape
    cap, d = y.shape[2], y.shape[3]
    return pl.pallas_call(
        _combine_kernel,
        grid=(b, s // tt),
        in_specs=[pl.BlockSpec((1, e, tt), lambda i, j: (i, 0, j)),
                  pl.BlockSpec((e, 1, cap, d), lambda i, j: (0, i, 0, 0)),
                  pl.BlockSpec((1, tt, d), lambda i, j: (i, j, 0)),
                  pl.BlockSpec((1, d), lambda i, j: (0, 0))],
        out_specs=pl.BlockSpec((1, tt, d), lambda i, j: (i, j, 0)),
        out_shape=jax.ShapeDtypeStruct((b, s, d), F32),
        compiler_params=_params(("parallel", "parallel")),
        name="combine",
    )(slot, y, x1, gain)


def _pick(n, prefs):
    for p in prefs:
        if n % p == 0:
            return p
    return n


def _layer(x, cos, sin, norm_mix, w_in, ret_decay_fwd, ret_decay_bwd, ret_norm,
           gla_gate_w_fwd, gla_gate_b_fwd, gla_gate_w_bwd, gla_gate_b_bwd, gla_norm,
           w_branch_ret, w_branch_gla, w_out, norm_ffn, w_router, w_gate, w_up, w_down, norm_out):
    b, s, d = x.shape
    t = b * s
    ret_qk, ret_v = RET_HEADS * RET_DK, RET_HEADS * RET_DV
    gla_qk, gla_v = GLA_HEADS * GLA_DK, GLA_HEADS * GLA_DV
    rank = GLA_GATE_RANK
    ga0 = 2 * ret_qk + 2 * ret_v + 2 * gla_qk + 2 * gla_v
    assert w_in.shape == (d, ga0 + 2 * rank + 2 * d)
    assert s % CHUNK == 0 and 2 * rank <= LANES

    w_main = jnp.concatenate([w_in[:, :ga0], w_in[:, ga0 + 2 * rank:]], axis=1).astype(BF16)
    w_ga = jnp.pad(w_in[:, ga0:ga0 + 2 * rank], ((0, 0), (0, LANES - 2 * rank))).astype(BF16)
    gate_w_f = jnp.pad(gla_gate_w_fwd, ((0, LANES - rank), (0, 0)))
    gate_w_b = jnp.pad(gla_gate_w_bwd, ((rank, LANES - 2 * rank), (0, 0)))
    dec = jnp.stack([ret_decay_fwd, ret_decay_bwd], axis=1)[:, :, None]
    dec = jnp.pad(jnp.broadcast_to(dec, (RET_HEADS, 2, RET_DV)), ((0, 0), (0, 6), (0, 0)))

    x2 = x.reshape(t, d)
    tm = _pick(t, (1024, 512, 256, 128))
    tn = _pick(w_main.shape[1], (1024, 512, 256, 128))
    proj, ga = _in_proj(x2, norm_mix[None, :], w_main, w_ga, tm, tn)
    proj3 = proj.reshape(b, s, -1)

    ret = _retention(proj3, cos, sin, dec, ret_norm[None, :], 0)
    gla = _gla(proj3, ga.reshape(b, s, LANES), gate_w_f, gate_w_b,
               gla_gate_b_fwd[None, :], gla_gate_b_bwd[None, :], gla_norm[None, :],
               2 * ret_qk + 2 * ret_v)

    n_e = w_router.shape[1]
    w_r = jnp.pad(w_router, ((0, 0), (0, LANES - n_e)))
    wr_hi = w_r.astype(BF16)
    wr_lo = (w_r - wr_hi.astype(F32)).astype(BF16)
    x1, h2, aff_t = _merge(ret.reshape(t, ret_v), gla.reshape(t, gla_v), proj, ga0, x2,
                           w_branch_ret.astype(BF16), w_branch_gla.astype(BF16), w_out.astype(BF16),
                           norm_ffn[None, :], wr_hi, wr_lo, n_e, b, _pick(s, (512, 256, 128)))

    capacity = EC_CAPACITY_FACTOR * s // n_e
    slot = _route(aff_t, capacity)
    xg, gate = _dispatch(slot, aff_t, h2.reshape(b, s, d), capacity)
    f = w_gate.shape[2]
    y = _ffn(xg.reshape(n_e, b * capacity, d), gate.reshape(n_e, b * capacity, LANES),
             w_gate, w_up, w_down, _pick(f, (256, 128)), _pick(b * capacity, (512, 256, 128)))
    return _combine(slot, y.reshape(n_e, b, capacity, d), x1.reshape(b, s, d), norm_out,
                    _pick(s, (512, 256, 128)))


def kernel(x, positions, norm_mix, w_in, ret_decay_fwd, ret_decay_bwd, ret_norm, gla_gate_w_fwd,
           gla_gate_b_fwd, gla_gate_w_bwd, gla_gate_b_bwd, gla_norm, w_branch_ret, w_branch_gla,
           w_out, norm_ffn, w_router, w_gate, w_up, w_down, norm_final):
    depth = norm_mix.shape[0]
    assert depth == 1, "the final RMSNorm is fused into the last layer's combine stage"
    cos, sin = _rope_table(positions, RET_DK)
    return _layer(x, cos, sin, norm_mix[0], w_in[0], ret_decay_fwd[0], ret_decay_bwd[0], ret_norm[0],
                  gla_gate_w_fwd[0], gla_gate_b_fwd[0], gla_gate_w_bwd[0], gla_gate_b_bwd[0], gla_norm[0],
                  w_branch_ret[0], w_branch_gla[0], w_out[0], norm_ffn[0], w_router[0],
                  w_gate[0], w_up[0], w_down[0], norm_final[None, :])
```

```python
import functools

import jax
import jax.numpy as jnp
from jax import lax
from jax.experimental import pallas as pl
from jax.experimental.pallas import tpu as pltpu

F32 = jnp.float32
BF16 = jnp.bfloat16

RET_HEADS = 4
RET_DK = 128
RET_DV = 256
GLA_HEADS = 4
GLA_DK = 128
GLA_DV = 256
GLA_GATE_RANK = 16
GLA_GATE_NORMALIZER = 16.0
CHUNK = 128
EC_CAPACITY_FACTOR = 2
ROPE_THETA = 10000.0
NORM_EPS = 1e-6

CHUNK_UNROLL = 4
LANES = 128
VMEM_LIMIT = 56 << 20


def _params(sem, vmem=VMEM_LIMIT):
    return pltpu.CompilerParams(dimension_semantics=sem, vmem_limit_bytes=vmem)


def _resident(shape):
    return pl.BlockSpec(shape, lambda *_: (0,) * len(shape), pipeline_mode=pl.Buffered(1))


def _pick(n, prefs):
    for p in prefs:
        if n % p == 0:
            return p
    return n


def _sigmoid(x):
    return 1.0 / (1.0 + jnp.exp(-x))


def _log_sigmoid(x):
    return jnp.minimum(x, 0.0) - jnp.log(1.0 + jnp.exp(-jnp.abs(x)))


def _rms(x, gain):
    return x * lax.rsqrt(jnp.mean(x * x, axis=-1, keepdims=True) + NORM_EPS) * gain


def _dot(a, b):
    return jnp.dot(a, b, preferred_element_type=F32)


def _dot_nt(a, b):
    return lax.dot_general(a, b, (((1,), (1,)), ((), ())), preferred_element_type=F32)


def _dot_tn(a, b):
    return lax.dot_general(a, b, (((0,), (0,)), ((), ())), preferred_element_type=F32)


def _split2(x):
    hi = x.astype(BF16)
    lo = (x - hi.astype(F32)).astype(BF16)
    return hi, lo


def _rope_table_kernel(pos_ref, freq_ref, sign_ref, cos_ref, sin_ref):
    ang = pos_ref[0].astype(F32) * freq_ref[...]
    cos_ref[0] = jnp.cos(ang)
    sin_ref[0] = jnp.sin(ang) * sign_ref[...]


def _rope_table(positions, dk):
    b, s = positions.shape
    half = jnp.arange(0, dk, 2, dtype=F32) / dk
    inv_freq = ROPE_THETA ** (-half)
    freq = jnp.concatenate([inv_freq, inv_freq])[None, :]
    sign = jnp.concatenate([-jnp.ones(dk // 2, F32), jnp.ones(dk // 2, F32)])[None, :]
    out = jax.ShapeDtypeStruct((b, s, dk), F32)
    return pl.pallas_call(
        _rope_table_kernel,
        grid=(b,),
        in_specs=[pl.BlockSpec((1, s, 1), lambda i: (i, 0, 0)),
                  pl.BlockSpec((1, dk), lambda i: (0, 0)),
                  pl.BlockSpec((1, dk), lambda i: (0, 0))],
        out_specs=[pl.BlockSpec((1, s, dk), lambda i: (i, 0, 0))] * 2,
        out_shape=[out, out],
        compiler_params=_params(("parallel",)),
        name="rope_table",
    )(positions[:, :, None], freq, sign)


def _rope(t, cos, sin_signed):
    return t * cos + pltpu.roll(t, t.shape[-1] // 2, 1) * sin_signed


def _mix_norm_kernel(x_ref, gain_ref, h_ref):
    h_ref[...] = _rms(x_ref[...], gain_ref[...]).astype(h_ref.dtype)


def _mix_norm(x2, gain, tm):
    t, d = x2.shape
    return pl.pallas_call(
        _mix_norm_kernel,
        grid=(t // tm,),
        in_specs=[pl.BlockSpec((tm, d), lambda i: (i, 0)),
                  pl.BlockSpec((1, d), lambda i: (0, 0))],
        out_specs=pl.BlockSpec((tm, d), lambda i: (i, 0)),
        out_shape=jax.ShapeDtypeStruct((t, d), BF16),
        compiler_params=_params(("parallel",)),
        name="mix_norm",
    )(x2, gain)


def _prefix_states(n_chunks, dk, dv, decay_f, decay_b, kv_ref, st_ref):
    def fwd(n, s):
        st_ref[n, 0:dk, :] = s.astype(BF16)
        return decay_f(n) * s + kv_ref[n, 0:dk, :]

    def bwd(i, s):
        n = n_chunks - 1 - i
        st_ref[n, dk:2 * dk, :] = s.astype(BF16)
        return decay_b(n) * s + kv_ref[n, dk:2 * dk, :]

    zero = jnp.zeros((dk, dv), F32)
    lax.fori_loop(0, n_chunks, fwd, zero)
    lax.fori_loop(0, n_chunks, bwd, zero)


def _chunk_rows(n):
    return pl.ds(pl.multiple_of(n * CHUNK, CHUNK), CHUNK)


def _for_row_tiles(s_len, body):
    tile = _pick(s_len, (512, 256, CHUNK))

    def step(i, carry):
        body(pl.ds(pl.multiple_of(i * tile, tile), tile))
        return carry

    lax.fori_loop(0, s_len // tile, step, 0)


def _retention_kernel(h_ref, w_ref, cos_ref, sin_ref, dec_ref, gain_ref, o_ref,
                      qr_ref, kr_ref, v_ref, g_ref, kv_ref, st_ref):
    c = CHUNK
    s_len, dk, dv = qr_ref.shape[0], qr_ref.shape[1], v_ref.shape[1]
    n_chunks = s_len // c
    scale = dk ** -0.5

    def project(rows):
        h = h_ref[0, rows, :]
        qk = _dot(h, w_ref[0, :, 0:2 * dk])
        cos, sin = cos_ref[0, rows, :], sin_ref[0, rows, :]
        qr_ref[rows, :] = _rope(qk[:, :dk], cos, sin) * scale
        kr_ref[rows, :] = _rope(qk[:, dk:], cos, sin)
        v_ref[rows, :] = _dot(h, w_ref[0, :, 2 * dk:2 * dk + dv]).astype(BF16)
        g_ref[rows, :] = _dot(h, w_ref[0, :, 2 * dk + dv:2 * dk + 2 * dv])

    _for_row_tiles(s_len, project)

    lg_f = _log_sigmoid(dec_ref[0, 0:1, :])
    lg_b = _log_sigmoid(dec_ref[0, 1:2, :])
    lgf_k, lgb_k = lg_f[:, :dk], lg_b[:, :dk]
    pos = lax.broadcasted_iota(jnp.int32, (c, dk), 0).astype(F32)
    wq_f = jnp.exp((pos + 1.0) * lgf_k)
    wk_f = jnp.exp((c - 1.0 - pos) * lgf_k)
    wq_b = jnp.exp((c - pos) * lgb_k)
    wk_b = jnp.exp(pos * lgb_k)
    ri = lax.broadcasted_iota(jnp.int32, (c, c), 0)
    ci = lax.broadcasted_iota(jnp.int32, (c, c), 1)
    lower = ri >= ci
    rel = (ri - ci).astype(F32)
    decay_mask = jnp.where(lower,
                           jnp.exp(jnp.where(lower, rel, 0.0) * lg_f[:, :c]),
                           jnp.exp(jnp.where(lower, 0.0, -rel) * lg_b[:, :c]))
    chunk_decay_f = jnp.exp(c * lg_f)
    chunk_decay_b = jnp.exp(c * lg_b)

    def phase1(n, carry):
        rows = _chunk_rows(n)
        kr = kr_ref[rows, :]
        kw = jnp.concatenate([kr * wk_f, kr * wk_b], axis=1).astype(BF16)
        kv_ref[n] = _dot_tn(kw, v_ref[rows, :])
        return carry

    lax.fori_loop(0, n_chunks, phase1, 0, unroll=CHUNK_UNROLL)
    _prefix_states(n_chunks, dk, dv, lambda n: chunk_decay_f, lambda n: chunk_decay_b, kv_ref, st_ref)

    def phase2(n, carry):
        rows = _chunk_rows(n)
        qr, kr = qr_ref[rows, :], kr_ref[rows, :]
        scores = _dot_nt(qr.astype(BF16), kr.astype(BF16)) * decay_mask
        intra = _dot(scores.astype(BF16), v_ref[rows, :])
        qw = jnp.concatenate([qr * wq_f, qr * wq_b], axis=1).astype(BF16)
        o = intra + _dot(qw, st_ref[n])
        mu = jnp.mean(o, axis=-1, keepdims=True)
        d = o - mu
        var = jnp.mean(d * d, axis=-1, keepdims=True)
        y = d * lax.rsqrt(var + NORM_EPS) * gain_ref[...]
        g = g_ref[rows, :]
        o_ref[0, rows, :] = (y * (g * _sigmoid(g))).astype(o_ref.dtype)
        return carry

    lax.fori_loop(0, n_chunks, phase2, 0, unroll=CHUNK_UNROLL)


def _retention(h3, w_heads, cos, sin, dec, gain):
    b, s, d = h3.shape
    h, dk, dv = RET_HEADS, RET_DK, RET_DV
    n_chunks = s // CHUNK
    return pl.pallas_call(
        _retention_kernel,
        grid=(b, h),
        in_specs=[pl.BlockSpec((1, s, d), lambda i, j: (i, 0, 0)),
                  pl.BlockSpec((1, d, 2 * dk + 2 * dv), lambda i, j: (j, 0, 0)),
                  pl.BlockSpec((1, s, dk), lambda i, j: (i, 0, 0)),
                  pl.BlockSpec((1, s, dk), lambda i, j: (i, 0, 0)),
                  pl.BlockSpec((1, 8, dv), lambda i, j: (j, 0, 0)),
                  pl.BlockSpec((1, dv), lambda i, j: (0, j))],
        out_specs=pl.BlockSpec((1, s, dv), lambda i, j: (i, 0, j)),
        out_shape=jax.ShapeDtypeStruct((b, s, h * dv), BF16),
        scratch_shapes=[pltpu.VMEM((s, dk), F32),
                        pltpu.VMEM((s, dk), F32),
                        pltpu.VMEM((s, dv), BF16),
                        pltpu.VMEM((s, dv), F32),
                        pltpu.VMEM((n_chunks, 2 * dk, dv), F32),
                        pltpu.VMEM((n_chunks, 2 * dk, dv), BF16)],
        compiler_params=_params(("parallel", "arbitrary")),
        name="retention",
    )(h3, w_heads, cos, sin, dec, gain)


GATE_COPY = 2 * GLA_GATE_RANK


def _cumsum_dot(tri2, x):
    hi, lo = _split2(x)
    return _dot(tri2, jnp.concatenate([hi, lo], axis=0))


def _gla_kernel(h_ref, w_ref, gw_ref, gb_ref, gain_ref, o_ref,
                q_ref, k_ref, v_ref, g_ref, la_ref, cumf_ref, cumb_ref, kv_ref, dcol_ref, st_ref):
    c = CHUNK
    s_len, dk, dv = q_ref.shape[0], q_ref.shape[1], v_ref.shape[1]
    n_chunks = s_len // c
    scale = dk ** -0.5
    inv_norm = 1.0 / GLA_GATE_NORMALIZER

    def project(rows):
        h = h_ref[0, rows, :]
        qk = _dot(h, w_ref[0, :, 0:2 * dk])
        q_ref[rows, :] = qk[:, :dk] * scale
        k_ref[rows, :] = qk[:, dk:]
        v_ref[rows, :] = _dot(h, w_ref[0, :, 2 * dk:2 * dk + dv]).astype(BF16)
        g_ref[rows, :] = _dot(h, w_ref[0, :, 2 * dk + dv:2 * dk + 2 * dv])
        ga_hi, ga_lo = _split2(_dot(h, w_ref[0, :, 2 * dk + 2 * dv:]))
        lane = lax.broadcasted_iota(jnp.int32, ga_hi.shape, 1)
        ga = jnp.where(lane < 2 * GATE_COPY, ga_hi, ga_lo)
        la_ref[rows, :] = _log_sigmoid(_dot(ga, gw_ref[0]) + gb_ref[0]) * inv_norm

    _for_row_tiles(s_len, project)

    ri = lax.broadcasted_iota(jnp.int32, (c, c), 0)
    ci = lax.broadcasted_iota(jnp.int32, (c, c), 1)
    lower = ri >= ci
    tri_lower = jnp.where(lower, 1.0, 0.0).astype(BF16)
    tri_upper = jnp.where(ci >= ri, 1.0, 0.0).astype(BF16)
    tri2_lower = jnp.concatenate([tri_lower, tri_lower], axis=1)
    tri2_upper = jnp.concatenate([tri_upper, tri_upper], axis=1)

    def cumulate(n, carry):
        rows = _chunk_rows(n)
        cumf_ref[rows, :] = _cumsum_dot(tri2_lower, la_ref[rows, 0:dk])
        cumb_ref[rows, :] = _cumsum_dot(tri2_upper, la_ref[rows, dk:2 * dk])
        return carry

    lax.fori_loop(0, n_chunks, cumulate, 0, unroll=CHUNK_UNROLL)

    def phase1(n, carry):
        rows = _chunk_rows(n)
        cum_f, cum_b = cumf_ref[rows, :], cumb_ref[rows, :]
        last_f = cum_f[c - 1:c, :]
        last_b = cum_b[0:1, :]
        k = k_ref[rows, :]
        kw = jnp.concatenate([k * jnp.exp(last_f - cum_f), k * jnp.exp(last_b - cum_b)], axis=1).astype(BF16)
        kv_ref[n] = _dot_tn(kw, v_ref[rows, :])
        dcol_ref[n, 0:dk, :] = jnp.broadcast_to(jnp.exp(last_f), (dk, dk)).T
        dcol_ref[n, dk:2 * dk, :] = jnp.broadcast_to(jnp.exp(last_b), (dk, dk)).T
        return carry

    lax.fori_loop(0, n_chunks, phase1, 0, unroll=CHUNK_UNROLL)

    def decay_f(n):
        d = dcol_ref[n, 0:dk, :]
        return jnp.concatenate([d] * (dv // dk), axis=1)

    def decay_b(n):
        d = dcol_ref[n, dk:2 * dk, :]
        return jnp.concatenate([d] * (dv // dk), axis=1)

    _prefix_states(n_chunks, dk, dv, decay_f, decay_b, kv_ref, st_ref)

    def phase2(n, carry):
        rows = _chunk_rows(n)
        cum_f, cum_b = cumf_ref[rows, :], cumb_ref[rows, :]
        ref_f = cum_f[c // 2:c // 2 + 1, :]
        ref_b = cum_b[c // 2 - 1:c // 2, :]
        q = q_ref[rows, :]
        k = k_ref[rows, :]
        s_f = _dot_nt((q * jnp.exp(cum_f - ref_f)).astype(BF16), (k * jnp.exp(ref_f - cum_f)).astype(BF16))
        s_b = _dot_nt((q * jnp.exp(cum_b - ref_b)).astype(BF16), (k * jnp.exp(ref_b - cum_b)).astype(BF16))
        scores = jnp.where(lower, s_f, s_b)
        intra = _dot(scores.astype(BF16), v_ref[rows, :])
        qw = jnp.concatenate([q * jnp.exp(cum_f), q * jnp.exp(cum_b)], axis=1).astype(BF16)
        o = intra + _dot(qw, st_ref[n])
        y = _rms(o, gain_ref[...])
        g = g_ref[rows, :]
        o_ref[0, rows, :] = (y * (g * _sigmoid(g))).astype(o_ref.dtype)
        return carry

    lax.fori_loop(0, n_chunks, phase2, 0, unroll=CHUNK_UNROLL)


def _gla(h3, w_heads, gate_w, gate_b, gain):
    b, s, d = h3.shape
    h, dk, dv = GLA_HEADS, GLA_DK, GLA_DV
    n_chunks = s // CHUNK
    return pl.pallas_call(
        _gla_kernel,
        grid=(b, h),
        in_specs=[pl.BlockSpec((1, s, d), lambda i, j: (i, 0, 0)),
                  pl.BlockSpec((1, d, 2 * dk + 2 * dv + LANES), lambda i, j: (j, 0, 0)),
                  pl.BlockSpec((1, LANES, 2 * dk), lambda i, j: (j, 0, 0)),
                  pl.BlockSpec((1, 1, 2 * dk), lambda i, j: (j, 0, 0)),
                  pl.BlockSpec((1, dv), lambda i, j: (0, j))],
        out_specs=pl.BlockSpec((1, s, dv), lambda i, j: (i, 0, j)),
        out_shape=jax.ShapeDtypeStruct((b, s, h * dv), BF16),
        scratch_shapes=[pltpu.VMEM((s, dk), F32),
                        pltpu.VMEM((s, dk), F32),
                        pltpu.VMEM((s, dv), BF16),
                        pltpu.VMEM((s, dv), F32),
                        pltpu.VMEM((s, 2 * dk), F32),
                        pltpu.VMEM((s, dk), F32),
                        pltpu.VMEM((s, dk), F32),
                        pltpu.VMEM((n_chunks, 2 * dk, dv), F32),
                        pltpu.VMEM((n_chunks, 2 * dk, dk), F32),
                        pltpu.VMEM((n_chunks, 2 * dk, dv), BF16)],
        compiler_params=_params(("parallel", "arbitrary")),
        name="gla",
    )(h3, w_heads, gate_w, gate_b, gain)


def _merge_kernel(n_experts, h_ref, ret_ref, gla_ref, x_ref, wgl_ref, wr_ref, wg_ref, wo_ref, gain_ref,
                  wrh_ref, wrl_ref, x1_ref, h2_ref, aff_ref):
    d = x_ref.shape[1]
    gate_logits = _dot(h_ref[...], wgl_ref[...])
    a = _dot(ret_ref[...], wr_ref[...])
    b = _dot(gla_ref[...], wg_ref[...])
    merged = _sigmoid(gate_logits[:, 0:d]) * a + _sigmoid(gate_logits[:, d:2 * d]) * b
    x1 = x_ref[...] + _dot(merged.astype(BF16), wo_ref[...])
    x1_ref[...] = x1
    h2 = _rms(x1, gain_ref[...])
    h_hi, h_lo = _split2(h2)
    h2_ref[...] = h_hi
    logits = _dot(h_hi, wrh_ref[...]) + _dot(h_hi, wrl_ref[...]) + _dot(h_lo, wrh_ref[...])
    lane = lax.broadcasted_iota(jnp.int32, logits.shape, 1)
    logits = jnp.where(lane < n_experts, logits, -jnp.inf)
    p = jnp.exp(logits - jnp.max(logits, axis=-1, keepdims=True))
    aff = p / jnp.sum(p, axis=-1, keepdims=True)
    aff_ref[0] = aff.T[0:n_experts, :]


def _merge(h, ret, gla, x2, w_gl, w_ret, w_gla, w_out, gain, wr_hi, wr_lo, n_experts, batch, tm):
    t, d = x2.shape
    s = t // batch
    per_b = s // tm
    rows = lambda width: pl.BlockSpec((tm, width), lambda i: (i, 0))
    return pl.pallas_call(
        functools.partial(_merge_kernel, n_experts),
        grid=(t // tm,),
        in_specs=[rows(d), rows(ret.shape[1]), rows(gla.shape[1]), rows(d),
                  _resident(w_gl.shape), _resident(w_ret.shape), _resident(w_gla.shape),
                  _resident(w_out.shape), _resident((1, d)),
                  _resident(wr_hi.shape), _resident(wr_lo.shape)],
        out_specs=[rows(d), rows(d),
                   pl.BlockSpec((1, n_experts, tm), lambda i: (i // per_b, 0, i % per_b))],
        out_shape=[jax.ShapeDtypeStruct((t, d), F32),
                   jax.ShapeDtypeStruct((t, d), BF16),
                   jax.ShapeDtypeStruct((batch, n_experts, s), F32)],
        compiler_params=_params(("parallel",)),
        name="merge",
    )(h, ret, gla, x2, w_gl, w_ret, w_gla, w_out, gain, wr_hi, wr_lo)


def _prefix_count(mask):
    s = mask.shape[1]
    ri = lax.broadcasted_iota(jnp.int32, (LANES, LANES), 0)
    ci = lax.broadcasted_iota(jnp.int32, (LANES, LANES), 1)
    tri = jnp.where(ri <= ci, 1.0, 0.0).astype(BF16)
    off = jnp.zeros((mask.shape[0], 1), F32)
    parts = []
    for j in range(s // LANES):
        p = _dot(mask[:, j * LANES:(j + 1) * LANES].astype(BF16), tri) + off
        parts.append(p)
        off = p[:, LANES - 1:LANES]
    return jnp.concatenate(parts, axis=1)


def _route_kernel(capacity, aff_ref, slot_ref):
    a = aff_ref[0]
    bits = lax.bitcast_convert_type(a, jnp.int32)
    n_e = a.shape[0]
    cap = float(capacity)

    def count(pred):
        return jnp.sum(jnp.where(pred, 1.0, 0.0), axis=1, keepdims=True)

    def search(_, c):
        lo, hi = c
        mid = lo + lax.shift_right_logical(hi - lo, 1)
        ok = count(bits >= mid) >= cap
        return jnp.where(ok, mid, lo), jnp.where(ok, hi, mid)

    lo0 = jnp.zeros((n_e, 1), jnp.int32)
    hi0 = jnp.full((n_e, 1), 0x7F800000, jnp.int32)
    thr_bits, _ = lax.fori_loop(0, 31, search, (lo0, hi0))
    thr0 = jnp.max(jnp.where(bits <= thr_bits, a, -1.0), axis=1, keepdims=True)

    def counts(v):
        return count(a >= v), count(a > v)

    def unsettled(state):
        _, c_ge, c_gt = state
        bad = jnp.where(c_ge < cap, 1.0, jnp.where(c_gt >= cap, 1.0, 0.0))
        return jnp.max(bad, axis=0, keepdims=True)[0, 0] > 0.0

    def step(state):
        v, c_ge, c_gt = state
        below = jnp.max(jnp.where(a < v, a, -1.0), axis=1, keepdims=True)
        above = jnp.min(jnp.where(a > v, a, 2.0), axis=1, keepdims=True)
        v = jnp.where(c_ge < cap, below, jnp.where(c_gt >= cap, above, v))
        return (v,) + counts(v)

    thr, _, n_gt = lax.while_loop(unsettled, step, (thr0,) + counts(thr0))

    gt = a > thr
    eq = a == thr
    need = cap - n_gt
    eq_rank = _prefix_count(jnp.where(eq, 1.0, 0.0))
    sel = jnp.where(gt, 1.0, jnp.where(eq, jnp.where(eq_rank <= need, 1.0, 0.0), 0.0))
    pos = _prefix_count(sel)
    slot_ref[0] = jnp.where(sel > 0.0, pos - 1.0, -1.0)


def _route(aff_t, capacity):
    b, e, s = aff_t.shape
    return pl.pallas_call(
        functools.partial(_route_kernel, capacity),
        grid=(b,),
        in_specs=[pl.BlockSpec((1, e, s), lambda i: (i, 0, 0))],
        out_specs=pl.BlockSpec((1, e, s), lambda i: (i, 0, 0)),
        out_shape=jax.ShapeDtypeStruct((b, e, s), F32),
        compiler_params=_params(("parallel",)),
        name="route",
    )(aff_t)


def _dispatch_kernel(slot_ref, aff_ref, h_ref, xg_ref, gate_ref):
    cap = xg_ref.shape[2]
    slot = slot_ref[0, 0]
    hit = lax.broadcasted_iota(jnp.int32, (cap, slot.shape[1]), 0).astype(F32) == slot
    onehot = jnp.where(hit, 1.0, 0.0).astype(BF16)
    xg_ref[0, 0] = _dot(onehot, h_ref[0]).astype(xg_ref.dtype)
    gate = jnp.sum(jnp.where(hit, aff_ref[0, 0], 0.0), axis=1, keepdims=True)
    gate_ref[0, 0] = jnp.broadcast_to(gate, gate_ref.shape[2:])


def _dispatch(slot, aff_t, h2, capacity):
    b, e, s = slot.shape
    d = h2.shape[-1]
    return pl.pallas_call(
        _dispatch_kernel,
        grid=(b, e),
        in_specs=[pl.BlockSpec((1, 1, 1, s), lambda i, j: (i, j, 0, 0)),
                  pl.BlockSpec((1, 1, 1, s), lambda i, j: (i, j, 0, 0)),
                  pl.BlockSpec((1, s, d), lambda i, j: (i, 0, 0))],
        out_specs=[pl.BlockSpec((1, 1, capacity, d), lambda i, j: (j, i, 0, 0)),
                   pl.BlockSpec((1, 1, capacity, LANES), lambda i, j: (j, i, 0, 0))],
        out_shape=[jax.ShapeDtypeStruct((e, b, capacity, d), BF16),
                   jax.ShapeDtypeStruct((e, b, capacity, LANES), F32)],
        compiler_params=_params(("parallel", "parallel")),
        name="dispatch",
    )(slot.reshape(b, e, 1, s), aff_t.reshape(b, e, 1, s), h2)


def _ffn_kernel(tm, x_ref, gate_ref, wg_ref, wu_ref, wd_ref, y_ref, acc_ref):
    f = pl.program_id(1)
    w_gate = wg_ref[0].astype(BF16)
    w_up = wu_ref[0].astype(BF16)
    w_down = wd_ref[0].astype(BF16)
    m = x_ref.shape[1]
    d = x_ref.shape[2]

    @pl.when(f == 0)
    def _():
        acc_ref[...] = jnp.zeros_like(acc_ref)

    for i in range(m // tm):
        rows = pl.ds(i * tm, tm)
        x = x_ref[0, rows, :]
        g = _dot(x, w_gate)
        u = _dot(x, w_up)
        act = (g * _sigmoid(g) * u).astype(BF16)
        acc_ref[rows, :] += _dot(act, w_down)

    @pl.when(f == pl.num_programs(1) - 1)
    def _():
        gate = jnp.concatenate([gate_ref[0]] * (d // LANES), axis=1)
        y_ref[0] = (acc_ref[...] * gate).astype(y_ref.dtype)


def _ffn(xg, gate, w_gate, w_up, w_down, tf, tm):
    e, m, d = xg.shape
    f = w_gate.shape[2]
    return pl.pallas_call(
        functools.partial(_ffn_kernel, tm),
        grid=(e, f // tf),
        in_specs=[pl.BlockSpec((1, m, d), lambda i, j: (i, 0, 0)),
                  pl.BlockSpec((1, m, LANES), lambda i, j: (i, 0, 0)),
                  pl.BlockSpec((1, d, tf), lambda i, j: (i, 0, j)),
                  pl.BlockSpec((1, d, tf), lambda i, j: (i, 0, j)),
                  pl.BlockSpec((1, tf, d), lambda i, j: (i, j, 0))],
        out_specs=pl.BlockSpec((1, m, d), lambda i, j: (i, 0, 0)),
        out_shape=jax.ShapeDtypeStruct((e, m, d), BF16),
        scratch_shapes=[pltpu.VMEM((m, d), F32)],
        compiler_params=_params(("parallel", "arbitrary")),
        name="ffn",
    )(xg, gate, w_gate, w_up, w_down)


def _combine_kernel(slot_ref, y_ref, x1_ref, gain_ref, o_ref):
    n_e, cap = y_ref.shape[0], y_ref.shape[2]
    tt = x1_ref.shape[1]
    acc = x1_ref[0]
    row = lax.broadcasted_iota(jnp.int32, (cap, tt), 0).astype(F32)
    for e in range(n_e):
        onehot = jnp.where(row == slot_ref[0, e:e + 1, :], 1.0, 0.0).astype(BF16)
        acc = acc + _dot_tn(onehot, y_ref[e, 0])
    o_ref[0] = _rms(acc, gain_ref[...])


def _combine(slot, y, x1, gain, tt):
    b, e, s = slot.shape
    cap, d = y.shape[2], y.shape[3]
    return pl.pallas_call(
        _combine_kernel,
        grid=(b, s // tt),
        in_specs=[pl.BlockSpec((1, e, tt), lambda i, j: (i, 0, j)),
                  pl.BlockSpec((e, 1, cap, d), lambda i, j: (0, i, 0, 0)),
                  pl.BlockSpec((1, tt, d), lambda i, j: (i, j, 0)),
                  pl.BlockSpec((1, d), lambda i, j: (0, 0))],
        out_specs=pl.BlockSpec((1, tt, d), lambda i, j: (i, j, 0)),
        out_shape=jax.ShapeDtypeStruct((b, s, d), F32),
        compiler_params=_params(("parallel", "parallel")),
        name="combine",
    )(slot, y, x1, gain)


def _head_columns(w, starts, widths, heads):
    return jnp.stack([jnp.concatenate([w[:, s0 + h * wd:s0 + (h + 1) * wd] for s0, wd in zip(starts, widths)],
                                      axis=1) for h in range(heads)])


def _layer(x, cos, sin, norm_mix, w_in, ret_decay_fwd, ret_decay_bwd, ret_norm,
           gla_gate_w_fwd, gla_gate_b_fwd, gla_gate_w_bwd, gla_gate_b_bwd, gla_norm,
           w_branch_ret, w_branch_gla, w_out, norm_ffn, w_router, w_gate, w_up, w_down, norm_out):
    b, s, d = x.shape
    t = b * s
    ret_qk, ret_v = RET_HEADS * RET_DK, RET_HEADS * RET_DV
    gla_qk, gla_v = GLA_HEADS * GLA_DK, GLA_HEADS * GLA_DV
    rank = GLA_GATE_RANK
    gla0 = 2 * ret_qk + 2 * ret_v
    ga0 = gla0 + 2 * gla_qk + 2 * gla_v
    assert w_in.shape == (d, ga0 + 2 * rank + 2 * d)
    assert s % CHUNK == 0 and 3 * GATE_COPY <= LANES

    w_ret = _head_columns(w_in, (0, ret_qk, 2 * ret_qk, 2 * ret_qk + ret_v),
                          (RET_DK, RET_DK, RET_DV, RET_DV), RET_HEADS).astype(BF16)
    w_ga = jnp.pad(jnp.tile(w_in[:, ga0:ga0 + 2 * rank], (1, 3)), ((0, 0), (0, LANES - 3 * GATE_COPY)))
    w_gla = _head_columns(w_in, (gla0, gla0 + gla_qk, gla0 + 2 * gla_qk, gla0 + 2 * gla_qk + gla_v),
                          (GLA_DK, GLA_DK, GLA_DV, GLA_DV), GLA_HEADS)
    w_gla = jnp.concatenate([w_gla, jnp.broadcast_to(w_ga, (GLA_HEADS,) + w_ga.shape)], axis=2).astype(BF16)
    w_gl = w_in[:, ga0 + 2 * rank:].astype(BF16)

    gw = jnp.zeros((GLA_HEADS, GATE_COPY, 2 * GLA_DK), F32)
    gw = gw.at[:, :rank, :GLA_DK].set(gla_gate_w_fwd.reshape(rank, GLA_HEADS, GLA_DK).transpose(1, 0, 2))
    gw = gw.at[:, rank:, GLA_DK:].set(gla_gate_w_bwd.reshape(rank, GLA_HEADS, GLA_DK).transpose(1, 0, 2))
    gw_hi = gw.astype(BF16)
    gw_lo = (gw - gw_hi.astype(F32)).astype(BF16)
    gate_w = jnp.concatenate([gw_hi, gw_lo, gw_hi, jnp.zeros_like(gw_hi)], axis=1)
    gate_b = jnp.concatenate([gla_gate_b_fwd.reshape(GLA_HEADS, 1, GLA_DK),
                              gla_gate_b_bwd.reshape(GLA_HEADS, 1, GLA_DK)], axis=2)

    dec = jnp.stack([ret_decay_fwd, ret_decay_bwd], axis=1)[:, :, None]
    dec = jnp.pad(jnp.broadcast_to(dec, (RET_HEADS, 2, RET_DV)), ((0, 0), (0, 6), (0, 0)))

    x2 = x.reshape(t, d)
    h = _mix_norm(x2, norm_mix[None, :], _pick(t, (1024, 512, 256, 128)))
    h3 = h.reshape(b, s, d)
    ret = _retention(h3, w_ret, cos, sin, dec, ret_norm[None, :])
    gla = _gla(h3, w_gla, gate_w, gate_b, gla_norm[None, :])

    n_e = w_router.shape[1]
    w_r = jnp.pad(w_router, ((0, 0), (0, LANES - n_e)))
    wr_hi = w_r.astype(BF16)
    wr_lo = (w_r - wr_hi.astype(F32)).astype(BF16)
    x1, h2, aff_t = _merge(h, ret.reshape(t, ret_v), gla.reshape(t, gla_v), x2, w_gl,
                           w_branch_ret.astype(BF16), w_branch_gla.astype(BF16), w_out.astype(BF16),
                           norm_ffn[None, :], wr_hi, wr_lo, n_e, b, _pick(s, (512, 256, 128)))

    capacity = EC_CAPACITY_FACTOR * s // n_e
    slot = _route(aff_t, capacity)
    xg, gate = _dispatch(slot, aff_t, h2.reshape(b, s, d), capacity)
    f = w_gate.shape[2]
    y = _ffn(xg.reshape(n_e, b * capacity, d), gate.reshape(n_e, b * capacity, LANES),
             w_gate, w_up, w_down, _pick(f, (256, 128)), _pick(b * capacity, (512, 256, 128)))
    return _combine(slot, y.reshape(n_e, b, capacity, d), x1.reshape(b, s, d), norm_out,
                    _pick(s, (512, 256, 128)))


def kernel(x, positions, norm_mix, w_in, ret_decay_fwd, ret_decay_bwd, ret_norm, gla_gate_w_fwd,
           gla_gate_b_fwd, gla_gate_w_bwd, gla_gate_b_bwd, gla_norm, w_branch_ret, w_branch_gla,
           w_out, norm_ffn, w_router, w_gate, w_up, w_down, norm_final):
    depth = norm_mix.shape[0]
    assert depth == 1, "the final RMSNorm is fused into the last layer's combine stage"
    cos, sin = _rope_table(positions, RET_DK)
    return _layer(x, cos, sin, norm_mix[0], w_in[0], ret_decay_fwd[0], ret_decay_bwd[0], ret_norm[0],
                  gla_gate_w_fwd[0], gla_gate_b_fwd[0], gla_gate_w_bwd[0], gla_gate_b_bwd[0], gla_norm[0],
                  w_branch_ret[0], w_branch_gla[0], w_out[0], norm_ffn[0], w_router[0],
                  w_gate[0], w_up[0], w_down[0], norm_final[None, :])
```

```python
import functools

import jax
import jax.numpy as jnp
from jax import lax
from jax.experimental import pallas as pl
from jax.experimental.pallas import tpu as pltpu

F32 = jnp.float32
BF16 = jnp.bfloat16

RET_HEADS = 4
RET_DK = 128
RET_DV = 256
GLA_HEADS = 4
GLA_DK = 128
GLA_DV = 256
GLA_GATE_RANK = 16
GLA_GATE_NORMALIZER = 16.0
CHUNK = 128
EC_CAPACITY_FACTOR = 2
ROPE_THETA = 10000.0
NORM_EPS = 1e-6
LOG2_E = 1.4426950408889634

CHUNK_UNROLL = 8
LANES = 128
VMEM_LIMIT = 56 << 20


def _params(sem, vmem=VMEM_LIMIT):
    return pltpu.CompilerParams(dimension_semantics=sem, vmem_limit_bytes=vmem)


def _resident(shape):
    return pl.BlockSpec(shape, lambda *_: (0,) * len(shape), pipeline_mode=pl.Buffered(1))


def _pick(n, prefs):
    for p in prefs:
        if n % p == 0:
            return p
    return n


def _sigmoid(x):
    return 1.0 / (1.0 + jnp.exp(-x))


def _silu(x):
    half = 0.5 * x
    return half + half * jnp.tanh(half)


def _log_sigmoid(x):
    return jnp.minimum(x, 0.0) - jnp.log(1.0 + jnp.exp(-jnp.abs(x)))


def _rms(x, gain):
    return x * lax.rsqrt(jnp.mean(x * x, axis=-1, keepdims=True) + NORM_EPS) * gain


def _dot(a, b):
    return jnp.dot(a, b, preferred_element_type=F32)


def _dot_nt(a, b):
    return lax.dot_general(a, b, (((1,), (1,)), ((), ())), preferred_element_type=F32)


def _dot_tn(a, b):
    return lax.dot_general(a, b, (((0,), (0,)), ((), ())), preferred_element_type=F32)


def _split2(x):
    hi = x.astype(BF16)
    lo = (x - hi.astype(F32)).astype(BF16)
    return hi, lo


def _rope_table_kernel(pos_ref, freq_ref, sign_ref, cos_ref, sin_ref):
    ang = pos_ref[0].astype(F32) * freq_ref[...]
    cos_ref[0] = jnp.cos(ang)
    sin_ref[0] = jnp.sin(ang) * sign_ref[...]


def _rope_table(positions, dk):
    b, s = positions.shape
    half = jnp.arange(0, dk, 2, dtype=F32) / dk
    inv_freq = ROPE_THETA ** (-half)
    freq = jnp.concatenate([inv_freq, inv_freq])[None, :]
    sign = jnp.concatenate([-jnp.ones(dk // 2, F32), jnp.ones(dk // 2, F32)])[None, :]
    out = jax.ShapeDtypeStruct((b, s, dk), F32)
    return pl.pallas_call(
        _rope_table_kernel,
        grid=(b,),
        in_specs=[pl.BlockSpec((1, s, 1), lambda i: (i, 0, 0)),
                  pl.BlockSpec((1, dk), lambda i: (0, 0)),
                  pl.BlockSpec((1, dk), lambda i: (0, 0))],
        out_specs=[pl.BlockSpec((1, s, dk), lambda i: (i, 0, 0))] * 2,
        out_shape=[out, out],
        compiler_params=_params(("parallel",)),
        name="rope_table",
    )(positions[:, :, None], freq, sign)


def _rope(t, cos, sin_signed):
    return t * cos + pltpu.roll(t, t.shape[-1] // 2, 1) * sin_signed


def _mix_norm_kernel(x_ref, gain_ref, h_ref):
    h_ref[...] = _rms(x_ref[...], gain_ref[...]).astype(h_ref.dtype)


def _mix_norm(x2, gain, tm):
    t, d = x2.shape
    return pl.pallas_call(
        _mix_norm_kernel,
        grid=(t // tm,),
        in_specs=[pl.BlockSpec((tm, d), lambda i: (i, 0)),
                  pl.BlockSpec((1, d), lambda i: (0, 0))],
        out_specs=pl.BlockSpec((tm, d), lambda i: (i, 0)),
        out_shape=jax.ShapeDtypeStruct((t, d), BF16),
        compiler_params=_params(("parallel",)),
        name="mix_norm",
    )(x2, gain)


def _prefix_states(n_chunks, dk, dv, decay_f, decay_b, kv_ref, st_ref):
    def fwd(n, s):
        st_ref[n, 0:dk, :] = s.astype(BF16)
        return decay_f(n) * s + kv_ref[n, 0:dk, :]

    def bwd(i, s):
        n = n_chunks - 1 - i
        st_ref[n, dk:2 * dk, :] = s.astype(BF16)
        return decay_b(n) * s + kv_ref[n, dk:2 * dk, :]

    zero = jnp.zeros((dk, dv), F32)
    lax.fori_loop(0, n_chunks, fwd, zero)
    lax.fori_loop(0, n_chunks, bwd, zero)


def _chunk_rows(n):
    return pl.ds(pl.multiple_of(n * CHUNK, CHUNK), CHUNK)


def _for_row_tiles(s_len, body):
    tile = _pick(s_len, (512, 256, CHUNK))

    def step(i, carry):
        body(pl.ds(pl.multiple_of(i * tile, tile), tile))
        return carry

    lax.fori_loop(0, s_len // tile, step, 0)


def _retention_kernel(h_ref, w_ref, cos_ref, sin_ref, dec_ref, gain_ref, o_ref,
                      qr_ref, kr_ref, v_ref, g_ref, kv_ref, st_ref):
    c = CHUNK
    s_len, dk, dv = qr_ref.shape[0], qr_ref.shape[1], v_ref.shape[1]
    n_chunks = s_len // c
    scale = dk ** -0.5

    def project(rows):
        h = h_ref[0, rows, :]
        qk = _dot(h, w_ref[0, :, 0:2 * dk])
        cos, sin = cos_ref[0, rows, :], sin_ref[0, rows, :]
        qr_ref[rows, :] = _rope(qk[:, :dk], cos, sin) * scale
        kr_ref[rows, :] = _rope(qk[:, dk:], cos, sin)
        v_ref[rows, :] = _dot(h, w_ref[0, :, 2 * dk:2 * dk + dv]).astype(BF16)
        g_ref[rows, :] = _dot(h, w_ref[0, :, 2 * dk + dv:2 * dk + 2 * dv])

    _for_row_tiles(s_len, project)

    lg_f = _log_sigmoid(dec_ref[0, 0:1, :])
    lg_b = _log_sigmoid(dec_ref[0, 1:2, :])
    lgf_k, lgb_k = lg_f[:, :dk], lg_b[:, :dk]
    pos = lax.broadcasted_iota(jnp.int32, (c, dk), 0).astype(F32)
    wq_f = jnp.exp((pos + 1.0) * lgf_k)
    wk_f = jnp.exp((c - 1.0 - pos) * lgf_k)
    wq_b = jnp.exp((c - pos) * lgb_k)
    wk_b = jnp.exp(pos * lgb_k)
    ri = lax.broadcasted_iota(jnp.int32, (c, c), 0)
    ci = lax.broadcasted_iota(jnp.int32, (c, c), 1)
    lower = ri >= ci
    rel = (ri - ci).astype(F32)
    decay_mask = jnp.where(lower,
                           jnp.exp(jnp.where(lower, rel, 0.0) * lg_f[:, :c]),
                           jnp.exp(jnp.where(lower, 0.0, -rel) * lg_b[:, :c]))
    chunk_decay_f = jnp.exp(c * lg_f)
    chunk_decay_b = jnp.exp(c * lg_b)

    def phase1(n, carry):
        rows = _chunk_rows(n)
        kr = kr_ref[rows, :]
        kw = jnp.concatenate([kr * wk_f, kr * wk_b], axis=1).astype(BF16)
        kv_ref[n] = _dot_tn(kw, v_ref[rows, :])
        return carry

    lax.fori_loop(0, n_chunks, phase1, 0, unroll=CHUNK_UNROLL)
    _prefix_states(n_chunks, dk, dv, lambda n: chunk_decay_f, lambda n: chunk_decay_b, kv_ref, st_ref)

    def phase2(n, carry):
        rows = _chunk_rows(n)
        qr, kr = qr_ref[rows, :], kr_ref[rows, :]
        scores = _dot_nt(qr.astype(BF16), kr.astype(BF16)) * decay_mask
        intra = _dot(scores.astype(BF16), v_ref[rows, :])
        qw = jnp.concatenate([qr * wq_f, qr * wq_b], axis=1).astype(BF16)
        o = intra + _dot(qw, st_ref[n])
        mu = jnp.mean(o, axis=-1, keepdims=True)
        d = o - mu
        var = jnp.mean(d * d, axis=-1, keepdims=True)
        y = d * lax.rsqrt(var + NORM_EPS) * gain_ref[...]
        o_ref[0, rows, :] = (y * _silu(g_ref[rows, :])).astype(o_ref.dtype)
        return carry

    lax.fori_loop(0, n_chunks, phase2, 0, unroll=CHUNK_UNROLL)


def _retention(h3, w_heads, cos, sin, dec, gain):
    b, s, d = h3.shape
    h, dk, dv = RET_HEADS, RET_DK, RET_DV
    n_chunks = s // CHUNK
    return pl.pallas_call(
        _retention_kernel,
        grid=(b, h),
        in_specs=[pl.BlockSpec((1, s, d), lambda i, j: (i, 0, 0)),
                  pl.BlockSpec((1, d, 2 * dk + 2 * dv), lambda i, j: (j, 0, 0)),
                  pl.BlockSpec((1, s, dk), lambda i, j: (i, 0, 0)),
                  pl.BlockSpec((1, s, dk), lambda i, j: (i, 0, 0)),
                  pl.BlockSpec((1, 8, dv), lambda i, j: (j, 0, 0)),
                  pl.BlockSpec((1, dv), lambda i, j: (0, j))],
        out_specs=pl.BlockSpec((1, s, dv), lambda i, j: (i, 0, j)),
        out_shape=jax.ShapeDtypeStruct((b, s, h * dv), BF16),
        scratch_shapes=[pltpu.VMEM((s, dk), F32),
                        pltpu.VMEM((s, dk), F32),
                        pltpu.VMEM((s, dv), BF16),
                        pltpu.VMEM((s, dv), F32),
                        pltpu.VMEM((n_chunks, 2 * dk, dv), F32),
                        pltpu.VMEM((n_chunks, 2 * dk, dv), BF16)],
        compiler_params=_params(("parallel", "arbitrary")),
        name="retention",
    )(h3, w_heads, cos, sin, dec, gain)


GATE_COPY = 2 * GLA_GATE_RANK


def _cumsum_dot(tri2, x):
    hi, lo = _split2(x)
    return _dot(tri2, jnp.concatenate([hi, lo], axis=0))


def _gla_kernel(h_ref, w_ref, gw_ref, gb_ref, gain_ref, o_ref,
                q_ref, k_ref, v_ref, g_ref, la_ref, cumf_ref, cumb_ref, kv_ref, st_ref):
    c = CHUNK
    s_len, dk, dv = q_ref.shape[0], q_ref.shape[1], v_ref.shape[1]
    n_chunks = s_len // c
    scale = dk ** -0.5
    inv_norm = LOG2_E / GLA_GATE_NORMALIZER

    def project(rows):
        h = h_ref[0, rows, :]
        qk = _dot(h, w_ref[0, :, 0:2 * dk])
        q_ref[rows, :] = qk[:, :dk] * scale
        k_ref[rows, :] = qk[:, dk:]
        v_ref[rows, :] = _dot(h, w_ref[0, :, 2 * dk:2 * dk + dv]).astype(BF16)
        g_ref[rows, :] = _dot(h, w_ref[0, :, 2 * dk + dv:2 * dk + 2 * dv])
        ga_hi, ga_lo = _split2(_dot(h, w_ref[0, :, 2 * dk + 2 * dv:]))
        lane = lax.broadcasted_iota(jnp.int32, ga_hi.shape, 1)
        ga = jnp.where(lane < 2 * GATE_COPY, ga_hi, ga_lo)
        la_ref[rows, :] = _log_sigmoid(_dot(ga, gw_ref[0]) + gb_ref[0]) * inv_norm

    _for_row_tiles(s_len, project)

    ri = lax.broadcasted_iota(jnp.int32, (c, c), 0)
    ci = lax.broadcasted_iota(jnp.int32, (c, c), 1)
    lower = ri >= ci
    tri_lower = jnp.where(lower, 1.0, 0.0).astype(BF16)
    tri_upper = jnp.where(ci >= ri, 1.0, 0.0).astype(BF16)
    tri2_lower = jnp.concatenate([tri_lower, tri_lower], axis=1)
    tri2_upper = jnp.concatenate([tri_upper, tri_upper], axis=1)

    def cumulate(n, carry):
        rows = _chunk_rows(n)
        cumf_ref[rows, :] = _cumsum_dot(tri2_lower, la_ref[rows, 0:dk])
        cumb_ref[rows, :] = _cumsum_dot(tri2_upper, la_ref[rows, dk:2 * dk])
        return carry

    lax.fori_loop(0, n_chunks, cumulate, 0, unroll=CHUNK_UNROLL)

    def phase1(n, carry):
        rows = _chunk_rows(n)
        cum_f, cum_b = cumf_ref[rows, :], cumb_ref[rows, :]
        last_f = cum_f[c - 1:c, :]
        last_b = cum_b[0:1, :]
        k = k_ref[rows, :]
        kw = jnp.concatenate([k * jnp.exp2(last_f - cum_f), k * jnp.exp2(last_b - cum_b)], axis=1).astype(BF16)
        kv_ref[n] = _dot_tn(v_ref[rows, :], kw)
        return carry

    lax.fori_loop(0, n_chunks, phase1, 0, unroll=CHUNK_UNROLL)

    def fwd(n, s):
        st_ref[n, :, 0:dk] = s.astype(BF16)
        last = cumf_ref[pl.ds(pl.multiple_of(n * c, c) + (c - 1), 1), :]
        return jnp.exp2(last) * s + kv_ref[n, :, 0:dk]

    def bwd(i, s):
        n = n_chunks - 1 - i
        st_ref[n, :, dk:2 * dk] = s.astype(BF16)
        last = cumb_ref[pl.ds(pl.multiple_of(n * c, c), 1), :]
        return jnp.exp2(last) * s + kv_ref[n, :, dk:2 * dk]

    zero = jnp.zeros((dv, dk), F32)
    lax.fori_loop(0, n_chunks, fwd, zero)
    lax.fori_loop(0, n_chunks, bwd, zero)

    def phase2(n, carry):
        rows = _chunk_rows(n)
        cum_f, cum_b = cumf_ref[rows, :], cumb_ref[rows, :]
        ref_f = cum_f[c // 2:c // 2 + 1, :]
        ref_b = cum_b[c // 2 - 1:c // 2, :]
        q = q_ref[rows, :]
        k = k_ref[rows, :]
        q_f = q * jnp.exp2(cum_f - ref_f)
        q_b = q * jnp.exp2(cum_b - ref_b)
        s_f = _dot_nt(q_f.astype(BF16), (k * jnp.exp2(ref_f - cum_f)).astype(BF16))
        s_b = _dot_nt(q_b.astype(BF16), (k * jnp.exp2(ref_b - cum_b)).astype(BF16))
        scores = jnp.where(lower, s_f, s_b)
        intra = _dot(scores.astype(BF16), v_ref[rows, :])
        qw = jnp.concatenate([q_f * jnp.exp2(ref_f), q_b * jnp.exp2(ref_b)], axis=1).astype(BF16)
        o = intra + _dot_nt(qw, st_ref[n])
        y = _rms(o, gain_ref[...])
        o_ref[0, rows, :] = (y * _silu(g_ref[rows, :])).astype(o_ref.dtype)
        return carry

    lax.fori_loop(0, n_chunks, phase2, 0, unroll=CHUNK_UNROLL)


def _gla(h3, w_heads, gate_w, gate_b, gain):
    b, s, d = h3.shape
    h, dk, dv = GLA_HEADS, GLA_DK, GLA_DV
    n_chunks = s // CHUNK
    return pl.pallas_call(
        _gla_kernel,
        grid=(b, h),
        in_specs=[pl.BlockSpec((1, s, d), lambda i, j: (i, 0, 0)),
                  pl.BlockSpec((1, d, 2 * dk + 2 * dv + LANES), lambda i, j: (j, 0, 0)),
                  pl.BlockSpec((1, LANES, 2 * dk), lambda i, j: (j, 0, 0)),
                  pl.BlockSpec((1, 1, 2 * dk), lambda i, j: (j, 0, 0)),
                  pl.BlockSpec((1, dv), lambda i, j: (0, j))],
        out_specs=pl.BlockSpec((1, s, dv), lambda i, j: (i, 0, j)),
        out_shape=jax.ShapeDtypeStruct((b, s, h * dv), BF16),
        scratch_shapes=[pltpu.VMEM((s, dk), F32),
                        pltpu.VMEM((s, dk), F32),
                        pltpu.VMEM((s, dv), BF16),
                        pltpu.VMEM((s, dv), F32),
                        pltpu.VMEM((s, 2 * dk), F32),
                        pltpu.VMEM((s, dk), F32),
                        pltpu.VMEM((s, dk), F32),
                        pltpu.VMEM((n_chunks, dv, 2 * dk), F32),
                        pltpu.VMEM((n_chunks, dv, 2 * dk), BF16)],
        compiler_params=_params(("parallel", "arbitrary")),
        name="gla",
    )(h3, w_heads, gate_w, gate_b, gain)


MERGE_COLS = 256


def _merge_kernel(n_experts, h_ref, ret_ref, gla_ref, x_ref, wgl_ref, wr_ref, wg_ref, wo_ref, gain_ref,
                  wr2_ref, x1_ref, h2_ref, aff_ref):
    d = x_ref.shape[1]
    h, ret, gla = h_ref[...], ret_ref[...], gla_ref[...]
    blocks = []
    for j in range(0, d, MERGE_COLS):
        cols = slice(j, j + MERGE_COLS)
        cols_gla = slice(d + j, d + j + MERGE_COLS)
        m = (_sigmoid(_dot(h, wgl_ref[:, cols])) * _dot(ret, wr_ref[:, cols])
             + _sigmoid(_dot(h, wgl_ref[:, cols_gla])) * _dot(gla, wg_ref[:, cols]))
        blocks.append(m.astype(BF16))
    x1 = x_ref[...] + _dot(jnp.concatenate(blocks, axis=1), wo_ref[...])
    x1_ref[...] = x1
    h2 = _rms(x1, gain_ref[...]).astype(BF16)
    h2_ref[...] = h2
    logits2 = _dot(h2, wr2_ref[...])
    logits = logits2[:, :LANES] + logits2[:, LANES:]
    lane = lax.broadcasted_iota(jnp.int32, logits.shape, 1)
    logits = jnp.where(lane < n_experts, logits, -jnp.inf)
    p = jnp.exp(logits - jnp.max(logits, axis=-1, keepdims=True))
    aff = p / jnp.sum(p, axis=-1, keepdims=True)
    aff_ref[0] = aff.T[0:n_experts, :]


def _merge(h, ret, gla, x2, w_gl, w_ret, w_gla, w_out, gain, w_router2, n_experts, batch, tm):
    t, d = x2.shape
    s = t // batch
    per_b = s // tm
    rows = lambda width: pl.BlockSpec((tm, width), lambda i: (i, 0))
    return pl.pallas_call(
        functools.partial(_merge_kernel, n_experts),
        grid=(t // tm,),
        in_specs=[rows(d), rows(ret.shape[1]), rows(gla.shape[1]), rows(d),
                  _resident(w_gl.shape), _resident(w_ret.shape), _resident(w_gla.shape),
                  _resident(w_out.shape), _resident((1, d)),
                  _resident(w_router2.shape)],
        out_specs=[rows(d), rows(d),
                   pl.BlockSpec((1, n_experts, tm), lambda i: (i // per_b, 0, i % per_b))],
        out_shape=[jax.ShapeDtypeStruct((t, d), F32),
                   jax.ShapeDtypeStruct((t, d), BF16),
                   jax.ShapeDtypeStruct((batch, n_experts, s), F32)],
        compiler_params=_params(("parallel",)),
        name="merge",
    )(h, ret, gla, x2, w_gl, w_ret, w_gla, w_out, gain, w_router2)


def _prefix_count(mask):
    s = mask.shape[1]
    ri = lax.broadcasted_iota(jnp.int32, (LANES, LANES), 0)
    ci = lax.broadcasted_iota(jnp.int32, (LANES, LANES), 1)
    tri = jnp.where(ri <= ci, 1.0, 0.0).astype(BF16)
    off = jnp.zeros((mask.shape[0], 1), F32)
    parts = []
    for j in range(s // LANES):
        p = _dot(mask[:, j * LANES:(j + 1) * LANES].astype(BF16), tri) + off
        parts.append(p)
        off = p[:, LANES - 1:LANES]
    return jnp.concatenate(parts, axis=1)


def _route_kernel(capacity, aff_ref, slot_ref):
    a = aff_ref[0]
    bits = lax.bitcast_convert_type(a, jnp.int32)
    n_e = a.shape[0]
    cap = float(capacity)

    def count(pred):
        return jnp.sum(jnp.where(pred, 1.0, 0.0), axis=1, keepdims=True)

    def search(_, c):
        lo, hi = c
        mid = lo + lax.shift_right_logical(hi - lo, 1)
        ok = count(bits >= mid) >= cap
        return jnp.where(ok, mid, lo), jnp.where(ok, hi, mid)

    lo0 = jnp.zeros((n_e, 1), jnp.int32)
    hi0 = jnp.full((n_e, 1), 0x7F800000, jnp.int32)
    thr_bits, _ = lax.fori_loop(0, 31, search, (lo0, hi0))
    thr0 = jnp.max(jnp.where(bits <= thr_bits, a, -1.0), axis=1, keepdims=True)

    def counts(v):
        return count(a >= v), count(a > v)

    def unsettled(state):
        _, c_ge, c_gt = state
        bad = jnp.where(c_ge < cap, 1.0, jnp.where(c_gt >= cap, 1.0, 0.0))
        return jnp.max(bad, axis=0, keepdims=True)[0, 0] > 0.0

    def step(state):
        v, c_ge, c_gt = state
        below = jnp.max(jnp.where(a < v, a, -1.0), axis=1, keepdims=True)
        above = jnp.min(jnp.where(a > v, a, 2.0), axis=1, keepdims=True)
        v = jnp.where(c_ge < cap, below, jnp.where(c_gt >= cap, above, v))
        return (v,) + counts(v)

    thr, _, n_gt = lax.while_loop(unsettled, step, (thr0,) + counts(thr0))

    gt = a > thr
    eq = a == thr
    need = cap - n_gt
    eq_rank = _prefix_count(jnp.where(eq, 1.0, 0.0))
    sel = jnp.where(gt, 1.0, jnp.where(eq, jnp.where(eq_rank <= need, 1.0, 0.0), 0.0))
    pos = _prefix_count(sel)
    slot_ref[0] = jnp.where(sel > 0.0, pos - 1.0, -1.0)


def _route(aff_t, capacity):
    b, e, s = aff_t.shape
    return pl.pallas_call(
        functools.partial(_route_kernel, capacity),
        grid=(b,),
        in_specs=[pl.BlockSpec((1, e, s), lambda i: (i, 0, 0))],
        out_specs=pl.BlockSpec((1, e, s), lambda i: (i, 0, 0)),
        out_shape=jax.ShapeDtypeStruct((b, e, s), F32),
        compiler_params=_params(("parallel",)),
        name="route",
    )(aff_t)


def _dispatch_kernel(slot_ref, aff_ref, h_ref, xg_ref, gate_ref):
    cap = xg_ref.shape[2]
    slot = slot_ref[0, 0]
    hit = lax.broadcasted_iota(jnp.int32, (cap, slot.shape[1]), 0).astype(F32) == slot
    onehot = jnp.where(hit, 1.0, 0.0).astype(BF16)
    xg_ref[0, 0] = _dot(onehot, h_ref[0]).astype(xg_ref.dtype)
    gate = jnp.sum(jnp.where(hit, aff_ref[0, 0], 0.0), axis=1, keepdims=True)
    gate_ref[0, 0] = jnp.broadcast_to(gate, gate_ref.shape[2:])


def _dispatch(slot, aff_t, h2, capacity):
    b, e, s = slot.shape
    d = h2.shape[-1]
    return pl.pallas_call(
        _dispatch_kernel,
        grid=(b, e),
        in_specs=[pl.BlockSpec((1, 1, 1, s), lambda i, j: (i, j, 0, 0)),
                  pl.BlockSpec((1, 1, 1, s), lambda i, j: (i, j, 0, 0)),
                  pl.BlockSpec((1, s, d), lambda i, j: (i, 0, 0))],
        out_specs=[pl.BlockSpec((1, 1, capacity, d), lambda i, j: (j, i, 0, 0)),
                   pl.BlockSpec((1, 1, capacity, LANES), lambda i, j: (j, i, 0, 0))],
        out_shape=[jax.ShapeDtypeStruct((e, b, capacity, d), BF16),
                   jax.ShapeDtypeStruct((e, b, capacity, LANES), F32)],
        compiler_params=_params(("parallel", "parallel")),
        name="dispatch",
    )(slot.reshape(b, e, 1, s), aff_t.reshape(b, e, 1, s), h2)


def _ffn_kernel(tm, x_ref, gate_ref, wg_ref, wu_ref, wd_ref, y_ref, acc_ref):
    f = pl.program_id(1)
    w_gate = wg_ref[0].astype(BF16)
    w_up = wu_ref[0].astype(BF16)
    w_down = wd_ref[0].astype(BF16)
    m = x_ref.shape[1]
    d = x_ref.shape[2]

    @pl.when(f == 0)
    def _():
        acc_ref[...] = jnp.zeros_like(acc_ref)

    for i in range(m // tm):
        rows = pl.ds(i * tm, tm)
        x = x_ref[0, rows, :]
        g = _dot(x, w_gate)
        u = _dot(x, w_up)
        act = (_silu(g) * u).astype(BF16)
        acc_ref[rows, :] += _dot(act, w_down)

    @pl.when(f == pl.num_programs(1) - 1)
    def _():
        gate = jnp.concatenate([gate_ref[0]] * (d // LANES), axis=1)
        y_ref[0] = (acc_ref[...] * gate).astype(y_ref.dtype)


def _ffn(xg, gate, w_gate, w_up, w_down, tf, tm):
    e, m, d = xg.shape
    f = w_gate.shape[2]
    return pl.pallas_call(
        functools.partial(_ffn_kernel, tm),
        grid=(e, f // tf),
        in_specs=[pl.BlockSpec((1, m, d), lambda i, j: (i, 0, 0)),
                  pl.BlockSpec((1, m, LANES), lambda i, j: (i, 0, 0)),
                  pl.BlockSpec((1, d, tf), lambda i, j: (i, 0, j)),
                  pl.BlockSpec((1, d, tf), lambda i, j: (i, 0, j)),
                  pl.BlockSpec((1, tf, d), lambda i, j: (i, j, 0))],
        out_specs=pl.BlockSpec((1, m, d), lambda i, j: (i, 0, 0)),
        out_shape=jax.ShapeDtypeStruct((e, m, d), BF16),
        scratch_shapes=[pltpu.VMEM((m, d), F32)],
        compiler_params=_params(("parallel", "arbitrary")),
        name="ffn",
    )(xg, gate, w_gate, w_up, w_down)


def _combine_kernel(slot_ref, y_ref, x1_ref, gain_ref, o_ref):
    n_e, cap = y_ref.shape[0], y_ref.shape[2]
    tt = x1_ref.shape[1]
    acc = x1_ref[0]
    row = lax.broadcasted_iota(jnp.int32, (cap, tt), 0).astype(F32)
    for e in range(n_e):
        onehot = jnp.where(row == slot_ref[0, e:e + 1, :], 1.0, 0.0).astype(BF16)
        acc = acc + _dot_tn(onehot, y_ref[e, 0])
    o_ref[0] = _rms(acc, gain_ref[...])


def _combine(slot, y, x1, gain, tt):
    b, e, s = slot.shape
    cap, d = y.shape[2], y.shape[3]
    return pl.pallas_call(
        _combine_kernel,
        grid=(b, s // tt),
        in_specs=[pl.BlockSpec((1, e, tt), lambda i, j: (i, 0, j)),
                  pl.BlockSpec((e, 1, cap, d), lambda i, j: (0, i, 0, 0)),
                  pl.BlockSpec((1, tt, d), lambda i, j: (i, j, 0)),
                  pl.BlockSpec((1, d), lambda i, j: (0, 0))],
        out_specs=pl.BlockSpec((1, tt, d), lambda i, j: (i, j, 0)),
        out_shape=jax.ShapeDtypeStruct((b, s, d), F32),
        compiler_params=_params(("parallel", "parallel")),
        name="combine",
    )(slot, y, x1, gain)


def _head_columns(w, starts, widths, heads):
    return jnp.stack([jnp.concatenate([w[:, s0 + h * wd:s0 + (h + 1) * wd] for s0, wd in zip(starts, widths)],
                                      axis=1) for h in range(heads)])


def _layer(x, cos, sin, norm_mix, w_in, ret_decay_fwd, ret_decay_bwd, ret_norm,
           gla_gate_w_fwd, gla_gate_b_fwd, gla_gate_w_bwd, gla_gate_b_bwd, gla_norm,
           w_branch_ret, w_branch_gla, w_out, norm_ffn, w_router, w_gate, w_up, w_down, norm_out):
    b, s, d = x.shape
    t = b * s
    ret_qk, ret_v = RET_HEADS * RET_DK, RET_HEADS * RET_DV
    gla_qk, gla_v = GLA_HEADS * GLA_DK, GLA_HEADS * GLA_DV
    rank = GLA_GATE_RANK
    gla0 = 2 * ret_qk + 2 * ret_v
    ga0 = gla0 + 2 * gla_qk + 2 * gla_v
    assert w_in.shape == (d, ga0 + 2 * rank + 2 * d)
    assert s % CHUNK == 0 and 3 * GATE_COPY <= LANES

    w_ret = _head_columns(w_in, (0, ret_qk, 2 * ret_qk, 2 * ret_qk + ret_v),
                          (RET_DK, RET_DK, RET_DV, RET_DV), RET_HEADS).astype(BF16)
    w_ga = jnp.pad(jnp.tile(w_in[:, ga0:ga0 + 2 * rank], (1, 3)), ((0, 0), (0, LANES - 3 * GATE_COPY)))
    w_gla = _head_columns(w_in, (gla0, gla0 + gla_qk, gla0 + 2 * gla_qk, gla0 + 2 * gla_qk + gla_v),
                          (GLA_DK, GLA_DK, GLA_DV, GLA_DV), GLA_HEADS)
    w_gla = jnp.concatenate([w_gla, jnp.broadcast_to(w_ga, (GLA_HEADS,) + w_ga.shape)], axis=2).astype(BF16)
    w_gl = w_in[:, ga0 + 2 * rank:].astype(BF16)

    gw = jnp.zeros((GLA_HEADS, GATE_COPY, 2 * GLA_DK), F32)
    gw = gw.at[:, :rank, :GLA_DK].set(gla_gate_w_fwd.reshape(rank, GLA_HEADS, GLA_DK).transpose(1, 0, 2))
    gw = gw.at[:, rank:, GLA_DK:].set(gla_gate_w_bwd.reshape(rank, GLA_HEADS, GLA_DK).transpose(1, 0, 2))
    gw_hi = gw.astype(BF16)
    gw_lo = (gw - gw_hi.astype(F32)).astype(BF16)
    gate_w = jnp.concatenate([gw_hi, gw_lo, gw_hi, jnp.zeros_like(gw_hi)], axis=1)
    gate_b = jnp.concatenate([gla_gate_b_fwd.reshape(GLA_HEADS, 1, GLA_DK),
                              gla_gate_b_bwd.reshape(GLA_HEADS, 1, GLA_DK)], axis=2)

    dec = jnp.stack([ret_decay_fwd, ret_decay_bwd], axis=1)[:, :, None]
    dec = jnp.pad(jnp.broadcast_to(dec, (RET_HEADS, 2, RET_DV)), ((0, 0), (0, 6), (0, 0)))

    x2 = x.reshape(t, d)
    h = _mix_norm(x2, norm_mix[None, :], _pick(t, (1024, 512, 256, 128)))
    h3 = h.reshape(b, s, d)
    ret = _retention(h3, w_ret, cos, sin, dec, ret_norm[None, :])
    gla = _gla(h3, w_gla, gate_w, gate_b, gla_norm[None, :])

    n_e = w_router.shape[1]
    w_r = jnp.pad(w_router, ((0, 0), (0, LANES - n_e)))
    wr_hi = w_r.astype(BF16)
    wr_lo = (w_r - wr_hi.astype(F32)).astype(BF16)
    x1, h2, aff_t = _merge(h, ret.reshape(t, ret_v), gla.reshape(t, gla_v), x2, w_gl,
                           w_branch_ret.astype(BF16), w_branch_gla.astype(BF16), w_out.astype(BF16),
                           norm_ffn[None, :], jnp.concatenate([wr_hi, wr_lo], axis=1), n_e, b,
                           _pick(s, (512, 256, 128)))

    capacity = EC_CAPACITY_FACTOR * s // n_e
    slot = _route(aff_t, capacity)
    xg, gate = _dispatch(slot, aff_t, h2.reshape(b, s, d), capacity)
    f = w_gate.shape[2]
    y = _ffn(xg.reshape(n_e, b * capacity, d), gate.reshape(n_e, b * capacity, LANES),
             w_gate, w_up, w_down, _pick(f, (256, 128)), _pick(b * capacity, (512, 256, 128)))
    return _combine(slot, y.reshape(n_e, b, capacity, d), x1.reshape(b, s, d), norm_out,
                    _pick(s, (512, 256, 128)))


def kernel(x, positions, norm_mix, w_in, ret_decay_fwd, ret_decay_bwd, ret_norm, gla_gate_w_fwd,
           gla_gate_b_fwd, gla_gate_w_bwd, gla_gate_b_bwd, gla_norm, w_branch_ret, w_branch_gla,
           w_out, norm_ffn, w_router, w_gate, w_up, w_down, norm_final):
    depth = norm_mix.shape[0]
    assert depth == 1, "the final RMSNorm is fused into the last layer's combine stage"
    cos, sin = _rope_table(positions, RET_DK)
    return _layer(x, cos, sin, norm_mix[0], w_in[0], ret_decay_fwd[0], ret_decay_bwd[0], ret_norm[0],
                  gla_gate_w_fwd[0], gla_gate_b_fwd[0], gla_gate_w_bwd[0], gla_gate_b_bwd[0], gla_norm[0],
                  w_branch_ret[0], w_branch_gla[0], w_out[0], norm_ffn[0], w_router[0],
                  w_gate[0], w_up[0], w_down[0], norm_final[None, :])
```

```python
import functools

import jax
import jax.numpy as jnp
from jax import lax
from jax.experimental import pallas as pl
from jax.experimental.pallas import tpu as pltpu

F32 = jnp.float32
BF16 = jnp.bfloat16

RET_HEADS = 4
RET_DK = 128
RET_DV = 256
GLA_HEADS = 4
GLA_DK = 128
GLA_DV = 256
GLA_GATE_RANK = 16
GLA_GATE_NORMALIZER = 16.0
CHUNK = 128
EC_CAPACITY_FACTOR = 2
ROPE_THETA = 10000.0
NORM_EPS = 1e-6
LOG2_E = 1.4426950408889634

CHUNK_UNROLL = 8
LANES = 128
VMEM_LIMIT = 56 << 20


def _params(sem, vmem=VMEM_LIMIT):
    return pltpu.CompilerParams(dimension_semantics=sem, vmem_limit_bytes=vmem)


def _resident(shape):
    return pl.BlockSpec(shape, lambda *_: (0,) * len(shape), pipeline_mode=pl.Buffered(1))


def _pick(n, prefs):
    for p in prefs:
        if n % p == 0:
            return p
    return n


def _sigmoid(x):
    return 1.0 / (1.0 + jnp.exp(-x))


def _silu(x):
    half = 0.5 * x
    return half + half * jnp.tanh(half)


def _log_sigmoid(x):
    return jnp.minimum(x, 0.0) - jnp.log(1.0 + jnp.exp(-jnp.abs(x)))


def _rms(x, gain):
    return x * lax.rsqrt(jnp.mean(x * x, axis=-1, keepdims=True) + NORM_EPS) * gain


def _dot(a, b):
    return jnp.dot(a, b, preferred_element_type=F32)


def _dot_nt(a, b):
    return lax.dot_general(a, b, (((1,), (1,)), ((), ())), preferred_element_type=F32)


def _dot_tn(a, b):
    return lax.dot_general(a, b, (((0,), (0,)), ((), ())), preferred_element_type=F32)


def _split2(x):
    hi = x.astype(BF16)
    lo = (x - hi.astype(F32)).astype(BF16)
    return hi, lo


def _rope_table_kernel(pos_ref, freq_ref, shift_ref, cos_ref, sin_ref):
    half = freq_ref.shape[1] // 2
    ang = pos_ref[0].astype(F32) * freq_ref[...]
    tab = jnp.cos(ang - shift_ref[...])
    swapped = pltpu.roll(tab, half, 1)
    lower = lax.broadcasted_iota(jnp.int32, tab.shape, 1) < half
    cos_ref[0] = jnp.where(lower, tab, swapped)
    sin_ref[0] = jnp.where(lower, -swapped, tab)


def _rope_table(positions, dk):
    b, s = positions.shape
    half = jnp.arange(0, dk, 2, dtype=F32) / dk
    inv_freq = ROPE_THETA ** (-half)
    freq = jnp.concatenate([inv_freq, inv_freq])[None, :]
    shift = jnp.concatenate([jnp.zeros(dk // 2, F32), jnp.full(dk // 2, jnp.pi / 2, F32)])[None, :]
    out = jax.ShapeDtypeStruct((b, s, dk), F32)
    return pl.pallas_call(
        _rope_table_kernel,
        grid=(b,),
        in_specs=[pl.BlockSpec((1, s, 1), lambda i: (i, 0, 0)),
                  pl.BlockSpec((1, dk), lambda i: (0, 0)),
                  pl.BlockSpec((1, dk), lambda i: (0, 0))],
        out_specs=[pl.BlockSpec((1, s, dk), lambda i: (i, 0, 0))] * 2,
        out_shape=[out, out],
        compiler_params=_params(("parallel",)),
        name="rope_table",
    )(positions[:, :, None], freq, shift)


def _rope(t, cos, sin_signed):
    return t * cos + pltpu.roll(t, t.shape[-1] // 2, 1) * sin_signed


def _prefix_states(n_chunks, dk, dv, decay_f, decay_b, kv_ref, st_ref):
    def fwd(n, s):
        st_ref[n, 0:dk, :] = s.astype(BF16)
        return decay_f(n) * s + kv_ref[n, 0:dk, :]

    def bwd(i, s):
        n = n_chunks - 1 - i
        st_ref[n, dk:2 * dk, :] = s.astype(BF16)
        return decay_b(n) * s + kv_ref[n, dk:2 * dk, :]

    zero = jnp.zeros((dk, dv), F32)
    lax.fori_loop(0, n_chunks, fwd, zero)
    lax.fori_loop(0, n_chunks, bwd, zero)


def _chunk_rows(n):
    return pl.ds(pl.multiple_of(n * CHUNK, CHUNK), CHUNK)


def _for_row_tiles(s_len, body):
    tile = _pick(s_len, (512, 256, CHUNK))

    def step(i, carry):
        body(pl.ds(pl.multiple_of(i * tile, tile), tile))
        return carry

    lax.fori_loop(0, s_len // tile, step, 0)


def _retention_kernel(x_ref, mix_gain_ref, wq_ref, wk_ref, wv_ref, wg_ref, cos_ref, sin_ref, dec_ref,
                      gain_ref, o_ref, h_ref, qr_ref, kr_ref, v_ref, g_ref, kv_ref, st_ref):
    c = CHUNK
    s_len, dk, dv = qr_ref.shape[0], qr_ref.shape[1], v_ref.shape[1]
    n_chunks = s_len // c
    scale = dk ** -0.5

    @pl.when(pl.program_id(1) == 0)
    def _():
        def norm(rows):
            h_ref[0, rows, :] = _rms(x_ref[0, rows, :], mix_gain_ref[...]).astype(h_ref.dtype)
        _for_row_tiles(s_len, norm)

    w_qk = jnp.concatenate([wq_ref[...], wk_ref[...]], axis=1).astype(BF16)
    w_v = wv_ref[...].astype(BF16)
    w_g = wg_ref[...].astype(BF16)

    def project(rows):
        h = h_ref[0, rows, :]
        qk = _dot(h, w_qk)
        cos, sin = cos_ref[0, rows, :], sin_ref[0, rows, :]
        qr_ref[rows, :] = _rope(qk[:, :dk], cos, sin) * scale
        kr_ref[rows, :] = _rope(qk[:, dk:], cos, sin)
        v_ref[rows, :] = _dot(h, w_v).astype(BF16)
        g_ref[rows, :] = _dot(h, w_g)

    _for_row_tiles(s_len, project)

    lg_f = _log_sigmoid(dec_ref[0, 0:1, :])
    lg_b = _log_sigmoid(dec_ref[0, 1:2, :])
    lgf_k, lgb_k = lg_f[:, :dk], lg_b[:, :dk]
    pos = lax.broadcasted_iota(jnp.int32, (c, dk), 0).astype(F32)
    wq_f = jnp.exp((pos + 1.0) * lgf_k)
    wk_f = jnp.exp((c - 1.0 - pos) * lgf_k)
    wq_b = jnp.exp((c - pos) * lgb_k)
    wk_b = jnp.exp(pos * lgb_k)
    ri = lax.broadcasted_iota(jnp.int32, (c, c), 0)
    ci = lax.broadcasted_iota(jnp.int32, (c, c), 1)
    lower = ri >= ci
    rel = (ri - ci).astype(F32)
    decay_mask = jnp.where(lower,
                           jnp.exp(jnp.where(lower, rel, 0.0) * lg_f[:, :c]),
                           jnp.exp(jnp.where(lower, 0.0, -rel) * lg_b[:, :c]))
    chunk_decay_f = jnp.exp(c * lg_f)
    chunk_decay_b = jnp.exp(c * lg_b)

    def phase1(n, carry):
        rows = _chunk_rows(n)
        kr = kr_ref[rows, :]
        kw = jnp.concatenate([kr * wk_f, kr * wk_b], axis=1).astype(BF16)
        kv_ref[n] = _dot_tn(kw, v_ref[rows, :])
        return carry

    lax.fori_loop(0, n_chunks, phase1, 0, unroll=CHUNK_UNROLL)
    _prefix_states(n_chunks, dk, dv, lambda n: chunk_decay_f, lambda n: chunk_decay_b, kv_ref, st_ref)

    def phase2(n, carry):
        rows = _chunk_rows(n)
        qr, kr = qr_ref[rows, :], kr_ref[rows, :]
        scores = _dot_nt(qr.astype(BF16), kr.astype(BF16)) * decay_mask
        intra = _dot(scores.astype(BF16), v_ref[rows, :])
        qw = jnp.concatenate([qr * wq_f, qr * wq_b], axis=1).astype(BF16)
        o = intra + _dot(qw, st_ref[n])
        mu = jnp.mean(o, axis=-1, keepdims=True)
        d = o - mu
        var = jnp.mean(d * d, axis=-1, keepdims=True)
        y = d * lax.rsqrt(var + NORM_EPS) * gain_ref[...]
        o_ref[0, rows, :] = (y * _silu(g_ref[rows, :])).astype(o_ref.dtype)
        return carry

    lax.fori_loop(0, n_chunks, phase2, 0, unroll=CHUNK_UNROLL)


def _retention(x, mix_gain, w_in, col0, cos, sin, dec, gain):
    b, s, d = x.shape
    h, dk, dv = RET_HEADS, RET_DK, RET_DV
    q0 = col0 // dk
    v0 = (col0 + 2 * h * dk) // dv
    n_chunks = s // CHUNK
    return pl.pallas_call(
        _retention_kernel,
        grid=(b, h),
        in_specs=[pl.BlockSpec((1, s, d), lambda i, j: (i, 0, 0)),
                  pl.BlockSpec((1, d), lambda i, j: (0, 0)),
                  pl.BlockSpec((d, dk), lambda i, j: (0, q0 + j)),
                  pl.BlockSpec((d, dk), lambda i, j: (0, q0 + h + j)),
                  pl.BlockSpec((d, dv), lambda i, j: (0, v0 + j)),
                  pl.BlockSpec((d, dv), lambda i, j: (0, v0 + h + j)),
                  pl.BlockSpec((1, s, dk), lambda i, j: (i, 0, 0)),
                  pl.BlockSpec((1, s, dk), lambda i, j: (i, 0, 0)),
                  pl.BlockSpec((1, 8, dv), lambda i, j: (j, 0, 0)),
                  pl.BlockSpec((1, dv), lambda i, j: (0, j))],
        out_specs=[pl.BlockSpec((1, s, dv), lambda i, j: (i, 0, j)),
                   pl.BlockSpec((1, s, d), lambda i, j: (i, 0, 0))],
        out_shape=[jax.ShapeDtypeStruct((b, s, h * dv), BF16),
                   jax.ShapeDtypeStruct((b, s, d), BF16)],
        scratch_shapes=[pltpu.VMEM((s, dk), F32),
                        pltpu.VMEM((s, dk), F32),
                        pltpu.VMEM((s, dv), BF16),
                        pltpu.VMEM((s, dv), F32),
                        pltpu.VMEM((n_chunks, 2 * dk, dv), F32),
                        pltpu.VMEM((n_chunks, 2 * dk, dv), BF16)],
        compiler_params=_params(("parallel", "arbitrary")),
        name="retention",
    )(x, mix_gain, w_in, w_in, w_in, w_in, cos, sin, dec, gain)


GATE_COPY = 2 * GLA_GATE_RANK


def _cumsum_dot(tri2, x):
    hi, lo = _split2(x)
    return _dot(tri2, jnp.concatenate([hi, lo], axis=0))


def _gla_kernel(h_ref, wq_ref, wk_ref, wv_ref, wg_ref, wa_ref, gw_ref, gb_ref, gain_ref, o_ref,
                q_ref, k_ref, v_ref, g_ref, la_ref, cumf_ref, cumb_ref, kv_ref, st_ref):
    c = CHUNK
    s_len, dk, dv = q_ref.shape[0], q_ref.shape[1], v_ref.shape[1]
    n_chunks = s_len // c
    scale = dk ** -0.5
    inv_norm = LOG2_E / GLA_GATE_NORMALIZER

    w_qk = jnp.concatenate([wq_ref[...], wk_ref[...]], axis=1).astype(BF16)
    w_v = wv_ref[...].astype(BF16)
    w_g = wg_ref[...].astype(BF16)
    w_a = wa_ref[...].astype(BF16)

    def project(rows):
        h = h_ref[0, rows, :]
        qk = _dot(h, w_qk)
        q_ref[rows, :] = qk[:, :dk] * scale
        k_ref[rows, :] = qk[:, dk:]
        v_ref[rows, :] = _dot(h, w_v).astype(BF16)
        g_ref[rows, :] = _dot(h, w_g)
        x = _dot(h, w_a)
        x_hi = x.astype(BF16).astype(F32)
        lane = lax.broadcasted_iota(jnp.int32, x.shape, 1)
        ga = jnp.where(lane < GATE_COPY, x_hi,
                       jnp.where(lane < 2 * GATE_COPY, pltpu.roll(x_hi, GATE_COPY, 1),
                                 jnp.where(lane < 3 * GATE_COPY, pltpu.roll(x - x_hi, 2 * GATE_COPY, 1), 0.0)))
        la_ref[rows, :] = _log_sigmoid(_dot(ga.astype(BF16), gw_ref[0]) + gb_ref[0]) * inv_norm

    _for_row_tiles(s_len, project)

    ri = lax.broadcasted_iota(jnp.int32, (c, c), 0)
    ci = lax.broadcasted_iota(jnp.int32, (c, c), 1)
    lower = ri >= ci
    tri_lower = jnp.where(lower, 1.0, 0.0).astype(BF16)
    tri_upper = jnp.where(ci >= ri, 1.0, 0.0).astype(BF16)
    tri2_lower = jnp.concatenate([tri_lower, tri_lower], axis=1)
    tri2_upper = jnp.concatenate([tri_upper, tri_upper], axis=1)

    def cumulate(n, carry):
        rows = _chunk_rows(n)
        cumf_ref[rows, :] = _cumsum_dot(tri2_lower, la_ref[rows, 0:dk])
        cumb_ref[rows, :] = _cumsum_dot(tri2_upper, la_ref[rows, dk:2 * dk])
        return carry

    lax.fori_loop(0, n_chunks, cumulate, 0, unroll=CHUNK_UNROLL)

    def phase1(n, carry):
        rows = _chunk_rows(n)
        cum_f, cum_b = cumf_ref[rows, :], cumb_ref[rows, :]
        last_f = cum_f[c - 1:c, :]
        last_b = cum_b[0:1, :]
        k = k_ref[rows, :]
        kw = jnp.concatenate([k * jnp.exp2(last_f - cum_f), k * jnp.exp2(last_b - cum_b)], axis=1).astype(BF16)
        kv_ref[n] = _dot_tn(v_ref[rows, :], kw)
        return carry

    lax.fori_loop(0, n_chunks, phase1, 0, unroll=CHUNK_UNROLL)

    def fwd(n, s):
        st_ref[n, :, 0:dk] = s.astype(BF16)
        last = cumf_ref[pl.ds(pl.multiple_of(n * c, c) + (c - 1), 1), :]
        return jnp.exp2(last) * s + kv_ref[n, :, 0:dk]

    def bwd(i, s):
        n = n_chunks - 1 - i
        st_ref[n, :, dk:2 * dk] = s.astype(BF16)
        last = cumb_ref[pl.ds(pl.multiple_of(n * c, c), 1), :]
        return jnp.exp2(last) * s + kv_ref[n, :, dk:2 * dk]

    zero = jnp.zeros((dv, dk), F32)
    lax.fori_loop(0, n_chunks, fwd, zero)
    lax.fori_loop(0, n_chunks, bwd, zero)

    def phase2(n, carry):
        rows = _chunk_rows(n)
        cum_f, cum_b = cumf_ref[rows, :], cumb_ref[rows, :]
        ref_f = cum_f[c // 2:c // 2 + 1, :]
        ref_b = cum_b[c // 2 - 1:c // 2, :]
        q = q_ref[rows, :]
        k = k_ref[rows, :]
        q_f = q * jnp.exp2(cum_f - ref_f)
        q_b = q * jnp.exp2(cum_b - ref_b)
        s_f = _dot_nt(q_f.astype(BF16), (k * jnp.exp2(ref_f - cum_f)).astype(BF16))
        s_b = _dot_nt(q_b.astype(BF16), (k * jnp.exp2(ref_b - cum_b)).astype(BF16))
        scores = jnp.where(lower, s_f, s_b)
        intra = _dot(scores.astype(BF16), v_ref[rows, :])
        qw = jnp.concatenate([q_f * jnp.exp2(ref_f), q_b * jnp.exp2(ref_b)], axis=1).astype(BF16)
        o = intra + _dot_nt(qw, st_ref[n])
        y = _rms(o, gain_ref[...])
        o_ref[0, rows, :] = (y * _silu(g_ref[rows, :])).astype(o_ref.dtype)
        return carry

    lax.fori_loop(0, n_chunks, phase2, 0, unroll=CHUNK_UNROLL)


def _gla(h3, w_in, col0, gate_w, gate_b, gain):
    b, s, d = h3.shape
    h, dk, dv = GLA_HEADS, GLA_DK, GLA_DV
    q0 = col0 // dk
    v0 = (col0 + 2 * h * dk) // dv
    a0 = (col0 + 2 * h * dk + 2 * h * dv) // LANES
    n_chunks = s // CHUNK
    return pl.pallas_call(
        _gla_kernel,
        grid=(b, h),
        in_specs=[pl.BlockSpec((1, s, d), lambda i, j: (i, 0, 0)),
                  pl.BlockSpec((d, dk), lambda i, j: (0, q0 + j)),
                  pl.BlockSpec((d, dk), lambda i, j: (0, q0 + h + j)),
                  pl.BlockSpec((d, dv), lambda i, j: (0, v0 + j)),
                  pl.BlockSpec((d, dv), lambda i, j: (0, v0 + h + j)),
                  pl.BlockSpec((d, LANES), lambda i, j: (0, a0)),
                  pl.BlockSpec((1, LANES, 2 * dk), lambda i, j: (j, 0, 0)),
                  pl.BlockSpec((1, 1, 2 * dk), lambda i, j: (j, 0, 0)),
                  pl.BlockSpec((1, dv), lambda i, j: (0, j))],
        out_specs=pl.BlockSpec((1, s, dv), lambda i, j: (i, 0, j)),
        out_shape=jax.ShapeDtypeStruct((b, s, h * dv), BF16),
        scratch_shapes=[pltpu.VMEM((s, dk), F32),
                        pltpu.VMEM((s, dk), F32),
                        pltpu.VMEM((s, dv), BF16),
                        pltpu.VMEM((s, dv), F32),
                        pltpu.VMEM((s, 2 * dk), F32),
                        pltpu.VMEM((s, dk), F32),
                        pltpu.VMEM((s, dk), F32),
                        pltpu.VMEM((n_chunks, dv, 2 * dk), F32),
                        pltpu.VMEM((n_chunks, dv, 2 * dk), BF16)],
        compiler_params=_params(("parallel", "arbitrary")),
        name="gla",
    )(h3, w_in, w_in, w_in, w_in, w_in, gate_w, gate_b, gain)


MERGE_COLS = 256


def _merge_kernel(n_experts, h_ref, ret_ref, gla_ref, x_ref, wgl_ref, wr_ref, wg_ref, wo_ref, gain_ref,
                  wr2_ref, x1_ref, h2_ref, aff_ref):
    d = x_ref.shape[1]
    h, ret, gla = h_ref[...], ret_ref[...], gla_ref[...]
    blocks = []
    for j in range(0, d, MERGE_COLS):
        cols = slice(j, j + MERGE_COLS)
        cols_gla = slice(d + j, d + j + MERGE_COLS)
        m = (_sigmoid(_dot(h, wgl_ref[:, cols])) * _dot(ret, wr_ref[:, cols])
             + _sigmoid(_dot(h, wgl_ref[:, cols_gla])) * _dot(gla, wg_ref[:, cols]))
        blocks.append(m.astype(BF16))
    x1 = x_ref[...] + _dot(jnp.concatenate(blocks, axis=1), wo_ref[...])
    x1_ref[...] = x1
    h2 = _rms(x1, gain_ref[...]).astype(BF16)
    h2_ref[...] = h2
    logits2 = _dot(h2, wr2_ref[...])
    logits = logits2[:, :LANES] + logits2[:, LANES:]
    lane = lax.broadcasted_iota(jnp.int32, logits.shape, 1)
    logits = jnp.where(lane < n_experts, logits, -jnp.inf)
    p = jnp.exp(logits - jnp.max(logits, axis=-1, keepdims=True))
    aff = p / jnp.sum(p, axis=-1, keepdims=True)
    aff_ref[0] = aff.T[0:n_experts, :]


def _merge(h, ret, gla, x2, w_gl, w_ret, w_gla, w_out, gain, w_router2, n_experts, batch, tm):
    t, d = x2.shape
    s = t // batch
    per_b = s // tm
    rows = lambda width: pl.BlockSpec((tm, width), lambda i: (i, 0))
    return pl.pallas_call(
        functools.partial(_merge_kernel, n_experts),
        grid=(t // tm,),
        in_specs=[rows(d), rows(ret.shape[1]), rows(gla.shape[1]), rows(d),
                  _resident(w_gl.shape), _resident(w_ret.shape), _resident(w_gla.shape),
                  _resident(w_out.shape), _resident((1, d)),
                  _resident(w_router2.shape)],
        out_specs=[rows(d), rows(d),
                   pl.BlockSpec((1, n_experts, tm), lambda i: (i // per_b, 0, i % per_b))],
        out_shape=[jax.ShapeDtypeStruct((t, d), F32),
                   jax.ShapeDtypeStruct((t, d), BF16),
                   jax.ShapeDtypeStruct((batch, n_experts, s), F32)],
        compiler_params=_params(("parallel",)),
        name="merge",
    )(h, ret, gla, x2, w_gl, w_ret, w_gla, w_out, gain, w_router2)


def _prefix_count(mask):
    s = mask.shape[1]
    ri = lax.broadcasted_iota(jnp.int32, (LANES, LANES), 0)
    ci = lax.broadcasted_iota(jnp.int32, (LANES, LANES), 1)
    tri = jnp.where(ri <= ci, 1.0, 0.0).astype(BF16)
    off = jnp.zeros((mask.shape[0], 1), F32)
    parts = []
    for j in range(s // LANES):
        p = _dot(mask[:, j * LANES:(j + 1) * LANES].astype(BF16), tri) + off
        parts.append(p)
        off = p[:, LANES - 1:LANES]
    return jnp.concatenate(parts, axis=1)


def _route_kernel(capacity, aff_ref, slot_ref):
    a = aff_ref[0]
    bits = lax.bitcast_convert_type(a, jnp.int32)
    n_e = a.shape[0]
    cap = float(capacity)

    def count(pred):
        return jnp.sum(jnp.where(pred, 1.0, 0.0), axis=1, keepdims=True)

    def search(_, c):
        lo, hi = c
        mid = lo + lax.shift_right_logical(hi - lo, 1)
        ok = count(bits >= mid) >= cap
        return jnp.where(ok, mid, lo), jnp.where(ok, hi, mid)

    lo0 = jnp.zeros((n_e, 1), jnp.int32)
    hi0 = jnp.full((n_e, 1), 0x7F800000, jnp.int32)
    thr_bits, _ = lax.fori_loop(0, 31, search, (lo0, hi0))
    thr0 = jnp.max(jnp.where(bits <= thr_bits, a, -1.0), axis=1, keepdims=True)

    def counts(v):
        return count(a >= v), count(a > v)

    def unsettled(state):
        _, c_ge, c_gt = state
        bad = jnp.where(c_ge < cap, 1.0, jnp.where(c_gt >= cap, 1.0, 0.0))
        return jnp.max(bad, axis=0, keepdims=True)[0, 0] > 0.0

    def step(state):
        v, c_ge, c_gt = state
        below = jnp.max(jnp.where(a < v, a, -1.0), axis=1, keepdims=True)
        above = jnp.min(jnp.where(a > v, a, 2.0), axis=1, keepdims=True)
        v = jnp.where(c_ge < cap, below, jnp.where(c_gt >= cap, above, v))
        return (v,) + counts(v)

    thr, _, n_gt = lax.while_loop(unsettled, step, (thr0,) + counts(thr0))

    gt = a > thr
    eq = a == thr
    need = cap - n_gt
    eq_rank = _prefix_count(jnp.where(eq, 1.0, 0.0))
    sel = jnp.where(gt, 1.0, jnp.where(eq, jnp.where(eq_rank <= need, 1.0, 0.0), 0.0))
    pos = _prefix_count(sel)
    slot_ref[0] = jnp.where(sel > 0.0, pos - 1.0, -1.0)


def _route(aff_t, capacity):
    b, e, s = aff_t.shape
    return pl.pallas_call(
        functools.partial(_route_kernel, capacity),
        grid=(b,),
        in_specs=[pl.BlockSpec((1, e, s), lambda i: (i, 0, 0))],
        out_specs=pl.BlockSpec((1, e, s), lambda i: (i, 0, 0)),
        out_shape=jax.ShapeDtypeStruct((b, e, s), F32),
        compiler_params=_params(("parallel",)),
        name="route",
    )(aff_t)


def _dispatch_kernel(slot_ref, aff_ref, h_ref, xg_ref, gate_ref):
    cap = xg_ref.shape[2]
    slot = slot_ref[0, 0]
    hit = lax.broadcasted_iota(jnp.int32, (cap, slot.shape[1]), 0).astype(F32) == slot
    onehot = jnp.where(hit, 1.0, 0.0).astype(BF16)
    xg_ref[0, 0] = _dot(onehot, h_ref[0]).astype(xg_ref.dtype)
    gate = jnp.sum(jnp.where(hit, aff_ref[0, 0], 0.0), axis=1, keepdims=True)
    gate_ref[0, 0] = jnp.broadcast_to(gate, gate_ref.shape[2:])


def _dispatch(slot, aff_t, h2, capacity):
    b, e, s = slot.shape
    d = h2.shape[-1]
    return pl.pallas_call(
        _dispatch_kernel,
        grid=(b, e),
        in_specs=[pl.BlockSpec((1, 1, 1, s), lambda i, j: (i, j, 0, 0)),
                  pl.BlockSpec((1, 1, 1, s), lambda i, j: (i, j, 0, 0)),
                  pl.BlockSpec((1, s, d), lambda i, j: (i, 0, 0))],
        out_specs=[pl.BlockSpec((1, 1, capacity, d), lambda i, j: (j, i, 0, 0)),
                   pl.BlockSpec((1, 1, capacity, LANES), lambda i, j: (j, i, 0, 0))],
        out_shape=[jax.ShapeDtypeStruct((e, b, capacity, d), BF16),
                   jax.ShapeDtypeStruct((e, b, capacity, LANES), F32)],
        compiler_params=_params(("parallel", "parallel")),
        name="dispatch",
    )(slot.reshape(b, e, 1, s), aff_t.reshape(b, e, 1, s), h2)


def _ffn_kernel(tm, n_f, x_ref, gate_ref, wg_ref, wu_ref, wd_ref, y_ref, acc_ref):
    f = pl.program_id(1)
    w_gate = wg_ref[0].astype(BF16)
    w_up = wu_ref[0].astype(BF16)
    w_down = wd_ref[0].astype(BF16)
    m = x_ref.shape[1]
    d = x_ref.shape[2]

    def step(first, final):
        for i in range(m // tm):
            rows = pl.ds(i * tm, tm)
            x = x_ref[0, rows, :]
            act = (_silu(_dot(x, w_gate)) * _dot(x, w_up)).astype(BF16)
            part = _dot(act, w_down)
            if not first:
                part = acc_ref[rows, :] + part
            if final:
                gate = jnp.concatenate([gate_ref[0, rows, :]] * (d // LANES), axis=1)
                y_ref[0, rows, :] = (part * gate).astype(y_ref.dtype)
            else:
                acc_ref[rows, :] = part

    if n_f == 1:
        step(True, True)
    else:
        pl.when(f == 0)(lambda: step(True, False))
        if n_f > 2:
            pl.when(jnp.logical_and(f > 0, f < n_f - 1))(lambda: step(False, False))
        pl.when(f == n_f - 1)(lambda: step(False, True))


def _ffn(xg, gate, w_gate, w_up, w_down, tf, tm):
    e, m, d = xg.shape
    f = w_gate.shape[2]
    return pl.pallas_call(
        functools.partial(_ffn_kernel, tm, f // tf),
        grid=(e, f // tf),
        in_specs=[pl.BlockSpec((1, m, d), lambda i, j: (i, 0, 0)),
                  pl.BlockSpec((1, m, LANES), lambda i, j: (i, 0, 0)),
                  pl.BlockSpec((1, d, tf), lambda i, j: (i, 0, j)),
                  pl.BlockSpec((1, d, tf), lambda i, j: (i, 0, j)),
                  pl.BlockSpec((1, tf, d), lambda i, j: (i, j, 0))],
        out_specs=pl.BlockSpec((1, m, d), lambda i, j: (i, 0, 0)),
        out_shape=jax.ShapeDtypeStruct((e, m, d), BF16),
        scratch_shapes=[pltpu.VMEM((m, d), F32)],
        compiler_params=_params(("parallel", "arbitrary")),
        name="ffn",
    )(xg, gate, w_gate, w_up, w_down)


def _combine_kernel(slot_ref, y_ref, x1_ref, gain_ref, o_ref):
    n_e, cap = y_ref.shape[0], y_ref.shape[2]
    tt = x1_ref.shape[1]
    acc = x1_ref[0]
    row = lax.broadcasted_iota(jnp.int32, (cap, tt), 0).astype(F32)
    for e in range(n_e):
        onehot = jnp.where(row == slot_ref[0, e:e + 1, :], 1.0, 0.0).astype(BF16)
        acc = acc + _dot_tn(onehot, y_ref[e, 0])
    o_ref[0] = _rms(acc, gain_ref[...])


def _combine(slot, y, x1, gain, tt):
    b, e, s = slot.shape
    cap, d = y.shape[2], y.shape[3]
    return pl.pallas_call(
        _combine_kernel,
        grid=(b, s // tt),
        in_specs=[pl.BlockSpec((1, e, tt), lambda i, j: (i, 0, j)),
                  pl.BlockSpec((e, 1, cap, d), lambda i, j: (0, i, 0, 0)),
                  pl.BlockSpec((1, tt, d), lambda i, j: (i, j, 0)),
                  pl.BlockSpec((1, d), lambda i, j: (0, 0))],
        out_specs=pl.BlockSpec((1, tt, d), lambda i, j: (i, j, 0)),
        out_shape=jax.ShapeDtypeStruct((b, s, d), F32),
        compiler_params=_params(("parallel", "parallel")),
        name="combine",
    )(slot, y, x1, gain)


def _layer(x, cos, sin, norm_mix, w_in, ret_decay_fwd, ret_decay_bwd, ret_norm,
           gla_gate_w_fwd, gla_gate_b_fwd, gla_gate_w_bwd, gla_gate_b_bwd, gla_norm,
           w_branch_ret, w_branch_gla, w_out, norm_ffn, w_router, w_gate, w_up, w_down, norm_out):
    b, s, d = x.shape
    t = b * s
    ret_qk, ret_v = RET_HEADS * RET_DK, RET_HEADS * RET_DV
    gla_qk, gla_v = GLA_HEADS * GLA_DK, GLA_HEADS * GLA_DV
    rank = GLA_GATE_RANK
    gla0 = 2 * ret_qk + 2 * ret_v
    ga0 = gla0 + 2 * gla_qk + 2 * gla_v
    assert w_in.shape == (d, ga0 + 2 * rank + 2 * d)
    assert s % CHUNK == 0 and 3 * GATE_COPY <= LANES and ga0 % LANES == 0

    w_gl = w_in[:, ga0 + 2 * rank:].astype(BF16)

    gw = jnp.zeros((GLA_HEADS, GATE_COPY, 2 * GLA_DK), F32)
    gw = gw.at[:, :rank, :GLA_DK].set(gla_gate_w_fwd.reshape(rank, GLA_HEADS, GLA_DK).transpose(1, 0, 2))
    gw = gw.at[:, rank:, GLA_DK:].set(gla_gate_w_bwd.reshape(rank, GLA_HEADS, GLA_DK).transpose(1, 0, 2))
    gw_hi = gw.astype(BF16)
    gw_lo = (gw - gw_hi.astype(F32)).astype(BF16)
    gate_w = jnp.concatenate([gw_hi, gw_lo, gw_hi, jnp.zeros_like(gw_hi)], axis=1)
    gate_b = jnp.concatenate([gla_gate_b_fwd.reshape(GLA_HEADS, 1, GLA_DK),
                              gla_gate_b_bwd.reshape(GLA_HEADS, 1, GLA_DK)], axis=2)

    dec = jnp.stack([ret_decay_fwd, ret_decay_bwd], axis=1)[:, :, None]
    dec = jnp.pad(jnp.broadcast_to(dec, (RET_HEADS, 2, RET_DV)), ((0, 0), (0, 6), (0, 0)))

    x2 = x.reshape(t, d)
    ret, h3 = _retention(x, norm_mix[None, :], w_in, 0, cos, sin, dec, ret_norm[None, :])
    gla = _gla(h3, w_in, gla0, gate_w, gate_b, gla_norm[None, :])
    h = h3.reshape(t, d)

    n_e = w_router.shape[1]
    w_r = jnp.pad(w_router, ((0, 0), (0, LANES - n_e)))
    wr_hi = w_r.astype(BF16)
    wr_lo = (w_r - wr_hi.astype(F32)).astype(BF16)
    x1, h2, aff_t = _merge(h, ret.reshape(t, ret_v), gla.reshape(t, gla_v), x2, w_gl,
                           w_branch_ret.astype(BF16), w_branch_gla.astype(BF16), w_out.astype(BF16),
                           norm_ffn[None, :], jnp.concatenate([wr_hi, wr_lo], axis=1), n_e, b,
                           _pick(s, (512, 256, 128)))

    capacity = EC_CAPACITY_FACTOR * s // n_e
    slot = _route(aff_t, capacity)
    xg, gate = _dispatch(slot, aff_t, h2.reshape(b, s, d), capacity)
    f = w_gate.shape[2]
    y = _ffn(xg.reshape(n_e, b * capacity, d), gate.reshape(n_e, b * capacity, LANES),
             w_gate, w_up, w_down, _pick(f, (256, 128)), _pick(b * capacity, (1024, 512, 256, 128)))
    return _combine(slot, y.reshape(n_e, b, capacity, d), x1.reshape(b, s, d), norm_out,
                    _pick(s, (512, 256, 128)))


def kernel(x, positions, norm_mix, w_in, ret_decay_fwd, ret_decay_bwd, ret_norm, gla_gate_w_fwd,
           gla_gate_b_fwd, gla_gate_w_bwd, gla_gate_b_bwd, gla_norm, w_branch_ret, w_branch_gla,
           w_out, norm_ffn, w_router, w_gate, w_up, w_down, norm_final):
    depth = norm_mix.shape[0]
    assert depth == 1, "the final RMSNorm is fused into the last layer's combine stage"
    cos, sin = _rope_table(positions, RET_DK)
    return _layer(x, cos, sin, norm_mix[0], w_in[0], ret_decay_fwd[0], ret_decay_bwd[0], ret_norm[0],
                  gla_gate_w_fwd[0], gla_gate_b_fwd[0], gla_gate_w_bwd[0], gla_gate_b_bwd[0], gla_norm[0],
                  w_branch_ret[0], w_branch_gla[0], w_out[0], norm_ffn[0], w_router[0],
                  w_gate[0], w_up[0], w_down[0], norm_final[None, :])
```

```python
import functools

import jax
import jax.numpy as jnp
from jax import lax
from jax.experimental import pallas as pl
from jax.experimental.pallas import tpu as pltpu

F32 = jnp.float32
BF16 = jnp.bfloat16

RET_HEADS = 4
RET_DK = 128
RET_DV = 256
GLA_HEADS = 4
GLA_DK = 128
GLA_DV = 256
GLA_GATE_RANK = 16
GLA_GATE_NORMALIZER = 16.0
CHUNK = 128
EC_CAPACITY_FACTOR = 2
ROPE_THETA = 10000.0
NORM_EPS = 1e-6
LOG2_E = 1.4426950408889634

CHUNK_UNROLL = 8
LANES = 128
VMEM_LIMIT = 56 << 20


def _params(sem, vmem=VMEM_LIMIT):
    return pltpu.CompilerParams(dimension_semantics=sem, vmem_limit_bytes=vmem)


def _resident(shape):
    return pl.BlockSpec(shape, lambda *_: (0,) * len(shape), pipeline_mode=pl.Buffered(1))


def _pick(n, prefs):
    for p in prefs:
        if n % p == 0:
            return p
    return n


def _sigmoid(x):
    return 1.0 / (1.0 + jnp.exp(-x))


def _silu(x):
    half = 0.5 * x
    return half + half * jnp.tanh(half)


def _log_sigmoid(x):
    return jnp.minimum(x, 0.0) - jnp.log(1.0 + jnp.exp(-jnp.abs(x)))


def _rms(x, gain):
    return x * lax.rsqrt(jnp.mean(x * x, axis=-1, keepdims=True) + NORM_EPS) * gain


def _dot(a, b):
    return jnp.dot(a, b, preferred_element_type=F32)


def _dot_nt(a, b):
    return lax.dot_general(a, b, (((1,), (1,)), ((), ())), preferred_element_type=F32)


def _dot_tn(a, b):
    return lax.dot_general(a, b, (((0,), (0,)), ((), ())), preferred_element_type=F32)


def _split2(x):
    hi = x.astype(BF16)
    lo = (x - hi.astype(F32)).astype(BF16)
    return hi, lo


def _rope_table_kernel(pos_ref, freq_ref, shift_ref, cos_ref, sin_ref):
    half = freq_ref.shape[1] // 2
    ang = pos_ref[0].astype(F32) * freq_ref[...]
    tab = jnp.cos(ang - shift_ref[...])
    swapped = pltpu.roll(tab, half, 1)
    lower = lax.broadcasted_iota(jnp.int32, tab.shape, 1) < half
    cos_ref[0] = jnp.where(lower, tab, swapped)
    sin_ref[0] = jnp.where(lower, -swapped, tab)


def _rope_table(positions, dk):
    b, s = positions.shape
    half = jnp.arange(0, dk, 2, dtype=F32) / dk
    inv_freq = ROPE_THETA ** (-half)
    freq = jnp.concatenate([inv_freq, inv_freq])[None, :]
    shift = jnp.concatenate([jnp.zeros(dk // 2, F32), jnp.full(dk // 2, jnp.pi / 2, F32)])[None, :]
    out = jax.ShapeDtypeStruct((b, s, dk), F32)
    return pl.pallas_call(
        _rope_table_kernel,
        grid=(b,),
        in_specs=[pl.BlockSpec((1, s, 1), lambda i: (i, 0, 0)),
                  pl.BlockSpec((1, dk), lambda i: (0, 0)),
                  pl.BlockSpec((1, dk), lambda i: (0, 0))],
        out_specs=[pl.BlockSpec((1, s, dk), lambda i: (i, 0, 0))] * 2,
        out_shape=[out, out],
        compiler_params=_params(("parallel",)),
        name="rope_table",
    )(positions[:, :, None], freq, shift)


def _rope(t, cos, sin_signed):
    return t * cos + pltpu.roll(t, t.shape[-1] // 2, 1) * sin_signed


def _prefix_states(n_chunks, dk, dv, decay_f, decay_b, kv_ref, st_ref):
    def fwd(n, s):
        st_ref[n, 0:dk, :] = s.astype(BF16)
        return decay_f(n) * s + kv_ref[n, 0:dk, :]

    def bwd(i, s):
        n = n_chunks - 1 - i
        st_ref[n, dk:2 * dk, :] = s.astype(BF16)
        return decay_b(n) * s + kv_ref[n, dk:2 * dk, :]

    zero = jnp.zeros((dk, dv), F32)
    lax.fori_loop(0, n_chunks, fwd, zero)
    lax.fori_loop(0, n_chunks, bwd, zero)


def _chunk_rows(n):
    return pl.ds(pl.multiple_of(n * CHUNK, CHUNK), CHUNK)


def _for_row_tiles(s_len, body):
    tile = _pick(s_len, (512, 256, CHUNK))

    def step(i, carry):
        body(pl.ds(pl.multiple_of(i * tile, tile), tile))
        return carry

    lax.fori_loop(0, s_len // tile, step, 0)


def _retention_kernel(x_ref, mix_gain_ref, wq_ref, wk_ref, wv_ref, wg_ref, cos_ref, sin_ref, dec_ref,
                      gain_ref, o_ref, h_ref, qr_ref, kr_ref, v_ref, g_ref, kv_ref, st_ref):
    c = CHUNK
    s_len, dk, dv = qr_ref.shape[0], qr_ref.shape[1], v_ref.shape[1]
    n_chunks = s_len // c
    scale = dk ** -0.5

    @pl.when(pl.program_id(1) == 0)
    def _():
        def norm(rows):
            h_ref[0, rows, :] = _rms(x_ref[0, rows, :], mix_gain_ref[...]).astype(h_ref.dtype)
        _for_row_tiles(s_len, norm)

    w_qk = jnp.concatenate([wq_ref[...], wk_ref[...]], axis=0).astype(BF16)
    w_v = wv_ref[...].astype(BF16)
    w_g = wg_ref[...].astype(BF16)

    def project(rows):
        h = h_ref[0, rows, :]
        qk = _dot_nt(h, w_qk)
        cos, sin = cos_ref[0, rows, :], sin_ref[0, rows, :]
        qr_ref[rows, :] = _rope(qk[:, :dk], cos, sin) * scale
        kr_ref[rows, :] = _rope(qk[:, dk:], cos, sin)
        v_ref[rows, :] = _dot_nt(h, w_v).astype(BF16)
        g_ref[rows, :] = _dot_nt(h, w_g)

    _for_row_tiles(s_len, project)

    lg_f = _log_sigmoid(dec_ref[0, 0:1, :])
    lg_b = _log_sigmoid(dec_ref[0, 1:2, :])
    lgf_k, lgb_k = lg_f[:, :dk], lg_b[:, :dk]
    pos = lax.broadcasted_iota(jnp.int32, (c, dk), 0).astype(F32)
    wq_f = jnp.exp((pos + 1.0) * lgf_k)
    wk_f = jnp.exp((c - 1.0 - pos) * lgf_k)
    wq_b = jnp.exp((c - pos) * lgb_k)
    wk_b = jnp.exp(pos * lgb_k)
    ri = lax.broadcasted_iota(jnp.int32, (c, c), 0)
    ci = lax.broadcasted_iota(jnp.int32, (c, c), 1)
    lower = ri >= ci
    rel = (ri - ci).astype(F32)
    decay_mask = jnp.where(lower,
                           jnp.exp(jnp.where(lower, rel, 0.0) * lg_f[:, :c]),
                           jnp.exp(jnp.where(lower, 0.0, -rel) * lg_b[:, :c]))
    chunk_decay_f = jnp.exp(c * lg_f)
    chunk_decay_b = jnp.exp(c * lg_b)

    def phase1(n, carry):
        rows = _chunk_rows(n)
        kr = kr_ref[rows, :]
        kw = jnp.concatenate([kr * wk_f, kr * wk_b], axis=1).astype(BF16)
        kv_ref[n] = _dot_tn(kw, v_ref[rows, :])
        return carry

    lax.fori_loop(0, n_chunks, phase1, 0, unroll=CHUNK_UNROLL)
    _prefix_states(n_chunks, dk, dv, lambda n: chunk_decay_f, lambda n: chunk_decay_b, kv_ref, st_ref)

    def phase2(n, carry):
        rows = _chunk_rows(n)
        qr, kr = qr_ref[rows, :], kr_ref[rows, :]
        scores = _dot_nt(qr.astype(BF16), kr.astype(BF16)) * decay_mask
        intra = _dot(scores.astype(BF16), v_ref[rows, :])
        qw = jnp.concatenate([qr * wq_f, qr * wq_b], axis=1).astype(BF16)
        o = intra + _dot(qw, st_ref[n])
        mu = jnp.mean(o, axis=-1, keepdims=True)
        d = o - mu
        var = jnp.mean(d * d, axis=-1, keepdims=True)
        y = d * lax.rsqrt(var + NORM_EPS) * gain_ref[...]
        o_ref[0, rows, :] = (y * _silu(g_ref[rows, :])).astype(o_ref.dtype)
        return carry

    lax.fori_loop(0, n_chunks, phase2, 0, unroll=CHUNK_UNROLL)


def _retention(x, mix_gain, w_in, col0, cos, sin, dec, gain):
    b, s, d = x.shape
    h, dk, dv = RET_HEADS, RET_DK, RET_DV
    q0 = col0 // dk
    v0 = (col0 + 2 * h * dk) // dv
    n_chunks = s // CHUNK
    return pl.pallas_call(
        _retention_kernel,
        grid=(b, h),
        in_specs=[pl.BlockSpec((1, s, d), lambda i, j: (i, 0, 0)),
                  pl.BlockSpec((1, d), lambda i, j: (0, 0)),
                  pl.BlockSpec((dk, d), lambda i, j: (q0 + j, 0)),
                  pl.BlockSpec((dk, d), lambda i, j: (q0 + h + j, 0)),
                  pl.BlockSpec((dv, d), lambda i, j: (v0 + j, 0)),
                  pl.BlockSpec((dv, d), lambda i, j: (v0 + h + j, 0)),
                  pl.BlockSpec((1, s, dk), lambda i, j: (i, 0, 0)),
                  pl.BlockSpec((1, s, dk), lambda i, j: (i, 0, 0)),
                  pl.BlockSpec((1, 8, dv), lambda i, j: (j, 0, 0)),
                  pl.BlockSpec((1, dv), lambda i, j: (0, j))],
        out_specs=[pl.BlockSpec((1, s, dv), lambda i, j: (i, 0, j)),
                   pl.BlockSpec((1, s, d), lambda i, j: (i, 0, 0))],
        out_shape=[jax.ShapeDtypeStruct((b, s, h * dv), BF16),
                   jax.ShapeDtypeStruct((b, s, d), BF16)],
        scratch_shapes=[pltpu.VMEM((s, dk), F32),
                        pltpu.VMEM((s, dk), F32),
                        pltpu.VMEM((s, dv), BF16),
                        pltpu.VMEM((s, dv), F32),
                        pltpu.VMEM((n_chunks, 2 * dk, dv), F32),
                        pltpu.VMEM((n_chunks, 2 * dk, dv), BF16)],
        compiler_params=_params(("parallel", "arbitrary")),
        name="retention",
    )(x, mix_gain, w_in, w_in, w_in, w_in, cos, sin, dec, gain)


GATE_COPY = 2 * GLA_GATE_RANK


def _cumsum_dot(tri2, x):
    hi, lo = _split2(x)
    return _dot(tri2, jnp.concatenate([hi, lo], axis=0))


def _gla_kernel(h_ref, wq_ref, wk_ref, wv_ref, wg_ref, wa_ref, gw_ref, gb_ref, gain_ref, o_ref,
                q_ref, k_ref, v_ref, g_ref, la_ref, cumf_ref, cumb_ref, kv_ref, st_ref):
    c = CHUNK
    s_len, dk, dv = q_ref.shape[0], q_ref.shape[1], v_ref.shape[1]
    n_chunks = s_len // c
    scale = dk ** -0.5
    inv_norm = LOG2_E / GLA_GATE_NORMALIZER

    w_qk = jnp.concatenate([wq_ref[...], wk_ref[...]], axis=0).astype(BF16)
    w_v = wv_ref[...].astype(BF16)
    w_g = wg_ref[...].astype(BF16)
    w_a = wa_ref[...].astype(BF16)

    def project(rows):
        h = h_ref[0, rows, :]
        qk = _dot_nt(h, w_qk)
        q_ref[rows, :] = qk[:, :dk] * scale
        k_ref[rows, :] = qk[:, dk:]
        v_ref[rows, :] = _dot_nt(h, w_v).astype(BF16)
        g_ref[rows, :] = _dot_nt(h, w_g)
        x = _dot_nt(h, w_a)
        x_hi = x.astype(BF16).astype(F32)
        lane = lax.broadcasted_iota(jnp.int32, x.shape, 1)
        ga = jnp.where(lane < GATE_COPY, x_hi,
                       jnp.where(lane < 2 * GATE_COPY, pltpu.roll(x_hi, GATE_COPY, 1),
                                 jnp.where(lane < 3 * GATE_COPY, pltpu.roll(x - x_hi, 2 * GATE_COPY, 1), 0.0)))
        la_ref[rows, :] = _log_sigmoid(_dot(ga.astype(BF16), gw_ref[0]) + gb_ref[0]) * inv_norm

    _for_row_tiles(s_len, project)

    ri = lax.broadcasted_iota(jnp.int32, (c, c), 0)
    ci = lax.broadcasted_iota(jnp.int32, (c, c), 1)
    lower = ri >= ci
    tri_lower = jnp.where(lower, 1.0, 0.0).astype(BF16)
    tri_upper = jnp.where(ci >= ri, 1.0, 0.0).astype(BF16)
    tri2_lower = jnp.concatenate([tri_lower, tri_lower], axis=1)
    tri2_upper = jnp.concatenate([tri_upper, tri_upper], axis=1)

    def cumulate(n, carry):
        rows = _chunk_rows(n)
        cumf_ref[rows, :] = _cumsum_dot(tri2_lower, la_ref[rows, 0:dk])
        cumb_ref[rows, :] = _cumsum_dot(tri2_upper, la_ref[rows, dk:2 * dk])
        return carry

    lax.fori_loop(0, n_chunks, cumulate, 0, unroll=CHUNK_UNROLL)

    def phase1(n, carry):
        rows = _chunk_rows(n)
        cum_f, cum_b = cumf_ref[rows, :], cumb_ref[rows, :]
        last_f = cum_f[c - 1:c, :]
        last_b = cum_b[0:1, :]
        k = k_ref[rows, :]
        kw = jnp.concatenate([k * jnp.exp2(last_f - cum_f), k * jnp.exp2(last_b - cum_b)], axis=1).astype(BF16)
        kv_ref[n] = _dot_tn(v_ref[rows, :], kw)
        return carry

    lax.fori_loop(0, n_chunks, phase1, 0, unroll=CHUNK_UNROLL)

    def fwd(n, s):
        st_ref[n, :, 0:dk] = s.astype(BF16)
        last = cumf_ref[pl.ds(pl.multiple_of(n * c, c) + (c - 1), 1), :]
        return jnp.exp2(last) * s + kv_ref[n, :, 0:dk]

    def bwd(i, s):
        n = n_chunks - 1 - i
        st_ref[n, :, dk:2 * dk] = s.astype(BF16)
        last = cumb_ref[pl.ds(pl.multiple_of(n * c, c), 1), :]
        return jnp.exp2(last) * s + kv_ref[n, :, dk:2 * dk]

    zero = jnp.zeros((dv, dk), F32)
    lax.fori_loop(0, n_chunks, fwd, zero)
    lax.fori_loop(0, n_chunks, bwd, zero)

    def phase2(n, carry):
        rows = _chunk_rows(n)
        cum_f, cum_b = cumf_ref[rows, :], cumb_ref[rows, :]
        ref_f = cum_f[c // 2:c // 2 + 1, :]
        ref_b = cum_b[c // 2 - 1:c // 2, :]
        q = q_ref[rows, :]
        k = k_ref[rows, :]
        q_f = q * jnp.exp2(cum_f - ref_f)
        q_b = q * jnp.exp2(cum_b - ref_b)
        s_f = _dot_nt(q_f.astype(BF16), (k * jnp.exp2(ref_f - cum_f)).astype(BF16))
        s_b = _dot_nt(q_b.astype(BF16), (k * jnp.exp2(ref_b - cum_b)).astype(BF16))
        scores = jnp.where(lower, s_f, s_b)
        intra = _dot(scores.astype(BF16), v_ref[rows, :])
        qw = jnp.concatenate([q_f * jnp.exp2(ref_f), q_b * jnp.exp2(ref_b)], axis=1).astype(BF16)
        o = intra + _dot_nt(qw, st_ref[n])
        y = _rms(o, gain_ref[...])
        o_ref[0, rows, :] = (y * _silu(g_ref[rows, :])).astype(o_ref.dtype)
        return carry

    lax.fori_loop(0, n_chunks, phase2, 0, unroll=CHUNK_UNROLL)


def _gla(h3, w_in, col0, gate_w, gate_b, gain):
    b, s, d = h3.shape
    h, dk, dv = GLA_HEADS, GLA_DK, GLA_DV
    q0 = col0 // dk
    v0 = (col0 + 2 * h * dk) // dv
    a0 = (col0 + 2 * h * dk + 2 * h * dv) // LANES
    n_chunks = s // CHUNK
    return pl.pallas_call(
        _gla_kernel,
        grid=(b, h),
        in_specs=[pl.BlockSpec((1, s, d), lambda i, j: (i, 0, 0)),
                  pl.BlockSpec((dk, d), lambda i, j: (q0 + j, 0)),
                  pl.BlockSpec((dk, d), lambda i, j: (q0 + h + j, 0)),
                  pl.BlockSpec((dv, d), lambda i, j: (v0 + j, 0)),
                  pl.BlockSpec((dv, d), lambda i, j: (v0 + h + j, 0)),
                  pl.BlockSpec((LANES, d), lambda i, j: (a0, 0)),
                  pl.BlockSpec((1, LANES, 2 * dk), lambda i, j: (j, 0, 0)),
                  pl.BlockSpec((1, 1, 2 * dk), lambda i, j: (j, 0, 0)),
                  pl.BlockSpec((1, dv), lambda i, j: (0, j))],
        out_specs=pl.BlockSpec((1, s, dv), lambda i, j: (i, 0, j)),
        out_shape=jax.ShapeDtypeStruct((b, s, h * dv), BF16),
        scratch_shapes=[pltpu.VMEM((s, dk), F32),
                        pltpu.VMEM((s, dk), F32),
                        pltpu.VMEM((s, dv), BF16),
                        pltpu.VMEM((s, dv), F32),
                        pltpu.VMEM((s, 2 * dk), F32),
                        pltpu.VMEM((s, dk), F32),
                        pltpu.VMEM((s, dk), F32),
                        pltpu.VMEM((n_chunks, dv, 2 * dk), F32),
                        pltpu.VMEM((n_chunks, dv, 2 * dk), BF16)],
        compiler_params=_params(("parallel", "arbitrary")),
        name="gla",
    )(h3, w_in, w_in, w_in, w_in, w_in, gate_w, gate_b, gain)


MERGE_COLS = 256


def _merge_kernel(n_experts, h_ref, ret_ref, gla_ref, x_ref, wgl_ref, wr_ref, wg_ref, wo_ref, gain_ref,
                  wr2_ref, x1_ref, h2_ref, aff_ref):
    d = x_ref.shape[1]
    h, ret, gla = h_ref[...], ret_ref[...], gla_ref[...]
    blocks = []
    for j in range(0, d, MERGE_COLS):
        cols = slice(j, j + MERGE_COLS)
        cols_gla = slice(d + j, d + j + MERGE_COLS)
        m = (_sigmoid(_dot_nt(h, wgl_ref[cols, :])) * _dot(ret, wr_ref[:, cols])
             + _sigmoid(_dot_nt(h, wgl_ref[cols_gla, :])) * _dot(gla, wg_ref[:, cols]))
        blocks.append(m.astype(BF16))
    x1 = x_ref[...] + _dot(jnp.concatenate(blocks, axis=1), wo_ref[...])
    x1_ref[...] = x1
    h2 = _rms(x1, gain_ref[...]).astype(BF16)
    h2_ref[...] = h2
    logits2 = _dot(h2, wr2_ref[...])
    logits = logits2[:, :LANES] + logits2[:, LANES:]
    lane = lax.broadcasted_iota(jnp.int32, logits.shape, 1)
    logits = jnp.where(lane < n_experts, logits, -jnp.inf)
    p = jnp.exp(logits - jnp.max(logits, axis=-1, keepdims=True))
    aff = p / jnp.sum(p, axis=-1, keepdims=True)
    aff_ref[0] = aff.T[0:n_experts, :]


def _merge(h, ret, gla, x2, w_gl, w_ret, w_gla, w_out, gain, w_router2, n_experts, batch, tm):
    t, d = x2.shape
    s = t // batch
    per_b = s // tm
    rows = lambda width: pl.BlockSpec((tm, width), lambda i: (i, 0))
    return pl.pallas_call(
        functools.partial(_merge_kernel, n_experts),
        grid=(t // tm,),
        in_specs=[rows(d), rows(ret.shape[1]), rows(gla.shape[1]), rows(d),
                  _resident(w_gl.shape), _resident(w_ret.shape), _resident(w_gla.shape),
                  _resident(w_out.shape), _resident((1, d)),
                  _resident(w_router2.shape)],
        out_specs=[rows(d), rows(d),
                   pl.BlockSpec((1, n_experts, tm), lambda i: (i // per_b, 0, i % per_b))],
        out_shape=[jax.ShapeDtypeStruct((t, d), F32),
                   jax.ShapeDtypeStruct((t, d), BF16),
                   jax.ShapeDtypeStruct((batch, n_experts, s), F32)],
        compiler_params=_params(("parallel",)),
        name="merge",
    )(h, ret, gla, x2, w_gl, w_ret, w_gla, w_out, gain, w_router2)


def _prefix_count(mask):
    s = mask.shape[1]
    ri = lax.broadcasted_iota(jnp.int32, (LANES, LANES), 0)
    ci = lax.broadcasted_iota(jnp.int32, (LANES, LANES), 1)
    tri = jnp.where(ri <= ci, 1.0, 0.0).astype(BF16)
    off = jnp.zeros((mask.shape[0], 1), F32)
    parts = []
    for j in range(s // LANES):
        p = _dot(mask[:, j * LANES:(j + 1) * LANES].astype(BF16), tri) + off
        parts.append(p)
        off = p[:, LANES - 1:LANES]
    return jnp.concatenate(parts, axis=1)


def _route_kernel(capacity, aff_ref, slot_ref):
    a = aff_ref[0]
    bits = lax.bitcast_convert_type(a, jnp.int32)
    n_e = a.shape[0]
    cap = float(capacity)

    def count(pred):
        return jnp.sum(jnp.where(pred, 1.0, 0.0), axis=1, keepdims=True)

    def search(_, c):
        lo, hi = c
        mid = lo + lax.shift_right_logical(hi - lo, 1)
        ok = count(bits >= mid) >= cap
        return jnp.where(ok, mid, lo), jnp.where(ok, hi, mid)

    lo0 = jnp.zeros((n_e, 1), jnp.int32)
    hi0 = jnp.full((n_e, 1), 0x7F800000, jnp.int32)
    thr_bits, _ = lax.fori_loop(0, 31, search, (lo0, hi0))
    thr0 = jnp.max(jnp.where(bits <= thr_bits, a, -1.0), axis=1, keepdims=True)

    def counts(v):
        return count(a >= v), count(a > v)

    def unsettled(state):
        _, c_ge, c_gt = state
        bad = jnp.where(c_ge < cap, 1.0, jnp.where(c_gt >= cap, 1.0, 0.0))
        return jnp.max(bad, axis=0, keepdims=True)[0, 0] > 0.0

    def step(state):
        v, c_ge, c_gt = state
        below = jnp.max(jnp.where(a < v, a, -1.0), axis=1, keepdims=True)
        above = jnp.min(jnp.where(a > v, a, 2.0), axis=1, keepdims=True)
        v = jnp.where(c_ge < cap, below, jnp.where(c_gt >= cap, above, v))
        return (v,) + counts(v)

    thr, _, n_gt = lax.while_loop(unsettled, step, (thr0,) + counts(thr0))

    gt = a > thr
    eq = a == thr
    need = cap - n_gt
    eq_rank = _prefix_count(jnp.where(eq, 1.0, 0.0))
    sel = jnp.where(gt, 1.0, jnp.where(eq, jnp.where(eq_rank <= need, 1.0, 0.0), 0.0))
    pos = _prefix_count(sel)
    slot_ref[0] = jnp.where(sel > 0.0, pos - 1.0, -1.0)


def _route(aff_t, capacity):
    b, e, s = aff_t.shape
    return pl.pallas_call(
        functools.partial(_route_kernel, capacity),
        grid=(b,),
        in_specs=[pl.BlockSpec((1, e, s), lambda i: (i, 0, 0))],
        out_specs=pl.BlockSpec((1, e, s), lambda i: (i, 0, 0)),
        out_shape=jax.ShapeDtypeStruct((b, e, s), F32),
        compiler_params=_params(("parallel",)),
        name="route",
    )(aff_t)


def _dispatch_kernel(slot_ref, aff_ref, h_ref, xg_ref, gate_ref):
    cap = xg_ref.shape[2]
    slot = slot_ref[0, 0]
    hit = lax.broadcasted_iota(jnp.int32, (cap, slot.shape[1]), 0).astype(F32) == slot
    onehot = jnp.where(hit, 1.0, 0.0).astype(BF16)
    xg_ref[0, 0] = _dot(onehot, h_ref[0]).astype(xg_ref.dtype)
    gate = jnp.sum(jnp.where(hit, aff_ref[0, 0], 0.0), axis=1, keepdims=True)
    gate_ref[0, 0] = jnp.broadcast_to(gate, gate_ref.shape[2:])


def _dispatch(slot, aff_t, h2, capacity):
    b, e, s = slot.shape
    d = h2.shape[-1]
    return pl.pallas_call(
        _dispatch_kernel,
        grid=(b, e),
        in_specs=[pl.BlockSpec((1, 1, 1, s), lambda i, j: (i, j, 0, 0)),
                  pl.BlockSpec((1, 1, 1, s), lambda i, j: (i, j, 0, 0)),
                  pl.BlockSpec((1, s, d), lambda i, j: (i, 0, 0))],
        out_specs=[pl.BlockSpec((1, 1, capacity, d), lambda i, j: (j, i, 0, 0)),
                   pl.BlockSpec((1, 1, capacity, LANES), lambda i, j: (j, i, 0, 0))],
        out_shape=[jax.ShapeDtypeStruct((e, b, capacity, d), BF16),
                   jax.ShapeDtypeStruct((e, b, capacity, LANES), F32)],
        compiler_params=_params(("parallel", "parallel")),
        name="dispatch",
    )(slot.reshape(b, e, 1, s), aff_t.reshape(b, e, 1, s), h2)


def _ffn_kernel(tm, n_f, x_ref, gate_ref, wg_ref, wu_ref, wd_ref, y_ref, acc_ref):
    f = pl.program_id(1)
    w_gate = wg_ref[0].astype(BF16)
    w_up = wu_ref[0].astype(BF16)
    w_down = wd_ref[0].astype(BF16)
    m = x_ref.shape[1]
    d = x_ref.shape[2]

    def step(first, final):
        for i in range(m // tm):
            rows = pl.ds(i * tm, tm)
            x = x_ref[0, rows, :]
            act = (_silu(_dot(x, w_gate)) * _dot(x, w_up)).astype(BF16)
            part = _dot(act, w_down)
            if not first:
                part = acc_ref[rows, :] + part
            if final:
                gate = jnp.concatenate([gate_ref[0, rows, :]] * (d // LANES), axis=1)
                y_ref[0, rows, :] = (part * gate).astype(y_ref.dtype)
            else:
                acc_ref[rows, :] = part

    if n_f == 1:
        step(True, True)
    else:
        pl.when(f == 0)(lambda: step(True, False))
        if n_f > 2:
            pl.when(jnp.logical_and(f > 0, f < n_f - 1))(lambda: step(False, False))
        pl.when(f == n_f - 1)(lambda: step(False, True))


def _ffn(xg, gate, w_gate, w_up, w_down, tf, tm):
    e, m, d = xg.shape
    f = w_gate.shape[2]
    return pl.pallas_call(
        functools.partial(_ffn_kernel, tm, f // tf),
        grid=(e, f // tf),
        in_specs=[pl.BlockSpec((1, m, d), lambda i, j: (i, 0, 0)),
                  pl.BlockSpec((1, m, LANES), lambda i, j: (i, 0, 0)),
                  pl.BlockSpec((1, d, tf), lambda i, j: (i, 0, j)),
                  pl.BlockSpec((1, d, tf), lambda i, j: (i, 0, j)),
                  pl.BlockSpec((1, tf, d), lambda i, j: (i, j, 0))],
        out_specs=pl.BlockSpec((1, m, d), lambda i, j: (i, 0, 0)),
        out_shape=jax.ShapeDtypeStruct((e, m, d), BF16),
        scratch_shapes=[pltpu.VMEM((m, d), F32)],
        compiler_params=_params(("parallel", "arbitrary")),
        name="ffn",
    )(xg, gate, w_gate, w_up, w_down)


def _combine_kernel(slot_ref, y_ref, x1_ref, gain_ref, o_ref):
    n_e, cap = y_ref.shape[0], y_ref.shape[2]
    tt = x1_ref.shape[1]
    acc = x1_ref[0]
    row = lax.broadcasted_iota(jnp.int32, (cap, tt), 0).astype(F32)
    for e in range(n_e):
        onehot = jnp.where(row == slot_ref[0, e:e + 1, :], 1.0, 0.0).astype(BF16)
        acc = acc + _dot_tn(onehot, y_ref[e, 0])
    o_ref[0] = _rms(acc, gain_ref[...])


def _combine(slot, y, x1, gain, tt):
    b, e, s = slot.shape
    cap, d = y.shape[2], y.shape[3]
    return pl.pallas_call(
        _combine_kernel,
        grid=(b, s // tt),
        in_specs=[pl.BlockSpec((1, e, tt), lambda i, j: (i, 0, j)),
                  pl.BlockSpec((e, 1, cap, d), lambda i, j: (0, i, 0, 0)),
                  pl.BlockSpec((1, tt, d), lambda i, j: (i, j, 0)),
                  pl.BlockSpec((1, d), lambda i, j: (0, 0))],
        out_specs=pl.BlockSpec((1, tt, d), lambda i, j: (i, j, 0)),
        out_shape=jax.ShapeDtypeStruct((b, s, d), F32),
        compiler_params=_params(("parallel", "parallel")),
        name="combine",
    )(slot, y, x1, gain)


def _layer(x, cos, sin, norm_mix, w_in, ret_decay_fwd, ret_decay_bwd, ret_norm,
           gla_gate_w_fwd, gla_gate_b_fwd, gla_gate_w_bwd, gla_gate_b_bwd, gla_norm,
           w_branch_ret, w_branch_gla, w_out, norm_ffn, w_router, w_gate, w_up, w_down, norm_out):
    b, s, d = x.shape
    t = b * s
    ret_qk, ret_v = RET_HEADS * RET_DK, RET_HEADS * RET_DV
    gla_qk, gla_v = GLA_HEADS * GLA_DK, GLA_HEADS * GLA_DV
    rank = GLA_GATE_RANK
    gla0 = 2 * ret_qk + 2 * ret_v
    ga0 = gla0 + 2 * gla_qk + 2 * gla_v
    assert w_in.shape == (d, ga0 + 2 * rank + 2 * d)
    assert s % CHUNK == 0 and 3 * GATE_COPY <= LANES and ga0 % LANES == 0

    w_in_t = w_in.T
    w_gl = w_in_t[ga0 + 2 * rank:, :].astype(BF16)

    gw = jnp.zeros((GLA_HEADS, GATE_COPY, 2 * GLA_DK), F32)
    gw = gw.at[:, :rank, :GLA_DK].set(gla_gate_w_fwd.reshape(rank, GLA_HEADS, GLA_DK).transpose(1, 0, 2))
    gw = gw.at[:, rank:, GLA_DK:].set(gla_gate_w_bwd.reshape(rank, GLA_HEADS, GLA_DK).transpose(1, 0, 2))
    gw_hi = gw.astype(BF16)
    gw_lo = (gw - gw_hi.astype(F32)).astype(BF16)
    gate_w = jnp.concatenate([gw_hi, gw_lo, gw_hi, jnp.zeros_like(gw_hi)], axis=1)
    gate_b = jnp.concatenate([gla_gate_b_fwd.reshape(GLA_HEADS, 1, GLA_DK),
                              gla_gate_b_bwd.reshape(GLA_HEADS, 1, GLA_DK)], axis=2)

    dec = jnp.stack([ret_decay_fwd, ret_decay_bwd], axis=1)[:, :, None]
    dec = jnp.pad(jnp.broadcast_to(dec, (RET_HEADS, 2, RET_DV)), ((0, 0), (0, 6), (0, 0)))

    x2 = x.reshape(t, d)
    ret, h3 = _retention(x, norm_mix[None, :], w_in_t, 0, cos, sin, dec, ret_norm[None, :])
    gla = _gla(h3, w_in_t, gla0, gate_w, gate_b, gla_norm[None, :])
    h = h3.reshape(t, d)

    n_e = w_router.shape[1]
    w_r = jnp.pad(w_router, ((0, 0), (0, LANES - n_e)))
    wr_hi = w_r.astype(BF16)
    wr_lo = (w_r - wr_hi.astype(F32)).astype(BF16)
    x1, h2, aff_t = _merge(h, ret.reshape(t, ret_v), gla.reshape(t, gla_v), x2, w_gl,
                           w_branch_ret.astype(BF16), w_branch_gla.astype(BF16), w_out.astype(BF16),
                           norm_ffn[None, :], jnp.concatenate([wr_hi, wr_lo], axis=1), n_e, b,
                           _pick(s, (512, 256, 128)))

    capacity = EC_CAPACITY_FACTOR * s // n_e
    slot = _route(aff_t, capacity)
    xg, gate = _dispatch(slot, aff_t, h2.reshape(b, s, d), capacity)
    f = w_gate.shape[2]
    y = _ffn(xg.reshape(n_e, b * capacity, d), gate.reshape(n_e, b * capacity, LANES),
             w_gate, w_up, w_down, _pick(f, (256, 128)), _pick(b * capacity, (1024, 512, 256, 128)))
    return _combine(slot, y.reshape(n_e, b, capacity, d), x1.reshape(b, s, d), norm_out,
                    _pick(s, (512, 256, 128)))


def kernel(x, positions, norm_mix, w_in, ret_decay_fwd, ret_decay_bwd, ret_norm, gla_gate_w_fwd,
           gla_gate_b_fwd, gla_gate_w_bwd, gla_gate_b_bwd, gla_norm, w_branch_ret, w_branch_gla,
           w_out, norm_ffn, w_router, w_gate, w_up, w_down, norm_final):
    depth = norm_mix.shape[0]
    assert depth == 1, "the final RMSNorm is fused into the last layer's combine stage"
    cos, sin = _rope_table(positions, RET_DK)
    return _layer(x, cos, sin, norm_mix[0], w_in[0], ret_decay_fwd[0], ret_decay_bwd[0], ret_norm[0],
                  gla_gate_w_fwd[0], gla_gate_b_fwd[0], gla_gate_w_bwd[0], gla_gate_b_bwd[0], gla_norm[0],
                  w_branch_ret[0], w_branch_gla[0], w_out[0], norm_ffn[0], w_router[0],
                  w_gate[0], w_up[0], w_down[0], norm_final[None, :])
```

```python
import functools

import jax
import jax.numpy as jnp
from jax import lax
from jax.experimental import pallas as pl
from jax.experimental.pallas import tpu as pltpu

F32 = jnp.float32
BF16 = jnp.bfloat16

RET_HEADS = 4
RET_DK = 128
RET_DV = 256
GLA_HEADS = 4
GLA_DK = 128
GLA_DV = 256
GLA_GATE_RANK = 16
GLA_GATE_NORMALIZER = 16.0
CHUNK = 128
EC_CAPACITY_FACTOR = 2
ROPE_THETA = 10000.0
NORM_EPS = 1e-6
LOG2_E = 1.4426950408889634

CHUNK_UNROLL = 8
LANES = 128
VMEM_LIMIT = 56 << 20


def _params(sem, vmem=VMEM_LIMIT):
    return pltpu.CompilerParams(dimension_semantics=sem, vmem_limit_bytes=vmem)


def _resident(shape):
    return pl.BlockSpec(shape, lambda *_: (0,) * len(shape), pipeline_mode=pl.Buffered(1))


def _pick(n, prefs):
    for p in prefs:
        if n % p == 0:
            return p
    return n


def _sigmoid(x):
    return 1.0 / (1.0 + jnp.exp(-x))


def _silu(x):
    half = 0.5 * x
    return half + half * jnp.tanh(half)


def _log_sigmoid(x):
    return jnp.minimum(x, 0.0) - jnp.log(1.0 + jnp.exp(-jnp.abs(x)))


def _rms(x, gain):
    return x * lax.rsqrt(jnp.mean(x * x, axis=-1, keepdims=True) + NORM_EPS) * gain


def _dot(a, b):
    return jnp.dot(a, b, preferred_element_type=F32)


def _dot_nt(a, b):
    return lax.dot_general(a, b, (((1,), (1,)), ((), ())), preferred_element_type=F32)


def _dot_tn(a, b):
    return lax.dot_general(a, b, (((0,), (0,)), ((), ())), preferred_element_type=F32)


def _split2(x):
    hi = x.astype(BF16)
    lo = (x - hi.astype(F32)).astype(BF16)
    return hi, lo


def _rope_table_kernel(pos_ref, freq_ref, shift_ref, cos_ref, sin_ref):
    half = freq_ref.shape[1] // 2
    ang = pos_ref[0].astype(F32) * freq_ref[...]
    tab = jnp.cos(ang - shift_ref[...])
    swapped = pltpu.roll(tab, half, 1)
    lower = lax.broadcasted_iota(jnp.int32, tab.shape, 1) < half
    cos_ref[0] = jnp.where(lower, tab, swapped)
    sin_ref[0] = jnp.where(lower, -swapped, tab)


def _rope_table(positions, dk):
    b, s = positions.shape
    half = jnp.arange(0, dk, 2, dtype=F32) / dk
    inv_freq = ROPE_THETA ** (-half)
    freq = jnp.concatenate([inv_freq, inv_freq])[None, :]
    shift = jnp.concatenate([jnp.zeros(dk // 2, F32), jnp.full(dk // 2, jnp.pi / 2, F32)])[None, :]
    out = jax.ShapeDtypeStruct((b, s, dk), F32)
    return pl.pallas_call(
        _rope_table_kernel,
        grid=(b,),
        in_specs=[pl.BlockSpec((1, s, 1), lambda i: (i, 0, 0)),
                  pl.BlockSpec((1, dk), lambda i: (0, 0)),
                  pl.BlockSpec((1, dk), lambda i: (0, 0))],
        out_specs=[pl.BlockSpec((1, s, dk), lambda i: (i, 0, 0))] * 2,
        out_shape=[out, out],
        compiler_params=_params(("parallel",)),
        name="rope_table",
    )(positions[:, :, None], freq, shift)


def _rope(t, cos, sin_signed):
    return t * cos + pltpu.roll(t, t.shape[-1] // 2, 1) * sin_signed


def _prefix_states(n_chunks, dk, dv, decay_f, decay_b, kv_ref, st_ref):
    def fwd(n, s):
        st_ref[n, 0:dk, :] = s.astype(BF16)
        return decay_f(n) * s + kv_ref[n, 0:dk, :]

    def bwd(i, s):
        n = n_chunks - 1 - i
        st_ref[n, dk:2 * dk, :] = s.astype(BF16)
        return decay_b(n) * s + kv_ref[n, dk:2 * dk, :]

    zero = jnp.zeros((dk, dv), F32)
    lax.fori_loop(0, n_chunks, fwd, zero)
    lax.fori_loop(0, n_chunks, bwd, zero)


def _chunk_rows(n):
    return pl.ds(pl.multiple_of(n * CHUNK, CHUNK), CHUNK)


def _for_row_tiles(s_len, body):
    tile = _pick(s_len, (512, 256, CHUNK))

    def step(i, carry):
        body(pl.ds(pl.multiple_of(i * tile, tile), tile))
        return carry

    lax.fori_loop(0, s_len // tile, step, 0)


def _retention_kernel(x_ref, mix_gain_ref, wq_ref, wk_ref, wv_ref, wg_ref, cos_ref, sin_ref, dec_ref,
                      gain_ref, o_ref, h_ref, qr_ref, kr_ref, v_ref, g_ref, kv_ref, st_ref):
    c = CHUNK
    s_len, dk, dv = qr_ref.shape[0], qr_ref.shape[1], v_ref.shape[1]
    n_chunks = s_len // c
    scale = dk ** -0.5

    @pl.when(pl.program_id(1) == 0)
    def _():
        def norm(rows):
            h_ref[0, rows, :] = _rms(x_ref[0, rows, :], mix_gain_ref[...]).astype(h_ref.dtype)
        _for_row_tiles(s_len, norm)

    w_qk = jnp.concatenate([wq_ref[...], wk_ref[...]], axis=0).astype(BF16)
    w_v = wv_ref[...].astype(BF16)
    w_g = wg_ref[...].astype(BF16)

    def project(rows):
        h = h_ref[0, rows, :]
        qk = _dot_nt(h, w_qk)
        cos, sin = cos_ref[0, rows, :], sin_ref[0, rows, :]
        qr_ref[rows, :] = _rope(qk[:, :dk], cos, sin) * scale
        kr_ref[rows, :] = _rope(qk[:, dk:], cos, sin)
        v_ref[rows, :] = _dot_nt(h, w_v).astype(BF16)
        g_ref[rows, :] = _dot_nt(h, w_g)

    _for_row_tiles(s_len, project)

    lg_f = _log_sigmoid(dec_ref[0, 0:1, :])
    lg_b = _log_sigmoid(dec_ref[0, 1:2, :])
    lgf_k, lgb_k = lg_f[:, :dk], lg_b[:, :dk]
    pos = lax.broadcasted_iota(jnp.int32, (c, dk), 0).astype(F32)
    wq_f = jnp.exp((pos + 1.0) * lgf_k)
    wk_f = jnp.exp((c - 1.0 - pos) * lgf_k)
    wq_b = jnp.exp((c - pos) * lgb_k)
    wk_b = jnp.exp(pos * lgb_k)
    ri = lax.broadcasted_iota(jnp.int32, (c, c), 0)
    ci = lax.broadcasted_iota(jnp.int32, (c, c), 1)
    lower = ri >= ci
    rel = (ri - ci).astype(F32)
    decay_mask = jnp.where(lower,
                           jnp.exp(jnp.where(lower, rel, 0.0) * lg_f[:, :c]),
                           jnp.exp(jnp.where(lower, 0.0, -rel) * lg_b[:, :c]))
    chunk_decay_f = jnp.exp(c * lg_f)
    chunk_decay_b = jnp.exp(c * lg_b)

    def phase1(n, carry):
        rows = _chunk_rows(n)
        kr = kr_ref[rows, :]
        kw = jnp.concatenate([kr * wk_f, kr * wk_b], axis=1).astype(BF16)
        kv_ref[n] = _dot_tn(kw, v_ref[rows, :])
        return carry

    lax.fori_loop(0, n_chunks, phase1, 0, unroll=CHUNK_UNROLL)
    _prefix_states(n_chunks, dk, dv, lambda n: chunk_decay_f, lambda n: chunk_decay_b, kv_ref, st_ref)

    def phase2(n, carry):
        rows = _chunk_rows(n)
        qr, kr = qr_ref[rows, :], kr_ref[rows, :]
        scores = _dot_nt(qr.astype(BF16), kr.astype(BF16)) * decay_mask
        intra = _dot(scores.astype(BF16), v_ref[rows, :])
        qw = jnp.concatenate([qr * wq_f, qr * wq_b], axis=1).astype(BF16)
        o = intra + _dot(qw, st_ref[n])
        mu = jnp.mean(o, axis=-1, keepdims=True)
        d = o - mu
        var = jnp.mean(d * d, axis=-1, keepdims=True)
        y = d * lax.rsqrt(var + NORM_EPS) * gain_ref[...]
        o_ref[0, rows, :] = (y * _silu(g_ref[rows, :])).astype(o_ref.dtype)
        return carry

    lax.fori_loop(0, n_chunks, phase2, 0, unroll=CHUNK_UNROLL)


def _retention(x, mix_gain, w_in, col0, cos, sin, dec, gain):
    b, s, d = x.shape
    h, dk, dv = RET_HEADS, RET_DK, RET_DV
    q0 = col0 // dk
    v0 = (col0 + 2 * h * dk) // dv
    n_chunks = s // CHUNK
    return pl.pallas_call(
        _retention_kernel,
        grid=(b, h),
        in_specs=[pl.BlockSpec((1, s, d), lambda i, j: (i, 0, 0)),
                  pl.BlockSpec((1, d), lambda i, j: (0, 0)),
                  pl.BlockSpec((dk, d), lambda i, j: (q0 + j, 0)),
                  pl.BlockSpec((dk, d), lambda i, j: (q0 + h + j, 0)),
                  pl.BlockSpec((dv, d), lambda i, j: (v0 + j, 0)),
                  pl.BlockSpec((dv, d), lambda i, j: (v0 + h + j, 0)),
                  pl.BlockSpec((1, s, dk), lambda i, j: (i, 0, 0)),
                  pl.BlockSpec((1, s, dk), lambda i, j: (i, 0, 0)),
                  pl.BlockSpec((1, 8, dv), lambda i, j: (j, 0, 0)),
                  pl.BlockSpec((1, dv), lambda i, j: (0, j))],
        out_specs=[pl.BlockSpec((1, s, dv), lambda i, j: (i, 0, j)),
                   pl.BlockSpec((1, s, d), lambda i, j: (i, 0, 0))],
        out_shape=[jax.ShapeDtypeStruct((b, s, h * dv), BF16),
                   jax.ShapeDtypeStruct((b, s, d), BF16)],
        scratch_shapes=[pltpu.VMEM((s, dk), F32),
                        pltpu.VMEM((s, dk), F32),
                        pltpu.VMEM((s, dv), BF16),
                        pltpu.VMEM((s, dv), F32),
                        pltpu.VMEM((n_chunks, 2 * dk, dv), F32),
                        pltpu.VMEM((n_chunks, 2 * dk, dv), BF16)],
        compiler_params=_params(("parallel", "arbitrary")),
        name="retention",
    )(x, mix_gain, w_in, w_in, w_in, w_in, cos, sin, dec, gain)


GATE_COPY = 2 * GLA_GATE_RANK


def _cumsum_dot(tri2, x):
    hi, lo = _split2(x)
    return _dot(tri2, jnp.concatenate([hi, lo], axis=0))


def _gla_kernel(h_ref, wq_ref, wk_ref, wv_ref, wg_ref, wa_ref, gw_ref, gb_ref, gain_ref, o_ref,
                q_ref, k_ref, v_ref, g_ref, ga_ref, la_ref, cumf_ref, cumb_ref, kv_ref, st_ref):
    c = CHUNK
    s_len, dk, dv = q_ref.shape[0], q_ref.shape[1], v_ref.shape[1]
    n_chunks = s_len // c
    scale = dk ** -0.5
    inv_norm = LOG2_E / GLA_GATE_NORMALIZER

    w_qk = jnp.concatenate([wq_ref[...], wk_ref[...]], axis=0).astype(BF16)
    w_v = wv_ref[...].astype(BF16)
    w_g = wg_ref[...].astype(BF16)

    @pl.when(pl.program_id(1) == 0)
    def _():
        w_a = wa_ref[...].astype(BF16)

        def gate_inputs(rows):
            x = _dot_nt(h_ref[0, rows, :], w_a)
            x_hi = x.astype(BF16).astype(F32)
            lane = lax.broadcasted_iota(jnp.int32, x.shape, 1)
            ga = jnp.where(lane < GATE_COPY, x_hi,
                           jnp.where(lane < 2 * GATE_COPY, pltpu.roll(x_hi, GATE_COPY, 1),
                                     jnp.where(lane < 3 * GATE_COPY, pltpu.roll(x - x_hi, 2 * GATE_COPY, 1),
                                               0.0)))
            ga_ref[rows, :] = ga.astype(BF16)

        _for_row_tiles(s_len, gate_inputs)

    def project(rows):
        h = h_ref[0, rows, :]
        qk = _dot_nt(h, w_qk)
        q_ref[rows, :] = qk[:, :dk] * scale
        k_ref[rows, :] = qk[:, dk:]
        v_ref[rows, :] = _dot_nt(h, w_v).astype(BF16)
        g_ref[rows, :] = _dot_nt(h, w_g)
        la_ref[rows, :] = _log_sigmoid(_dot(ga_ref[rows, :], gw_ref[0]) + gb_ref[0]) * inv_norm

    _for_row_tiles(s_len, project)

    ri = lax.broadcasted_iota(jnp.int32, (c, c), 0)
    ci = lax.broadcasted_iota(jnp.int32, (c, c), 1)
    lower = ri >= ci
    tri_lower = jnp.where(lower, 1.0, 0.0).astype(BF16)
    tri_upper = jnp.where(ci >= ri, 1.0, 0.0).astype(BF16)
    tri2_lower = jnp.concatenate([tri_lower, tri_lower], axis=1)
    tri2_upper = jnp.concatenate([tri_upper, tri_upper], axis=1)

    def cumulate(n, carry):
        rows = _chunk_rows(n)
        cumf_ref[rows, :] = _cumsum_dot(tri2_lower, la_ref[rows, 0:dk])
        cumb_ref[rows, :] = _cumsum_dot(tri2_upper, la_ref[rows, dk:2 * dk])
        return carry

    lax.fori_loop(0, n_chunks, cumulate, 0, unroll=CHUNK_UNROLL)

    def phase1(n, carry):
        rows = _chunk_rows(n)
        cum_f, cum_b = cumf_ref[rows, :], cumb_ref[rows, :]
        last_f = cum_f[c - 1:c, :]
        last_b = cum_b[0:1, :]
        k = k_ref[rows, :]
        kw = jnp.concatenate([k * jnp.exp2(last_f - cum_f), k * jnp.exp2(last_b - cum_b)], axis=1).astype(BF16)
        kv_ref[n] = _dot_tn(v_ref[rows, :], kw)
        return carry

    lax.fori_loop(0, n_chunks, phase1, 0, unroll=CHUNK_UNROLL)

    def fwd(n, s):
        st_ref[n, :, 0:dk] = s.astype(BF16)
        last = cumf_ref[pl.ds(pl.multiple_of(n * c, c) + (c - 1), 1), :]
        return jnp.exp2(last) * s + kv_ref[n, :, 0:dk]

    def bwd(i, s):
        n = n_chunks - 1 - i
        st_ref[n, :, dk:2 * dk] = s.astype(BF16)
        last = cumb_ref[pl.ds(pl.multiple_of(n * c, c), 1), :]
        return jnp.exp2(last) * s + kv_ref[n, :, dk:2 * dk]

    zero = jnp.zeros((dv, dk), F32)
    lax.fori_loop(0, n_chunks, fwd, zero)
    lax.fori_loop(0, n_chunks, bwd, zero)

    def phase2(n, carry):
        rows = _chunk_rows(n)
        cum_f, cum_b = cumf_ref[rows, :], cumb_ref[rows, :]
        ref_f = cum_f[c // 2:c // 2 + 1, :]
        ref_b = cum_b[c // 2 - 1:c // 2, :]
        q = q_ref[rows, :]
        k = k_ref[rows, :]
        q_f = q * jnp.exp2(cum_f - ref_f)
        q_b = q * jnp.exp2(cum_b - ref_b)
        s_f = _dot_nt(q_f.astype(BF16), (k * jnp.exp2(ref_f - cum_f)).astype(BF16))
        s_b = _dot_nt(q_b.astype(BF16), (k * jnp.exp2(ref_b - cum_b)).astype(BF16))
        scores = jnp.where(lower, s_f, s_b)
        intra = _dot(scores.astype(BF16), v_ref[rows, :])
        qw = jnp.concatenate([q_f * jnp.exp2(ref_f), q_b * jnp.exp2(ref_b)], axis=1).astype(BF16)
        o = intra + _dot_nt(qw, st_ref[n])
        y = _rms(o, gain_ref[...])
        o_ref[0, rows, :] = (y * _silu(g_ref[rows, :])).astype(o_ref.dtype)
        return carry

    lax.fori_loop(0, n_chunks, phase2, 0, unroll=CHUNK_UNROLL)


def _gla(h3, w_in, col0, gate_w, gate_b, gain):
    b, s, d = h3.shape
    h, dk, dv = GLA_HEADS, GLA_DK, GLA_DV
    q0 = col0 // dk
    v0 = (col0 + 2 * h * dk) // dv
    a0 = (col0 + 2 * h * dk + 2 * h * dv) // LANES
    n_chunks = s // CHUNK
    return pl.pallas_call(
        _gla_kernel,
        grid=(b, h),
        in_specs=[pl.BlockSpec((1, s, d), lambda i, j: (i, 0, 0)),
                  pl.BlockSpec((dk, d), lambda i, j: (q0 + j, 0)),
                  pl.BlockSpec((dk, d), lambda i, j: (q0 + h + j, 0)),
                  pl.BlockSpec((dv, d), lambda i, j: (v0 + j, 0)),
                  pl.BlockSpec((dv, d), lambda i, j: (v0 + h + j, 0)),
                  pl.BlockSpec((LANES, d), lambda i, j: (a0, 0)),
                  pl.BlockSpec((1, LANES, 2 * dk), lambda i, j: (j, 0, 0)),
                  pl.BlockSpec((1, 1, 2 * dk), lambda i, j: (j, 0, 0)),
                  pl.BlockSpec((1, dv), lambda i, j: (0, j))],
        out_specs=pl.BlockSpec((1, s, dv), lambda i, j: (i, 0, j)),
        out_shape=jax.ShapeDtypeStruct((b, s, h * dv), BF16),
        scratch_shapes=[pltpu.VMEM((s, dk), F32),
                        pltpu.VMEM((s, dk), F32),
                        pltpu.VMEM((s, dv), BF16),
                        pltpu.VMEM((s, dv), F32),
                        pltpu.VMEM((s, LANES), BF16),
                        pltpu.VMEM((s, 2 * dk), F32),
                        pltpu.VMEM((s, dk), F32),
                        pltpu.VMEM((s, dk), F32),
                        pltpu.VMEM((n_chunks, dv, 2 * dk), F32),
                        pltpu.VMEM((n_chunks, dv, 2 * dk), BF16)],
        compiler_params=_params(("parallel", "arbitrary")),
        name="gla",
    )(h3, w_in, w_in, w_in, w_in, w_in, gate_w, gate_b, gain)


MERGE_COLS = 256


def _merge_kernel(n_experts, h_ref, ret_ref, gla_ref, x_ref, wgl_ref, wr_ref, wg_ref, wo_ref, gain_ref,
                  wr2_ref, x1_ref, h2_ref, aff_ref):
    d = x_ref.shape[1]
    h, ret, gla = h_ref[...], ret_ref[...], gla_ref[...]
    blocks = []
    for j in range(0, d, MERGE_COLS):
        cols = slice(j, j + MERGE_COLS)
        cols_gla = slice(d + j, d + j + MERGE_COLS)
        m = (_sigmoid(_dot_nt(h, wgl_ref[cols, :])) * _dot(ret, wr_ref[:, cols])
             + _sigmoid(_dot_nt(h, wgl_ref[cols_gla, :])) * _dot(gla, wg_ref[:, cols]))
        blocks.append(m.astype(BF16))
    x1 = x_ref[...] + _dot(jnp.concatenate(blocks, axis=1), wo_ref[...])
    x1_ref[...] = x1
    h2 = _rms(x1, gain_ref[...]).astype(BF16)
    h2_ref[...] = h2
    logits2 = _dot(h2, wr2_ref[...])
    logits = logits2[:, :LANES] + logits2[:, LANES:]
    lane = lax.broadcasted_iota(jnp.int32, logits.shape, 1)
    logits = jnp.where(lane < n_experts, logits, -jnp.inf)
    p = jnp.exp(logits - jnp.max(logits, axis=-1, keepdims=True))
    aff = p / jnp.sum(p, axis=-1, keepdims=True)
    aff_ref[0] = aff.T[0:n_experts, :]


def _merge(h, ret, gla, x2, w_gl, w_ret, w_gla, w_out, gain, w_router2, n_experts, batch, tm):
    t, d = x2.shape
    s = t // batch
    per_b = s // tm
    rows = lambda width: pl.BlockSpec((tm, width), lambda i: (i, 0))
    return pl.pallas_call(
        functools.partial(_merge_kernel, n_experts),
        grid=(t // tm,),
        in_specs=[rows(d), rows(ret.shape[1]), rows(gla.shape[1]), rows(d),
                  _resident(w_gl.shape), _resident(w_ret.shape), _resident(w_gla.shape),
                  _resident(w_out.shape), _resident((1, d)),
                  _resident(w_router2.shape)],
        out_specs=[rows(d), rows(d),
                   pl.BlockSpec((1, n_experts, tm), lambda i: (i // per_b, 0, i % per_b))],
        out_shape=[jax.ShapeDtypeStruct((t, d), F32),
                   jax.ShapeDtypeStruct((t, d), BF16),
                   jax.ShapeDtypeStruct((batch, n_experts, s), F32)],
        compiler_params=_params(("parallel",)),
        name="merge",
    )(h, ret, gla, x2, w_gl, w_ret, w_gla, w_out, gain, w_router2)


def _prefix_count(mask):
    s = mask.shape[1]
    ri = lax.broadcasted_iota(jnp.int32, (LANES, LANES), 0)
    ci = lax.broadcasted_iota(jnp.int32, (LANES, LANES), 1)
    tri = jnp.where(ri <= ci, 1.0, 0.0).astype(BF16)
    off = jnp.zeros((mask.shape[0], 1), F32)
    parts = []
    for j in range(s // LANES):
        p = _dot(mask[:, j * LANES:(j + 1) * LANES].astype(BF16), tri) + off
        parts.append(p)
        off = p[:, LANES - 1:LANES]
    return jnp.concatenate(parts, axis=1)


def _route_kernel(capacity, aff_ref, slot_ref):
    a = aff_ref[0]
    bits = lax.bitcast_convert_type(a, jnp.int32)
    n_e = a.shape[0]
    cap = float(capacity)

    def count(pred):
        return jnp.sum(jnp.where(pred, 1.0, 0.0), axis=1, keepdims=True)

    def search(_, c):
        lo, hi = c
        mid = lo + lax.shift_right_logical(hi - lo, 1)
        ok = count(bits >= mid) >= cap
        return jnp.where(ok, mid, lo), jnp.where(ok, hi, mid)

    lo0 = jnp.zeros((n_e, 1), jnp.int32)
    hi0 = jnp.full((n_e, 1), 0x7F800000, jnp.int32)
    thr_bits, _ = lax.fori_loop(0, 31, search, (lo0, hi0))
    thr0 = jnp.max(jnp.where(bits <= thr_bits, a, -1.0), axis=1, keepdims=True)

    def counts(v):
        return count(a >= v), count(a > v)

    def unsettled(state):
        _, c_ge, c_gt = state
        bad = jnp.where(c_ge < cap, 1.0, jnp.where(c_gt >= cap, 1.0, 0.0))
        return jnp.max(bad, axis=0, keepdims=True)[0, 0] > 0.0

    def step(state):
        v, c_ge, c_gt = state
        below = jnp.max(jnp.where(a < v, a, -1.0), axis=1, keepdims=True)
        above = jnp.min(jnp.where(a > v, a, 2.0), axis=1, keepdims=True)
        v = jnp.where(c_ge < cap, below, jnp.where(c_gt >= cap, above, v))
        return (v,) + counts(v)

    thr, _, n_gt = lax.while_loop(unsettled, step, (thr0,) + counts(thr0))

    gt = a > thr
    eq = a == thr
    need = cap - n_gt
    eq_rank = _prefix_count(jnp.where(eq, 1.0, 0.0))
    sel = jnp.where(gt, 1.0, jnp.where(eq, jnp.where(eq_rank <= need, 1.0, 0.0), 0.0))
    pos = _prefix_count(sel)
    slot_ref[0] = jnp.where(sel > 0.0, pos - 1.0, -1.0)


def _route(aff_t, capacity):
    b, e, s = aff_t.shape
    return pl.pallas_call(
        functools.partial(_route_kernel, capacity),
        grid=(1,),
        in_specs=[pl.BlockSpec((1, b * e, s), lambda i: (0, 0, 0))],
        out_specs=pl.BlockSpec((1, b * e, s), lambda i: (0, 0, 0)),
        out_shape=jax.ShapeDtypeStruct((1, b * e, s), F32),
        compiler_params=_params(("arbitrary",)),
        name="route",
    )(aff_t.reshape(1, b * e, s)).reshape(b, e, s)


def _dispatch_kernel(slot_ref, aff_ref, h_ref, xg_ref, gate_ref):
    cap = xg_ref.shape[2]
    slot = slot_ref[0, 0]
    hit = lax.broadcasted_iota(jnp.int32, (cap, slot.shape[1]), 0).astype(F32) == slot
    onehot = jnp.where(hit, 1.0, 0.0).astype(BF16)
    xg_ref[0, 0] = _dot(onehot, h_ref[0]).astype(xg_ref.dtype)
    gate = jnp.sum(jnp.where(hit, aff_ref[0, 0], 0.0), axis=1, keepdims=True)
    gate_ref[0, 0] = jnp.broadcast_to(gate, gate_ref.shape[2:])


def _dispatch(slot, aff_t, h2, capacity):
    b, e, s = slot.shape
    d = h2.shape[-1]
    return pl.pallas_call(
        _dispatch_kernel,
        grid=(b, e),
        in_specs=[pl.BlockSpec((1, 1, 1, s), lambda i, j: (i, j, 0, 0)),
                  pl.BlockSpec((1, 1, 1, s), lambda i, j: (i, j, 0, 0)),
                  pl.BlockSpec((1, s, d), lambda i, j: (i, 0, 0))],
        out_specs=[pl.BlockSpec((1, 1, capacity, d), lambda i, j: (j, i, 0, 0)),
                   pl.BlockSpec((1, 1, capacity, LANES), lambda i, j: (j, i, 0, 0))],
        out_shape=[jax.ShapeDtypeStruct((e, b, capacity, d), BF16),
                   jax.ShapeDtypeStruct((e, b, capacity, LANES), F32)],
        compiler_params=_params(("parallel", "parallel")),
        name="dispatch",
    )(slot.reshape(b, e, 1, s), aff_t.reshape(b, e, 1, s), h2)


def _ffn_kernel(tm, n_f, x_ref, gate_ref, wg_ref, wu_ref, wd_ref, y_ref, acc_ref):
    f = pl.program_id(1)
    w_gate = wg_ref[0].astype(BF16)
    w_up = wu_ref[0].astype(BF16)
    w_down = wd_ref[0].astype(BF16)
    m = x_ref.shape[1]
    d = x_ref.shape[2]

    def step(first, final):
        for i in range(m // tm):
            rows = pl.ds(i * tm, tm)
            x = x_ref[0, rows, :]
            act = (_silu(_dot(x, w_gate)) * _dot(x, w_up)).astype(BF16)
            part = _dot(act, w_down)
            if not first:
                part = acc_ref[rows, :] + part
            if final:
                gate = jnp.concatenate([gate_ref[0, rows, :]] * (d // LANES), axis=1)
                y_ref[0, rows, :] = (part * gate).astype(y_ref.dtype)
            else:
                acc_ref[rows, :] = part

    if n_f == 1:
        step(True, True)
    else:
        pl.when(f == 0)(lambda: step(True, False))
        if n_f > 2:
            pl.when(jnp.logical_and(f > 0, f < n_f - 1))(lambda: step(False, False))
        pl.when(f == n_f - 1)(lambda: step(False, True))


def _ffn(xg, gate, w_gate, w_up, w_down, tf, tm):
    e, m, d = xg.shape
    f = w_gate.shape[2]
    return pl.pallas_call(
        functools.partial(_ffn_kernel, tm, f // tf),
        grid=(e, f // tf),
        in_specs=[pl.BlockSpec((1, m, d), lambda i, j: (i, 0, 0)),
                  pl.BlockSpec((1, m, LANES), lambda i, j: (i, 0, 0)),
                  pl.BlockSpec((1, d, tf), lambda i, j: (i, 0, j)),
                  pl.BlockSpec((1, d, tf), lambda i, j: (i, 0, j)),
                  pl.BlockSpec((1, tf, d), lambda i, j: (i, j, 0))],
        out_specs=pl.BlockSpec((1, m, d), lambda i, j: (i, 0, 0)),
        out_shape=jax.ShapeDtypeStruct((e, m, d), BF16),
        scratch_shapes=[pltpu.VMEM((m, d), F32)],
        compiler_params=_params(("parallel", "arbitrary")),
        name="ffn",
    )(xg, gate, w_gate, w_up, w_down)


def _combine_kernel(slot_ref, y_ref, x1_ref, gain_ref, o_ref):
    n_e, cap = y_ref.shape[0], y_ref.shape[2]
    tt = x1_ref.shape[1]
    acc = x1_ref[0]
    row = lax.broadcasted_iota(jnp.int32, (cap, tt), 0).astype(F32)
    for e in range(n_e):
        onehot = jnp.where(row == slot_ref[0, e:e + 1, :], 1.0, 0.0).astype(BF16)
        acc = acc + _dot_tn(onehot, y_ref[e, 0])
    o_ref[0] = _rms(acc, gain_ref[...])


def _combine(slot, y, x1, gain, tt):
    b, e, s = slot.shape
    cap, d = y.shape[2], y.shape[3]
    return pl.pallas_call(
        _combine_kernel,
        grid=(b, s // tt),
        in_specs=[pl.BlockSpec((1, e, tt), lambda i, j: (i, 0, j)),
                  pl.BlockSpec((e, 1, cap, d), lambda i, j: (0, i, 0, 0)),
                  pl.BlockSpec((1, tt, d), lambda i, j: (i, j, 0)),
                  pl.BlockSpec((1, d), lambda i, j: (0, 0))],
        out_specs=pl.BlockSpec((1, tt, d), lambda i, j: (i, j, 0)),
        out_shape=jax.ShapeDtypeStruct((b, s, d), F32),
        compiler_params=_params(("parallel", "parallel")),
        name="combine",
    )(slot, y, x1, gain)


def _layer(x, cos, sin, norm_mix, w_in, ret_decay_fwd, ret_decay_bwd, ret_norm,
           gla_gate_w_fwd, gla_gate_b_fwd, gla_gate_w_bwd, gla_gate_b_bwd, gla_norm,
           w_branch_ret, w_branch_gla, w_out, norm_ffn, w_router, w_gate, w_up, w_down, norm_out):
    b, s, d = x.shape
    t = b * s
    ret_qk, ret_v = RET_HEADS * RET_DK, RET_HEADS * RET_DV
    gla_qk, gla_v = GLA_HEADS * GLA_DK, GLA_HEADS * GLA_DV
    rank = GLA_GATE_RANK
    gla0 = 2 * ret_qk + 2 * ret_v
    ga0 = gla0 + 2 * gla_qk + 2 * gla_v
    assert w_in.shape == (d, ga0 + 2 * rank + 2 * d)
    assert s % CHUNK == 0 and 3 * GATE_COPY <= LANES and ga0 % LANES == 0

    w_in_t = w_in.T
    w_gl = w_in_t[ga0 + 2 * rank:, :].astype(BF16)

    gw = jnp.zeros((GLA_HEADS, GATE_COPY, 2 * GLA_DK), F32)
    gw = gw.at[:, :rank, :GLA_DK].set(gla_gate_w_fwd.reshape(rank, GLA_HEADS, GLA_DK).transpose(1, 0, 2))
    gw = gw.at[:, rank:, GLA_DK:].set(gla_gate_w_bwd.reshape(rank, GLA_HEADS, GLA_DK).transpose(1, 0, 2))
    gw_hi = gw.astype(BF16)
    gw_lo = (gw - gw_hi.astype(F32)).astype(BF16)
    gate_w = jnp.concatenate([gw_hi, gw_lo, gw_hi, jnp.zeros_like(gw_hi)], axis=1)
    gate_b = jnp.concatenate([gla_gate_b_fwd.reshape(GLA_HEADS, 1, GLA_DK),
                              gla_gate_b_bwd.reshape(GLA_HEADS, 1, GLA_DK)], axis=2)

    dec = jnp.stack([ret_decay_fwd, ret_decay_bwd], axis=1)[:, :, None]
    dec = jnp.pad(jnp.broadcast_to(dec, (RET_HEADS, 2, RET_DV)), ((0, 0), (0, 6), (0, 0)))

    x2 = x.reshape(t, d)
    ret, h3 = _retention(x, norm_mix[None, :], w_in_t, 0, cos, sin, dec, ret_norm[None, :])
    gla = _gla(h3, w_in_t, gla0, gate_w, gate_b, gla_norm[None, :])
    h = h3.reshape(t, d)

    n_e = w_router.shape[1]
    w_r = jnp.pad(w_router, ((0, 0), (0, LANES - n_e)))
    wr_hi = w_r.astype(BF16)
    wr_lo = (w_r - wr_hi.astype(F32)).astype(BF16)
    x1, h2, aff_t = _merge(h, ret.reshape(t, ret_v), gla.reshape(t, gla_v), x2, w_gl,
                           w_branch_ret.astype(BF16), w_branch_gla.astype(BF16), w_out.astype(BF16),
                           norm_ffn[None, :], jnp.concatenate([wr_hi, wr_lo], axis=1), n_e, b,
                           _pick(s, (512, 256, 128)))

    capacity = EC_CAPACITY_FACTOR * s // n_e
    slot = _route(aff_t, capacity)
    xg, gate = _dispatch(slot, aff_t, h2.reshape(b, s, d), capacity)
    f = w_gate.shape[2]
    y = _ffn(xg.reshape(n_e, b * capacity, d), gate.reshape(n_e, b * capacity, LANES),
             w_gate, w_up, w_down, _pick(f, (256, 128)), _pick(b * capacity, (1024, 512, 256, 128)))
    return _combine(slot, y.reshape(n_e, b, capacity, d), x1.reshape(b, s, d), norm_out,
                    _pick(s, (512, 256, 128)))


def kernel(x, positions, norm_mix, w_in, ret_decay_fwd, ret_decay_bwd, ret_norm, gla_gate_w_fwd,
           gla_gate_b_fwd, gla_gate_w_bwd, gla_gate_b_bwd, gla_norm, w_branch_ret, w_branch_gla,
           w_out, norm_ffn, w_router, w_gate, w_up, w_down, norm_final):
    depth = norm_mix.shape[0]
    assert depth == 1, "the final RMSNorm is fused into the last layer's combine stage"
    cos, sin = _rope_table(positions, RET_DK)
    return _layer(x, cos, sin, norm_mix[0], w_in[0], ret_decay_fwd[0], ret_decay_bwd[0], ret_norm[0],
                  gla_gate_w_fwd[0], gla_gate_b_fwd[0], gla_gate_w_bwd[0], gla_gate_b_bwd[0], gla_norm[0],
                  w_branch_ret[0], w_branch_gla[0], w_out[0], norm_ffn[0], w_router[0],
                  w_gate[0], w_up[0], w_down[0], norm_final[None, :])
```

```python
import functools

import jax
import jax.numpy as jnp
from jax import lax
from jax.experimental import pallas as pl
from jax.experimental.pallas import tpu as pltpu

F32 = jnp.float32
BF16 = jnp.bfloat16

RET_HEADS = 4
RET_DK = 128
RET_DV = 256
GLA_HEADS = 4
GLA_DK = 128
GLA_DV = 256
GLA_GATE_RANK = 16
GLA_GATE_NORMALIZER = 16.0
CHUNK = 128
EC_CAPACITY_FACTOR = 2
ROPE_THETA = 10000.0
NORM_EPS = 1e-6
LOG2_E = 1.4426950408889634

CHUNK_UNROLL = 16
LANES = 128
VMEM_LIMIT = 56 << 20


def _params(sem, vmem=VMEM_LIMIT):
    return pltpu.CompilerParams(dimension_semantics=sem, vmem_limit_bytes=vmem)


def _resident(shape):
    return pl.BlockSpec(shape, lambda *_: (0,) * len(shape), pipeline_mode=pl.Buffered(1))


def _pick(n, prefs):
    for p in prefs:
        if n % p == 0:
            return p
    return n


def _sigmoid(x):
    return 1.0 / (1.0 + jnp.exp(-x))


def _silu(x):
    half = 0.5 * x
    return half + half * jnp.tanh(half)


def _log_sigmoid(x):
    return jnp.minimum(x, 0.0) - jnp.log(1.0 + jnp.exp(-jnp.abs(x)))


def _rms(x, gain):
    return x * lax.rsqrt(jnp.mean(x * x, axis=-1, keepdims=True) + NORM_EPS) * gain


def _dot(a, b):
    return jnp.dot(a, b, preferred_element_type=F32)


def _dot_nt(a, b):
    return lax.dot_general(a, b, (((1,), (1,)), ((), ())), preferred_element_type=F32)


def _dot_tn(a, b):
    return lax.dot_general(a, b, (((0,), (0,)), ((), ())), preferred_element_type=F32)


def _split2(x):
    hi = x.astype(BF16)
    lo = (x - hi.astype(F32)).astype(BF16)
    return hi, lo


def _rope_table_kernel(pos_ref, freq_ref, shift_ref, cos_ref, sin_ref):
    half = freq_ref.shape[1] // 2
    ang = pos_ref[0].astype(F32) * freq_ref[...]
    tab = jnp.cos(ang - shift_ref[...])
    swapped = pltpu.roll(tab, half, 1)
    lower = lax.broadcasted_iota(jnp.int32, tab.shape, 1) < half
    cos_ref[0] = jnp.where(lower, tab, swapped)
    sin_ref[0] = jnp.where(lower, -swapped, tab)


def _rope_table(positions, dk):
    b, s = positions.shape
    half = jnp.arange(0, dk, 2, dtype=F32) / dk
    inv_freq = ROPE_THETA ** (-half)
    freq = jnp.concatenate([inv_freq, inv_freq])[None, :]
    shift = jnp.concatenate([jnp.zeros(dk // 2, F32), jnp.full(dk // 2, jnp.pi / 2, F32)])[None, :]
    out = jax.ShapeDtypeStruct((b, s, dk), F32)
    return pl.pallas_call(
        _rope_table_kernel,
        grid=(b,),
        in_specs=[pl.BlockSpec((1, s, 1), lambda i: (i, 0, 0)),
                  pl.BlockSpec((1, dk), lambda i: (0, 0)),
                  pl.BlockSpec((1, dk), lambda i: (0, 0))],
        out_specs=[pl.BlockSpec((1, s, dk), lambda i: (i, 0, 0))] * 2,
        out_shape=[out, out],
        compiler_params=_params(("parallel",)),
        name="rope_table",
    )(positions[:, :, None], freq, shift)


def _rope(t, cos, sin_signed):
    return t * cos + pltpu.roll(t, t.shape[-1] // 2, 1) * sin_signed


def _prefix_states(n_chunks, dk, dv, decay_f, decay_b, kv_ref, st_ref):
    def fwd(n, s):
        st_ref[n, 0:dk, :] = s.astype(BF16)
        return decay_f(n) * s + kv_ref[n, 0:dk, :]

    def bwd(i, s):
        n = n_chunks - 1 - i
        st_ref[n, dk:2 * dk, :] = s.astype(BF16)
        return decay_b(n) * s + kv_ref[n, dk:2 * dk, :]

    zero = jnp.zeros((dk, dv), F32)
    lax.fori_loop(0, n_chunks, fwd, zero)
    lax.fori_loop(0, n_chunks, bwd, zero)


def _chunk_rows(n):
    return pl.ds(pl.multiple_of(n * CHUNK, CHUNK), CHUNK)


def _row_tile(s_len):
    return _pick(s_len, (512, 256, CHUNK))


def _per_tile(chunk_pattern, s_len):
    return jnp.concatenate([chunk_pattern] * (_row_tile(s_len) // CHUNK), axis=0)


def _for_row_tiles(s_len, body):
    tile = _row_tile(s_len)

    def step(i, carry):
        body(pl.ds(pl.multiple_of(i * tile, tile), tile))
        return carry

    lax.fori_loop(0, s_len // tile, step, 0)


def _retention_kernel(x_ref, mix_gain_ref, wq_ref, wk_ref, wv_ref, wg_ref, cos_ref, sin_ref, dec_ref,
                      gain_ref, o_ref, h_ref, qk_ref, qw_ref, kw_ref, v_ref, gg_ref, kv_ref, st_ref):
    c = CHUNK
    s_len, dk, dv = qk_ref.shape[0], qk_ref.shape[1] // 2, v_ref.shape[1]
    n_chunks = s_len // c
    scale = dk ** -0.5

    lg_f = _log_sigmoid(dec_ref[0, 0:1, :])
    lg_b = _log_sigmoid(dec_ref[0, 1:2, :])
    lgf_k, lgb_k = lg_f[:, :dk], lg_b[:, :dk]
    pos = lax.broadcasted_iota(jnp.int32, (c, dk), 0).astype(F32)
    wq_f = _per_tile(jnp.exp((pos + 1.0) * lgf_k), s_len)
    wk_f = _per_tile(jnp.exp((c - 1.0 - pos) * lgf_k), s_len)
    wq_b = _per_tile(jnp.exp((c - pos) * lgb_k), s_len)
    wk_b = _per_tile(jnp.exp(pos * lgb_k), s_len)
    ri = lax.broadcasted_iota(jnp.int32, (c, c), 0)
    ci = lax.broadcasted_iota(jnp.int32, (c, c), 1)
    lower = ri >= ci
    rel = (ri - ci).astype(F32)
    decay_mask = jnp.where(lower,
                           jnp.exp(jnp.where(lower, rel, 0.0) * lg_f[:, :c]),
                           jnp.exp(jnp.where(lower, 0.0, -rel) * lg_b[:, :c]))
    chunk_decay_f = jnp.exp(c * lg_f)
    chunk_decay_b = jnp.exp(c * lg_b)

    @pl.when(pl.program_id(1) == 0)
    def _():
        def norm(rows):
            h_ref[0, rows, :] = _rms(x_ref[0, rows, :], mix_gain_ref[...]).astype(h_ref.dtype)
        _for_row_tiles(s_len, norm)

    w_qk = jnp.concatenate([wq_ref[...], wk_ref[...]], axis=0).astype(BF16)
    w_v = wv_ref[...].astype(BF16)
    w_g = wg_ref[...].astype(BF16)

    def project(rows):
        h = h_ref[0, rows, :]
        qk = _dot_nt(h, w_qk)
        cos, sin = cos_ref[0, rows, :], sin_ref[0, rows, :]
        qr = _rope(qk[:, :dk], cos, sin) * scale
        kr = _rope(qk[:, dk:], cos, sin)
        qk_ref[rows, :] = jnp.concatenate([qr, kr], axis=1).astype(BF16)
        qw_ref[rows, :] = jnp.concatenate([qr * wq_f, qr * wq_b], axis=1).astype(BF16)
        kw_ref[rows, :] = jnp.concatenate([kr * wk_f, kr * wk_b], axis=1).astype(BF16)
        v_ref[rows, :] = _dot_nt(h, w_v).astype(BF16)
        gg_ref[rows, :] = _silu(_dot_nt(h, w_g)) * gain_ref[...]

    _for_row_tiles(s_len, project)

    def phase1(n, carry):
        rows = _chunk_rows(n)
        kv_ref[n] = _dot_tn(kw_ref[rows, :], v_ref[rows, :])
        return carry

    lax.fori_loop(0, n_chunks, phase1, 0, unroll=CHUNK_UNROLL)
    _prefix_states(n_chunks, dk, dv, lambda n: chunk_decay_f, lambda n: chunk_decay_b, kv_ref, st_ref)

    def phase2(n, carry):
        rows = _chunk_rows(n)
        qk = qk_ref[rows, :]
        scores = _dot_nt(qk[:, :dk], qk[:, dk:]) * decay_mask
        o = _dot(scores.astype(BF16), v_ref[rows, :]) + _dot(qw_ref[rows, :], st_ref[n])
        mu = jnp.mean(o, axis=-1, keepdims=True)
        d = o - mu
        var = jnp.mean(d * d, axis=-1, keepdims=True)
        o_ref[0, rows, :] = (d * lax.rsqrt(var + NORM_EPS) * gg_ref[rows, :]).astype(o_ref.dtype)
        return carry

    lax.fori_loop(0, n_chunks, phase2, 0, unroll=CHUNK_UNROLL)


def _retention(x, mix_gain, w_in, col0, cos, sin, dec, gain):
    b, s, d = x.shape
    h, dk, dv = RET_HEADS, RET_DK, RET_DV
    q0 = col0 // dk
    v0 = (col0 + 2 * h * dk) // dv
    n_chunks = s // CHUNK
    return pl.pallas_call(
        _retention_kernel,
        grid=(b, h),
        in_specs=[pl.BlockSpec((1, s, d), lambda i, j: (i, 0, 0)),
                  pl.BlockSpec((1, d), lambda i, j: (0, 0)),
                  pl.BlockSpec((dk, d), lambda i, j: (q0 + j, 0)),
                  pl.BlockSpec((dk, d), lambda i, j: (q0 + h + j, 0)),
                  pl.BlockSpec((dv, d), lambda i, j: (v0 + j, 0)),
                  pl.BlockSpec((dv, d), lambda i, j: (v0 + h + j, 0)),
                  pl.BlockSpec((1, s, dk), lambda i, j: (i, 0, 0)),
                  pl.BlockSpec((1, s, dk), lambda i, j: (i, 0, 0)),
                  pl.BlockSpec((1, 8, dv), lambda i, j: (j, 0, 0)),
                  pl.BlockSpec((1, dv), lambda i, j: (0, j))],
        out_specs=[pl.BlockSpec((1, s, dv), lambda i, j: (i, 0, j)),
                   pl.BlockSpec((1, s, d), lambda i, j: (i, 0, 0))],
        out_shape=[jax.ShapeDtypeStruct((b, s, h * dv), BF16),
                   jax.ShapeDtypeStruct((b, s, d), BF16)],
        scratch_shapes=[pltpu.VMEM((s, 2 * dk), BF16),
                        pltpu.VMEM((s, 2 * dk), BF16),
                        pltpu.VMEM((s, 2 * dk), BF16),
                        pltpu.VMEM((s, dv), BF16),
                        pltpu.VMEM((s, dv), F32),
                        pltpu.VMEM((n_chunks, 2 * dk, dv), F32),
                        pltpu.VMEM((n_chunks, 2 * dk, dv), BF16)],
        compiler_params=_params(("parallel", "arbitrary")),
        name="retention",
    )(x, mix_gain, w_in, w_in, w_in, w_in, cos, sin, dec, gain)


GATE_COPY = 2 * GLA_GATE_RANK


def _cumsum_dot(tri2, x):
    hi, lo = _split2(x)
    return _dot(tri2, jnp.concatenate([hi, lo], axis=0))


def _gla_kernel(h_ref, wq_ref, wk_ref, wv_ref, wg_ref, wa_ref, gw_ref, gb_ref, gain_ref, o_ref,
                q_ref, k_ref, v_ref, gg_ref, ga_ref, la_ref, qs_ref, ks_ref, qw_ref, kw_ref, last_ref,
                kv_ref, st_ref):
    c = CHUNK
    s_len, dk, dv = q_ref.shape[0], q_ref.shape[1], v_ref.shape[1]
    n_chunks = s_len // c
    scale = dk ** -0.5
    inv_norm = LOG2_E / GLA_GATE_NORMALIZER

    w_qk = jnp.concatenate([wq_ref[...], wk_ref[...]], axis=0).astype(BF16)
    w_v = wv_ref[...].astype(BF16)
    w_g = wg_ref[...].astype(BF16)

    @pl.when(pl.program_id(1) == 0)
    def _():
        w_a = wa_ref[...].astype(BF16)

        def gate_inputs(rows):
            x = _dot_nt(h_ref[0, rows, :], w_a)
            x_hi = x.astype(BF16).astype(F32)
            lane = lax.broadcasted_iota(jnp.int32, x.shape, 1)
            ga = jnp.where(lane < GATE_COPY, x_hi,
                           jnp.where(lane < 2 * GATE_COPY, pltpu.roll(x_hi, GATE_COPY, 1),
                                     jnp.where(lane < 3 * GATE_COPY, pltpu.roll(x - x_hi, 2 * GATE_COPY, 1),
                                               0.0)))
            ga_ref[rows, :] = ga.astype(BF16)

        _for_row_tiles(s_len, gate_inputs)

    def project(rows):
        h = h_ref[0, rows, :]
        qk = _dot_nt(h, w_qk)
        q_ref[rows, :] = qk[:, :dk] * scale
        k_ref[rows, :] = qk[:, dk:]
        v_ref[rows, :] = _dot_nt(h, w_v).astype(BF16)
        gg_ref[rows, :] = _silu(_dot_nt(h, w_g)) * gain_ref[...]
        la_ref[rows, :] = _log_sigmoid(_dot(ga_ref[rows, :], gw_ref[0]) + gb_ref[0]) * inv_norm

    _for_row_tiles(s_len, project)

    ri = lax.broadcasted_iota(jnp.int32, (c, c), 0)
    ci = lax.broadcasted_iota(jnp.int32, (c, c), 1)
    lower = ri >= ci
    tri_lower = jnp.where(lower, 1.0, 0.0).astype(BF16)
    tri_upper = jnp.where(ci >= ri, 1.0, 0.0).astype(BF16)
    tri2_lower = jnp.concatenate([tri_lower, tri_lower], axis=1)
    tri2_upper = jnp.concatenate([tri_upper, tri_upper], axis=1)

    def cumulate(n, carry):
        rows = _chunk_rows(n)
        cum_f = _cumsum_dot(tri2_lower, la_ref[rows, 0:dk])
        cum_b = _cumsum_dot(tri2_upper, la_ref[rows, dk:2 * dk])
        last_f, last_b = cum_f[c - 1:c, :], cum_b[0:1, :]
        ref_f, ref_b = cum_f[c // 2:c // 2 + 1, :], cum_b[c // 2 - 1:c // 2, :]
        q, k = q_ref[rows, :], k_ref[rows, :]
        q_f = q * jnp.exp2(cum_f - ref_f)
        q_b = q * jnp.exp2(cum_b - ref_b)
        qs_ref[rows, :] = jnp.concatenate([q_f, q_b], axis=1).astype(BF16)
        ks_ref[rows, :] = jnp.concatenate([k * jnp.exp2(ref_f - cum_f), k * jnp.exp2(ref_b - cum_b)],
                                          axis=1).astype(BF16)
        qw_ref[rows, :] = jnp.concatenate([q_f * jnp.exp2(ref_f), q_b * jnp.exp2(ref_b)], axis=1).astype(BF16)
        kw_ref[rows, :] = jnp.concatenate([k * jnp.exp2(last_f - cum_f), k * jnp.exp2(last_b - cum_b)],
                                          axis=1).astype(BF16)
        last_ref[n, 0:1, :] = last_f
        last_ref[n, 1:2, :] = last_b
        return carry

    lax.fori_loop(0, n_chunks, cumulate, 0, unroll=CHUNK_UNROLL)

    def phase1(n, carry):
        rows = _chunk_rows(n)
        kv_ref[n] = _dot_tn(v_ref[rows, :], kw_ref[rows, :])
        return carry

    lax.fori_loop(0, n_chunks, phase1, 0, unroll=CHUNK_UNROLL)

    def fwd(n, s):
        st_ref[n, :, 0:dk] = s.astype(BF16)
        return jnp.exp2(last_ref[n, 0:1, :]) * s + kv_ref[n, :, 0:dk]

    def bwd(i, s):
        n = n_chunks - 1 - i
        st_ref[n, :, dk:2 * dk] = s.astype(BF16)
        return jnp.exp2(last_ref[n, 1:2, :]) * s + kv_ref[n, :, dk:2 * dk]

    zero = jnp.zeros((dv, dk), F32)
    lax.fori_loop(0, n_chunks, fwd, zero)
    lax.fori_loop(0, n_chunks, bwd, zero)

    def phase2(n, carry):
        rows = _chunk_rows(n)
        qs, ks = qs_ref[rows, :], ks_ref[rows, :]
        scores = jnp.where(lower, _dot_nt(qs[:, :dk], ks[:, :dk]), _dot_nt(qs[:, dk:], ks[:, dk:]))
        o = _dot(scores.astype(BF16), v_ref[rows, :]) + _dot_nt(qw_ref[rows, :], st_ref[n])
        rms = lax.rsqrt(jnp.mean(o * o, axis=-1, keepdims=True) + NORM_EPS)
        o_ref[0, rows, :] = (o * rms * gg_ref[rows, :]).astype(o_ref.dtype)
        return carry

    lax.fori_loop(0, n_chunks, phase2, 0, unroll=CHUNK_UNROLL)


def _gla(h3, w_in, col0, gate_w, gate_b, gain):
    b, s, d = h3.shape
    h, dk, dv = GLA_HEADS, GLA_DK, GLA_DV
    q0 = col0 // dk
    v0 = (col0 + 2 * h * dk) // dv
    a0 = (col0 + 2 * h * dk + 2 * h * dv) // LANES
    n_chunks = s // CHUNK
    return pl.pallas_call(
        _gla_kernel,
        grid=(b, h),
        in_specs=[pl.BlockSpec((1, s, d), lambda i, j: (i, 0, 0)),
                  pl.BlockSpec((dk, d), lambda i, j: (q0 + j, 0)),
                  pl.BlockSpec((dk, d), lambda i, j: (q0 + h + j, 0)),
                  pl.BlockSpec((dv, d), lambda i, j: (v0 + j, 0)),
                  pl.BlockSpec((dv, d), lambda i, j: (v0 + h + j, 0)),
                  pl.BlockSpec((LANES, d), lambda i, j: (a0, 0)),
                  pl.BlockSpec((1, LANES, 2 * dk), lambda i, j: (j, 0, 0)),
                  pl.BlockSpec((1, 1, 2 * dk), lambda i, j: (j, 0, 0)),
                  pl.BlockSpec((1, dv), lambda i, j: (0, j))],
        out_specs=pl.BlockSpec((1, s, dv), lambda i, j: (i, 0, j)),
        out_shape=jax.ShapeDtypeStruct((b, s, h * dv), BF16),
        scratch_shapes=[pltpu.VMEM((s, dk), F32),
                        pltpu.VMEM((s, dk), F32),
                        pltpu.VMEM((s, dv), BF16),
                        pltpu.VMEM((s, dv), F32),
                        pltpu.VMEM((s, LANES), BF16),
                        pltpu.VMEM((s, 2 * dk), F32),
                        pltpu.VMEM((s, 2 * dk), BF16),
                        pltpu.VMEM((s, 2 * dk), BF16),
                        pltpu.VMEM((s, 2 * dk), BF16),
                        pltpu.VMEM((s, 2 * dk), BF16),
                        pltpu.VMEM((n_chunks, 8, dk), F32),
                        pltpu.VMEM((n_chunks, dv, 2 * dk), F32),
                        pltpu.VMEM((n_chunks, dv, 2 * dk), BF16)],
        compiler_params=_params(("parallel", "arbitrary")),
        name="gla",
    )(h3, w_in, w_in, w_in, w_in, w_in, gate_w, gate_b, gain)


MERGE_COLS = 256


def _merge_kernel(n_experts, h_ref, ret_ref, gla_ref, x_ref, wgl_ref, wr_ref, wg_ref, wo_ref, gain_ref,
                  wr2_ref, x1_ref, h2_ref, aff_ref):
    d = x_ref.shape[1]
    h, ret, gla = h_ref[...], ret_ref[...], gla_ref[...]
    blocks = []
    for j in range(0, d, MERGE_COLS):
        cols = slice(j, j + MERGE_COLS)
        cols_gla = slice(d + j, d + j + MERGE_COLS)
        m = (_sigmoid(_dot_nt(h, wgl_ref[cols, :])) * _dot(ret, wr_ref[:, cols])
             + _sigmoid(_dot_nt(h, wgl_ref[cols_gla, :])) * _dot(gla, wg_ref[:, cols]))
        blocks.append(m.astype(BF16))
    x1 = x_ref[...] + _dot(jnp.concatenate(blocks, axis=1), wo_ref[...])
    x1_ref[...] = x1
    h2 = _rms(x1, gain_ref[...]).astype(BF16)
    h2_ref[...] = h2
    logits2 = _dot(h2, wr2_ref[...])
    logits = logits2[:, :LANES] + logits2[:, LANES:]
    lane = lax.broadcasted_iota(jnp.int32, logits.shape, 1)
    logits = jnp.where(lane < n_experts, logits, -jnp.inf)
    p = jnp.exp(logits - jnp.max(logits, axis=-1, keepdims=True))
    aff = p / jnp.sum(p, axis=-1, keepdims=True)
    aff_ref[0] = aff.T[0:n_experts, :]


def _merge(h, ret, gla, x2, w_gl, w_ret, w_gla, w_out, gain, w_router2, n_experts, batch, tm):
    t, d = x2.shape
    s = t // batch
    per_b = s // tm
    rows = lambda width: pl.BlockSpec((tm, width), lambda i: (i, 0))
    return pl.pallas_call(
        functools.partial(_merge_kernel, n_experts),
        grid=(t // tm,),
        in_specs=[rows(d), rows(ret.shape[1]), rows(gla.shape[1]), rows(d),
                  _resident(w_gl.shape), _resident(w_ret.shape), _resident(w_gla.shape),
                  _resident(w_out.shape), _resident((1, d)),
                  _resident(w_router2.shape)],
        out_specs=[rows(d), rows(d),
                   pl.BlockSpec((1, n_experts, tm), lambda i: (i // per_b, 0, i % per_b))],
        out_shape=[jax.ShapeDtypeStruct((t, d), F32),
                   jax.ShapeDtypeStruct((t, d), BF16),
                   jax.ShapeDtypeStruct((batch, n_experts, s), F32)],
        compiler_params=_params(("parallel",)),
        name="merge",
    )(h, ret, gla, x2, w_gl, w_ret, w_gla, w_out, gain, w_router2)


def _prefix_count(mask):
    s = mask.shape[1]
    ri = lax.broadcasted_iota(jnp.int32, (LANES, LANES), 0)
    ci = lax.broadcasted_iota(jnp.int32, (LANES, LANES), 1)
    tri = jnp.where(ri <= ci, 1.0, 0.0).astype(BF16)
    off = jnp.zeros((mask.shape[0], 1), F32)
    parts = []
    for j in range(s // LANES):
        p = _dot(mask[:, j * LANES:(j + 1) * LANES].astype(BF16), tri) + off
        parts.append(p)
        off = p[:, LANES - 1:LANES]
    return jnp.concatenate(parts, axis=1)


def _route_kernel(capacity, aff_ref, slot_ref):
    a = aff_ref[0]
    bits = lax.bitcast_convert_type(a, jnp.int32)
    n_e = a.shape[0]
    cap = float(capacity)

    def count(pred):
        return jnp.sum(jnp.where(pred, 1.0, 0.0), axis=1, keepdims=True)

    def search(_, c):
        lo, hi = c
        mid = lo + lax.shift_right_logical(hi - lo, 1)
        ok = count(bits >= mid) >= cap
        return jnp.where(ok, mid, lo), jnp.where(ok, hi, mid)

    lo0 = jnp.zeros((n_e, 1), jnp.int32)
    hi0 = jnp.full((n_e, 1), 0x7F800000, jnp.int32)
    thr_bits, _ = lax.fori_loop(0, 31, search, (lo0, hi0))
    thr0 = jnp.max(jnp.where(bits <= thr_bits, a, -1.0), axis=1, keepdims=True)

    def counts(v):
        return count(a >= v), count(a > v)

    def unsettled(state):
        _, c_ge, c_gt = state
        bad = jnp.where(c_ge < cap, 1.0, jnp.where(c_gt >= cap, 1.0, 0.0))
        return jnp.max(bad, axis=0, keepdims=True)[0, 0] > 0.0

    def step(state):
        v, c_ge, c_gt = state
        below = jnp.max(jnp.where(a < v, a, -1.0), axis=1, keepdims=True)
        above = jnp.min(jnp.where(a > v, a, 2.0), axis=1, keepdims=True)
        v = jnp.where(c_ge < cap, below, jnp.where(c_gt >= cap, above, v))
        return (v,) + counts(v)

    thr, _, n_gt = lax.while_loop(unsettled, step, (thr0,) + counts(thr0))

    gt = a > thr
    eq = a == thr
    need = cap - n_gt
    eq_rank = _prefix_count(jnp.where(eq, 1.0, 0.0))
    sel = jnp.where(gt, 1.0, jnp.where(eq, jnp.where(eq_rank <= need, 1.0, 0.0), 0.0))
    pos = _prefix_count(sel)
    slot_ref[0] = jnp.where(sel > 0.0, pos - 1.0, -1.0)


def _route(aff_t, capacity):
    b, e, s = aff_t.shape
    return pl.pallas_call(
        functools.partial(_route_kernel, capacity),
        grid=(1,),
        in_specs=[pl.BlockSpec((1, b * e, s), lambda i: (0, 0, 0))],
        out_specs=pl.BlockSpec((1, b * e, s), lambda i: (0, 0, 0)),
        out_shape=jax.ShapeDtypeStruct((1, b * e, s), F32),
        compiler_params=_params(("arbitrary",)),
        name="route",
    )(aff_t.reshape(1, b * e, s)).reshape(b, e, s)


def _dispatch_kernel(slot_ref, aff_ref, h_ref, xg_ref, gate_ref):
    cap = xg_ref.shape[2]
    slot = slot_ref[0, 0]
    hit = lax.broadcasted_iota(jnp.int32, (cap, slot.shape[1]), 0).astype(F32) == slot
    onehot = jnp.where(hit, 1.0, 0.0).astype(BF16)
    xg_ref[0, 0] = _dot(onehot, h_ref[0]).astype(xg_ref.dtype)
    gate = jnp.sum(jnp.where(hit, aff_ref[0, 0], 0.0), axis=1, keepdims=True)
    gate_ref[0, 0] = jnp.broadcast_to(gate, gate_ref.shape[2:])


def _dispatch(slot, aff_t, h2, capacity):
    b, e, s = slot.shape
    d = h2.shape[-1]
    return pl.pallas_call(
        _dispatch_kernel,
        grid=(b, e),
        in_specs=[pl.BlockSpec((1, 1, 1, s), lambda i, j: (i, j, 0, 0)),
                  pl.BlockSpec((1, 1, 1, s), lambda i, j: (i, j, 0, 0)),
                  pl.BlockSpec((1, s, d), lambda i, j: (i, 0, 0))],
        out_specs=[pl.BlockSpec((1, 1, capacity, d), lambda i, j: (j, i, 0, 0)),
                   pl.BlockSpec((1, 1, capacity, LANES), lambda i, j: (j, i, 0, 0))],
        out_shape=[jax.ShapeDtypeStruct((e, b, capacity, d), BF16),
                   jax.ShapeDtypeStruct((e, b, capacity, LANES), F32)],
        compiler_params=_params(("parallel", "parallel")),
        name="dispatch",
    )(slot.reshape(b, e, 1, s), aff_t.reshape(b, e, 1, s), h2)


def _ffn_kernel(tm, n_f, x_ref, gate_ref, wg_ref, wu_ref, wd_ref, y_ref, acc_ref):
    f = pl.program_id(1)
    w_gate = wg_ref[0].astype(BF16)
    w_up = wu_ref[0].astype(BF16)
    w_down = wd_ref[0].astype(BF16)
    m = x_ref.shape[1]
    d = x_ref.shape[2]

    def step(first, final):
        for i in range(m // tm):
            rows = pl.ds(i * tm, tm)
            x = x_ref[0, rows, :]
            act = (_silu(_dot(x, w_gate)) * _dot(x, w_up)).astype(BF16)
            part = _dot(act, w_down)
            if not first:
                part = acc_ref[rows, :] + part
            if final:
                gate = jnp.concatenate([gate_ref[0, rows, :]] * (d // LANES), axis=1)
                y_ref[0, rows, :] = (part * gate).astype(y_ref.dtype)
            else:
                acc_ref[rows, :] = part

    if n_f == 1:
        step(True, True)
    else:
        pl.when(f == 0)(lambda: step(True, False))
        if n_f > 2:
            pl.when(jnp.logical_and(f > 0, f < n_f - 1))(lambda: step(False, False))
        pl.when(f == n_f - 1)(lambda: step(False, True))


def _ffn(xg, gate, w_gate, w_up, w_down, tf, tm):
    e, m, d = xg.shape
    f = w_gate.shape[2]
    return pl.pallas_call(
        functools.partial(_ffn_kernel, tm, f // tf),
        grid=(e, f // tf),
        in_specs=[pl.BlockSpec((1, m, d), lambda i, j: (i, 0, 0)),
                  pl.BlockSpec((1, m, LANES), lambda i, j: (i, 0, 0)),
                  pl.BlockSpec((1, d, tf), lambda i, j: (i, 0, j)),
                  pl.BlockSpec((1, d, tf), lambda i, j: (i, 0, j)),
                  pl.BlockSpec((1, tf, d), lambda i, j: (i, j, 0))],
        out_specs=pl.BlockSpec((1, m, d), lambda i, j: (i, 0, 0)),
        out_shape=jax.ShapeDtypeStruct((e, m, d), BF16),
        scratch_shapes=[pltpu.VMEM((m, d), F32)],
        compiler_params=_params(("parallel", "arbitrary")),
        name="ffn",
    )(xg, gate, w_gate, w_up, w_down)


def _combine_kernel(slot_ref, y_ref, x1_ref, gain_ref, o_ref):
    n_e, cap = y_ref.shape[0], y_ref.shape[2]
    tt = x1_ref.shape[1]
    acc = x1_ref[0]
    row = lax.broadcasted_iota(jnp.int32, (cap, tt), 0).astype(F32)
    for e in range(n_e):
        onehot = jnp.where(row == slot_ref[0, e:e + 1, :], 1.0, 0.0).astype(BF16)
        acc = acc + _dot_tn(onehot, y_ref[e, 0])
    o_ref[0] = _rms(acc, gain_ref[...])


def _combine(slot, y, x1, gain, tt):
    b, e, s = slot.shape
    cap, d = y.shape[2], y.shape[3]
    return pl.pallas_call(
        _combine_kernel,
        grid=(b, s // tt),
        in_specs=[pl.BlockSpec((1, e, tt), lambda i, j: (i, 0, j)),
                  pl.BlockSpec((e, 1, cap, d), lambda i, j: (0, i, 0, 0)),
                  pl.BlockSpec((1, tt, d), lambda i, j: (i, j, 0)),
                  pl.BlockSpec((1, d), lambda i, j: (0, 0))],
        out_specs=pl.BlockSpec((1, tt, d), lambda i, j: (i, j, 0)),
        out_shape=jax.ShapeDtypeStruct((b, s, d), F32),
        compiler_params=_params(("parallel", "parallel")),
        name="combine",
    )(slot, y, x1, gain)


def _layer(x, cos, sin, norm_mix, w_in, ret_decay_fwd, ret_decay_bwd, ret_norm,
           gla_gate_w_fwd, gla_gate_b_fwd, gla_gate_w_bwd, gla_gate_b_bwd, gla_norm,
           w_branch_ret, w_branch_gla, w_out, norm_ffn, w_router, w_gate, w_up, w_down, norm_out):
    b, s, d = x.shape
    t = b * s
    ret_qk, ret_v = RET_HEADS * RET_DK, RET_HEADS * RET_DV
    gla_qk, gla_v = GLA_HEADS * GLA_DK, GLA_HEADS * GLA_DV
    rank = GLA_GATE_RANK
    gla0 = 2 * ret_qk + 2 * ret_v
    ga0 = gla0 + 2 * gla_qk + 2 * gla_v
    assert w_in.shape == (d, ga0 + 2 * rank + 2 * d)
    assert s % CHUNK == 0 and 3 * GATE_COPY <= LANES and ga0 % LANES == 0

    w_in_t = w_in.T
    w_gl = w_in_t[ga0 + 2 * rank:, :].astype(BF16)

    gw = jnp.zeros((GLA_HEADS, GATE_COPY, 2 * GLA_DK), F32)
    gw = gw.at[:, :rank, :GLA_DK].set(gla_gate_w_fwd.reshape(rank, GLA_HEADS, GLA_DK).transpose(1, 0, 2))
    gw = gw.at[:, rank:, GLA_DK:].set(gla_gate_w_bwd.reshape(rank, GLA_HEADS, GLA_DK).transpose(1, 0, 2))
    gw_hi = gw.astype(BF16)
    gw_lo = (gw - gw_hi.astype(F32)).astype(BF16)
    gate_w = jnp.concatenate([gw_hi, gw_lo, gw_hi, jnp.zeros_like(gw_hi)], axis=1)
    gate_b = jnp.concatenate([gla_gate_b_fwd.reshape(GLA_HEADS, 1, GLA_DK),
                              gla_gate_b_bwd.reshape(GLA_HEADS, 1, GLA_DK)], axis=2)

    dec = jnp.stack([ret_decay_fwd, ret_decay_bwd], axis=1)[:, :, None]
    dec = jnp.pad(jnp.broadcast_to(dec, (RET_HEADS, 2, RET_DV)), ((0, 0), (0, 6), (0, 0)))

    x2 = x.reshape(t, d)
    ret, h3 = _retention(x, norm_mix[None, :], w_in_t, 0, cos, sin, dec, ret_norm[None, :])
    gla = _gla(h3, w_in_t, gla0, gate_w, gate_b, gla_norm[None, :])
    h = h3.reshape(t, d)

    n_e = w_router.shape[1]
    w_r = jnp.pad(w_router, ((0, 0), (0, LANES - n_e)))
    wr_hi = w_r.astype(BF16)
    wr_lo = (w_r - wr_hi.astype(F32)).astype(BF16)
    x1, h2, aff_t = _merge(h, ret.reshape(t, ret_v), gla.reshape(t, gla_v), x2, w_gl,
                           w_branch_ret.astype(BF16), w_branch_gla.astype(BF16), w_out.astype(BF16),
                           norm_ffn[None, :], jnp.concatenate([wr_hi, wr_lo], axis=1), n_e, b,
                           _pick(s, (512, 256, 128)))

    capacity = EC_CAPACITY_FACTOR * s // n_e
    slot = _route(aff_t, capacity)
    xg, gate = _dispatch(slot, aff_t, h2.reshape(b, s, d), capacity)
    f = w_gate.shape[2]
    y = _ffn(xg.reshape(n_e, b * capacity, d), gate.reshape(n_e, b * capacity, LANES),
             w_gate, w_up, w_down, _pick(f, (256, 128)), _pick(b * capacity, (1024, 512, 256, 128)))
    return _combine(slot, y.reshape(n_e, b, capacity, d), x1.reshape(b, s, d), norm_out,
                    _pick(s, (512, 256, 128)))


def kernel(x, positions, norm_mix, w_in, ret_decay_fwd, ret_decay_bwd, ret_norm, gla_gate_w_fwd,
           gla_gate_b_fwd, gla_gate_w_bwd, gla_gate_b_bwd, gla_norm, w_branch_ret, w_branch_gla,
           w_out, norm_ffn, w_router, w_gate, w_up, w_down, norm_final):
    depth = norm_mix.shape[0]
    assert depth == 1, "the final RMSNorm is fused into the last layer's combine stage"
    cos, sin = _rope_table(positions, RET_DK)
    return _layer(x, cos, sin, norm_mix[0], w_in[0], ret_decay_fwd[0], ret_decay_bwd[0], ret_norm[0],
                  gla_gate_w_fwd[0], gla_gate_b_fwd[0], gla_gate_w_bwd[0], gla_gate_b_bwd[0], gla_norm[0],
                  w_branch_ret[0], w_branch_gla[0], w_out[0], norm_ffn[0], w_router[0],
                  w_gate[0], w_up[0], w_down[0], norm_final[None, :])
```

```python
import functools

import jax
import jax.numpy as jnp
from jax import lax
from jax.experimental import pallas as pl
from jax.experimental.pallas import tpu as pltpu

F32 = jnp.float32
BF16 = jnp.bfloat16

RET_HEADS = 4
RET_DK = 128
RET_DV = 256
GLA_HEADS = 4
GLA_DK = 128
GLA_DV = 256
GLA_GATE_RANK = 16
GLA_GATE_NORMALIZER = 16.0
CHUNK = 128
EC_CAPACITY_FACTOR = 2
ROPE_THETA = 10000.0
NORM_EPS = 1e-6
LOG2_E = 1.4426950408889634

CHUNK_UNROLL = 16
LANES = 128
VMEM_LIMIT = 56 << 20


def _params(sem, vmem=VMEM_LIMIT):
    return pltpu.CompilerParams(dimension_semantics=sem, vmem_limit_bytes=vmem)


def _resident(shape):
    return pl.BlockSpec(shape, lambda *_: (0,) * len(shape), pipeline_mode=pl.Buffered(1))


def _pick(n, prefs):
    for p in prefs:
        if n % p == 0:
            return p
    return n


def _sigmoid(x):
    return 1.0 / (1.0 + jnp.exp(-x))


def _silu(x):
    half = 0.5 * x
    return half + half * jnp.tanh(half)


def _log_sigmoid(x):
    return jnp.minimum(x, 0.0) - jnp.log(1.0 + jnp.exp(-jnp.abs(x)))


def _rms(x, gain):
    return x * lax.rsqrt(jnp.mean(x * x, axis=-1, keepdims=True) + NORM_EPS) * gain


def _dot(a, b):
    return jnp.dot(a, b, preferred_element_type=F32)


def _dot_nt(a, b):
    return lax.dot_general(a, b, (((1,), (1,)), ((), ())), preferred_element_type=F32)


def _dot_tn(a, b):
    return lax.dot_general(a, b, (((0,), (0,)), ((), ())), preferred_element_type=F32)


def _split2(x):
    hi = x.astype(BF16)
    lo = (x - hi.astype(F32)).astype(BF16)
    return hi, lo


def _rope_table_kernel(pos_ref, freq_ref, shift_ref, cos_ref, sin_ref):
    half = freq_ref.shape[1] // 2
    ang = pos_ref[0].astype(F32) * freq_ref[...]
    tab = jnp.cos(ang - shift_ref[...])
    swapped = pltpu.roll(tab, half, 1)
    lower = lax.broadcasted_iota(jnp.int32, tab.shape, 1) < half
    cos_ref[0] = jnp.where(lower, tab, swapped)
    sin_ref[0] = jnp.where(lower, -swapped, tab)


def _rope_table(positions, dk):
    b, s = positions.shape
    half = jnp.arange(0, dk, 2, dtype=F32) / dk
    inv_freq = ROPE_THETA ** (-half)
    freq = jnp.concatenate([inv_freq, inv_freq])[None, :]
    shift = jnp.concatenate([jnp.zeros(dk // 2, F32), jnp.full(dk // 2, jnp.pi / 2, F32)])[None, :]
    out = jax.ShapeDtypeStruct((b, s, dk), F32)
    return pl.pallas_call(
        _rope_table_kernel,
        grid=(b,),
        in_specs=[pl.BlockSpec((1, s, 1), lambda i: (i, 0, 0)),
                  pl.BlockSpec((1, dk), lambda i: (0, 0)),
                  pl.BlockSpec((1, dk), lambda i: (0, 0))],
        out_specs=[pl.BlockSpec((1, s, dk), lambda i: (i, 0, 0))] * 2,
        out_shape=[out, out],
        compiler_params=_params(("parallel",)),
        name="rope_table",
    )(positions[:, :, None], freq, shift)


def _rope(t, cos, sin_signed):
    return t * cos + pltpu.roll(t, t.shape[-1] // 2, 1) * sin_signed


def _prefix_states(n_chunks, dk, dv, decay_f, decay_b, kv_ref, st_ref):
    def step(i, states):
        s_f, s_b = states
        n = n_chunks - 1 - i
        st_ref[i, 0:dk, :] = s_f.astype(BF16)
        st_ref[n, dk:2 * dk, :] = s_b.astype(BF16)
        return (decay_f(i) * s_f + kv_ref[i, 0:dk, :],
                decay_b(n) * s_b + kv_ref[n, dk:2 * dk, :])

    zero = jnp.zeros((dk, dv), F32)
    lax.fori_loop(0, n_chunks, step, (zero, zero), unroll=2)


def _chunk_rows(n):
    return pl.ds(pl.multiple_of(n * CHUNK, CHUNK), CHUNK)


def _row_tile(s_len):
    return _pick(s_len, (512, 256, CHUNK))


def _per_tile(chunk_pattern, s_len):
    return jnp.concatenate([chunk_pattern] * (_row_tile(s_len) // CHUNK), axis=0)


def _for_row_tiles(s_len, body):
    tile = _row_tile(s_len)

    def step(i, carry):
        body(pl.ds(pl.multiple_of(i * tile, tile), tile))
        return carry

    lax.fori_loop(0, s_len // tile, step, 0, unroll=True)


def _retention_kernel(x_ref, mix_gain_ref, wq_ref, wk_ref, wv_ref, wg_ref, cos_ref, sin_ref, dec_ref,
                      gain_ref, o_ref, h_ref, qk_ref, qw_ref, kw_ref, v_ref, gg_ref, a_ref, kv_ref, st_ref):
    c = CHUNK
    s_len, dk, dv = qk_ref.shape[0], qk_ref.shape[1] // 2, v_ref.shape[1]
    n_chunks = s_len // c
    scale = dk ** -0.5

    lg_f = _log_sigmoid(dec_ref[0, 0:1, :])
    lg_b = _log_sigmoid(dec_ref[0, 1:2, :])
    lgf_k, lgb_k = lg_f[:, :dk], lg_b[:, :dk]
    pos = lax.broadcasted_iota(jnp.int32, (c, dk), 0).astype(F32)
    wq_f = _per_tile(jnp.exp((pos + 1.0) * lgf_k), s_len)
    wk_f = _per_tile(jnp.exp((c - 1.0 - pos) * lgf_k), s_len)
    wq_b = _per_tile(jnp.exp((c - pos) * lgb_k), s_len)
    wk_b = _per_tile(jnp.exp(pos * lgb_k), s_len)
    ri = lax.broadcasted_iota(jnp.int32, (c, c), 0)
    ci = lax.broadcasted_iota(jnp.int32, (c, c), 1)
    lower = ri >= ci
    rel = (ri - ci).astype(F32)
    decay_mask = jnp.where(lower,
                           jnp.exp(jnp.where(lower, rel, 0.0) * lg_f[:, :c]),
                           jnp.exp(jnp.where(lower, 0.0, -rel) * lg_b[:, :c]))
    chunk_decay_f = jnp.exp(c * lg_f)
    chunk_decay_b = jnp.exp(c * lg_b)

    @pl.when(pl.program_id(1) == 0)
    def _():
        def norm(rows):
            h_ref[0, rows, :] = _rms(x_ref[0, rows, :], mix_gain_ref[...]).astype(h_ref.dtype)
        _for_row_tiles(s_len, norm)

    w_qk = jnp.concatenate([wq_ref[...], wk_ref[...]], axis=0)
    w_v = wv_ref[...]
    w_g = wg_ref[...]

    def project(rows):
        h = h_ref[0, rows, :]
        qk = _dot_nt(h, w_qk)
        cos, sin = cos_ref[0, rows, :], sin_ref[0, rows, :]
        qr = _rope(qk[:, :dk], cos, sin) * scale
        kr = _rope(qk[:, dk:], cos, sin)
        qk_ref[rows, :] = jnp.concatenate([qr, kr], axis=1).astype(BF16)
        qw_ref[rows, :] = jnp.concatenate([qr * wq_f, qr * wq_b], axis=1).astype(BF16)
        kw_ref[rows, :] = jnp.concatenate([kr * wk_f, kr * wk_b], axis=1).astype(BF16)
        v_ref[rows, :] = _dot_nt(h, w_v).astype(BF16)
        gg_ref[rows, :] = _silu(_dot_nt(h, w_g)) * gain_ref[...]

    _for_row_tiles(s_len, project)

    def phase1(n, carry):
        rows = _chunk_rows(n)
        kv_ref[n] = _dot_tn(kw_ref[rows, :], v_ref[rows, :])
        return carry

    lax.fori_loop(0, n_chunks, phase1, 0, unroll=CHUNK_UNROLL)
    _prefix_states(n_chunks, dk, dv, lambda n: chunk_decay_f, lambda n: chunk_decay_b, kv_ref, st_ref)

    def scores(n, carry):
        rows = _chunk_rows(n)
        qk = qk_ref[rows, :]
        a_ref[rows, :] = (_dot_nt(qk[:, :dk], qk[:, dk:]) * decay_mask).astype(BF16)
        return carry

    lax.fori_loop(0, n_chunks, scores, 0, unroll=CHUNK_UNROLL)

    def phase2(n, carry):
        rows = _chunk_rows(n)
        o = _dot(a_ref[rows, :], v_ref[rows, :]) + _dot(qw_ref[rows, :], st_ref[n])
        mu = jnp.mean(o, axis=-1, keepdims=True)
        d = o - mu
        var = jnp.mean(d * d, axis=-1, keepdims=True)
        o_ref[0, rows, :] = (d * lax.rsqrt(var + NORM_EPS) * gg_ref[rows, :]).astype(o_ref.dtype)
        return carry

    lax.fori_loop(0, n_chunks, phase2, 0, unroll=CHUNK_UNROLL)


def _retention(x, mix_gain, w_in, col0, cos, sin, dec, gain):
    b, s, d = x.shape
    h, dk, dv = RET_HEADS, RET_DK, RET_DV
    q0 = col0 // dk
    v0 = (col0 + 2 * h * dk) // dv
    n_chunks = s // CHUNK
    return pl.pallas_call(
        _retention_kernel,
        grid=(b, h),
        in_specs=[pl.BlockSpec((1, s, d), lambda i, j: (i, 0, 0)),
                  pl.BlockSpec((1, d), lambda i, j: (0, 0)),
                  pl.BlockSpec((dk, d), lambda i, j: (q0 + j, 0)),
                  pl.BlockSpec((dk, d), lambda i, j: (q0 + h + j, 0)),
                  pl.BlockSpec((dv, d), lambda i, j: (v0 + j, 0)),
                  pl.BlockSpec((dv, d), lambda i, j: (v0 + h + j, 0)),
                  pl.BlockSpec((1, s, dk), lambda i, j: (i, 0, 0)),
                  pl.BlockSpec((1, s, dk), lambda i, j: (i, 0, 0)),
                  pl.BlockSpec((1, 8, dv), lambda i, j: (j, 0, 0)),
                  pl.BlockSpec((1, dv), lambda i, j: (0, j))],
        out_specs=[pl.BlockSpec((1, s, dv), lambda i, j: (i, 0, j)),
                   pl.BlockSpec((1, s, d), lambda i, j: (i, 0, 0))],
        out_shape=[jax.ShapeDtypeStruct((b, s, h * dv), BF16),
                   jax.ShapeDtypeStruct((b, s, d), BF16)],
        scratch_shapes=[pltpu.VMEM((s, 2 * dk), BF16),
                        pltpu.VMEM((s, 2 * dk), BF16),
                        pltpu.VMEM((s, 2 * dk), BF16),
                        pltpu.VMEM((s, dv), BF16),
                        pltpu.VMEM((s, dv), F32),
                        pltpu.VMEM((s, CHUNK), BF16),
                        pltpu.VMEM((n_chunks, 2 * dk, dv), F32),
                        pltpu.VMEM((n_chunks, 2 * dk, dv), BF16)],
        compiler_params=_params(("parallel", "arbitrary")),
        name="retention",
    )(x, mix_gain, w_in, w_in, w_in, w_in, cos, sin, dec, gain)


GATE_COPY = 2 * GLA_GATE_RANK


def _cumsum_dot(tri2, x):
    hi, lo = _split2(x)
    return _dot(tri2, jnp.concatenate([hi, lo], axis=0))


def _gla_kernel(h_ref, wq_ref, wk_ref, wv_ref, wg_ref, wa_ref, gw_ref, gb_ref, gain_ref, o_ref,
                q_ref, k_ref, v_ref, gg_ref, ga_ref, la_ref, qs_ref, ks_ref, qw_ref, kw_ref, last_ref,
                a_ref, kv_ref, st_ref):
    c = CHUNK
    s_len, dk, dv = q_ref.shape[0], q_ref.shape[1], v_ref.shape[1]
    n_chunks = s_len // c
    scale = dk ** -0.5
    inv_norm = LOG2_E / GLA_GATE_NORMALIZER

    w_qk = jnp.concatenate([wq_ref[...], wk_ref[...]], axis=0)
    w_v = wv_ref[...]
    w_g = wg_ref[...]

    @pl.when(pl.program_id(1) == 0)
    def _():
        w_a = wa_ref[...]

        def gate_inputs(rows):
            x = _dot_nt(h_ref[0, rows, :], w_a)
            x_hi = x.astype(BF16).astype(F32)
            lane = lax.broadcasted_iota(jnp.int32, x.shape, 1)
            ga = jnp.where(lane < GATE_COPY, x_hi,
                           jnp.where(lane < 2 * GATE_COPY, pltpu.roll(x_hi, GATE_COPY, 1),
                                     jnp.where(lane < 3 * GATE_COPY, pltpu.roll(x - x_hi, 2 * GATE_COPY, 1),
                                               0.0)))
            ga_ref[rows, :] = ga.astype(BF16)

        _for_row_tiles(s_len, gate_inputs)

    def project(rows):
        h = h_ref[0, rows, :]
        qk = _dot_nt(h, w_qk)
        q_ref[rows, :] = qk[:, :dk] * scale
        k_ref[rows, :] = qk[:, dk:]
        v_ref[rows, :] = _dot_nt(h, w_v).astype(BF16)
        gg_ref[rows, :] = _silu(_dot_nt(h, w_g)) * gain_ref[...]
        la_ref[rows, :] = _log_sigmoid(_dot(ga_ref[rows, :], gw_ref[0]) + gb_ref[0]) * inv_norm

    _for_row_tiles(s_len, project)

    ri = lax.broadcasted_iota(jnp.int32, (c, c), 0)
    ci = lax.broadcasted_iota(jnp.int32, (c, c), 1)
    lower = ri >= ci
    tri_lower = jnp.where(lower, 1.0, 0.0).astype(BF16)
    tri_upper = jnp.where(ci >= ri, 1.0, 0.0).astype(BF16)
    tri2_lower = jnp.concatenate([tri_lower, tri_lower], axis=1)
    tri2_upper = jnp.concatenate([tri_upper, tri_upper], axis=1)

    def cumulate(n, carry):
        rows = _chunk_rows(n)
        cum_f = _cumsum_dot(tri2_lower, la_ref[rows, 0:dk])
        cum_b = _cumsum_dot(tri2_upper, la_ref[rows, dk:2 * dk])
        last_f, last_b = cum_f[c - 1:c, :], cum_b[0:1, :]
        ref_f, ref_b = cum_f[c // 2:c // 2 + 1, :], cum_b[c // 2 - 1:c // 2, :]
        q, k = q_ref[rows, :], k_ref[rows, :]
        q_f = q * jnp.exp2(cum_f - ref_f)
        q_b = q * jnp.exp2(cum_b - ref_b)
        qs_ref[rows, :] = jnp.concatenate([q_f, q_b], axis=1).astype(BF16)
        ks_ref[rows, :] = jnp.concatenate([k * jnp.exp2(ref_f - cum_f), k * jnp.exp2(ref_b - cum_b)],
                                          axis=1).astype(BF16)
        qw_ref[rows, :] = jnp.concatenate([q_f * jnp.exp2(ref_f), q_b * jnp.exp2(ref_b)], axis=1).astype(BF16)
        kw_ref[rows, :] = jnp.concatenate([k * jnp.exp2(last_f - cum_f), k * jnp.exp2(last_b - cum_b)],
                                          axis=1).astype(BF16)
        last_ref[n, 0:1, :] = last_f
        last_ref[n, 1:2, :] = last_b
        return carry

    lax.fori_loop(0, n_chunks, cumulate, 0, unroll=CHUNK_UNROLL)

    def phase1(n, carry):
        rows = _chunk_rows(n)
        kv_ref[n] = _dot_tn(v_ref[rows, :], kw_ref[rows, :])
        return carry

    lax.fori_loop(0, n_chunks, phase1, 0, unroll=CHUNK_UNROLL)

    def scan(i, states):
        s_f, s_b = states
        n = n_chunks - 1 - i
        st_ref[i, :, 0:dk] = s_f.astype(BF16)
        st_ref[n, :, dk:2 * dk] = s_b.astype(BF16)
        return (jnp.exp2(last_ref[i, 0:1, :]) * s_f + kv_ref[i, :, 0:dk],
                jnp.exp2(last_ref[n, 1:2, :]) * s_b + kv_ref[n, :, dk:2 * dk])

    zero = jnp.zeros((dv, dk), F32)
    lax.fori_loop(0, n_chunks, scan, (zero, zero), unroll=2)

    def scores(n, carry):
        rows = _chunk_rows(n)
        qs, ks = qs_ref[rows, :], ks_ref[rows, :]
        a_ref[rows, :] = jnp.where(lower, _dot_nt(qs[:, :dk], ks[:, :dk]),
                                   _dot_nt(qs[:, dk:], ks[:, dk:])).astype(BF16)
        return carry

    lax.fori_loop(0, n_chunks, scores, 0, unroll=CHUNK_UNROLL)

    def phase2(n, carry):
        rows = _chunk_rows(n)
        o = _dot(a_ref[rows, :], v_ref[rows, :]) + _dot_nt(qw_ref[rows, :], st_ref[n])
        rms = lax.rsqrt(jnp.mean(o * o, axis=-1, keepdims=True) + NORM_EPS)
        o_ref[0, rows, :] = (o * rms * gg_ref[rows, :]).astype(o_ref.dtype)
        return carry

    lax.fori_loop(0, n_chunks, phase2, 0, unroll=CHUNK_UNROLL)


def _gla(h3, w_in, col0, gate_w, gate_b, gain):
    b, s, d = h3.shape
    h, dk, dv = GLA_HEADS, GLA_DK, GLA_DV
    q0 = col0 // dk
    v0 = (col0 + 2 * h * dk) // dv
    a0 = (col0 + 2 * h * dk + 2 * h * dv) // LANES
    n_chunks = s // CHUNK
    return pl.pallas_call(
        _gla_kernel,
        grid=(b, h),
        in_specs=[pl.BlockSpec((1, s, d), lambda i, j: (i, 0, 0)),
                  pl.BlockSpec((dk, d), lambda i, j: (q0 + j, 0)),
                  pl.BlockSpec((dk, d), lambda i, j: (q0 + h + j, 0)),
                  pl.BlockSpec((dv, d), lambda i, j: (v0 + j, 0)),
                  pl.BlockSpec((dv, d), lambda i, j: (v0 + h + j, 0)),
                  pl.BlockSpec((LANES, d), lambda i, j: (a0, 0)),
                  pl.BlockSpec((1, LANES, 2 * dk), lambda i, j: (j, 0, 0)),
                  pl.BlockSpec((1, 1, 2 * dk), lambda i, j: (j, 0, 0)),
                  pl.BlockSpec((1, dv), lambda i, j: (0, j))],
        out_specs=pl.BlockSpec((1, s, dv), lambda i, j: (i, 0, j)),
        out_shape=jax.ShapeDtypeStruct((b, s, h * dv), BF16),
        scratch_shapes=[pltpu.VMEM((s, dk), F32),
                        pltpu.VMEM((s, dk), F32),
                        pltpu.VMEM((s, dv), BF16),
                        pltpu.VMEM((s, dv), F32),
                        pltpu.VMEM((s, LANES), BF16),
                        pltpu.VMEM((s, 2 * dk), F32),
                        pltpu.VMEM((s, 2 * dk), BF16),
                        pltpu.VMEM((s, 2 * dk), BF16),
                        pltpu.VMEM((s, 2 * dk), BF16),
                        pltpu.VMEM((s, 2 * dk), BF16),
                        pltpu.VMEM((n_chunks, 8, dk), F32),
                        pltpu.VMEM((s, CHUNK), BF16),
                        pltpu.VMEM((n_chunks, dv, 2 * dk), F32),
                        pltpu.VMEM((n_chunks, dv, 2 * dk), BF16)],
        compiler_params=_params(("parallel", "arbitrary")),
        name="gla",
    )(h3, w_in, w_in, w_in, w_in, w_in, gate_w, gate_b, gain)


MERGE_COLS = 256


def _merge_kernel(n_experts, h_ref, ret_ref, gla_ref, x_ref, wgl_ref, wr_ref, wg_ref, wo_ref, gain_ref,
                  wr2_ref, x1_ref, h2_ref, aff_ref):
    d = x_ref.shape[1]
    h, ret, gla = h_ref[...], ret_ref[...], gla_ref[...]
    blocks = []
    for j in range(0, d, MERGE_COLS):
        cols = slice(j, j + MERGE_COLS)
        cols_gla = slice(d + j, d + j + MERGE_COLS)
        m = (_sigmoid(_dot_nt(h, wgl_ref[cols, :])) * _dot(ret, wr_ref[:, cols])
             + _sigmoid(_dot_nt(h, wgl_ref[cols_gla, :])) * _dot(gla, wg_ref[:, cols]))
        blocks.append(m.astype(BF16))
    x1 = x_ref[...] + _dot(jnp.concatenate(blocks, axis=1), wo_ref[...])
    x1_ref[...] = x1
    h2 = _rms(x1, gain_ref[...]).astype(BF16)
    h2_ref[...] = h2
    logits2 = _dot(h2, wr2_ref[...])
    logits = logits2[:, :LANES] + logits2[:, LANES:]
    lane = lax.broadcasted_iota(jnp.int32, logits.shape, 1)
    logits = jnp.where(lane < n_experts, logits, -jnp.inf)
    p = jnp.exp(logits - jnp.max(logits, axis=-1, keepdims=True))
    aff = p / jnp.sum(p, axis=-1, keepdims=True)
    aff_ref[0] = aff.T[0:n_experts, :]


def _merge(h, ret, gla, x2, w_gl, w_ret, w_gla, w_out, gain, w_router2, n_experts, batch, tm):
    t, d = x2.shape
    s = t // batch
    per_b = s // tm
    rows = lambda width: pl.BlockSpec((tm, width), lambda i: (i, 0))
    return pl.pallas_call(
        functools.partial(_merge_kernel, n_experts),
        grid=(t // tm,),
        in_specs=[rows(d), rows(ret.shape[1]), rows(gla.shape[1]), rows(d),
                  _resident(w_gl.shape), _resident(w_ret.shape), _resident(w_gla.shape),
                  _resident(w_out.shape), _resident((1, d)),
                  _resident(w_router2.shape)],
        out_specs=[rows(d), rows(d),
                   pl.BlockSpec((1, n_experts, tm), lambda i: (i // per_b, 0, i % per_b))],
        out_shape=[jax.ShapeDtypeStruct((t, d), F32),
                   jax.ShapeDtypeStruct((t, d), BF16),
                   jax.ShapeDtypeStruct((batch, n_experts, s), F32)],
        compiler_params=_params(("parallel",)),
        name="merge",
    )(h, ret, gla, x2, w_gl, w_ret, w_gla, w_out, gain, w_router2)


def _prefix_count(mask):
    s = mask.shape[1]
    ri = lax.broadcasted_iota(jnp.int32, (LANES, LANES), 0)
    ci = lax.broadcasted_iota(jnp.int32, (LANES, LANES), 1)
    tri = jnp.where(ri <= ci, 1.0, 0.0).astype(BF16)
    off = jnp.zeros((mask.shape[0], 1), F32)
    parts = []
    for j in range(s // LANES):
        p = _dot(mask[:, j * LANES:(j + 1) * LANES].astype(BF16), tri) + off
        parts.append(p)
        off = p[:, LANES - 1:LANES]
    return jnp.concatenate(parts, axis=1)


def _route_kernel(capacity, aff_ref, slot_ref):
    a = aff_ref[0]
    bits = lax.bitcast_convert_type(a, jnp.int32)
    n_e = a.shape[0]
    cap = float(capacity)

    def count(pred):
        return jnp.sum(jnp.where(pred, 1.0, 0.0), axis=1, keepdims=True)

    def search(_, c):
        lo, hi = c
        mid = lo + lax.shift_right_logical(hi - lo, 1)
        ok = count(bits >= mid) >= cap
        return jnp.where(ok, mid, lo), jnp.where(ok, hi, mid)

    lo0 = jnp.zeros((n_e, 1), jnp.int32)
    hi0 = jnp.full((n_e, 1), 0x7F800000, jnp.int32)
    thr_bits, _ = lax.fori_loop(0, 31, search, (lo0, hi0))
    thr0 = jnp.max(jnp.where(bits <= thr_bits, a, -1.0), axis=1, keepdims=True)

    def counts(v):
        return count(a >= v), count(a > v)

    def unsettled(state):
        _, c_ge, c_gt = state
        bad = jnp.where(c_ge < cap, 1.0, jnp.where(c_gt >= cap, 1.0, 0.0))
        return jnp.max(bad, axis=0, keepdims=True)[0, 0] > 0.0

    def step(state):
        v, c_ge, c_gt = state
        below = jnp.max(jnp.where(a < v, a, -1.0), axis=1, keepdims=True)
        above = jnp.min(jnp.where(a > v, a, 2.0), axis=1, keepdims=True)
        v = jnp.where(c_ge < cap, below, jnp.where(c_gt >= cap, above, v))
        return (v,) + counts(v)

    thr, _, n_gt = lax.while_loop(unsettled, step, (thr0,) + counts(thr0))

    gt = a > thr
    eq = a == thr
    need = cap - n_gt
    eq_rank = _prefix_count(jnp.where(eq, 1.0, 0.0))
    sel = jnp.where(gt, 1.0, jnp.where(eq, jnp.where(eq_rank <= need, 1.0, 0.0), 0.0))
    pos = _prefix_count(sel)
    slot_ref[0] = jnp.where(sel > 0.0, pos - 1.0, -1.0)


def _route(aff_t, capacity):
    b, e, s = aff_t.shape
    return pl.pallas_call(
        functools.partial(_route_kernel, capacity),
        grid=(1,),
        in_specs=[pl.BlockSpec((1, b * e, s), lambda i: (0, 0, 0))],
        out_specs=pl.BlockSpec((1, b * e, s), lambda i: (0, 0, 0)),
        out_shape=jax.ShapeDtypeStruct((1, b * e, s), F32),
        compiler_params=_params(("arbitrary",)),
        name="route",
    )(aff_t.reshape(1, b * e, s)).reshape(b, e, s)


def _dispatch_kernel(slot_ref, aff_ref, h_ref, xg_ref, gate_ref):
    cap = xg_ref.shape[2]
    slot = slot_ref[0, 0]
    hit = lax.broadcasted_iota(jnp.int32, (cap, slot.shape[1]), 0).astype(F32) == slot
    onehot = jnp.where(hit, 1.0, 0.0).astype(BF16)
    xg_ref[0, 0] = _dot(onehot, h_ref[0]).astype(xg_ref.dtype)
    gate = jnp.sum(jnp.where(hit, aff_ref[0, 0], 0.0), axis=1, keepdims=True)
    gate_ref[0, 0] = jnp.broadcast_to(gate, gate_ref.shape[2:])


def _dispatch(slot, aff_t, h2, capacity):
    b, e, s = slot.shape
    d = h2.shape[-1]
    return pl.pallas_call(
        _dispatch_kernel,
        grid=(b, e),
        in_specs=[pl.BlockSpec((1, 1, 1, s), lambda i, j: (i, j, 0, 0)),
                  pl.BlockSpec((1, 1, 1, s), lambda i, j: (i, j, 0, 0)),
                  pl.BlockSpec((1, s, d), lambda i, j: (i, 0, 0))],
        out_specs=[pl.BlockSpec((1, 1, capacity, d), lambda i, j: (j, i, 0, 0)),
                   pl.BlockSpec((1, 1, capacity, LANES), lambda i, j: (j, i, 0, 0))],
        out_shape=[jax.ShapeDtypeStruct((e, b, capacity, d), BF16),
                   jax.ShapeDtypeStruct((e, b, capacity, LANES), F32)],
        compiler_params=_params(("parallel", "parallel")),
        name="dispatch",
    )(slot.reshape(b, e, 1, s), aff_t.reshape(b, e, 1, s), h2)


def _ffn_kernel(tm, n_f, x_ref, gate_ref, wg_ref, wu_ref, wd_ref, y_ref, acc_ref):
    f = pl.program_id(1)
    w_gate = wg_ref[0].astype(BF16)
    w_up = wu_ref[0].astype(BF16)
    w_down = wd_ref[0].astype(BF16)
    m = x_ref.shape[1]
    d = x_ref.shape[2]

    def step(first, final):
        for i in range(m // tm):
            rows = pl.ds(i * tm, tm)
            x = x_ref[0, rows, :]
            act = (_silu(_dot(x, w_gate)) * _dot(x, w_up)).astype(BF16)
            part = _dot(act, w_down)
            if not first:
                part = acc_ref[rows, :] + part
            if final:
                gate = jnp.concatenate([gate_ref[0, rows, :]] * (d // LANES), axis=1)
                y_ref[0, rows, :] = (part * gate).astype(y_ref.dtype)
            else:
                acc_ref[rows, :] = part

    if n_f == 1:
        step(True, True)
    else:
        pl.when(f == 0)(lambda: step(True, False))
        if n_f > 2:
            pl.when(jnp.logical_and(f > 0, f < n_f - 1))(lambda: step(False, False))
        pl.when(f == n_f - 1)(lambda: step(False, True))


def _ffn(xg, gate, w_gate, w_up, w_down, tf, tm):
    e, m, d = xg.shape
    f = w_gate.shape[2]
    return pl.pallas_call(
        functools.partial(_ffn_kernel, tm, f // tf),
        grid=(e, f // tf),
        in_specs=[pl.BlockSpec((1, m, d), lambda i, j: (i, 0, 0)),
                  pl.BlockSpec((1, m, LANES), lambda i, j: (i, 0, 0)),
                  pl.BlockSpec((1, d, tf), lambda i, j: (i, 0, j)),
                  pl.BlockSpec((1, d, tf), lambda i, j: (i, 0, j)),
                  pl.BlockSpec((1, tf, d), lambda i, j: (i, j, 0))],
        out_specs=pl.BlockSpec((1, m, d), lambda i, j: (i, 0, 0)),
        out_shape=jax.ShapeDtypeStruct((e, m, d), BF16),
        scratch_shapes=[pltpu.VMEM((m, d), F32)],
        compiler_params=_params(("parallel", "arbitrary")),
        name="ffn",
    )(xg, gate, w_gate, w_up, w_down)


def _combine_kernel(slot_ref, y_ref, x1_ref, gain_ref, o_ref):
    n_e, cap = y_ref.shape[0], y_ref.shape[2]
    tt = x1_ref.shape[1]
    acc = x1_ref[0]
    row = lax.broadcasted_iota(jnp.int32, (cap, tt), 0).astype(F32)
    for e in range(n_e):
        onehot = jnp.where(row == slot_ref[0, e:e + 1, :], 1.0, 0.0).astype(BF16)
        acc = acc + _dot_tn(onehot, y_ref[e, 0])
    o_ref[0] = _rms(acc, gain_ref[...])


def _combine(slot, y, x1, gain, tt):
    b, e, s = slot.shape
    cap, d = y.shape[2], y.shape[3]
    return pl.pallas_call(
        _combine_kernel,
        grid=(b, s // tt),
        in_specs=[pl.BlockSpec((1, e, tt), lambda i, j: (i, 0, j)),
                  pl.BlockSpec((e, 1, cap, d), lambda i, j: (0, i, 0, 0)),
                  pl.BlockSpec((1, tt, d), lambda i, j: (i, j, 0)),
                  pl.BlockSpec((1, d), lambda i, j: (0, 0))],
        out_specs=pl.BlockSpec((1, tt, d), lambda i, j: (i, j, 0)),
        out_shape=jax.ShapeDtypeStruct((b, s, d), F32),
        compiler_params=_params(("parallel", "parallel")),
        name="combine",
    )(slot, y, x1, gain)


def _layer(x, cos, sin, norm_mix, w_in, ret_decay_fwd, ret_decay_bwd, ret_norm,
           gla_gate_w_fwd, gla_gate_b_fwd, gla_gate_w_bwd, gla_gate_b_bwd, gla_norm,
           w_branch_ret, w_branch_gla, w_out, norm_ffn, w_router, w_gate, w_up, w_down, norm_out):
    b, s, d = x.shape
    t = b * s
    ret_qk, ret_v = RET_HEADS * RET_DK, RET_HEADS * RET_DV
    gla_qk, gla_v = GLA_HEADS * GLA_DK, GLA_HEADS * GLA_DV
    rank = GLA_GATE_RANK
    gla0 = 2 * ret_qk + 2 * ret_v
    ga0 = gla0 + 2 * gla_qk + 2 * gla_v
    assert w_in.shape == (d, ga0 + 2 * rank + 2 * d)
    assert s % CHUNK == 0 and 3 * GATE_COPY <= LANES and ga0 % LANES == 0

    w_in_t = w_in.T.astype(BF16)
    w_gl = w_in_t[ga0 + 2 * rank:, :]

    gw = jnp.zeros((GLA_HEADS, GATE_COPY, 2 * GLA_DK), F32)
    gw = gw.at[:, :rank, :GLA_DK].set(gla_gate_w_fwd.reshape(rank, GLA_HEADS, GLA_DK).transpose(1, 0, 2))
    gw = gw.at[:, rank:, GLA_DK:].set(gla_gate_w_bwd.reshape(rank, GLA_HEADS, GLA_DK).transpose(1, 0, 2))
    gw_hi = gw.astype(BF16)
    gw_lo = (gw - gw_hi.astype(F32)).astype(BF16)
    gate_w = jnp.concatenate([gw_hi, gw_lo, gw_hi, jnp.zeros_like(gw_hi)], axis=1)
    gate_b = jnp.concatenate([gla_gate_b_fwd.reshape(GLA_HEADS, 1, GLA_DK),
                              gla_gate_b_bwd.reshape(GLA_HEADS, 1, GLA_DK)], axis=2)

    dec = jnp.stack([ret_decay_fwd, ret_decay_bwd], axis=1)[:, :, None]
    dec = jnp.pad(jnp.broadcast_to(dec, (RET_HEADS, 2, RET_DV)), ((0, 0), (0, 6), (0, 0)))

    x2 = x.reshape(t, d)
    ret, h3 = _retention(x, norm_mix[None, :], w_in_t, 0, cos, sin, dec, ret_norm[None, :])
    gla = _gla(h3, w_in_t, gla0, gate_w, gate_b, gla_norm[None, :])
    h = h3.reshape(t, d)

    n_e = w_router.shape[1]
    w_r = jnp.pad(w_router, ((0, 0), (0, LANES - n_e)))
    wr_hi = w_r.astype(BF16)
    wr_lo = (w_r - wr_hi.astype(F32)).astype(BF16)
    x1, h2, aff_t = _merge(h, ret.reshape(t, ret_v), gla.reshape(t, gla_v), x2, w_gl,
                           w_branch_ret.astype(BF16), w_branch_gla.astype(BF16), w_out.astype(BF16),
                           norm_ffn[None, :], jnp.concatenate([wr_hi, wr_lo], axis=1), n_e, b,
                           _pick(s, (512, 256, 128)))

    capacity = EC_CAPACITY_FACTOR * s // n_e
    slot = _route(aff_t, capacity)
    xg, gate = _dispatch(slot, aff_t, h2.reshape(b, s, d), capacity)
    f = w_gate.shape[2]
    y = _ffn(xg.reshape(n_e, b * capacity, d), gate.reshape(n_e, b * capacity, LANES),
             w_gate, w_up, w_down, _pick(f, (256, 128)), _pick(b * capacity, (1024, 512, 256, 128)))
    return _combine(slot, y.reshape(n_e, b, capacity, d), x1.reshape(b, s, d), norm_out,
                    _pick(s, (512, 256, 128)))


def kernel(x, positions, norm_mix, w_in, ret_decay_fwd, ret_decay_bwd, ret_norm, gla_gate_w_fwd,
           gla_gate_b_fwd, gla_gate_w_bwd, gla_gate_b_bwd, gla_norm, w_branch_ret, w_branch_gla,
           w_out, norm_ffn, w_router, w_gate, w_up, w_down, norm_final):
    depth = norm_mix.shape[0]
    assert depth == 1, "the final RMSNorm is fused into the last layer's combine stage"
    cos, sin = _rope_table(positions, RET_DK)
    return _layer(x, cos, sin, norm_mix[0], w_in[0], ret_decay_fwd[0], ret_decay_bwd[0], ret_norm[0],
                  gla_gate_w_fwd[0], gla_gate_b_fwd[0], gla_gate_w_bwd[0], gla_gate_b_bwd[0], gla_norm[0],
                  w_branch_ret[0], w_branch_gla[0], w_out[0], norm_ffn[0], w_router[0],
                  w_gate[0], w_up[0], w_down[0], norm_final[None, :])
```

```python
import functools

import jax
import jax.numpy as jnp
from jax import lax
from jax.experimental import pallas as pl
from jax.experimental.pallas import tpu as pltpu

F32 = jnp.float32
BF16 = jnp.bfloat16

RET_HEADS = 4
RET_DK = 128
RET_DV = 256
GLA_HEADS = 4
GLA_DK = 128
GLA_DV = 256
GLA_GATE_RANK = 16
GLA_GATE_NORMALIZER = 16.0
CHUNK = 128
EC_CAPACITY_FACTOR = 2
ROPE_THETA = 10000.0
NORM_EPS = 1e-6
LOG2_E = 1.4426950408889634

CHUNK_UNROLL = 16
LANES = 128
VMEM_LIMIT = 56 << 20


def _params(sem, vmem=VMEM_LIMIT):
    return pltpu.CompilerParams(dimension_semantics=sem, vmem_limit_bytes=vmem)


def _resident(shape):
    return pl.BlockSpec(shape, lambda *_: (0,) * len(shape), pipeline_mode=pl.Buffered(1))


def _pick(n, prefs):
    for p in prefs:
        if n % p == 0:
            return p
    return n


def _sigmoid(x):
    return 1.0 / (1.0 + jnp.exp(-x))


def _silu(x):
    half = 0.5 * x
    return half + half * jnp.tanh(half)


def _log_sigmoid(x):
    return jnp.minimum(x, 0.0) - jnp.log(1.0 + jnp.exp(-jnp.abs(x)))


def _rms(x, gain):
    return x * lax.rsqrt(jnp.mean(x * x, axis=-1, keepdims=True) + NORM_EPS) * gain


def _dot(a, b):
    return jnp.dot(a, b, preferred_element_type=F32)


def _dot_nt(a, b):
    return lax.dot_general(a, b, (((1,), (1,)), ((), ())), preferred_element_type=F32)


def _dot_tn(a, b):
    return lax.dot_general(a, b, (((0,), (0,)), ((), ())), preferred_element_type=F32)


def _split2(x):
    hi = x.astype(BF16)
    lo = (x - hi.astype(F32)).astype(BF16)
    return hi, lo


def _rope_table_kernel(pos_ref, freq_ref, shift_ref, cos_ref, sin_ref):
    half = freq_ref.shape[1] // 2
    ang = pos_ref[0].astype(F32) * freq_ref[...]
    tab = jnp.cos(ang - shift_ref[...])
    swapped = pltpu.roll(tab, half, 1)
    lower = lax.broadcasted_iota(jnp.int32, tab.shape, 1) < half
    cos_ref[0] = jnp.where(lower, tab, swapped)
    sin_ref[0] = jnp.where(lower, -swapped, tab)


def _rope_table(positions, dk):
    b, s = positions.shape
    half = jnp.arange(0, dk, 2, dtype=F32) / dk
    inv_freq = ROPE_THETA ** (-half)
    freq = jnp.concatenate([inv_freq, inv_freq])[None, :]
    shift = jnp.concatenate([jnp.zeros(dk // 2, F32), jnp.full(dk // 2, jnp.pi / 2, F32)])[None, :]
    out = jax.ShapeDtypeStruct((b, s, dk), F32)
    return pl.pallas_call(
        _rope_table_kernel,
        grid=(b,),
        in_specs=[pl.BlockSpec((1, s, 1), lambda i: (i, 0, 0)),
                  pl.BlockSpec((1, dk), lambda i: (0, 0)),
                  pl.BlockSpec((1, dk), lambda i: (0, 0))],
        out_specs=[pl.BlockSpec((1, s, dk), lambda i: (i, 0, 0))] * 2,
        out_shape=[out, out],
        compiler_params=_params(("parallel",)),
        name="rope_table",
    )(positions[:, :, None], freq, shift)


def _rope(t, cos, sin_signed):
    return t * cos + pltpu.roll(t, t.shape[-1] // 2, 1) * sin_signed


def _prefix_states(n_chunks, dk, dv, decay_f, decay_b, kv_ref, st_ref):
    def step(i, states):
        s_f, s_b = states
        n = n_chunks - 1 - i
        st_ref[i, 0:dk, :] = s_f.astype(BF16)
        st_ref[n, dk:2 * dk, :] = s_b.astype(BF16)
        return (decay_f(i) * s_f + kv_ref[i, 0:dk, :],
                decay_b(n) * s_b + kv_ref[n, dk:2 * dk, :])

    zero = jnp.zeros((dk, dv), F32)
    lax.fori_loop(0, n_chunks, step, (zero, zero), unroll=2)


def _chunk_rows(n):
    return pl.ds(pl.multiple_of(n * CHUNK, CHUNK), CHUNK)


def _row_tile(s_len):
    return _pick(s_len, (512, 256, CHUNK))


def _per_tile(chunk_pattern, s_len):
    return jnp.concatenate([chunk_pattern] * (_row_tile(s_len) // CHUNK), axis=0)


def _for_row_tiles(s_len, body):
    tile = _row_tile(s_len)

    def step(i, carry):
        body(pl.ds(pl.multiple_of(i * tile, tile), tile))
        return carry

    lax.fori_loop(0, s_len // tile, step, 0, unroll=True)


def _retention_kernel(x_ref, mix_gain_ref, wq_ref, wk_ref, wv_ref, wg_ref, cos_ref, sin_ref, dec_ref,
                      gain_ref, o_ref, h_ref, qk_ref, qw_ref, kw_ref, v_ref, gg_ref, a_ref, kv_ref, st_ref):
    c = CHUNK
    s_len, dk, dv = qk_ref.shape[0], qk_ref.shape[1] // 2, v_ref.shape[1]
    n_chunks = s_len // c
    scale = dk ** -0.5

    lg_f = _log_sigmoid(dec_ref[0, 0:1, :])
    lg_b = _log_sigmoid(dec_ref[0, 1:2, :])
    lgf_k, lgb_k = lg_f[:, :dk], lg_b[:, :dk]
    pos = lax.broadcasted_iota(jnp.int32, (c, dk), 0).astype(F32)
    wq_f = _per_tile(jnp.exp((pos + 1.0) * lgf_k), s_len)
    wk_f = _per_tile(jnp.exp((c - 1.0 - pos) * lgf_k), s_len)
    wq_b = _per_tile(jnp.exp((c - pos) * lgb_k), s_len)
    wk_b = _per_tile(jnp.exp(pos * lgb_k), s_len)
    ri = lax.broadcasted_iota(jnp.int32, (c, c), 0)
    ci = lax.broadcasted_iota(jnp.int32, (c, c), 1)
    lower = ri >= ci
    rel = (ri - ci).astype(F32)
    decay_mask = jnp.where(lower,
                           jnp.exp(jnp.where(lower, rel, 0.0) * lg_f[:, :c]),
                           jnp.exp(jnp.where(lower, 0.0, -rel) * lg_b[:, :c]))
    chunk_decay_f = jnp.exp(c * lg_f)
    chunk_decay_b = jnp.exp(c * lg_b)

    @pl.when(pl.program_id(1) == 0)
    def _():
        def norm(rows):
            h_ref[0, rows, :] = _rms(x_ref[0, rows, :], mix_gain_ref[...]).astype(h_ref.dtype)
        _for_row_tiles(s_len, norm)

    w_qk = jnp.concatenate([wq_ref[...], wk_ref[...]], axis=0)
    w_v = wv_ref[...]
    w_g = wg_ref[...]

    def project(rows):
        h = h_ref[0, rows, :]
        qk = _dot_nt(h, w_qk)
        cos, sin = cos_ref[0, rows, :], sin_ref[0, rows, :]
        qr = _rope(qk[:, :dk], cos, sin) * scale
        kr = _rope(qk[:, dk:], cos, sin)
        qk_ref[rows, :] = jnp.concatenate([qr, kr], axis=1).astype(BF16)
        qw_ref[rows, :] = jnp.concatenate([qr * wq_f, qr * wq_b], axis=1).astype(BF16)
        kw_ref[rows, :] = jnp.concatenate([kr * wk_f, kr * wk_b], axis=1).astype(BF16)
        v_ref[rows, :] = _dot_nt(h, w_v).astype(BF16)
        gg_ref[rows, :] = _silu(_dot_nt(h, w_g)) * gain_ref[...]

    _for_row_tiles(s_len, project)

    def phase1(n, carry):
        rows = _chunk_rows(n)
        kv_ref[n] = _dot_tn(kw_ref[rows, :], v_ref[rows, :])
        return carry

    lax.fori_loop(0, n_chunks, phase1, 0, unroll=CHUNK_UNROLL)
    _prefix_states(n_chunks, dk, dv, lambda n: chunk_decay_f, lambda n: chunk_decay_b, kv_ref, st_ref)

    def scores(n, carry):
        rows = _chunk_rows(n)
        qk = qk_ref[rows, :]
        a_ref[rows, :] = (_dot_nt(qk[:, :dk], qk[:, dk:]) * decay_mask).astype(BF16)
        return carry

    lax.fori_loop(0, n_chunks, scores, 0, unroll=CHUNK_UNROLL)

    def phase2(n, carry):
        rows = _chunk_rows(n)
        o = _dot(a_ref[rows, :], v_ref[rows, :]) + _dot(qw_ref[rows, :], st_ref[n])
        mu = jnp.mean(o, axis=-1, keepdims=True)
        d = o - mu
        var = jnp.mean(d * d, axis=-1, keepdims=True)
        o_ref[0, rows, :] = (d * lax.rsqrt(var + NORM_EPS) * gg_ref[rows, :]).astype(o_ref.dtype)
        return carry

    lax.fori_loop(0, n_chunks, phase2, 0, unroll=CHUNK_UNROLL)


def _retention(x, mix_gain, w_in, col0, cos, sin, dec, gain):
    b, s, d = x.shape
    h, dk, dv = RET_HEADS, RET_DK, RET_DV
    q0 = col0 // dk
    v0 = (col0 + 2 * h * dk) // dv
    n_chunks = s // CHUNK
    return pl.pallas_call(
        _retention_kernel,
        grid=(b, h),
        in_specs=[pl.BlockSpec((1, s, d), lambda i, j: (i, 0, 0)),
                  pl.BlockSpec((1, d), lambda i, j: (0, 0)),
                  pl.BlockSpec((dk, d), lambda i, j: (q0 + j, 0)),
                  pl.BlockSpec((dk, d), lambda i, j: (q0 + h + j, 0)),
                  pl.BlockSpec((dv, d), lambda i, j: (v0 + j, 0)),
                  pl.BlockSpec((dv, d), lambda i, j: (v0 + h + j, 0)),
                  pl.BlockSpec((1, s, dk), lambda i, j: (i, 0, 0)),
                  pl.BlockSpec((1, s, dk), lambda i, j: (i, 0, 0)),
                  pl.BlockSpec((1, 8, dv), lambda i, j: (j, 0, 0)),
                  pl.BlockSpec((1, dv), lambda i, j: (0, j))],
        out_specs=[pl.BlockSpec((1, s, dv), lambda i, j: (i, 0, j)),
                   pl.BlockSpec((1, s, d), lambda i, j: (i, 0, 0))],
        out_shape=[jax.ShapeDtypeStruct((b, s, h * dv), BF16),
                   jax.ShapeDtypeStruct((b, s, d), BF16)],
        scratch_shapes=[pltpu.VMEM((s, 2 * dk), BF16),
                        pltpu.VMEM((s, 2 * dk), BF16),
                        pltpu.VMEM((s, 2 * dk), BF16),
                        pltpu.VMEM((s, dv), BF16),
                        pltpu.VMEM((s, dv), F32),
                        pltpu.VMEM((s, CHUNK), BF16),
                        pltpu.VMEM((n_chunks, 2 * dk, dv), F32),
                        pltpu.VMEM((n_chunks, 2 * dk, dv), BF16)],
        compiler_params=_params(("parallel", "arbitrary")),
        name="retention",
    )(x, mix_gain, w_in, w_in, w_in, w_in, cos, sin, dec, gain)


GATE_COPY = 2 * GLA_GATE_RANK


def _cumsum_dot(tri2, x):
    hi, lo = _split2(x)
    return _dot(tri2, jnp.concatenate([hi, lo], axis=0))


def _gla_kernel(h_ref, wq_ref, wk_ref, wv_ref, wg_ref, wa_ref, gw_ref, gb_ref, gain_ref, o_ref,
                q_ref, k_ref, v_ref, gg_ref, ga_ref, la_ref, qs_ref, ks_ref, qw_ref, kw_ref, last_ref,
                a_ref, kv_ref, st_ref):
    c = CHUNK
    s_len, dk, dv = q_ref.shape[0], q_ref.shape[1], v_ref.shape[1]
    n_chunks = s_len // c
    scale = dk ** -0.5
    inv_norm = LOG2_E / GLA_GATE_NORMALIZER

    w_qk = jnp.concatenate([wq_ref[...], wk_ref[...]], axis=0)
    w_v = wv_ref[...]
    w_g = wg_ref[...]

    @pl.when(pl.program_id(1) == 0)
    def _():
        w_a = wa_ref[...]

        def gate_inputs(rows):
            x = _dot_nt(h_ref[0, rows, :], w_a)
            x_hi = x.astype(BF16).astype(F32)
            lane = lax.broadcasted_iota(jnp.int32, x.shape, 1)
            ga = jnp.where(lane < GATE_COPY, x_hi,
                           jnp.where(lane < 2 * GATE_COPY, pltpu.roll(x_hi, GATE_COPY, 1),
                                     jnp.where(lane < 3 * GATE_COPY, pltpu.roll(x - x_hi, 2 * GATE_COPY, 1),
                                               0.0)))
            ga_ref[rows, :] = ga.astype(BF16)

        _for_row_tiles(s_len, gate_inputs)

    def project(rows):
        h = h_ref[0, rows, :]
        qk = _dot_nt(h, w_qk)
        q_ref[rows, :] = qk[:, :dk] * scale
        k_ref[rows, :] = qk[:, dk:]
        v_ref[rows, :] = _dot_nt(h, w_v).astype(BF16)
        gg_ref[rows, :] = _silu(_dot_nt(h, w_g)) * gain_ref[...]
        la_ref[rows, :] = _log_sigmoid(_dot(ga_ref[rows, :], gw_ref[0]) + gb_ref[0]) * inv_norm

    _for_row_tiles(s_len, project)

    ri = lax.broadcasted_iota(jnp.int32, (c, c), 0)
    ci = lax.broadcasted_iota(jnp.int32, (c, c), 1)
    lower = ri >= ci
    tri_lower = jnp.where(lower, 1.0, 0.0).astype(BF16)
    tri_upper = jnp.where(ci >= ri, 1.0, 0.0).astype(BF16)
    tri2_lower = jnp.concatenate([tri_lower, tri_lower], axis=1)
    tri2_upper = jnp.concatenate([tri_upper, tri_upper], axis=1)

    def cumulate(n, carry):
        rows = _chunk_rows(n)
        cum_f = _cumsum_dot(tri2_lower, la_ref[rows, 0:dk])
        cum_b = _cumsum_dot(tri2_upper, la_ref[rows, dk:2 * dk])
        last_f, last_b = cum_f[c - 1:c, :], cum_b[0:1, :]
        ref_f, ref_b = cum_f[c // 2:c // 2 + 1, :], cum_b[c // 2 - 1:c // 2, :]
        q, k = q_ref[rows, :], k_ref[rows, :]
        q_f = q * jnp.exp2(cum_f - ref_f)
        q_b = q * jnp.exp2(cum_b - ref_b)
        qs_ref[rows, :] = jnp.concatenate([q_f, q_b], axis=1).astype(BF16)
        ks_ref[rows, :] = jnp.concatenate([k * jnp.exp2(ref_f - cum_f), k * jnp.exp2(ref_b - cum_b)],
                                          axis=1).astype(BF16)
        qw_ref[rows, :] = jnp.concatenate([q_f * jnp.exp2(ref_f), q_b * jnp.exp2(ref_b)], axis=1).astype(BF16)
        kw_ref[rows, :] = jnp.concatenate([k * jnp.exp2(last_f - cum_f), k * jnp.exp2(last_b - cum_b)],
                                          axis=1).astype(BF16)
        last_ref[n, 0:1, :] = last_f
        last_ref[n, 1:2, :] = last_b
        return carry

    lax.fori_loop(0, n_chunks, cumulate, 0, unroll=CHUNK_UNROLL)

    def phase1(n, carry):
        rows = _chunk_rows(n)
        kv_ref[n] = _dot_tn(v_ref[rows, :], kw_ref[rows, :])
        return carry

    lax.fori_loop(0, n_chunks, phase1, 0, unroll=CHUNK_UNROLL)

    def scan(i, states):
        s_f, s_b = states
        n = n_chunks - 1 - i
        st_ref[i, :, 0:dk] = s_f.astype(BF16)
        st_ref[n, :, dk:2 * dk] = s_b.astype(BF16)
        return (jnp.exp2(last_ref[i, 0:1, :]) * s_f + kv_ref[i, :, 0:dk],
                jnp.exp2(last_ref[n, 1:2, :]) * s_b + kv_ref[n, :, dk:2 * dk])

    zero = jnp.zeros((dv, dk), F32)
    lax.fori_loop(0, n_chunks, scan, (zero, zero), unroll=2)

    def scores(n, carry):
        rows = _chunk_rows(n)
        qs, ks = qs_ref[rows, :], ks_ref[rows, :]
        a_ref[rows, :] = jnp.where(lower, _dot_nt(qs[:, :dk], ks[:, :dk]),
                                   _dot_nt(qs[:, dk:], ks[:, dk:])).astype(BF16)
        return carry

    lax.fori_loop(0, n_chunks, scores, 0, unroll=CHUNK_UNROLL)

    def phase2(n, carry):
        rows = _chunk_rows(n)
        o = _dot(a_ref[rows, :], v_ref[rows, :]) + _dot_nt(qw_ref[rows, :], st_ref[n])
        rms = lax.rsqrt(jnp.mean(o * o, axis=-1, keepdims=True) + NORM_EPS)
        o_ref[0, rows, :] = (o * rms * gg_ref[rows, :]).astype(o_ref.dtype)
        return carry

    lax.fori_loop(0, n_chunks, phase2, 0, unroll=CHUNK_UNROLL)


def _gla(h3, w_in, col0, gate_w, gate_b, gain):
    b, s, d = h3.shape
    h, dk, dv = GLA_HEADS, GLA_DK, GLA_DV
    q0 = col0 // dk
    v0 = (col0 + 2 * h * dk) // dv
    a0 = (col0 + 2 * h * dk + 2 * h * dv) // LANES
    n_chunks = s // CHUNK
    return pl.pallas_call(
        _gla_kernel,
        grid=(b, h),
        in_specs=[pl.BlockSpec((1, s, d), lambda i, j: (i, 0, 0)),
                  pl.BlockSpec((dk, d), lambda i, j: (q0 + j, 0)),
                  pl.BlockSpec((dk, d), lambda i, j: (q0 + h + j, 0)),
                  pl.BlockSpec((dv, d), lambda i, j: (v0 + j, 0)),
                  pl.BlockSpec((dv, d), lambda i, j: (v0 + h + j, 0)),
                  pl.BlockSpec((LANES, d), lambda i, j: (a0, 0)),
                  pl.BlockSpec((1, LANES, 2 * dk), lambda i, j: (j, 0, 0)),
                  pl.BlockSpec((1, 1, 2 * dk), lambda i, j: (j, 0, 0)),
                  pl.BlockSpec((1, dv), lambda i, j: (0, j))],
        out_specs=pl.BlockSpec((1, s, dv), lambda i, j: (i, 0, j)),
        out_shape=jax.ShapeDtypeStruct((b, s, h * dv), BF16),
        scratch_shapes=[pltpu.VMEM((s, dk), F32),
                        pltpu.VMEM((s, dk), F32),
                        pltpu.VMEM((s, dv), BF16),
                        pltpu.VMEM((s, dv), F32),
                        pltpu.VMEM((s, LANES), BF16),
                        pltpu.VMEM((s, 2 * dk), F32),
                        pltpu.VMEM((s, 2 * dk), BF16),
                        pltpu.VMEM((s, 2 * dk), BF16),
                        pltpu.VMEM((s, 2 * dk), BF16),
                        pltpu.VMEM((s, 2 * dk), BF16),
                        pltpu.VMEM((n_chunks, 8, dk), F32),
                        pltpu.VMEM((s, CHUNK), BF16),
                        pltpu.VMEM((n_chunks, dv, 2 * dk), F32),
                        pltpu.VMEM((n_chunks, dv, 2 * dk), BF16)],
        compiler_params=_params(("parallel", "arbitrary")),
        name="gla",
    )(h3, w_in, w_in, w_in, w_in, w_in, gate_w, gate_b, gain)


MERGE_COLS = 256


def _merge_kernel(n_experts, h_ref, ret_ref, gla_ref, x_ref, wgl_ref, wr_ref, wg_ref, wo_ref, gain_ref,
                  wr2_ref, x1_ref, h2_ref, aff_ref):
    d = x_ref.shape[1]
    h, ret, gla = h_ref[...], ret_ref[...], gla_ref[...]
    blocks = []
    for j in range(0, d, MERGE_COLS):
        cols = slice(j, j + MERGE_COLS)
        cols_gla = slice(d + j, d + j + MERGE_COLS)
        m = (_sigmoid(_dot_nt(h, wgl_ref[cols, :])) * _dot(ret, wr_ref[:, cols])
             + _sigmoid(_dot_nt(h, wgl_ref[cols_gla, :])) * _dot(gla, wg_ref[:, cols]))
        blocks.append(m.astype(BF16))
    x1 = x_ref[...] + _dot(jnp.concatenate(blocks, axis=1), wo_ref[...])
    x1_ref[...] = x1
    h2 = _rms(x1, gain_ref[...]).astype(BF16)
    h2_ref[...] = h2
    logits2 = _dot(h2, wr2_ref[...])
    logits = logits2[:, :LANES] + logits2[:, LANES:]
    lane = lax.broadcasted_iota(jnp.int32, logits.shape, 1)
    logits = jnp.where(lane < n_experts, logits, -jnp.inf)
    p = jnp.exp(logits - jnp.max(logits, axis=-1, keepdims=True))
    aff = p / jnp.sum(p, axis=-1, keepdims=True)
    aff_ref[0] = aff.T[0:n_experts, :]


def _merge(h, ret, gla, x2, w_gl, w_ret, w_gla, w_out, gain, w_router2, n_experts, batch, tm):
    t, d = x2.shape
    s = t // batch
    per_b = s // tm
    rows = lambda width: pl.BlockSpec((tm, width), lambda i: (i, 0))
    return pl.pallas_call(
        functools.partial(_merge_kernel, n_experts),
        grid=(t // tm,),
        in_specs=[rows(d), rows(ret.shape[1]), rows(gla.shape[1]), rows(d),
                  _resident(w_gl.shape), _resident(w_ret.shape), _resident(w_gla.shape),
                  _resident(w_out.shape), _resident((1, d)),
                  _resident(w_router2.shape)],
        out_specs=[rows(d), rows(d),
                   pl.BlockSpec((1, n_experts, tm), lambda i: (i // per_b, 0, i % per_b))],
        out_shape=[jax.ShapeDtypeStruct((t, d), F32),
                   jax.ShapeDtypeStruct((t, d), BF16),
                   jax.ShapeDtypeStruct((batch, n_experts, s), F32)],
        compiler_params=_params(("parallel",)),
        name="merge",
    )(h, ret, gla, x2, w_gl, w_ret, w_gla, w_out, gain, w_router2)


def _prefix_count(mask):
    s = mask.shape[1]
    ri = lax.broadcasted_iota(jnp.int32, (LANES, LANES), 0)
    ci = lax.broadcasted_iota(jnp.int32, (LANES, LANES), 1)
    tri = jnp.where(ri <= ci, 1.0, 0.0).astype(BF16)
    off = jnp.zeros((mask.shape[0], 1), F32)
    parts = []
    for j in range(s // LANES):
        p = _dot(mask[:, j * LANES:(j + 1) * LANES].astype(BF16), tri) + off
        parts.append(p)
        off = p[:, LANES - 1:LANES]
    return jnp.concatenate(parts, axis=1)


def _route_kernel(capacity, aff_ref, slot_ref):
    a = aff_ref[0]
    bits = lax.bitcast_convert_type(a, jnp.int32)
    n_e = a.shape[0]
    cap = float(capacity)

    def count(pred):
        return jnp.sum(jnp.where(pred, 1.0, 0.0), axis=1, keepdims=True)

    def search(_, c):
        lo, hi = c
        mid = lo + lax.shift_right_logical(hi - lo, 1)
        ok = count(bits >= mid) >= cap
        return jnp.where(ok, mid, lo), jnp.where(ok, hi, mid)

    lo0 = jnp.zeros((n_e, 1), jnp.int32)
    hi0 = jnp.full((n_e, 1), 0x7F800000, jnp.int32)
    thr_bits, _ = lax.fori_loop(0, 31, search, (lo0, hi0))
    thr0 = jnp.max(jnp.where(bits <= thr_bits, a, -1.0), axis=1, keepdims=True)

    def counts(v):
        return count(a >= v), count(a > v)

    def unsettled(state):
        _, c_ge, c_gt = state
        bad = jnp.where(c_ge < cap, 1.0, jnp.where(c_gt >= cap, 1.0, 0.0))
        return jnp.max(bad, axis=0, keepdims=True)[0, 0] > 0.0

    def step(state):
        v, c_ge, c_gt = state
        below = jnp.max(jnp.where(a < v, a, -1.0), axis=1, keepdims=True)
        above = jnp.min(jnp.where(a > v, a, 2.0), axis=1, keepdims=True)
        v = jnp.where(c_ge < cap, below, jnp.where(c_gt >= cap, above, v))
        return (v,) + counts(v)

    thr, _, n_gt = lax.while_loop(unsettled, step, (thr0,) + counts(thr0))

    gt = a > thr
    eq = a == thr
    need = cap - n_gt
    eq_rank = _prefix_count(jnp.where(eq, 1.0, 0.0))
    sel = jnp.where(gt, 1.0, jnp.where(eq, jnp.where(eq_rank <= need, 1.0, 0.0), 0.0))
    pos = _prefix_count(sel)
    slot_ref[0] = jnp.where(sel > 0.0, pos - 1.0, -1.0)


def _route(aff_t, capacity):
    b, e, s = aff_t.shape
    return pl.pallas_call(
        functools.partial(_route_kernel, capacity),
        grid=(1,),
        in_specs=[pl.BlockSpec((1, b * e, s), lambda i: (0, 0, 0))],
        out_specs=pl.BlockSpec((1, b * e, s), lambda i: (0, 0, 0)),
        out_shape=jax.ShapeDtypeStruct((1, b * e, s), F32),
        compiler_params=_params(("arbitrary",)),
        name="route",
    )(aff_t.reshape(1, b * e, s)).reshape(b, e, s)


def _dispatch_kernel(slot_ref, aff_ref, h_ref, xg_ref, gate_ref):
    group, cap = xg_ref.shape[0], xg_ref.shape[2]
    h = h_ref[0]
    row = lax.broadcasted_iota(jnp.int32, (cap, h.shape[0]), 0).astype(F32)
    for g in range(group):
        hit = row == slot_ref[0, g:g + 1, :]
        xg_ref[g, 0] = _dot(jnp.where(hit, 1.0, 0.0).astype(BF16), h).astype(xg_ref.dtype)
        gate = jnp.sum(jnp.where(hit, aff_ref[0, g:g + 1, :], 0.0), axis=1, keepdims=True)
        gate_ref[g, 0] = jnp.broadcast_to(gate, gate_ref.shape[2:])


def _dispatch(slot, aff_t, h2, capacity):
    b, e, s = slot.shape
    d = h2.shape[-1]
    group = _pick(e, (8,))
    return pl.pallas_call(
        _dispatch_kernel,
        grid=(b, e // group),
        in_specs=[pl.BlockSpec((1, group, s), lambda i, j: (i, j, 0)),
                  pl.BlockSpec((1, group, s), lambda i, j: (i, j, 0)),
                  pl.BlockSpec((1, s, d), lambda i, j: (i, 0, 0))],
        out_specs=[pl.BlockSpec((group, 1, capacity, d), lambda i, j: (j, i, 0, 0)),
                   pl.BlockSpec((group, 1, capacity, LANES), lambda i, j: (j, i, 0, 0))],
        out_shape=[jax.ShapeDtypeStruct((e, b, capacity, d), BF16),
                   jax.ShapeDtypeStruct((e, b, capacity, LANES), F32)],
        compiler_params=_params(("parallel", "parallel")),
        name="dispatch",
    )(slot, aff_t, h2)


def _ffn_kernel(tm, n_f, x_ref, gate_ref, wg_ref, wu_ref, wd_ref, y_ref, acc_ref):
    f = pl.program_id(1)
    w_gate = wg_ref[0].astype(BF16)
    w_up = wu_ref[0].astype(BF16)
    w_down = wd_ref[0].astype(BF16)
    m = x_ref.shape[1]
    d = x_ref.shape[2]

    def step(first, final):
        for i in range(m // tm):
            rows = pl.ds(i * tm, tm)
            x = x_ref[0, rows, :]
            act = (_silu(_dot(x, w_gate)) * _dot(x, w_up)).astype(BF16)
            part = _dot(act, w_down)
            if not first:
                part = acc_ref[rows, :] + part
            if final:
                gate = jnp.concatenate([gate_ref[0, rows, :]] * (d // LANES), axis=1)
                y_ref[0, rows, :] = (part * gate).astype(y_ref.dtype)
            else:
                acc_ref[rows, :] = part

    if n_f == 1:
        step(True, True)
    else:
        pl.when(f == 0)(lambda: step(True, False))
        if n_f > 2:
            pl.when(jnp.logical_and(f > 0, f < n_f - 1))(lambda: step(False, False))
        pl.when(f == n_f - 1)(lambda: step(False, True))


def _ffn(xg, gate, w_gate, w_up, w_down, tf, tm):
    e, m, d = xg.shape
    f = w_gate.shape[2]
    return pl.pallas_call(
        functools.partial(_ffn_kernel, tm, f // tf),
        grid=(e, f // tf),
        in_specs=[pl.BlockSpec((1, m, d), lambda i, j: (i, 0, 0)),
                  pl.BlockSpec((1, m, LANES), lambda i, j: (i, 0, 0)),
                  pl.BlockSpec((1, d, tf), lambda i, j: (i, 0, j)),
                  pl.BlockSpec((1, d, tf), lambda i, j: (i, 0, j)),
                  pl.BlockSpec((1, tf, d), lambda i, j: (i, j, 0))],
        out_specs=pl.BlockSpec((1, m, d), lambda i, j: (i, 0, 0)),
        out_shape=jax.ShapeDtypeStruct((e, m, d), BF16),
        scratch_shapes=[pltpu.VMEM((m, d), F32)],
        compiler_params=_params(("parallel", "arbitrary")),
        name="ffn",
    )(xg, gate, w_gate, w_up, w_down)


def _combine_kernel(slot_ref, y_ref, x1_ref, gain_ref, o_ref):
    n_e, cap = y_ref.shape[0], y_ref.shape[2]
    tt = x1_ref.shape[1]
    acc = x1_ref[0]
    row = lax.broadcasted_iota(jnp.int32, (cap, tt), 0).astype(F32)
    for e in range(n_e):
        onehot = jnp.where(row == slot_ref[0, e:e + 1, :], 1.0, 0.0).astype(BF16)
        acc = acc + _dot_tn(onehot, y_ref[e, 0])
    o_ref[0] = _rms(acc, gain_ref[...])


def _combine(slot, y, x1, gain, tt):
    b, e, s = slot.shape
    cap, d = y.shape[2], y.shape[3]
    return pl.pallas_call(
        _combine_kernel,
        grid=(b, s // tt),
        in_specs=[pl.BlockSpec((1, e, tt), lambda i, j: (i, 0, j)),
                  pl.BlockSpec((e, 1, cap, d), lambda i, j: (0, i, 0, 0)),
                  pl.BlockSpec((1, tt, d), lambda i, j: (i, j, 0)),
                  pl.BlockSpec((1, d), lambda i, j: (0, 0))],
        out_specs=pl.BlockSpec((1, tt, d), lambda i, j: (i, j, 0)),
        out_shape=jax.ShapeDtypeStruct((b, s, d), F32),
        compiler_params=_params(("parallel", "parallel")),
        name="combine",
    )(slot, y, x1, gain)


def _layer(x, cos, sin, norm_mix, w_in, ret_decay_fwd, ret_decay_bwd, ret_norm,
           gla_gate_w_fwd, gla_gate_b_fwd, gla_gate_w_bwd, gla_gate_b_bwd, gla_norm,
           w_branch_ret, w_branch_gla, w_out, norm_ffn, w_router, w_gate, w_up, w_down, norm_out):
    b, s, d = x.shape
    t = b * s
    ret_qk, ret_v = RET_HEADS * RET_DK, RET_HEADS * RET_DV
    gla_qk, gla_v = GLA_HEADS * GLA_DK, GLA_HEADS * GLA_DV
    rank = GLA_GATE_RANK
    gla0 = 2 * ret_qk + 2 * ret_v
    ga0 = gla0 + 2 * gla_qk + 2 * gla_v
    assert w_in.shape == (d, ga0 + 2 * rank + 2 * d)
    assert s % CHUNK == 0 and 3 * GATE_COPY <= LANES and ga0 % LANES == 0

    w_in_t = w_in.T.astype(BF16)
    w_gl = w_in_t[ga0 + 2 * rank:, :]

    gw = jnp.zeros((GLA_HEADS, GATE_COPY, 2 * GLA_DK), F32)
    gw = gw.at[:, :rank, :GLA_DK].set(gla_gate_w_fwd.reshape(rank, GLA_HEADS, GLA_DK).transpose(1, 0, 2))
    gw = gw.at[:, rank:, GLA_DK:].set(gla_gate_w_bwd.reshape(rank, GLA_HEADS, GLA_DK).transpose(1, 0, 2))
    gw_hi = gw.astype(BF16)
    gw_lo = (gw - gw_hi.astype(F32)).astype(BF16)
    gate_w = jnp.concatenate([gw_hi, gw_lo, gw_hi, jnp.zeros_like(gw_hi)], axis=1)
    gate_b = jnp.concatenate([gla_gate_b_fwd.reshape(GLA_HEADS, 1, GLA_DK),
                              gla_gate_b_bwd.reshape(GLA_HEADS, 1, GLA_DK)], axis=2)

    dec = jnp.stack([ret_decay_fwd, ret_decay_bwd], axis=1)[:, :, None]
    dec = jnp.pad(jnp.broadcast_to(dec, (RET_HEADS, 2, RET_DV)), ((0, 0), (0, 6), (0, 0)))

    x2 = x.reshape(t, d)
    ret, h3 = _retention(x, norm_mix[None, :], w_in_t, 0, cos, sin, dec, ret_norm[None, :])
    gla = _gla(h3, w_in_t, gla0, gate_w, gate_b, gla_norm[None, :])
    h = h3.reshape(t, d)

    n_e = w_router.shape[1]
    w_r = jnp.pad(w_router, ((0, 0), (0, LANES - n_e)))
    wr_hi = w_r.astype(BF16)
    wr_lo = (w_r - wr_hi.astype(F32)).astype(BF16)
    x1, h2, aff_t = _merge(h, ret.reshape(t, ret_v), gla.reshape(t, gla_v), x2, w_gl,
                           w_branch_ret.astype(BF16), w_branch_gla.astype(BF16), w_out.astype(BF16),
                           norm_ffn[None, :], jnp.concatenate([wr_hi, wr_lo], axis=1), n_e, b,
                           _pick(s, (1024, 512, 256, 128)))

    capacity = EC_CAPACITY_FACTOR * s // n_e
    slot = _route(aff_t, capacity)
    xg, gate = _dispatch(slot, aff_t, h2.reshape(b, s, d), capacity)
    f = w_gate.shape[2]
    y = _ffn(xg.reshape(n_e, b * capacity, d), gate.reshape(n_e, b * capacity, LANES),
             w_gate, w_up, w_down, _pick(f, (256, 128)), _pick(b * capacity, (1024, 512, 256, 128)))
    return _combine(slot, y.reshape(n_e, b, capacity, d), x1.reshape(b, s, d), norm_out,
                    _pick(s, (1024, 512, 256, 128)))


def kernel(x, positions, norm_mix, w_in, ret_decay_fwd, ret_decay_bwd, ret_norm, gla_gate_w_fwd,
           gla_gate_b_fwd, gla_gate_w_bwd, gla_gate_b_bwd, gla_norm, w_branch_ret, w_branch_gla,
           w_out, norm_ffn, w_router, w_gate, w_up, w_down, norm_final):
    depth = norm_mix.shape[0]
    assert depth == 1, "the final RMSNorm is fused into the last layer's combine stage"
    cos, sin = _rope_table(positions, RET_DK)
    return _layer(x, cos, sin, norm_mix[0], w_in[0], ret_decay_fwd[0], ret_decay_bwd[0], ret_norm[0],
                  gla_gate_w_fwd[0], gla_gate_b_fwd[0], gla_gate_w_bwd[0], gla_gate_b_bwd[0], gla_norm[0],
                  w_branch_ret[0], w_branch_gla[0], w_out[0], norm_ffn[0], w_router[0],
                  w_gate[0], w_up[0], w_down[0], norm_final[None, :])
```

```python
import functools

import jax
import jax.numpy as jnp
from jax import lax
from jax.experimental import pallas as pl
from jax.experimental.pallas import tpu as pltpu

F32 = jnp.float32
BF16 = jnp.bfloat16

RET_HEADS = 4
RET_DK = 128
RET_DV = 256
GLA_HEADS = 4
GLA_DK = 128
GLA_DV = 256
GLA_GATE_RANK = 16
GLA_GATE_NORMALIZER = 16.0
CHUNK = 128
EC_CAPACITY_FACTOR = 2
ROPE_THETA = 10000.0
NORM_EPS = 1e-6
LOG2_E = 1.4426950408889634

CHUNK_UNROLL = 16
LANES = 128
VMEM_LIMIT = 56 << 20


def _params(sem, vmem=VMEM_LIMIT):
    return pltpu.CompilerParams(dimension_semantics=sem, vmem_limit_bytes=vmem)


def _resident(shape):
    return pl.BlockSpec(shape, lambda *_: (0,) * len(shape), pipeline_mode=pl.Buffered(1))


def _pick(n, prefs):
    for p in prefs:
        if n % p == 0:
            return p
    return n


def _sigmoid(x):
    return 1.0 / (1.0 + jnp.exp(-x))


def _silu(x):
    half = 0.5 * x
    return half + half * jnp.tanh(half)


def _log_sigmoid(x):
    return jnp.minimum(x, 0.0) - jnp.log(1.0 + jnp.exp(-jnp.abs(x)))


def _rms(x, gain):
    return x * lax.rsqrt(jnp.mean(x * x, axis=-1, keepdims=True) + NORM_EPS) * gain


def _dot(a, b):
    return jnp.dot(a, b, preferred_element_type=F32)


def _dot_nt(a, b):
    return lax.dot_general(a, b, (((1,), (1,)), ((), ())), preferred_element_type=F32)


def _dot_tn(a, b):
    return lax.dot_general(a, b, (((0,), (0,)), ((), ())), preferred_element_type=F32)


def _split2(x):
    hi = x.astype(BF16)
    lo = (x - hi.astype(F32)).astype(BF16)
    return hi, lo


def _rope_table_kernel(pos_ref, freq_ref, shift_ref, cos_ref, sin_ref):
    half = freq_ref.shape[1] // 2
    ang = pos_ref[0].astype(F32) * freq_ref[...]
    tab = jnp.cos(ang - shift_ref[...])
    swapped = pltpu.roll(tab, half, 1)
    lower = lax.broadcasted_iota(jnp.int32, tab.shape, 1) < half
    cos_ref[0] = jnp.where(lower, tab, swapped)
    sin_ref[0] = jnp.where(lower, -swapped, tab)


def _rope_table(positions, dk):
    b, s = positions.shape
    half = jnp.arange(0, dk, 2, dtype=F32) / dk
    inv_freq = ROPE_THETA ** (-half)
    freq = jnp.concatenate([inv_freq, inv_freq])[None, :]
    shift = jnp.concatenate([jnp.zeros(dk // 2, F32), jnp.full(dk // 2, jnp.pi / 2, F32)])[None, :]
    out = jax.ShapeDtypeStruct((b, s, dk), F32)
    return pl.pallas_call(
        _rope_table_kernel,
        grid=(b,),
        in_specs=[pl.BlockSpec((1, s, 1), lambda i: (i, 0, 0)),
                  pl.BlockSpec((1, dk), lambda i: (0, 0)),
                  pl.BlockSpec((1, dk), lambda i: (0, 0))],
        out_specs=[pl.BlockSpec((1, s, dk), lambda i: (i, 0, 0))] * 2,
        out_shape=[out, out],
        compiler_params=_params(("parallel",)),
        name="rope_table",
    )(positions[:, :, None], freq, shift)


def _rope(t, cos, sin_signed):
    return t * cos + pltpu.roll(t, t.shape[-1] // 2, 1) * sin_signed


def _chunk_rows(n):
    return pl.ds(pl.multiple_of(n * CHUNK, CHUNK), CHUNK)


def _row_tile(s_len):
    return _pick(s_len, (512, 256, CHUNK))


def _per_tile(chunk_pattern, s_len):
    return jnp.concatenate([chunk_pattern] * (_row_tile(s_len) // CHUNK), axis=0)


def _for_row_tiles(s_len, body):
    tile = _row_tile(s_len)

    def step(i, carry):
        body(pl.ds(pl.multiple_of(i * tile, tile), tile))
        return carry

    lax.fori_loop(0, s_len // tile, step, 0, unroll=True)


def _retention_kernel(x_ref, mix_gain_ref, wq_ref, wk_ref, wv_ref, wg_ref, cos_ref, sin_ref, dec_ref,
                      gain_ref, o_ref, h_ref, qk_ref, qw_ref, kw_ref, v_ref, gg_ref, a_ref, st_ref):
    c = CHUNK
    s_len, dk, dv = qk_ref.shape[0], qk_ref.shape[1] // 2, v_ref.shape[1]
    n_chunks = s_len // c
    scale = dk ** -0.5

    lg_f = _log_sigmoid(dec_ref[0, 0:1, :])
    lg_b = _log_sigmoid(dec_ref[0, 1:2, :])
    lgf_k, lgb_k = lg_f[:, :dk], lg_b[:, :dk]
    pos = lax.broadcasted_iota(jnp.int32, (c, dk), 0).astype(F32)
    wq_f = _per_tile(jnp.exp((pos + 1.0) * lgf_k), s_len)
    wk_f = _per_tile(jnp.exp((c - 1.0 - pos) * lgf_k), s_len)
    wq_b = _per_tile(jnp.exp((c - pos) * lgb_k), s_len)
    wk_b = _per_tile(jnp.exp(pos * lgb_k), s_len)
    ri = lax.broadcasted_iota(jnp.int32, (c, c), 0)
    ci = lax.broadcasted_iota(jnp.int32, (c, c), 1)
    lower = ri >= ci
    rel = (ri - ci).astype(F32)
    decay_mask = jnp.where(lower,
                           jnp.exp(jnp.where(lower, rel, 0.0) * lg_f[:, :c]),
                           jnp.exp(jnp.where(lower, 0.0, -rel) * lg_b[:, :c]))
    chunk_decay_f = jnp.exp(c * lg_f)
    chunk_decay_b = jnp.exp(c * lg_b)

    @pl.when(pl.program_id(1) == 0)
    def _():
        def norm(rows):
            h_ref[0, rows, :] = _rms(x_ref[0, rows, :], mix_gain_ref[...]).astype(h_ref.dtype)
        _for_row_tiles(s_len, norm)

    w_qk = jnp.concatenate([wq_ref[...], wk_ref[...]], axis=0)
    w_v = wv_ref[...]
    w_g = wg_ref[...]

    def project(rows):
        h = h_ref[0, rows, :]
        qk = _dot_nt(h, w_qk)
        cos, sin = cos_ref[0, rows, :], sin_ref[0, rows, :]
        qr = _rope(qk[:, :dk], cos, sin) * scale
        kr = _rope(qk[:, dk:], cos, sin)
        qk_ref[rows, :] = jnp.concatenate([qr, kr], axis=1).astype(BF16)
        qw_ref[rows, :] = jnp.concatenate([qr * wq_f, qr * wq_b], axis=1).astype(BF16)
        kw_ref[rows, :] = jnp.concatenate([kr * wk_f, kr * wk_b], axis=1).astype(BF16)
        v_ref[rows, :] = _dot_nt(h, w_v).astype(BF16)
        gg_ref[rows, :] = _silu(_dot_nt(h, w_g)) * gain_ref[...]

    _for_row_tiles(s_len, project)

    kv = [_dot_tn(kw_ref[n * c:(n + 1) * c, :], v_ref[n * c:(n + 1) * c, :]) for n in range(n_chunks)]
    s_f = s_b = jnp.zeros((dk, dv), F32)
    for i in range(n_chunks):
        n = n_chunks - 1 - i
        st_ref[i, 0:dk, :] = s_f.astype(BF16)
        st_ref[n, dk:2 * dk, :] = s_b.astype(BF16)
        if i + 1 < n_chunks:
            s_f = chunk_decay_f * s_f + kv[i][0:dk, :]
            s_b = chunk_decay_b * s_b + kv[n][dk:2 * dk, :]

    def scores(n, carry):
        rows = _chunk_rows(n)
        qk = qk_ref[rows, :]
        a_ref[rows, :] = (_dot_nt(qk[:, :dk], qk[:, dk:]) * decay_mask).astype(BF16)
        return carry

    lax.fori_loop(0, n_chunks, scores, 0, unroll=CHUNK_UNROLL)

    def phase2(n, carry):
        rows = _chunk_rows(n)
        o = _dot(a_ref[rows, :], v_ref[rows, :]) + _dot(qw_ref[rows, :], st_ref[n])
        mu = jnp.mean(o, axis=-1, keepdims=True)
        d = o - mu
        var = jnp.mean(d * d, axis=-1, keepdims=True)
        o_ref[0, rows, :] = (d * lax.rsqrt(var + NORM_EPS) * gg_ref[rows, :]).astype(o_ref.dtype)
        return carry

    lax.fori_loop(0, n_chunks, phase2, 0, unroll=CHUNK_UNROLL)


def _retention(x, mix_gain, w_in, col0, cos, sin, dec, gain):
    b, s, d = x.shape
    h, dk, dv = RET_HEADS, RET_DK, RET_DV
    q0 = col0 // dk
    v0 = (col0 + 2 * h * dk) // dv
    n_chunks = s // CHUNK
    return pl.pallas_call(
        _retention_kernel,
        grid=(b, h),
        in_specs=[pl.BlockSpec((1, s, d), lambda i, j: (i, 0, 0)),
                  pl.BlockSpec((1, d), lambda i, j: (0, 0)),
                  pl.BlockSpec((dk, d), lambda i, j: (q0 + j, 0)),
                  pl.BlockSpec((dk, d), lambda i, j: (q0 + h + j, 0)),
                  pl.BlockSpec((dv, d), lambda i, j: (v0 + j, 0)),
                  pl.BlockSpec((dv, d), lambda i, j: (v0 + h + j, 0)),
                  pl.BlockSpec((1, s, dk), lambda i, j: (i, 0, 0)),
                  pl.BlockSpec((1, s, dk), lambda i, j: (i, 0, 0)),
                  pl.BlockSpec((1, 8, dv), lambda i, j: (j, 0, 0)),
                  pl.BlockSpec((1, dv), lambda i, j: (0, j))],
        out_specs=[pl.BlockSpec((1, s, dv), lambda i, j: (i, 0, j)),
                   pl.BlockSpec((1, s, d), lambda i, j: (i, 0, 0))],
        out_shape=[jax.ShapeDtypeStruct((b, s, h * dv), BF16),
                   jax.ShapeDtypeStruct((b, s, d), BF16)],
        scratch_shapes=[pltpu.VMEM((s, 2 * dk), BF16),
                        pltpu.VMEM((s, 2 * dk), BF16),
                        pltpu.VMEM((s, 2 * dk), BF16),
                        pltpu.VMEM((s, dv), BF16),
                        pltpu.VMEM((s, dv), F32),
                        pltpu.VMEM((s, CHUNK), BF16),
                        pltpu.VMEM((n_chunks, 2 * dk, dv), BF16)],
        compiler_params=_params(("parallel", "arbitrary")),
        name="retention",
    )(x, mix_gain, w_in, w_in, w_in, w_in, cos, sin, dec, gain)


GATE_COPY = 2 * GLA_GATE_RANK


def _cumsum_dot(tri2, x):
    hi, lo = _split2(x)
    return _dot(tri2, jnp.concatenate([hi, lo], axis=0))


def _gla_kernel(h_ref, wq_ref, wk_ref, wv_ref, wg_ref, wa_ref, gw_ref, gb_ref, gain_ref, o_ref,
                q_ref, k_ref, v_ref, gg_ref, ga_ref, la_ref, qs_ref, ks_ref, qw_ref, kw_ref, last_ref,
                a_ref, st_ref):
    c = CHUNK
    s_len, dk, dv = q_ref.shape[0], q_ref.shape[1], v_ref.shape[1]
    n_chunks = s_len // c
    scale = dk ** -0.5
    inv_norm = LOG2_E / GLA_GATE_NORMALIZER

    w_qk = jnp.concatenate([wq_ref[...], wk_ref[...]], axis=0)
    w_v = wv_ref[...]
    w_g = wg_ref[...]

    @pl.when(pl.program_id(1) == 0)
    def _():
        w_a = wa_ref[...]

        def gate_inputs(rows):
            x = _dot_nt(h_ref[0, rows, :], w_a)
            x_hi = x.astype(BF16).astype(F32)
            lane = lax.broadcasted_iota(jnp.int32, x.shape, 1)
            ga = jnp.where(lane < GATE_COPY, x_hi,
                           jnp.where(lane < 2 * GATE_COPY, pltpu.roll(x_hi, GATE_COPY, 1),
                                     jnp.where(lane < 3 * GATE_COPY, pltpu.roll(x - x_hi, 2 * GATE_COPY, 1),
                                               0.0)))
            ga_ref[rows, :] = ga.astype(BF16)

        _for_row_tiles(s_len, gate_inputs)

    def project(rows):
        h = h_ref[0, rows, :]
        qk = _dot_nt(h, w_qk)
        q_ref[rows, :] = qk[:, :dk] * scale
        k_ref[rows, :] = qk[:, dk:]
        v_ref[rows, :] = _dot_nt(h, w_v).astype(BF16)
        gg_ref[rows, :] = _silu(_dot_nt(h, w_g)) * gain_ref[...]
        la_ref[rows, :] = _log_sigmoid(_dot(ga_ref[rows, :], gw_ref[0]) + gb_ref[0]) * inv_norm

    _for_row_tiles(s_len, project)

    ri = lax.broadcasted_iota(jnp.int32, (c, c), 0)
    ci = lax.broadcasted_iota(jnp.int32, (c, c), 1)
    lower = ri >= ci
    tri_lower = jnp.where(lower, 1.0, 0.0).astype(BF16)
    tri_upper = jnp.where(ci >= ri, 1.0, 0.0).astype(BF16)
    tri2_lower = jnp.concatenate([tri_lower, tri_lower], axis=1)
    tri2_upper = jnp.concatenate([tri_upper, tri_upper], axis=1)

    def cumulate(n, carry):
        rows = _chunk_rows(n)
        cum_f = _cumsum_dot(tri2_lower, la_ref[rows, 0:dk])
        cum_b = _cumsum_dot(tri2_upper, la_ref[rows, dk:2 * dk])
        last_f, last_b = cum_f[c - 1:c, :], cum_b[0:1, :]
        ref_f, ref_b = cum_f[c // 2:c // 2 + 1, :], cum_b[c // 2 - 1:c // 2, :]
        q, k = q_ref[rows, :], k_ref[rows, :]
        q_f = q * jnp.exp2(cum_f - ref_f)
        q_b = q * jnp.exp2(cum_b - ref_b)
        qs_ref[rows, :] = jnp.concatenate([q_f, q_b], axis=1).astype(BF16)
        ks_ref[rows, :] = jnp.concatenate([k * jnp.exp2(ref_f - cum_f), k * jnp.exp2(ref_b - cum_b)],
                                          axis=1).astype(BF16)
        qw_ref[rows, :] = jnp.concatenate([q_f * jnp.exp2(ref_f), q_b * jnp.exp2(ref_b)], axis=1).astype(BF16)
        kw_ref[rows, :] = jnp.concatenate([k * jnp.exp2(last_f - cum_f), k * jnp.exp2(last_b - cum_b)],
                                          axis=1).astype(BF16)
        last_ref[n, 0:1, :] = last_f
        last_ref[n, 1:2, :] = last_b
        return carry

    lax.fori_loop(0, n_chunks, cumulate, 0, unroll=CHUNK_UNROLL)

    kv = [_dot_tn(v_ref[n * c:(n + 1) * c, :], kw_ref[n * c:(n + 1) * c, :]) for n in range(n_chunks)]
    s_f = s_b = jnp.zeros((dv, dk), F32)
    for i in range(n_chunks):
        n = n_chunks - 1 - i
        st_ref[i, :, 0:dk] = s_f.astype(BF16)
        st_ref[n, :, dk:2 * dk] = s_b.astype(BF16)
        if i + 1 < n_chunks:
            s_f = jnp.exp2(last_ref[i, 0:1, :]) * s_f + kv[i][:, 0:dk]
            s_b = jnp.exp2(last_ref[n, 1:2, :]) * s_b + kv[n][:, dk:2 * dk]

    def scores(n, carry):
        rows = _chunk_rows(n)
        qs, ks = qs_ref[rows, :], ks_ref[rows, :]
        a_ref[rows, :] = jnp.where(lower, _dot_nt(qs[:, :dk], ks[:, :dk]),
                                   _dot_nt(qs[:, dk:], ks[:, dk:])).astype(BF16)
        return carry

    lax.fori_loop(0, n_chunks, scores, 0, unroll=CHUNK_UNROLL)

    def phase2(n, carry):
        rows = _chunk_rows(n)
        o = _dot(a_ref[rows, :], v_ref[rows, :]) + _dot_nt(qw_ref[rows, :], st_ref[n])
        rms = lax.rsqrt(jnp.mean(o * o, axis=-1, keepdims=True) + NORM_EPS)
        o_ref[0, rows, :] = (o * rms * gg_ref[rows, :]).astype(o_ref.dtype)
        return carry

    lax.fori_loop(0, n_chunks, phase2, 0, unroll=CHUNK_UNROLL)


def _gla(h3, w_in, col0, gate_w, gate_b, gain):
    b, s, d = h3.shape
    h, dk, dv = GLA_HEADS, GLA_DK, GLA_DV
    q0 = col0 // dk
    v0 = (col0 + 2 * h * dk) // dv
    a0 = (col0 + 2 * h * dk + 2 * h * dv) // LANES
    n_chunks = s // CHUNK
    return pl.pallas_call(
        _gla_kernel,
        grid=(b, h),
        in_specs=[pl.BlockSpec((1, s, d), lambda i, j: (i, 0, 0)),
                  pl.BlockSpec((dk, d), lambda i, j: (q0 + j, 0)),
                  pl.BlockSpec((dk, d), lambda i, j: (q0 + h + j, 0)),
                  pl.BlockSpec((dv, d), lambda i, j: (v0 + j, 0)),
                  pl.BlockSpec((dv, d), lambda i, j: (v0 + h + j, 0)),
                  pl.BlockSpec((LANES, d), lambda i, j: (a0, 0)),
                  pl.BlockSpec((1, LANES, 2 * dk), lambda i, j: (j, 0, 0)),
                  pl.BlockSpec((1, 1, 2 * dk), lambda i, j: (j, 0, 0)),
                  pl.BlockSpec((1, dv), lambda i, j: (0, j))],
        out_specs=pl.BlockSpec((1, s, dv), lambda i, j: (i, 0, j)),
        out_shape=jax.ShapeDtypeStruct((b, s, h * dv), BF16),
        scratch_shapes=[pltpu.VMEM((s, dk), F32),
                        pltpu.VMEM((s, dk), F32),
                        pltpu.VMEM((s, dv), BF16),
                        pltpu.VMEM((s, dv), F32),
                        pltpu.VMEM((s, LANES), BF16),
                        pltpu.VMEM((s, 2 * dk), F32),
                        pltpu.VMEM((s, 2 * dk), BF16),
                        pltpu.VMEM((s, 2 * dk), BF16),
                        pltpu.VMEM((s, 2 * dk), BF16),
                        pltpu.VMEM((s, 2 * dk), BF16),
                        pltpu.VMEM((n_chunks, 8, dk), F32),
                        pltpu.VMEM((s, CHUNK), BF16),
                        pltpu.VMEM((n_chunks, dv, 2 * dk), BF16)],
        compiler_params=_params(("parallel", "arbitrary")),
        name="gla",
    )(h3, w_in, w_in, w_in, w_in, w_in, gate_w, gate_b, gain)


MERGE_COLS = 256


def _merge_kernel(n_experts, h_ref, ret_ref, gla_ref, x_ref, wgl_ref, wr_ref, wg_ref, wo_ref, gain_ref,
                  wr2_ref, x1_ref, h2_ref, aff_ref):
    d = x_ref.shape[1]
    h, ret, gla = h_ref[...], ret_ref[...], gla_ref[...]
    blocks = []
    for j in range(0, d, MERGE_COLS):
        cols = slice(j, j + MERGE_COLS)
        cols_gla = slice(d + j, d + j + MERGE_COLS)
        m = (_sigmoid(_dot_nt(h, wgl_ref[cols, :])) * _dot(ret, wr_ref[:, cols])
             + _sigmoid(_dot_nt(h, wgl_ref[cols_gla, :])) * _dot(gla, wg_ref[:, cols]))
        blocks.append(m.astype(BF16))
    x1 = x_ref[...] + _dot(jnp.concatenate(blocks, axis=1), wo_ref[...])
    x1_ref[...] = x1
    h2 = _rms(x1, gain_ref[...]).astype(BF16)
    h2_ref[...] = h2
    logits2 = _dot(h2, wr2_ref[...])
    logits = logits2[:, :LANES] + logits2[:, LANES:]
    lane = lax.broadcasted_iota(jnp.int32, logits.shape, 1)
    logits = jnp.where(lane < n_experts, logits, -jnp.inf)
    p = jnp.exp(logits - jnp.max(logits, axis=-1, keepdims=True))
    aff = p / jnp.sum(p, axis=-1, keepdims=True)
    aff_ref[0] = aff.T[0:n_experts, :]


def _merge(h, ret, gla, x2, w_gl, w_ret, w_gla, w_out, gain, w_router2, n_experts, batch, tm):
    t, d = x2.shape
    s = t // batch
    per_b = s // tm
    rows = lambda width: pl.BlockSpec((tm, width), lambda i: (i, 0))
    return pl.pallas_call(
        functools.partial(_merge_kernel, n_experts),
        grid=(t // tm,),
        in_specs=[rows(d), rows(ret.shape[1]), rows(gla.shape[1]), rows(d),
                  _resident(w_gl.shape), _resident(w_ret.shape), _resident(w_gla.shape),
                  _resident(w_out.shape), _resident((1, d)),
                  _resident(w_router2.shape)],
        out_specs=[rows(d), rows(d),
                   pl.BlockSpec((1, n_experts, tm), lambda i: (i // per_b, 0, i % per_b))],
        out_shape=[jax.ShapeDtypeStruct((t, d), F32),
                   jax.ShapeDtypeStruct((t, d), BF16),
                   jax.ShapeDtypeStruct((batch, n_experts, s), F32)],
        compiler_params=_params(("parallel",)),
        name="merge",
    )(h, ret, gla, x2, w_gl, w_ret, w_gla, w_out, gain, w_router2)


def _prefix_count(mask):
    s = mask.shape[1]
    ri = lax.broadcasted_iota(jnp.int32, (LANES, LANES), 0)
    ci = lax.broadcasted_iota(jnp.int32, (LANES, LANES), 1)
    tri = jnp.where(ri <= ci, 1.0, 0.0).astype(BF16)
    off = jnp.zeros((mask.shape[0], 1), F32)
    parts = []
    for j in range(s // LANES):
        p = _dot(mask[:, j * LANES:(j + 1) * LANES].astype(BF16), tri) + off
        parts.append(p)
        off = p[:, LANES - 1:LANES]
    return jnp.concatenate(parts, axis=1)


def _route_kernel(capacity, aff_ref, slot_ref):
    a = aff_ref[0]
    bits = lax.bitcast_convert_type(a, jnp.int32)
    n_e = a.shape[0]
    cap = float(capacity)

    def count(pred):
        return jnp.sum(jnp.where(pred, 1.0, 0.0), axis=1, keepdims=True)

    def search(_, c):
        lo, hi = c
        mid = lo + lax.shift_right_logical(hi - lo, 1)
        ok = count(bits >= mid) >= cap
        return jnp.where(ok, mid, lo), jnp.where(ok, hi, mid)

    lo0 = jnp.zeros((n_e, 1), jnp.int32)
    hi0 = jnp.full((n_e, 1), 0x7F800000, jnp.int32)
    thr_bits, _ = lax.fori_loop(0, 31, search, (lo0, hi0))
    thr0 = jnp.max(jnp.where(bits <= thr_bits, a, -1.0), axis=1, keepdims=True)

    def counts(v):
        return count(a >= v), count(a > v)

    def unsettled(state):
        _, c_ge, c_gt = state
        bad = jnp.where(c_ge < cap, 1.0, jnp.where(c_gt >= cap, 1.0, 0.0))
        return jnp.max(bad, axis=0, keepdims=True)[0, 0] > 0.0

    def step(state):
        v, c_ge, c_gt = state
        below = jnp.max(jnp.where(a < v, a, -1.0), axis=1, keepdims=True)
        above = jnp.min(jnp.where(a > v, a, 2.0), axis=1, keepdims=True)
        v = jnp.where(c_ge < cap, below, jnp.where(c_gt >= cap, above, v))
        return (v,) + counts(v)

    thr, _, n_gt = lax.while_loop(unsettled, step, (thr0,) + counts(thr0))

    gt = a > thr
    eq = a == thr
    need = cap - n_gt
    eq_rank = _prefix_count(jnp.where(eq, 1.0, 0.0))
    sel = jnp.where(gt, 1.0, jnp.where(eq, jnp.where(eq_rank <= need, 1.0, 0.0), 0.0))
    pos = _prefix_count(sel)
    slot_ref[0] = jnp.where(sel > 0.0, pos - 1.0, -1.0)


def _route(aff_t, capacity):
    b, e, s = aff_t.shape
    return pl.pallas_call(
        functools.partial(_route_kernel, capacity),
        grid=(1,),
        in_specs=[pl.BlockSpec((1, b * e, s), lambda i: (0, 0, 0))],
        out_specs=pl.BlockSpec((1, b * e, s), lambda i: (0, 0, 0)),
        out_shape=jax.ShapeDtypeStruct((1, b * e, s), F32),
        compiler_params=_params(("arbitrary",)),
        name="route",
    )(aff_t.reshape(1, b * e, s)).reshape(b, e, s)


def _dispatch_kernel(slot_ref, aff_ref, h_ref, xg_ref, gate_ref):
    group, cap = xg_ref.shape[0], xg_ref.shape[2]
    h = h_ref[0]
    row = lax.broadcasted_iota(jnp.int32, (cap, h.shape[0]), 0).astype(F32)
    for g in range(group):
        hit = row == slot_ref[0, g:g + 1, :]
        xg_ref[g, 0] = _dot(jnp.where(hit, 1.0, 0.0).astype(BF16), h).astype(xg_ref.dtype)
        gate = jnp.sum(jnp.where(hit, aff_ref[0, g:g + 1, :], 0.0), axis=1, keepdims=True)
        gate_ref[g, 0] = jnp.broadcast_to(gate, gate_ref.shape[2:])


def _dispatch(slot, aff_t, h2, capacity):
    b, e, s = slot.shape
    d = h2.shape[-1]
    group = _pick(e, (8,))
    return pl.pallas_call(
        _dispatch_kernel,
        grid=(b, e // group),
        in_specs=[pl.BlockSpec((1, group, s), lambda i, j: (i, j, 0)),
                  pl.BlockSpec((1, group, s), lambda i, j: (i, j, 0)),
                  pl.BlockSpec((1, s, d), lambda i, j: (i, 0, 0))],
        out_specs=[pl.BlockSpec((group, 1, capacity, d), lambda i, j: (j, i, 0, 0)),
                   pl.BlockSpec((group, 1, capacity, LANES), lambda i, j: (j, i, 0, 0))],
        out_shape=[jax.ShapeDtypeStruct((e, b, capacity, d), BF16),
                   jax.ShapeDtypeStruct((e, b, capacity, LANES), F32)],
        compiler_params=_params(("parallel", "parallel")),
        name="dispatch",
    )(slot, aff_t, h2)


def _ffn_kernel(tm, n_f, x_ref, gate_ref, wg_ref, wu_ref, wd_ref, y_ref, acc_ref):
    f = pl.program_id(1)
    w_gate = wg_ref[0].astype(BF16)
    w_up = wu_ref[0].astype(BF16)
    w_down = wd_ref[0].astype(BF16)
    m = x_ref.shape[1]
    d = x_ref.shape[2]

    def step(first, final):
        for i in range(m // tm):
            rows = pl.ds(i * tm, tm)
            x = x_ref[0, rows, :]
            act = (_silu(_dot(x, w_gate)) * _dot(x, w_up)).astype(BF16)
            part = _dot(act, w_down)
            if not first:
                part = acc_ref[rows, :] + part
            if final:
                gate = jnp.concatenate([gate_ref[0, rows, :]] * (d // LANES), axis=1)
                y_ref[0, rows, :] = (part * gate).astype(y_ref.dtype)
            else:
                acc_ref[rows, :] = part

    if n_f == 1:
        step(True, True)
    else:
        pl.when(f == 0)(lambda: step(True, False))
        if n_f > 2:
            pl.when(jnp.logical_and(f > 0, f < n_f - 1))(lambda: step(False, False))
        pl.when(f == n_f - 1)(lambda: step(False, True))


def _ffn(xg, gate, w_gate, w_up, w_down, tf, tm):
    e, m, d = xg.shape
    f = w_gate.shape[2]
    return pl.pallas_call(
        functools.partial(_ffn_kernel, tm, f // tf),
        grid=(e, f // tf),
        in_specs=[pl.BlockSpec((1, m, d), lambda i, j: (i, 0, 0)),
                  pl.BlockSpec((1, m, LANES), lambda i, j: (i, 0, 0)),
                  pl.BlockSpec((1, d, tf), lambda i, j: (i, 0, j)),
                  pl.BlockSpec((1, d, tf), lambda i, j: (i, 0, j)),
                  pl.BlockSpec((1, tf, d), lambda i, j: (i, j, 0))],
        out_specs=pl.BlockSpec((1, m, d), lambda i, j: (i, 0, 0)),
        out_shape=jax.ShapeDtypeStruct((e, m, d), BF16),
        scratch_shapes=[pltpu.VMEM((m, d), F32)],
        compiler_params=_params(("parallel", "arbitrary")),
        name="ffn",
    )(xg, gate, w_gate, w_up, w_down)


def _combine_kernel(slot_ref, y_ref, x1_ref, gain_ref, o_ref):
    n_e, cap = y_ref.shape[0], y_ref.shape[2]
    tt = x1_ref.shape[1]
    acc = x1_ref[0]
    row = lax.broadcasted_iota(jnp.int32, (cap, tt), 0).astype(F32)
    for e in range(n_e):
        onehot = jnp.where(row == slot_ref[0, e:e + 1, :], 1.0, 0.0).astype(BF16)
        acc = acc + _dot_tn(onehot, y_ref[e, 0])
    o_ref[0] = _rms(acc, gain_ref[...])


def _combine(slot, y, x1, gain, tt):
    b, e, s = slot.shape
    cap, d = y.shape[2], y.shape[3]
    return pl.pallas_call(
        _combine_kernel,
        grid=(b, s // tt),
        in_specs=[pl.BlockSpec((1, e, tt), lambda i, j: (i, 0, j)),
                  pl.BlockSpec((e, 1, cap, d), lambda i, j: (0, i, 0, 0)),
                  pl.BlockSpec((1, tt, d), lambda i, j: (i, j, 0)),
                  pl.BlockSpec((1, d), lambda i, j: (0, 0))],
        out_specs=pl.BlockSpec((1, tt, d), lambda i, j: (i, j, 0)),
        out_shape=jax.ShapeDtypeStruct((b, s, d), F32),
        compiler_params=_params(("parallel", "parallel")),
        name="combine",
    )(slot, y, x1, gain)


def _layer(x, cos, sin, norm_mix, w_in, ret_decay_fwd, ret_decay_bwd, ret_norm,
           gla_gate_w_fwd, gla_gate_b_fwd, gla_gate_w_bwd, gla_gate_b_bwd, gla_norm,
           w_branch_ret, w_branch_gla, w_out, norm_ffn, w_router, w_gate, w_up, w_down, norm_out):
    b, s, d = x.shape
    t = b * s
    ret_qk, ret_v = RET_HEADS * RET_DK, RET_HEADS * RET_DV
    gla_qk, gla_v = GLA_HEADS * GLA_DK, GLA_HEADS * GLA_DV
    rank = GLA_GATE_RANK
    gla0 = 2 * ret_qk + 2 * ret_v
    ga0 = gla0 + 2 * gla_qk + 2 * gla_v
    assert w_in.shape == (d, ga0 + 2 * rank + 2 * d)
    assert s % CHUNK == 0 and 3 * GATE_COPY <= LANES and ga0 % LANES == 0

    w_in_t = w_in.T.astype(BF16)
    w_gl = w_in_t[ga0 + 2 * rank:, :]

    gw = jnp.zeros((GLA_HEADS, GATE_COPY, 2 * GLA_DK), F32)
    gw = gw.at[:, :rank, :GLA_DK].set(gla_gate_w_fwd.reshape(rank, GLA_HEADS, GLA_DK).transpose(1, 0, 2))
    gw = gw.at[:, rank:, GLA_DK:].set(gla_gate_w_bwd.reshape(rank, GLA_HEADS, GLA_DK).transpose(1, 0, 2))
    gw_hi = gw.astype(BF16)
    gw_lo = (gw - gw_hi.astype(F32)).astype(BF16)
    gate_w = jnp.concatenate([gw_hi, gw_lo, gw_hi, jnp.zeros_like(gw_hi)], axis=1)
    gate_b = jnp.concatenate([gla_gate_b_fwd.reshape(GLA_HEADS, 1, GLA_DK),
                              gla_gate_b_bwd.reshape(GLA_HEADS, 1, GLA_DK)], axis=2)

    dec = jnp.stack([ret_decay_fwd, ret_decay_bwd], axis=1)[:, :, None]
    dec = jnp.pad(jnp.broadcast_to(dec, (RET_HEADS, 2, RET_DV)), ((0, 0), (0, 6), (0, 0)))

    x2 = x.reshape(t, d)
    ret, h3 = _retention(x, norm_mix[None, :], w_in_t, 0, cos, sin, dec, ret_norm[None, :])
    gla = _gla(h3, w_in_t, gla0, gate_w, gate_b, gla_norm[None, :])
    h = h3.reshape(t, d)

    n_e = w_router.shape[1]
    w_r = jnp.pad(w_router, ((0, 0), (0, LANES - n_e)))
    wr_hi = w_r.astype(BF16)
    wr_lo = (w_r - wr_hi.astype(F32)).astype(BF16)
    x1, h2, aff_t = _merge(h, ret.reshape(t, ret_v), gla.reshape(t, gla_v), x2, w_gl,
                           w_branch_ret.astype(BF16), w_branch_gla.astype(BF16), w_out.astype(BF16),
                           norm_ffn[None, :], jnp.concatenate([wr_hi, wr_lo], axis=1), n_e, b,
                           _pick(s, (1024, 512, 256, 128)))

    capacity = EC_CAPACITY_FACTOR * s // n_e
    slot = _route(aff_t, capacity)
    xg, gate = _dispatch(slot, aff_t, h2.reshape(b, s, d), capacity)
    f = w_gate.shape[2]
    y = _ffn(xg.reshape(n_e, b * capacity, d), gate.reshape(n_e, b * capacity, LANES),
             w_gate, w_up, w_down, _pick(f, (256, 128)), _pick(b * capacity, (1024, 512, 256, 128)))
    return _combine(slot, y.reshape(n_e, b, capacity, d), x1.reshape(b, s, d), norm_out,
                    _pick(s, (1024, 512, 256, 128)))


def kernel(x, positions, norm_mix, w_in, ret_decay_fwd, ret_decay_bwd, ret_norm, gla_gate_w_fwd,
           gla_gate_b_fwd, gla_gate_w_bwd, gla_gate_b_bwd, gla_norm, w_branch_ret, w_branch_gla,
           w_out, norm_ffn, w_router, w_gate, w_up, w_down, norm_final):
    depth = norm_mix.shape[0]
    assert depth == 1, "the final RMSNorm is fused into the last layer's combine stage"
    cos, sin = _rope_table(positions, RET_DK)
    return _layer(x, cos, sin, norm_mix[0], w_in[0], ret_decay_fwd[0], ret_decay_bwd[0], ret_norm[0],
                  gla_gate_w_fwd[0], gla_gate_b_fwd[0], gla_gate_w_bwd[0], gla_gate_b_bwd[0], gla_norm[0],
                  w_branch_ret[0], w_branch_gla[0], w_out[0], norm_ffn[0], w_router[0],
                  w_gate[0], w_up[0], w_down[0], norm_final[None, :])
```

```python
import functools

import jax
import jax.numpy as jnp
from jax import lax
from jax.experimental import pallas as pl
from jax.experimental.pallas import tpu as pltpu

F32 = jnp.float32
BF16 = jnp.bfloat16

RET_HEADS = 4
RET_DK = 128
RET_DV = 256
GLA_HEADS = 4
GLA_DK = 128
GLA_DV = 256
GLA_GATE_RANK = 16
GLA_GATE_NORMALIZER = 16.0
CHUNK = 128
EC_CAPACITY_FACTOR = 2
ROPE_THETA = 10000.0
NORM_EPS = 1e-6
LOG2_E = 1.4426950408889634

CHUNK_UNROLL = 16
LANES = 128
VMEM_LIMIT = 56 << 20


def _params(sem, vmem=VMEM_LIMIT):
    return pltpu.CompilerParams(dimension_semantics=sem, vmem_limit_bytes=vmem)


def _resident(shape):
    return pl.BlockSpec(shape, lambda *_: (0,) * len(shape), pipeline_mode=pl.Buffered(1))


def _pick(n, prefs):
    for p in prefs:
        if n % p == 0:
            return p
    return n


def _sigmoid(x):
    return 1.0 / (1.0 + jnp.exp(-x))


def _silu(x):
    half = 0.5 * x
    return half + half * jnp.tanh(half)


def _log_sigmoid(x):
    return jnp.minimum(x, 0.0) - jnp.log(1.0 + jnp.exp(-jnp.abs(x)))


def _rms(x, gain):
    return x * lax.rsqrt(jnp.mean(x * x, axis=-1, keepdims=True) + NORM_EPS) * gain


def _dot(a, b):
    return jnp.dot(a, b, preferred_element_type=F32)


def _dot_nt(a, b):
    return lax.dot_general(a, b, (((1,), (1,)), ((), ())), preferred_element_type=F32)


def _dot_tn(a, b):
    return lax.dot_general(a, b, (((0,), (0,)), ((), ())), preferred_element_type=F32)


def _split2(x):
    hi = x.astype(BF16)
    lo = (x - hi.astype(F32)).astype(BF16)
    return hi, lo


def _rope_table_kernel(pos_ref, freq_ref, shift_ref, cos_ref, sin_ref):
    half = freq_ref.shape[1] // 2
    ang = pos_ref[0].astype(F32) * freq_ref[...]
    tab = jnp.cos(ang - shift_ref[...])
    swapped = pltpu.roll(tab, half, 1)
    lower = lax.broadcasted_iota(jnp.int32, tab.shape, 1) < half
    cos_ref[0] = jnp.where(lower, tab, swapped)
    sin_ref[0] = jnp.where(lower, -swapped, tab)


def _rope_table(positions, dk):
    b, s = positions.shape
    half = jnp.arange(0, dk, 2, dtype=F32) / dk
    inv_freq = ROPE_THETA ** (-half)
    freq = jnp.concatenate([inv_freq, inv_freq])[None, :]
    shift = jnp.concatenate([jnp.zeros(dk // 2, F32), jnp.full(dk // 2, jnp.pi / 2, F32)])[None, :]
    out = jax.ShapeDtypeStruct((b, s, dk), F32)
    return pl.pallas_call(
        _rope_table_kernel,
        grid=(b,),
        in_specs=[pl.BlockSpec((1, s, 1), lambda i: (i, 0, 0)),
                  pl.BlockSpec((1, dk), lambda i: (0, 0)),
                  pl.BlockSpec((1, dk), lambda i: (0, 0))],
        out_specs=[pl.BlockSpec((1, s, dk), lambda i: (i, 0, 0))] * 2,
        out_shape=[out, out],
        compiler_params=_params(("parallel",)),
        name="rope_table",
    )(positions[:, :, None], freq, shift)


def _rope(t, cos, sin_signed):
    return t * cos + pltpu.roll(t, t.shape[-1] // 2, 1) * sin_signed


def _chunk_rows(n):
    return pl.ds(pl.multiple_of(n * CHUNK, CHUNK), CHUNK)


def _row_tile(s_len):
    return _pick(s_len, (512, 256, CHUNK))


def _per_tile(chunk_pattern, s_len):
    return jnp.concatenate([chunk_pattern] * (_row_tile(s_len) // CHUNK), axis=0)


def _for_row_tiles(s_len, body):
    tile = _row_tile(s_len)

    def step(i, carry):
        body(pl.ds(pl.multiple_of(i * tile, tile), tile))
        return carry

    lax.fori_loop(0, s_len // tile, step, 0, unroll=True)


def _retention_kernel(x_ref, mix_gain_ref, wq_ref, wk_ref, wv_ref, wg_ref, cos_ref, sin_ref, dec_ref,
                      gain_ref, o_ref, h_ref, qk_ref, qw_ref, kw_ref, v_ref, gg_ref, a_ref, st_ref):
    c = CHUNK
    s_len, dk, dv = qk_ref.shape[0], qk_ref.shape[1] // 2, v_ref.shape[1]
    n_chunks = s_len // c
    scale = dk ** -0.5

    lg_f = _log_sigmoid(dec_ref[0, 0:1, :])
    lg_b = _log_sigmoid(dec_ref[0, 1:2, :])
    lgf_k, lgb_k = lg_f[:, :dk], lg_b[:, :dk]
    pos = lax.broadcasted_iota(jnp.int32, (c, dk), 0).astype(F32)
    wq_f = _per_tile(jnp.exp((pos + 1.0) * lgf_k), s_len)
    wk_f = _per_tile(jnp.exp((c - 1.0 - pos) * lgf_k), s_len)
    wq_b = _per_tile(jnp.exp((c - pos) * lgb_k), s_len)
    wk_b = _per_tile(jnp.exp(pos * lgb_k), s_len)
    ri = lax.broadcasted_iota(jnp.int32, (c, c), 0)
    ci = lax.broadcasted_iota(jnp.int32, (c, c), 1)
    lower = ri >= ci
    rel = (ri - ci).astype(F32)
    decay_mask = jnp.where(lower,
                           jnp.exp(jnp.where(lower, rel, 0.0) * lg_f[:, :c]),
                           jnp.exp(jnp.where(lower, 0.0, -rel) * lg_b[:, :c]))
    chunk_decay_f = jnp.exp(c * lg_f)
    chunk_decay_b = jnp.exp(c * lg_b)

    @pl.when(pl.program_id(1) == 0)
    def _():
        def norm(rows):
            h_ref[0, rows, :] = _rms(x_ref[0, rows, :], mix_gain_ref[...]).astype(h_ref.dtype)
        _for_row_tiles(s_len, norm)

    w_qk = jnp.concatenate([wq_ref[...], wk_ref[...]], axis=0)
    w_v = wv_ref[...]
    w_g = wg_ref[...]

    def project(rows):
        h = h_ref[0, rows, :]
        qk = _dot_nt(h, w_qk)
        cos, sin = cos_ref[0, rows, :], sin_ref[0, rows, :]
        qr = _rope(qk[:, :dk], cos, sin) * scale
        kr = _rope(qk[:, dk:], cos, sin)
        qk_ref[rows, :] = jnp.concatenate([qr, kr], axis=1).astype(BF16)
        qw_ref[rows, :] = jnp.concatenate([qr * wq_f, qr * wq_b], axis=1).astype(BF16)
        kw_ref[rows, :] = jnp.concatenate([kr * wk_f, kr * wk_b], axis=1).astype(BF16)
        v_ref[rows, :] = _dot_nt(h, w_v).astype(BF16)
        gg_ref[rows, :] = _silu(_dot_nt(h, w_g)) * gain_ref[...]

    _for_row_tiles(s_len, project)

    kv = [_dot_tn(kw_ref[n * c:(n + 1) * c, :], v_ref[n * c:(n + 1) * c, :]) for n in range(n_chunks)]
    s_f = s_b = jnp.zeros((dk, dv), F32)
    for i in range(n_chunks):
        n = n_chunks - 1 - i
        st_ref[i, 0:dk, :] = s_f.astype(BF16)
        st_ref[n, dk:2 * dk, :] = s_b.astype(BF16)
        if i + 1 < n_chunks:
            s_f = chunk_decay_f * s_f + kv[i][0:dk, :]
            s_b = chunk_decay_b * s_b + kv[n][dk:2 * dk, :]

    def scores(n, carry):
        rows = _chunk_rows(n)
        qk = qk_ref[rows, :]
        a_ref[rows, :] = (_dot_nt(qk[:, :dk], qk[:, dk:]) * decay_mask).astype(BF16)
        return carry

    lax.fori_loop(0, n_chunks, scores, 0, unroll=CHUNK_UNROLL)

    def phase2(n, carry):
        rows = _chunk_rows(n)
        o = _dot(a_ref[rows, :], v_ref[rows, :]) + _dot(qw_ref[rows, :], st_ref[n])
        mu = jnp.mean(o, axis=-1, keepdims=True)
        d = o - mu
        var = jnp.mean(d * d, axis=-1, keepdims=True)
        o_ref[0, rows, :] = (d * lax.rsqrt(var + NORM_EPS) * gg_ref[rows, :]).astype(o_ref.dtype)
        return carry

    lax.fori_loop(0, n_chunks, phase2, 0, unroll=CHUNK_UNROLL)


def _retention(x, mix_gain, w_in, col0, cos, sin, dec, gain):
    b, s, d = x.shape
    h, dk, dv = RET_HEADS, RET_DK, RET_DV
    q0 = col0 // dk
    v0 = (col0 + 2 * h * dk) // dv
    n_chunks = s // CHUNK
    return pl.pallas_call(
        _retention_kernel,
        grid=(b, h),
        in_specs=[pl.BlockSpec((1, s, d), lambda i, j: (i, 0, 0)),
                  pl.BlockSpec((1, d), lambda i, j: (0, 0)),
                  pl.BlockSpec((dk, d), lambda i, j: (q0 + j, 0)),
                  pl.BlockSpec((dk, d), lambda i, j: (q0 + h + j, 0)),
                  pl.BlockSpec((dv, d), lambda i, j: (v0 + j, 0)),
                  pl.BlockSpec((dv, d), lambda i, j: (v0 + h + j, 0)),
                  pl.BlockSpec((1, s, dk), lambda i, j: (i, 0, 0)),
                  pl.BlockSpec((1, s, dk), lambda i, j: (i, 0, 0)),
                  pl.BlockSpec((1, 8, dv), lambda i, j: (j, 0, 0)),
                  pl.BlockSpec((1, dv), lambda i, j: (0, j))],
        out_specs=[pl.BlockSpec((1, s, dv), lambda i, j: (i, 0, j)),
                   pl.BlockSpec((1, s, d), lambda i, j: (i, 0, 0))],
        out_shape=[jax.ShapeDtypeStruct((b, s, h * dv), BF16),
                   jax.ShapeDtypeStruct((b, s, d), BF16)],
        scratch_shapes=[pltpu.VMEM((s, 2 * dk), BF16),
                        pltpu.VMEM((s, 2 * dk), BF16),
                        pltpu.VMEM((s, 2 * dk), BF16),
                        pltpu.VMEM((s, dv), BF16),
                        pltpu.VMEM((s, dv), F32),
                        pltpu.VMEM((s, CHUNK), BF16),
                        pltpu.VMEM((n_chunks, 2 * dk, dv), BF16)],
        compiler_params=_params(("parallel", "arbitrary")),
        name="retention",
    )(x, mix_gain, w_in, w_in, w_in, w_in, cos, sin, dec, gain)


GATE_COPY = 2 * GLA_GATE_RANK


def _cumsum_dot(tri2, x):
    hi, lo = _split2(x)
    return _dot(tri2, jnp.concatenate([hi, lo], axis=0))


def _gla_kernel(h_ref, wq_ref, wk_ref, wv_ref, wg_ref, wa_ref, gw_ref, gb_ref, gain_ref, o_ref,
                q_ref, k_ref, v_ref, gg_ref, ga_ref, la_ref, qs_ref, ks_ref, qw_ref, kw_ref, last_ref,
                a_ref, st_ref):
    c = CHUNK
    s_len, dk, dv = q_ref.shape[0], q_ref.shape[1], v_ref.shape[1]
    n_chunks = s_len // c
    scale = dk ** -0.5
    inv_norm = LOG2_E / GLA_GATE_NORMALIZER

    w_qk = jnp.concatenate([wq_ref[...], wk_ref[...]], axis=0)
    w_v = wv_ref[...]
    w_g = wg_ref[...]

    @pl.when(pl.program_id(1) == 0)
    def _():
        w_a = wa_ref[...]

        def gate_inputs(rows):
            x = _dot_nt(h_ref[0, rows, :], w_a)
            x_hi = x.astype(BF16).astype(F32)
            lane = lax.broadcasted_iota(jnp.int32, x.shape, 1)
            ga = jnp.where(lane < GATE_COPY, x_hi,
                           jnp.where(lane < 2 * GATE_COPY, pltpu.roll(x_hi, GATE_COPY, 1),
                                     jnp.where(lane < 3 * GATE_COPY, pltpu.roll(x - x_hi, 2 * GATE_COPY, 1),
                                               0.0)))
            ga_ref[rows, :] = ga.astype(BF16)

        _for_row_tiles(s_len, gate_inputs)

    def project(rows):
        h = h_ref[0, rows, :]
        qk = _dot_nt(h, w_qk)
        q_ref[rows, :] = qk[:, :dk] * scale
        k_ref[rows, :] = qk[:, dk:]
        v_ref[rows, :] = _dot_nt(h, w_v).astype(BF16)
        gg_ref[rows, :] = _silu(_dot_nt(h, w_g)) * gain_ref[...]
        la_ref[rows, :] = _log_sigmoid(_dot(ga_ref[rows, :], gw_ref[0]) + gb_ref[0]) * inv_norm

    _for_row_tiles(s_len, project)

    ri = lax.broadcasted_iota(jnp.int32, (c, c), 0)
    ci = lax.broadcasted_iota(jnp.int32, (c, c), 1)
    lower = ri >= ci
    tri_lower = jnp.where(lower, 1.0, 0.0).astype(BF16)
    tri2_lower = jnp.concatenate([tri_lower, tri_lower], axis=1)

    def cumulate(n, carry):
        rows = _chunk_rows(n)
        la = la_ref[rows, :]
        incl = _cumsum_dot(tri2_lower, la)
        cum_f = incl[:, 0:dk]
        cum_b = incl[c - 1:c, dk:2 * dk] - incl[:, dk:2 * dk] + la[:, dk:2 * dk]
        last_f, last_b = cum_f[c - 1:c, :], cum_b[0:1, :]
        ref_f, ref_b = cum_f[c // 2:c // 2 + 1, :], cum_b[c // 2 - 1:c // 2, :]
        q, k = q_ref[rows, :], k_ref[rows, :]
        q_f = q * jnp.exp2(cum_f - ref_f)
        q_b = q * jnp.exp2(cum_b - ref_b)
        qs_ref[rows, :] = jnp.concatenate([q_f, q_b], axis=1).astype(BF16)
        ks_ref[rows, :] = jnp.concatenate([k * jnp.exp2(ref_f - cum_f), k * jnp.exp2(ref_b - cum_b)],
                                          axis=1).astype(BF16)
        qw_ref[rows, :] = jnp.concatenate([q_f * jnp.exp2(ref_f), q_b * jnp.exp2(ref_b)], axis=1).astype(BF16)
        kw_ref[rows, :] = jnp.concatenate([k * jnp.exp2(last_f - cum_f), k * jnp.exp2(last_b - cum_b)],
                                          axis=1).astype(BF16)
        last_ref[n, 0:1, :] = last_f
        last_ref[n, 1:2, :] = last_b
        return carry

    lax.fori_loop(0, n_chunks, cumulate, 0, unroll=CHUNK_UNROLL)

    kv = [_dot_tn(v_ref[n * c:(n + 1) * c, :], kw_ref[n * c:(n + 1) * c, :]) for n in range(n_chunks)]
    s_f = s_b = jnp.zeros((dv, dk), F32)
    for i in range(n_chunks):
        n = n_chunks - 1 - i
        st_ref[i, :, 0:dk] = s_f.astype(BF16)
        st_ref[n, :, dk:2 * dk] = s_b.astype(BF16)
        if i + 1 < n_chunks:
            s_f = jnp.exp2(last_ref[i, 0:1, :]) * s_f + kv[i][:, 0:dk]
            s_b = jnp.exp2(last_ref[n, 1:2, :]) * s_b + kv[n][:, dk:2 * dk]

    def scores(n, carry):
        rows = _chunk_rows(n)
        qs, ks = qs_ref[rows, :], ks_ref[rows, :]
        a_ref[rows, :] = jnp.where(lower, _dot_nt(qs[:, :dk], ks[:, :dk]),
                                   _dot_nt(qs[:, dk:], ks[:, dk:])).astype(BF16)
        return carry

    lax.fori_loop(0, n_chunks, scores, 0, unroll=CHUNK_UNROLL)

    def phase2(n, carry):
        rows = _chunk_rows(n)
        o = _dot(a_ref[rows, :], v_ref[rows, :]) + _dot_nt(qw_ref[rows, :], st_ref[n])
        rms = lax.rsqrt(jnp.mean(o * o, axis=-1, keepdims=True) + NORM_EPS)
        o_ref[0, rows, :] = (o * rms * gg_ref[rows, :]).astype(o_ref.dtype)
        return carry

    lax.fori_loop(0, n_chunks, phase2, 0, unroll=CHUNK_UNROLL)


def _gla(h3, w_in, col0, gate_w, gate_b, gain):
    b, s, d = h3.shape
    h, dk, dv = GLA_HEADS, GLA_DK, GLA_DV
    q0 = col0 // dk
    v0 = (col0 + 2 * h * dk) // dv
    a0 = (col0 + 2 * h * dk + 2 * h * dv) // LANES
    n_chunks = s // CHUNK
    return pl.pallas_call(
        _gla_kernel,
        grid=(b, h),
        in_specs=[pl.BlockSpec((1, s, d), lambda i, j: (i, 0, 0)),
                  pl.BlockSpec((dk, d), lambda i, j: (q0 + j, 0)),
                  pl.BlockSpec((dk, d), lambda i, j: (q0 + h + j, 0)),
                  pl.BlockSpec((dv, d), lambda i, j: (v0 + j, 0)),
                  pl.BlockSpec((dv, d), lambda i, j: (v0 + h + j, 0)),
                  pl.BlockSpec((LANES, d), lambda i, j: (a0, 0)),
                  pl.BlockSpec((1, LANES, 2 * dk), lambda i, j: (j, 0, 0)),
                  pl.BlockSpec((1, 1, 2 * dk), lambda i, j: (j, 0, 0)),
                  pl.BlockSpec((1, dv), lambda i, j: (0, j))],
        out_specs=pl.BlockSpec((1, s, dv), lambda i, j: (i, 0, j)),
        out_shape=jax.ShapeDtypeStruct((b, s, h * dv), BF16),
        scratch_shapes=[pltpu.VMEM((s, dk), F32),
                        pltpu.VMEM((s, dk), F32),
                        pltpu.VMEM((s, dv), BF16),
                        pltpu.VMEM((s, dv), F32),
                        pltpu.VMEM((s, LANES), BF16),
                        pltpu.VMEM((s, 2 * dk), F32),
                        pltpu.VMEM((s, 2 * dk), BF16),
                        pltpu.VMEM((s, 2 * dk), BF16),
                        pltpu.VMEM((s, 2 * dk), BF16),
                        pltpu.VMEM((s, 2 * dk), BF16),
                        pltpu.VMEM((n_chunks, 8, dk), F32),
                        pltpu.VMEM((s, CHUNK), BF16),
                        pltpu.VMEM((n_chunks, dv, 2 * dk), BF16)],
        compiler_params=_params(("parallel", "arbitrary")),
        name="gla",
    )(h3, w_in, w_in, w_in, w_in, w_in, gate_w, gate_b, gain)


MERGE_COLS = 256


def _merge_kernel(n_experts, h_ref, ret_ref, gla_ref, x_ref, wgl_ref, wr_ref, wg_ref, wo_ref, gain_ref,
                  wr2_ref, x1_ref, h2_ref, aff_ref):
    d = x_ref.shape[1]
    h, ret, gla = h_ref[...], ret_ref[...], gla_ref[...]
    blocks = []
    for j in range(0, d, MERGE_COLS):
        cols = slice(j, j + MERGE_COLS)
        cols_gla = slice(d + j, d + j + MERGE_COLS)
        m = (_sigmoid(_dot_nt(h, wgl_ref[cols, :])) * _dot(ret, wr_ref[:, cols])
             + _sigmoid(_dot_nt(h, wgl_ref[cols_gla, :])) * _dot(gla, wg_ref[:, cols]))
        blocks.append(m.astype(BF16))
    x1 = x_ref[...] + _dot(jnp.concatenate(blocks, axis=1), wo_ref[...])
    x1_ref[...] = x1
    h2 = _rms(x1, gain_ref[...]).astype(BF16)
    h2_ref[...] = h2
    logits2 = _dot(h2, wr2_ref[...])
    logits = logits2[:, :LANES] + logits2[:, LANES:]
    lane = lax.broadcasted_iota(jnp.int32, logits.shape, 1)
    logits = jnp.where(lane < n_experts, logits, -jnp.inf)
    p = jnp.exp(logits - jnp.max(logits, axis=-1, keepdims=True))
    aff = p / jnp.sum(p, axis=-1, keepdims=True)
    aff_ref[0] = aff.T[0:n_experts, :]


def _merge(h, ret, gla, x2, w_gl, w_ret, w_gla, w_out, gain, w_router2, n_experts, batch, tm):
    t, d = x2.shape
    s = t // batch
    per_b = s // tm
    rows = lambda width: pl.BlockSpec((tm, width), lambda i: (i, 0))
    return pl.pallas_call(
        functools.partial(_merge_kernel, n_experts),
        grid=(t // tm,),
        in_specs=[rows(d), rows(ret.shape[1]), rows(gla.shape[1]), rows(d),
                  _resident(w_gl.shape), _resident(w_ret.shape), _resident(w_gla.shape),
                  _resident(w_out.shape), _resident((1, d)),
                  _resident(w_router2.shape)],
        out_specs=[rows(d), rows(d),
                   pl.BlockSpec((1, n_experts, tm), lambda i: (i // per_b, 0, i % per_b))],
        out_shape=[jax.ShapeDtypeStruct((t, d), F32),
                   jax.ShapeDtypeStruct((t, d), BF16),
                   jax.ShapeDtypeStruct((batch, n_experts, s), F32)],
        compiler_params=_params(("parallel",)),
        name="merge",
    )(h, ret, gla, x2, w_gl, w_ret, w_gla, w_out, gain, w_router2)


def _prefix_count(mask):
    s = mask.shape[1]
    ri = lax.broadcasted_iota(jnp.int32, (LANES, LANES), 0)
    ci = lax.broadcasted_iota(jnp.int32, (LANES, LANES), 1)
    tri = jnp.where(ri <= ci, 1.0, 0.0).astype(BF16)
    off = jnp.zeros((mask.shape[0], 1), F32)
    parts = []
    for j in range(s // LANES):
        p = _dot(mask[:, j * LANES:(j + 1) * LANES].astype(BF16), tri) + off
        parts.append(p)
        off = p[:, LANES - 1:LANES]
    return jnp.concatenate(parts, axis=1)


def _route_kernel(capacity, aff_ref, slot_ref):
    a = aff_ref[0]
    bits = lax.bitcast_convert_type(a, jnp.int32)
    n_e = a.shape[0]
    cap = float(capacity)

    def count(pred):
        return jnp.sum(jnp.where(pred, 1.0, 0.0), axis=1, keepdims=True)

    def search(_, c):
        lo, hi = c
        mid = lo + lax.shift_right_logical(hi - lo, 1)
        ok = count(bits >= mid) >= cap
        return jnp.where(ok, mid, lo), jnp.where(ok, hi, mid)

    lo0 = jnp.zeros((n_e, 1), jnp.int32)
    hi0 = jnp.full((n_e, 1), 0x7F800000, jnp.int32)
    thr_bits, _ = lax.fori_loop(0, 31, search, (lo0, hi0))
    thr0 = jnp.max(jnp.where(bits <= thr_bits, a, -1.0), axis=1, keepdims=True)

    def counts(v):
        return count(a >= v), count(a > v)

    def unsettled(state):
        _, c_ge, c_gt = state
        bad = jnp.where(c_ge < cap, 1.0, jnp.where(c_gt >= cap, 1.0, 0.0))
        return jnp.max(bad, axis=0, keepdims=True)[0, 0] > 0.0

    def step(state):
        v, c_ge, c_gt = state
        below = jnp.max(jnp.where(a < v, a, -1.0), axis=1, keepdims=True)
        above = jnp.min(jnp.where(a > v, a, 2.0), axis=1, keepdims=True)
        v = jnp.where(c_ge < cap, below, jnp.where(c_gt >= cap, above, v))
        return (v,) + counts(v)

    thr, _, n_gt = lax.while_loop(unsettled, step, (thr0,) + counts(thr0))

    gt = a > thr
    eq = a == thr
    need = cap - n_gt
    eq_rank = _prefix_count(jnp.where(eq, 1.0, 0.0))
    sel = jnp.where(gt, 1.0, jnp.where(eq, jnp.where(eq_rank <= need, 1.0, 0.0), 0.0))
    pos = _prefix_count(sel)
    slot_ref[0] = jnp.where(sel > 0.0, pos - 1.0, -1.0)


def _route(aff_t, capacity):
    b, e, s = aff_t.shape
    return pl.pallas_call(
        functools.partial(_route_kernel, capacity),
        grid=(1,),
        in_specs=[pl.BlockSpec((1, b * e, s), lambda i: (0, 0, 0))],
        out_specs=pl.BlockSpec((1, b * e, s), lambda i: (0, 0, 0)),
        out_shape=jax.ShapeDtypeStruct((1, b * e, s), F32),
        compiler_params=_params(("arbitrary",)),
        name="route",
    )(aff_t.reshape(1, b * e, s)).reshape(b, e, s)


def _dispatch_kernel(slot_ref, aff_ref, h_ref, xg_ref, gate_ref):
    group, cap = xg_ref.shape[0], xg_ref.shape[2]
    h = h_ref[0]
    row = lax.broadcasted_iota(jnp.int32, (cap, h.shape[0]), 0).astype(F32)
    for g in range(group):
        hit = row == slot_ref[0, g:g + 1, :]
        xg_ref[g, 0] = _dot(jnp.where(hit, 1.0, 0.0).astype(BF16), h).astype(xg_ref.dtype)
        gate = jnp.sum(jnp.where(hit, aff_ref[0, g:g + 1, :], 0.0), axis=1, keepdims=True)
        gate_ref[g, 0] = jnp.broadcast_to(gate, gate_ref.shape[2:])


def _dispatch(slot, aff_t, h2, capacity):
    b, e, s = slot.shape
    d = h2.shape[-1]
    group = _pick(e, (8,))
    return pl.pallas_call(
        _dispatch_kernel,
        grid=(b, e // group),
        in_specs=[pl.BlockSpec((1, group, s), lambda i, j: (i, j, 0)),
                  pl.BlockSpec((1, group, s), lambda i, j: (i, j, 0)),
                  pl.BlockSpec((1, s, d), lambda i, j: (i, 0, 0))],
        out_specs=[pl.BlockSpec((group, 1, capacity, d), lambda i, j: (j, i, 0, 0)),
                   pl.BlockSpec((group, 1, capacity, LANES), lambda i, j: (j, i, 0, 0))],
        out_shape=[jax.ShapeDtypeStruct((e, b, capacity, d), BF16),
                   jax.ShapeDtypeStruct((e, b, capacity, LANES), F32)],
        compiler_params=_params(("parallel", "parallel")),
        name="dispatch",
    )(slot, aff_t, h2)


def _ffn_kernel(tm, n_f, x_ref, gate_ref, wg_ref, wu_ref, wd_ref, y_ref, acc_ref):
    f = pl.program_id(1)
    w_gate = wg_ref[0].astype(BF16)
    w_up = wu_ref[0].astype(BF16)
    w_down = wd_ref[0].astype(BF16)
    m = x_ref.shape[1]
    d = x_ref.shape[2]

    def step(first, final):
        for i in range(m // tm):
            rows = pl.ds(i * tm, tm)
            x = x_ref[0, rows, :]
            act = (_silu(_dot(x, w_gate)) * _dot(x, w_up)).astype(BF16)
            part = _dot(act, w_down)
            if not first:
                part = acc_ref[rows, :] + part
            if final:
                gate = jnp.concatenate([gate_ref[0, rows, :]] * (d // LANES), axis=1)
                y_ref[0, rows, :] = (part * gate).astype(y_ref.dtype)
            else:
                acc_ref[rows, :] = part

    if n_f == 1:
        step(True, True)
    else:
        pl.when(f == 0)(lambda: step(True, False))
        if n_f > 2:
            pl.when(jnp.logical_and(f > 0, f < n_f - 1))(lambda: step(False, False))
        pl.when(f == n_f - 1)(lambda: step(False, True))


def _ffn(xg, gate, w_gate, w_up, w_down, tf, tm):
    e, m, d = xg.shape
    f = w_gate.shape[2]
    return pl.pallas_call(
        functools.partial(_ffn_kernel, tm, f // tf),
        grid=(e, f // tf),
        in_specs=[pl.BlockSpec((1, m, d), lambda i, j: (i, 0, 0)),
                  pl.BlockSpec((1, m, LANES), lambda i, j: (i, 0, 0)),
                  pl.BlockSpec((1, d, tf), lambda i, j: (i, 0, j)),
                  pl.BlockSpec((1, d, tf), lambda i, j: (i, 0, j)),
                  pl.BlockSpec((1, tf, d), lambda i, j: (i, j, 0))],
        out_specs=pl.BlockSpec((1, m, d), lambda i, j: (i, 0, 0)),
        out_shape=jax.ShapeDtypeStruct((e, m, d), BF16),
        scratch_shapes=[pltpu.VMEM((m, d), F32)],
        compiler_params=_params(("parallel", "arbitrary")),
        name="ffn",
    )(xg, gate, w_gate, w_up, w_down)


def _combine_kernel(slot_ref, y_ref, x1_ref, gain_ref, o_ref):
    n_e, cap = y_ref.shape[0], y_ref.shape[2]
    tt = x1_ref.shape[1]
    acc = x1_ref[0]
    row = lax.broadcasted_iota(jnp.int32, (cap, tt), 0).astype(F32)
    for e in range(n_e):
        onehot = jnp.where(row == slot_ref[0, e:e + 1, :], 1.0, 0.0).astype(BF16)
        acc = acc + _dot_tn(onehot, y_ref[e, 0])
    o_ref[0] = _rms(acc, gain_ref[...])


def _combine(slot, y, x1, gain, tt):
    b, e, s = slot.shape
    cap, d = y.shape[2], y.shape[3]
    return pl.pallas_call(
        _combine_kernel,
        grid=(b, s // tt),
        in_specs=[pl.BlockSpec((1, e, tt), lambda i, j: (i, 0, j)),
                  pl.BlockSpec((e, 1, cap, d), lambda i, j: (0, i, 0, 0)),
                  pl.BlockSpec((1, tt, d), lambda i, j: (i, j, 0)),
                  pl.BlockSpec((1, d), lambda i, j: (0, 0))],
        out_specs=pl.BlockSpec((1, tt, d), lambda i, j: (i, j, 0)),
        out_shape=jax.ShapeDtypeStruct((b, s, d), F32),
        compiler_params=_params(("parallel", "parallel")),
        name="combine",
    )(slot, y, x1, gain)


def _layer(x, cos, sin, norm_mix, w_in, ret_decay_fwd, ret_decay_bwd, ret_norm,
           gla_gate_w_fwd, gla_gate_b_fwd, gla_gate_w_bwd, gla_gate_b_bwd, gla_norm,
           w_branch_ret, w_branch_gla, w_out, norm_ffn, w_router, w_gate, w_up, w_down, norm_out):
    b, s, d = x.shape
    t = b * s
    ret_qk, ret_v = RET_HEADS * RET_DK, RET_HEADS * RET_DV
    gla_qk, gla_v = GLA_HEADS * GLA_DK, GLA_HEADS * GLA_DV
    rank = GLA_GATE_RANK
    gla0 = 2 * ret_qk + 2 * ret_v
    ga0 = gla0 + 2 * gla_qk + 2 * gla_v
    assert w_in.shape == (d, ga0 + 2 * rank + 2 * d)
    assert s % CHUNK == 0 and 3 * GATE_COPY <= LANES and ga0 % LANES == 0

    w_in_t = w_in.T.astype(BF16)
    w_gl = w_in_t[ga0 + 2 * rank:, :]

    gw = jnp.zeros((GLA_HEADS, GATE_COPY, 2 * GLA_DK), F32)
    gw = gw.at[:, :rank, :GLA_DK].set(gla_gate_w_fwd.reshape(rank, GLA_HEADS, GLA_DK).transpose(1, 0, 2))
    gw = gw.at[:, rank:, GLA_DK:].set(gla_gate_w_bwd.reshape(rank, GLA_HEADS, GLA_DK).transpose(1, 0, 2))
    gw_hi = gw.astype(BF16)
    gw_lo = (gw - gw_hi.astype(F32)).astype(BF16)
    gate_w = jnp.concatenate([gw_hi, gw_lo, gw_hi, jnp.zeros_like(gw_hi)], axis=1)
    gate_b = jnp.concatenate([gla_gate_b_fwd.reshape(GLA_HEADS, 1, GLA_DK),
                              gla_gate_b_bwd.reshape(GLA_HEADS, 1, GLA_DK)], axis=2)

    dec = jnp.stack([ret_decay_fwd, ret_decay_bwd], axis=1)[:, :, None]
    dec = jnp.pad(jnp.broadcast_to(dec, (RET_HEADS, 2, RET_DV)), ((0, 0), (0, 6), (0, 0)))

    x2 = x.reshape(t, d)
    ret, h3 = _retention(x, norm_mix[None, :], w_in_t, 0, cos, sin, dec, ret_norm[None, :])
    gla = _gla(h3, w_in_t, gla0, gate_w, gate_b, gla_norm[None, :])
    h = h3.reshape(t, d)

    n_e = w_router.shape[1]
    w_r = jnp.pad(w_router, ((0, 0), (0, LANES - n_e)))
    wr_hi = w_r.astype(BF16)
    wr_lo = (w_r - wr_hi.astype(F32)).astype(BF16)
    x1, h2, aff_t = _merge(h, ret.reshape(t, ret_v), gla.reshape(t, gla_v), x2, w_gl,
                           w_branch_ret.astype(BF16), w_branch_gla.astype(BF16), w_out.astype(BF16),
                           norm_ffn[None, :], jnp.concatenate([wr_hi, wr_lo], axis=1), n_e, b,
                           _pick(s, (1024, 512, 256, 128)))

    capacity = EC_CAPACITY_FACTOR * s // n_e
    slot = _route(aff_t, capacity)
    xg, gate = _dispatch(slot, aff_t, h2.reshape(b, s, d), capacity)
    f = w_gate.shape[2]
    y = _ffn(xg.reshape(n_e, b * capacity, d), gate.reshape(n_e, b * capacity, LANES),
             w_gate, w_up, w_down, _pick(f, (256, 128)), _pick(b * capacity, (1024, 512, 256, 128)))
    return _combine(slot, y.reshape(n_e, b, capacity, d), x1.reshape(b, s, d), norm_out,
                    _pick(s, (1024, 512, 256, 128)))


def kernel(x, positions, norm_mix, w_in, ret_decay_fwd, ret_decay_bwd, ret_norm, gla_gate_w_fwd,
           gla_gate_b_fwd, gla_gate_w_bwd, gla_gate_b_bwd, gla_norm, w_branch_ret, w_branch_gla,
           w_out, norm_ffn, w_router, w_gate, w_up, w_down, norm_final):
    depth = norm_mix.shape[0]
    assert depth == 1, "the final RMSNorm is fused into the last layer's combine stage"
    cos, sin = _rope_table(positions, RET_DK)
    return _layer(x, cos, sin, norm_mix[0], w_in[0], ret_decay_fwd[0], ret_decay_bwd[0], ret_norm[0],
                  gla_gate_w_fwd[0], gla_gate_b_fwd[0], gla_gate_w_bwd[0], gla_gate_b_bwd[0], gla_norm[0],
                  w_branch_ret[0], w_branch_gla[0], w_out[0], norm_ffn[0], w_router[0],
                  w_gate[0], w_up[0], w_down[0], norm_final[None, :])
```

```python
import functools

import jax
import jax.numpy as jnp
from jax import lax
from jax.experimental import pallas as pl
from jax.experimental.pallas import tpu as pltpu

F32 = jnp.float32
BF16 = jnp.bfloat16

RET_HEADS = 4
RET_DK = 128
RET_DV = 256
GLA_HEADS = 4
GLA_DK = 128
GLA_DV = 256
GLA_GATE_RANK = 16
GLA_GATE_NORMALIZER = 16.0
CHUNK = 128
EC_CAPACITY_FACTOR = 2
ROPE_THETA = 10000.0
NORM_EPS = 1e-6
LOG2_E = 1.4426950408889634

CHUNK_UNROLL = 16
LANES = 128
VMEM_LIMIT = 56 << 20


def _params(sem, vmem=VMEM_LIMIT):
    return pltpu.CompilerParams(dimension_semantics=sem, vmem_limit_bytes=vmem)


def _resident(shape):
    return pl.BlockSpec(shape, lambda *_: (0,) * len(shape), pipeline_mode=pl.Buffered(1))


def _pick(n, prefs):
    for p in prefs:
        if n % p == 0:
            return p
    return n


def _sigmoid(x):
    return 1.0 / (1.0 + jnp.exp(-x))


def _silu(x):
    half = 0.5 * x
    return half + half * jnp.tanh(half)


def _log_sigmoid(x):
    return jnp.minimum(x, 0.0) - jnp.log(1.0 + jnp.exp(-jnp.abs(x)))


def _rms(x, gain):
    return x * lax.rsqrt(jnp.mean(x * x, axis=-1, keepdims=True) + NORM_EPS) * gain


def _dot(a, b):
    return jnp.dot(a, b, preferred_element_type=F32)


def _dot_nt(a, b):
    return lax.dot_general(a, b, (((1,), (1,)), ((), ())), preferred_element_type=F32)


def _dot_tn(a, b):
    return lax.dot_general(a, b, (((0,), (0,)), ((), ())), preferred_element_type=F32)


def _split2(x):
    hi = x.astype(BF16)
    lo = (x - hi.astype(F32)).astype(BF16)
    return hi, lo


def _rope_table_kernel(pos_ref, freq_ref, sign_ref, cos_ref, sin_ref):
    ang = pos_ref[0].astype(F32) * freq_ref[...]
    cos_ref[0] = jnp.cos(ang)
    sin_ref[0] = jnp.sin(ang) * sign_ref[...]


def _rope_table(positions, dk):
    b, s = positions.shape
    half = jnp.arange(0, dk, 2, dtype=F32) / dk
    inv_freq = ROPE_THETA ** (-half)
    freq = jnp.concatenate([inv_freq, inv_freq])[None, :]
    sign = jnp.concatenate([-jnp.ones(dk // 2, F32), jnp.ones(dk // 2, F32)])[None, :]
    out = jax.ShapeDtypeStruct((b, s, dk), F32)
    return pl.pallas_call(
        _rope_table_kernel,
        grid=(b,),
        in_specs=[pl.BlockSpec((1, s, 1), lambda i: (i, 0, 0)),
                  pl.BlockSpec((1, dk), lambda i: (0, 0)),
                  pl.BlockSpec((1, dk), lambda i: (0, 0))],
        out_specs=[pl.BlockSpec((1, s, dk), lambda i: (i, 0, 0))] * 2,
        out_shape=[out, out],
        compiler_params=_params(("parallel",)),
        name="rope_table",
    )(positions[:, :, None], freq, sign)


def _rope(t, cos, sin_signed):
    return t * cos + pltpu.roll(t, t.shape[-1] // 2, 1) * sin_signed


def _chunk_rows(n):
    return pl.ds(pl.multiple_of(n * CHUNK, CHUNK), CHUNK)


def _row_tile(s_len):
    return _pick(s_len, (512, 256, CHUNK))


def _per_tile(chunk_pattern, s_len):
    return jnp.concatenate([chunk_pattern] * (_row_tile(s_len) // CHUNK), axis=0)


def _for_row_tiles(s_len, body):
    tile = _row_tile(s_len)

    def step(i, carry):
        body(pl.ds(pl.multiple_of(i * tile, tile), tile))
        return carry

    lax.fori_loop(0, s_len // tile, step, 0, unroll=True)


def _retention_kernel(x_ref, mix_gain_ref, wq_ref, wk_ref, wv_ref, wg_ref, cos_ref, sin_ref, dec_ref,
                      gain_ref, o_ref, h_ref, qk_ref, qw_ref, kw_ref, v_ref, gg_ref, a_ref, st_ref):
    c = CHUNK
    s_len, dk, dv = qk_ref.shape[0], qk_ref.shape[1] // 2, v_ref.shape[1]
    n_chunks = s_len // c
    scale = dk ** -0.5

    lg_f = _log_sigmoid(dec_ref[0, 0:1, :])
    lg_b = _log_sigmoid(dec_ref[0, 1:2, :])
    lgf_k, lgb_k = lg_f[:, :dk], lg_b[:, :dk]
    pos = lax.broadcasted_iota(jnp.int32, (c, dk), 0).astype(F32)
    wq_f = _per_tile(jnp.exp((pos + 1.0) * lgf_k), s_len)
    wk_f = _per_tile(jnp.exp((c - 1.0 - pos) * lgf_k), s_len)
    wq_b = _per_tile(jnp.exp((c - pos) * lgb_k), s_len)
    wk_b = _per_tile(jnp.exp(pos * lgb_k), s_len)
    ri = lax.broadcasted_iota(jnp.int32, (c, c), 0)
    ci = lax.broadcasted_iota(jnp.int32, (c, c), 1)
    lower = ri >= ci
    rel = (ri - ci).astype(F32)
    decay_mask = jnp.where(lower,
                           jnp.exp(jnp.where(lower, rel, 0.0) * lg_f[:, :c]),
                           jnp.exp(jnp.where(lower, 0.0, -rel) * lg_b[:, :c]))
    chunk_decay_f = jnp.exp(c * lg_f)
    chunk_decay_b = jnp.exp(c * lg_b)

    @pl.when(pl.program_id(1) == 0)
    def _():
        def norm(rows):
            h_ref[0, rows, :] = _rms(x_ref[0, rows, :], mix_gain_ref[...]).astype(h_ref.dtype)
        _for_row_tiles(s_len, norm)

    w_qk = jnp.concatenate([wq_ref[...], wk_ref[...]], axis=0)
    w_v = wv_ref[...]
    w_g = wg_ref[...]

    def project(rows):
        h = h_ref[0, rows, :]
        qk = _dot_nt(h, w_qk)
        cos, sin = cos_ref[0, rows, :], sin_ref[0, rows, :]
        qr = _rope(qk[:, :dk], cos, sin) * scale
        kr = _rope(qk[:, dk:], cos, sin)
        qk_ref[rows, :] = jnp.concatenate([qr, kr], axis=1).astype(BF16)
        qw_ref[rows, :] = jnp.concatenate([qr * wq_f, qr * wq_b], axis=1).astype(BF16)
        kw_ref[rows, :] = jnp.concatenate([kr * wk_f, kr * wk_b], axis=1).astype(BF16)
        v_ref[rows, :] = _dot_nt(h, w_v).astype(BF16)
        gg_ref[rows, :] = _silu(_dot_nt(h, w_g)) * gain_ref[...]

    _for_row_tiles(s_len, project)

    kv = [_dot_tn(kw_ref[n * c:(n + 1) * c, :], v_ref[n * c:(n + 1) * c, :]) for n in range(n_chunks)]
    s_f = s_b = jnp.zeros((dk, dv), F32)
    for i in range(n_chunks):
        n = n_chunks - 1 - i
        st_ref[i, 0:dk, :] = s_f.astype(BF16)
        st_ref[n, dk:2 * dk, :] = s_b.astype(BF16)
        if i + 1 < n_chunks:
            s_f = chunk_decay_f * s_f + kv[i][0:dk, :]
            s_b = chunk_decay_b * s_b + kv[n][dk:2 * dk, :]

    def scores(n, carry):
        rows = _chunk_rows(n)
        qk = qk_ref[rows, :]
        a_ref[rows, :] = (_dot_nt(qk[:, :dk], qk[:, dk:]) * decay_mask).astype(BF16)
        return carry

    lax.fori_loop(0, n_chunks, scores, 0, unroll=CHUNK_UNROLL)

    def phase2(n, carry):
        rows = _chunk_rows(n)
        o = _dot(a_ref[rows, :], v_ref[rows, :]) + _dot(qw_ref[rows, :], st_ref[n])
        mu = jnp.mean(o, axis=-1, keepdims=True)
        d = o - mu
        var = jnp.mean(d * d, axis=-1, keepdims=True)
        o_ref[0, rows, :] = (d * lax.rsqrt(var + NORM_EPS) * gg_ref[rows, :]).astype(o_ref.dtype)
        return carry

    lax.fori_loop(0, n_chunks, phase2, 0, unroll=CHUNK_UNROLL)


def _retention(x, mix_gain, w_in, col0, cos, sin, dec, gain):
    b, s, d = x.shape
    h, dk, dv = RET_HEADS, RET_DK, RET_DV
    q0 = col0 // dk
    v0 = (col0 + 2 * h * dk) // dv
    n_chunks = s // CHUNK
    return pl.pallas_call(
        _retention_kernel,
        grid=(b, h),
        in_specs=[pl.BlockSpec((1, s, d), lambda i, j: (i, 0, 0)),
                  pl.BlockSpec((1, d), lambda i, j: (0, 0)),
                  pl.BlockSpec((dk, d), lambda i, j: (q0 + j, 0)),
                  pl.BlockSpec((dk, d), lambda i, j: (q0 + h + j, 0)),
                  pl.BlockSpec((dv, d), lambda i, j: (v0 + j, 0)),
                  pl.BlockSpec((dv, d), lambda i, j: (v0 + h + j, 0)),
                  pl.BlockSpec((1, s, dk), lambda i, j: (i, 0, 0)),
                  pl.BlockSpec((1, s, dk), lambda i, j: (i, 0, 0)),
                  pl.BlockSpec((1, 8, dv), lambda i, j: (j, 0, 0)),
                  pl.BlockSpec((1, dv), lambda i, j: (0, j))],
        out_specs=[pl.BlockSpec((1, s, dv), lambda i, j: (i, 0, j)),
                   pl.BlockSpec((1, s, d), lambda i, j: (i, 0, 0))],
        out_shape=[jax.ShapeDtypeStruct((b, s, h * dv), BF16),
                   jax.ShapeDtypeStruct((b, s, d), BF16)],
        scratch_shapes=[pltpu.VMEM((s, 2 * dk), BF16),
                        pltpu.VMEM((s, 2 * dk), BF16),
                        pltpu.VMEM((s, 2 * dk), BF16),
                        pltpu.VMEM((s, dv), BF16),
                        pltpu.VMEM((s, dv), F32),
                        pltpu.VMEM((s, CHUNK), BF16),
                        pltpu.VMEM((n_chunks, 2 * dk, dv), BF16)],
        compiler_params=_params(("parallel", "arbitrary")),
        name="retention",
    )(x, mix_gain, w_in, w_in, w_in, w_in, cos, sin, dec, gain)


GATE_COPY = 2 * GLA_GATE_RANK


def _cumsum_dot(tri2, x):
    hi, lo = _split2(x)
    return _dot(tri2, jnp.concatenate([hi, lo], axis=0))


def _gla_kernel(h_ref, wq_ref, wk_ref, wv_ref, wg_ref, wa_ref, gw_ref, gb_ref, gain_ref, o_ref,
                q_ref, k_ref, v_ref, gg_ref, ga_ref, la_ref, qs_ref, ks_ref, qw_ref, kw_ref, last_ref,
                a_ref, st_ref):
    c = CHUNK
    s_len, dk, dv = q_ref.shape[0], q_ref.shape[1], v_ref.shape[1]
    n_chunks = s_len // c
    scale = dk ** -0.5
    inv_norm = LOG2_E / GLA_GATE_NORMALIZER

    w_qk = jnp.concatenate([wq_ref[...], wk_ref[...]], axis=0)
    w_v = wv_ref[...]
    w_g = wg_ref[...]

    @pl.when(pl.program_id(1) == 0)
    def _():
        w_a = wa_ref[...]

        def gate_inputs(rows):
            x = _dot_nt(h_ref[0, rows, :], w_a)
            x_hi = x.astype(BF16).astype(F32)
            lane = lax.broadcasted_iota(jnp.int32, x.shape, 1)
            ga = jnp.where(lane < GATE_COPY, x_hi,
                           jnp.where(lane < 2 * GATE_COPY, pltpu.roll(x_hi, GATE_COPY, 1),
                                     jnp.where(lane < 3 * GATE_COPY, pltpu.roll(x - x_hi, 2 * GATE_COPY, 1),
                                               0.0)))
            ga_ref[rows, :] = ga.astype(BF16)

        _for_row_tiles(s_len, gate_inputs)

    def project(rows):
        h = h_ref[0, rows, :]
        qk = _dot_nt(h, w_qk)
        q_ref[rows, :] = qk[:, :dk] * scale
        k_ref[rows, :] = qk[:, dk:]
        v_ref[rows, :] = _dot_nt(h, w_v).astype(BF16)
        gg_ref[rows, :] = _silu(_dot_nt(h, w_g)) * gain_ref[...]
        la_ref[rows, :] = _log_sigmoid(_dot(ga_ref[rows, :], gw_ref[0]) + gb_ref[0]) * inv_norm

    _for_row_tiles(s_len, project)

    ri = lax.broadcasted_iota(jnp.int32, (c, c), 0)
    ci = lax.broadcasted_iota(jnp.int32, (c, c), 1)
    lower = ri >= ci
    tri_lower = jnp.where(lower, 1.0, 0.0).astype(BF16)
    tri2_lower = jnp.concatenate([tri_lower, tri_lower], axis=1)

    def cumulate(n, carry):
        rows = _chunk_rows(n)
        la = la_ref[rows, :]
        incl = _cumsum_dot(tri2_lower, la)
        cum_f = incl[:, 0:dk]
        cum_b = incl[c - 1:c, dk:2 * dk] - incl[:, dk:2 * dk] + la[:, dk:2 * dk]
        last_f, last_b = cum_f[c - 1:c, :], cum_b[0:1, :]
        ref_f, ref_b = cum_f[c // 2:c // 2 + 1, :], cum_b[c // 2 - 1:c // 2, :]
        q, k = q_ref[rows, :], k_ref[rows, :]
        q_f = q * jnp.exp2(cum_f - ref_f)
        q_b = q * jnp.exp2(cum_b - ref_b)
        qs_ref[rows, :] = jnp.concatenate([q_f, q_b], axis=1).astype(BF16)
        ks_ref[rows, :] = jnp.concatenate([k * jnp.exp2(ref_f - cum_f), k * jnp.exp2(ref_b - cum_b)],
                                          axis=1).astype(BF16)
        qw_ref[rows, :] = jnp.concatenate([q_f * jnp.exp2(ref_f), q_b * jnp.exp2(ref_b)], axis=1).astype(BF16)
        kw_ref[rows, :] = jnp.concatenate([k * jnp.exp2(last_f - cum_f), k * jnp.exp2(last_b - cum_b)],
                                          axis=1).astype(BF16)
        last_ref[n, 0:1, :] = last_f
        last_ref[n, 1:2, :] = last_b
        return carry

    lax.fori_loop(0, n_chunks, cumulate, 0, unroll=CHUNK_UNROLL)

    kv = [_dot_tn(v_ref[n * c:(n + 1) * c, :], kw_ref[n * c:(n + 1) * c, :]) for n in range(n_chunks)]
    s_f = s_b = jnp.zeros((dv, dk), F32)
    for i in range(n_chunks):
        n = n_chunks - 1 - i
        st_ref[i, :, 0:dk] = s_f.astype(BF16)
        st_ref[n, :, dk:2 * dk] = s_b.astype(BF16)
        if i + 1 < n_chunks:
            s_f = jnp.exp2(last_ref[i, 0:1, :]) * s_f + kv[i][:, 0:dk]
            s_b = jnp.exp2(last_ref[n, 1:2, :]) * s_b + kv[n][:, dk:2 * dk]

    def scores(n, carry):
        rows = _chunk_rows(n)
        qs, ks = qs_ref[rows, :], ks_ref[rows, :]
        a_ref[rows, :] = jnp.where(lower, _dot_nt(qs[:, :dk], ks[:, :dk]),
                                   _dot_nt(qs[:, dk:], ks[:, dk:])).astype(BF16)
        return carry

    lax.fori_loop(0, n_chunks, scores, 0, unroll=CHUNK_UNROLL)

    def phase2(n, carry):
        rows = _chunk_rows(n)
        o = _dot(a_ref[rows, :], v_ref[rows, :]) + _dot_nt(qw_ref[rows, :], st_ref[n])
        rms = lax.rsqrt(jnp.mean(o * o, axis=-1, keepdims=True) + NORM_EPS)
        o_ref[0, rows, :] = (o * rms * gg_ref[rows, :]).astype(o_ref.dtype)
        return carry

    lax.fori_loop(0, n_chunks, phase2, 0, unroll=CHUNK_UNROLL)


def _gla(h3, w_in, col0, gate_w, gate_b, gain):
    b, s, d = h3.shape
    h, dk, dv = GLA_HEADS, GLA_DK, GLA_DV
    q0 = col0 // dk
    v0 = (col0 + 2 * h * dk) // dv
    a0 = (col0 + 2 * h * dk + 2 * h * dv) // LANES
    n_chunks = s // CHUNK
    return pl.pallas_call(
        _gla_kernel,
        grid=(b, h),
        in_specs=[pl.BlockSpec((1, s, d), lambda i, j: (i, 0, 0)),
                  pl.BlockSpec((dk, d), lambda i, j: (q0 + j, 0)),
                  pl.BlockSpec((dk, d), lambda i, j: (q0 + h + j, 0)),
                  pl.BlockSpec((dv, d), lambda i, j: (v0 + j, 0)),
                  pl.BlockSpec((dv, d), lambda i, j: (v0 + h + j, 0)),
                  pl.BlockSpec((LANES, d), lambda i, j: (a0, 0)),
                  pl.BlockSpec((1, LANES, 2 * dk), lambda i, j: (j, 0, 0)),
                  pl.BlockSpec((1, 1, 2 * dk), lambda i, j: (j, 0, 0)),
                  pl.BlockSpec((1, dv), lambda i, j: (0, j))],
        out_specs=pl.BlockSpec((1, s, dv), lambda i, j: (i, 0, j)),
        out_shape=jax.ShapeDtypeStruct((b, s, h * dv), BF16),
        scratch_shapes=[pltpu.VMEM((s, dk), F32),
                        pltpu.VMEM((s, dk), F32),
                        pltpu.VMEM((s, dv), BF16),
                        pltpu.VMEM((s, dv), F32),
                        pltpu.VMEM((s, LANES), BF16),
                        pltpu.VMEM((s, 2 * dk), F32),
                        pltpu.VMEM((s, 2 * dk), BF16),
                        pltpu.VMEM((s, 2 * dk), BF16),
                        pltpu.VMEM((s, 2 * dk), BF16),
                        pltpu.VMEM((s, 2 * dk), BF16),
                        pltpu.VMEM((n_chunks, 8, dk), F32),
                        pltpu.VMEM((s, CHUNK), BF16),
                        pltpu.VMEM((n_chunks, dv, 2 * dk), BF16)],
        compiler_params=_params(("parallel", "arbitrary")),
        name="gla",
    )(h3, w_in, w_in, w_in, w_in, w_in, gate_w, gate_b, gain)


MERGE_COLS = 256


def _merge_kernel(n_experts, h_ref, ret_ref, gla_ref, x_ref, wgl_ref, wr_ref, wg_ref, wo_ref, gain_ref,
                  wr2_ref, x1_ref, h2_ref, aff_ref):
    d = x_ref.shape[1]
    h, ret, gla = h_ref[...], ret_ref[...], gla_ref[...]
    blocks = []
    for j in range(0, d, MERGE_COLS):
        cols = slice(j, j + MERGE_COLS)
        cols_gla = slice(d + j, d + j + MERGE_COLS)
        m = (_sigmoid(_dot_nt(h, wgl_ref[cols, :])) * _dot(ret, wr_ref[:, cols])
             + _sigmoid(_dot_nt(h, wgl_ref[cols_gla, :])) * _dot(gla, wg_ref[:, cols]))
        blocks.append(m.astype(BF16))
    x1 = x_ref[...] + _dot(jnp.concatenate(blocks, axis=1), wo_ref[...])
    x1_ref[...] = x1
    h2 = _rms(x1, gain_ref[...]).astype(BF16)
    h2_ref[...] = h2
    logits2 = _dot(h2, wr2_ref[...])
    logits = logits2[:, :LANES] + logits2[:, LANES:]
    lane = lax.broadcasted_iota(jnp.int32, logits.shape, 1)
    logits = jnp.where(lane < n_experts, logits, -jnp.inf)
    p = jnp.exp(logits - jnp.max(logits, axis=-1, keepdims=True))
    aff = p / jnp.sum(p, axis=-1, keepdims=True)
    aff_ref[0] = aff.T[0:n_experts, :]


def _merge(h, ret, gla, x2, w_gl, w_ret, w_gla, w_out, gain, w_router2, n_experts, batch, tm):
    t, d = x2.shape
    s = t // batch
    per_b = s // tm
    rows = lambda width: pl.BlockSpec((tm, width), lambda i: (i, 0))
    return pl.pallas_call(
        functools.partial(_merge_kernel, n_experts),
        grid=(t // tm,),
        in_specs=[rows(d), rows(ret.shape[1]), rows(gla.shape[1]), rows(d),
                  _resident(w_gl.shape), _resident(w_ret.shape), _resident(w_gla.shape),
                  _resident(w_out.shape), _resident((1, d)),
                  _resident(w_router2.shape)],
        out_specs=[rows(d), rows(d),
                   pl.BlockSpec((1, n_experts, tm), lambda i: (i // per_b, 0, i % per_b))],
        out_shape=[jax.ShapeDtypeStruct((t, d), F32),
                   jax.ShapeDtypeStruct((t, d), BF16),
                   jax.ShapeDtypeStruct((batch, n_experts, s), F32)],
        compiler_params=_params(("parallel",)),
        name="merge",
    )(h, ret, gla, x2, w_gl, w_ret, w_gla, w_out, gain, w_router2)


def _prefix_count(mask):
    s = mask.shape[1]
    ri = lax.broadcasted_iota(jnp.int32, (LANES, LANES), 0)
    ci = lax.broadcasted_iota(jnp.int32, (LANES, LANES), 1)
    tri = jnp.where(ri <= ci, 1.0, 0.0).astype(BF16)
    off = jnp.zeros((mask.shape[0], 1), F32)
    parts = []
    for j in range(s // LANES):
        p = _dot(mask[:, j * LANES:(j + 1) * LANES].astype(BF16), tri) + off
        parts.append(p)
        off = p[:, LANES - 1:LANES]
    return jnp.concatenate(parts, axis=1)


def _route_kernel(capacity, aff_ref, slot_ref):
    a = aff_ref[0]
    bits = lax.bitcast_convert_type(a, jnp.int32)
    n_e = a.shape[0]
    cap = float(capacity)

    def count(pred):
        return jnp.sum(jnp.where(pred, 1.0, 0.0), axis=1, keepdims=True)

    def search(_, c):
        lo, hi = c
        mid = lo + lax.shift_right_logical(hi - lo, 1)
        ok = count(bits >= mid) >= cap
        return jnp.where(ok, mid, lo), jnp.where(ok, hi, mid)

    lo0 = jnp.zeros((n_e, 1), jnp.int32)
    hi0 = jnp.full((n_e, 1), 0x7F800000, jnp.int32)
    thr_bits, _ = lax.fori_loop(0, 31, search, (lo0, hi0))
    thr0 = jnp.max(jnp.where(bits <= thr_bits, a, -1.0), axis=1, keepdims=True)

    def counts(v):
        return count(a >= v), count(a > v)

    def unsettled(state):
        _, c_ge, c_gt = state
        bad = jnp.where(c_ge < cap, 1.0, jnp.where(c_gt >= cap, 1.0, 0.0))
        return jnp.max(bad, axis=0, keepdims=True)[0, 0] > 0.0

    def step(state):
        v, c_ge, c_gt = state
        below = jnp.max(jnp.where(a < v, a, -1.0), axis=1, keepdims=True)
        above = jnp.min(jnp.where(a > v, a, 2.0), axis=1, keepdims=True)
        v = jnp.where(c_ge < cap, below, jnp.where(c_gt >= cap, above, v))
        return (v,) + counts(v)

    thr, _, n_gt = lax.while_loop(unsettled, step, (thr0,) + counts(thr0))

    gt = a > thr
    eq = a == thr
    need = cap - n_gt
    eq_rank = _prefix_count(jnp.where(eq, 1.0, 0.0))
    sel = jnp.where(gt, 1.0, jnp.where(eq, jnp.where(eq_rank <= need, 1.0, 0.0), 0.0))
    pos = _prefix_count(sel)
    slot_ref[0] = jnp.where(sel > 0.0, pos - 1.0, -1.0)


def _route(aff_t, capacity):
    b, e, s = aff_t.shape
    return pl.pallas_call(
        functools.partial(_route_kernel, capacity),
        grid=(1,),
        in_specs=[pl.BlockSpec((1, b * e, s), lambda i: (0, 0, 0))],
        out_specs=pl.BlockSpec((1, b * e, s), lambda i: (0, 0, 0)),
        out_shape=jax.ShapeDtypeStruct((1, b * e, s), F32),
        compiler_params=_params(("arbitrary",)),
        name="route",
    )(aff_t.reshape(1, b * e, s)).reshape(b, e, s)


def _dispatch_kernel(slot_ref, aff_ref, h_ref, xg_ref, gate_ref):
    group, cap = xg_ref.shape[0], xg_ref.shape[2]
    h = h_ref[0]
    row = lax.broadcasted_iota(jnp.int32, (cap, h.shape[0]), 0).astype(F32)
    for g in range(group):
        hit = row == slot_ref[0, g:g + 1, :]
        xg_ref[g, 0] = _dot(jnp.where(hit, 1.0, 0.0).astype(BF16), h).astype(xg_ref.dtype)
        gate = jnp.sum(jnp.where(hit, aff_ref[0, g:g + 1, :], 0.0), axis=1, keepdims=True)
        gate_ref[g, 0] = jnp.broadcast_to(gate, gate_ref.shape[2:])


def _dispatch(slot, aff_t, h2, capacity):
    b, e, s = slot.shape
    d = h2.shape[-1]
    group = _pick(e, (8,))
    return pl.pallas_call(
        _dispatch_kernel,
        grid=(b, e // group),
        in_specs=[pl.BlockSpec((1, group, s), lambda i, j: (i, j, 0)),
                  pl.BlockSpec((1, group, s), lambda i, j: (i, j, 0)),
                  pl.BlockSpec((1, s, d), lambda i, j: (i, 0, 0))],
        out_specs=[pl.BlockSpec((group, 1, capacity, d), lambda i, j: (j, i, 0, 0)),
                   pl.BlockSpec((group, 1, capacity, LANES), lambda i, j: (j, i, 0, 0))],
        out_shape=[jax.ShapeDtypeStruct((e, b, capacity, d), BF16),
                   jax.ShapeDtypeStruct((e, b, capacity, LANES), F32)],
        compiler_params=_params(("parallel", "parallel")),
        name="dispatch",
    )(slot, aff_t, h2)


def _ffn_kernel(tm, n_f, x_ref, gate_ref, wg_ref, wu_ref, wd_ref, y_ref, acc_ref):
    f = pl.program_id(1)
    w_gate = wg_ref[0].astype(BF16)
    w_up = wu_ref[0].astype(BF16)
    w_down = wd_ref[0].astype(BF16)
    m = x_ref.shape[1]
    d = x_ref.shape[2]

    def step(first, final):
        for i in range(m // tm):
            rows = pl.ds(i * tm, tm)
            x = x_ref[0, rows, :]
            act = (_silu(_dot(x, w_gate)) * _dot(x, w_up)).astype(BF16)
            part = _dot(act, w_down)
            if not first:
                part = acc_ref[rows, :] + part
            if final:
                gate = jnp.concatenate([gate_ref[0, rows, :]] * (d // LANES), axis=1)
                y_ref[0, rows, :] = (part * gate).astype(y_ref.dtype)
            else:
                acc_ref[rows, :] = part

    if n_f == 1:
        step(True, True)
    else:
        pl.when(f == 0)(lambda: step(True, False))
        if n_f > 2:
            pl.when(jnp.logical_and(f > 0, f < n_f - 1))(lambda: step(False, False))
        pl.when(f == n_f - 1)(lambda: step(False, True))


def _ffn(xg, gate, w_gate, w_up, w_down, tf, tm):
    e, m, d = xg.shape
    f = w_gate.shape[2]
    return pl.pallas_call(
        functools.partial(_ffn_kernel, tm, f // tf),
        grid=(e, f // tf),
        in_specs=[pl.BlockSpec((1, m, d), lambda i, j: (i, 0, 0)),
                  pl.BlockSpec((1, m, LANES), lambda i, j: (i, 0, 0)),
                  pl.BlockSpec((1, d, tf), lambda i, j: (i, 0, j)),
                  pl.BlockSpec((1, d, tf), lambda i, j: (i, 0, j)),
                  pl.BlockSpec((1, tf, d), lambda i, j: (i, j, 0))],
        out_specs=pl.BlockSpec((1, m, d), lambda i, j: (i, 0, 0)),
        out_shape=jax.ShapeDtypeStruct((e, m, d), BF16),
        scratch_shapes=[pltpu.VMEM((m, d), F32)],
        compiler_params=_params(("parallel", "arbitrary")),
        name="ffn",
    )(xg, gate, w_gate, w_up, w_down)


def _combine_kernel(slot_ref, y_ref, x1_ref, gain_ref, o_ref):
    n_e, cap = y_ref.shape[0], y_ref.shape[2]
    tt = x1_ref.shape[1]
    acc = x1_ref[0]
    row = lax.broadcasted_iota(jnp.int32, (cap, tt), 0).astype(F32)
    for e in range(n_e):
        onehot = jnp.where(row == slot_ref[0, e:e + 1, :], 1.0, 0.0).astype(BF16)
        acc = acc + _dot_tn(onehot, y_ref[e, 0])
    o_ref[0] = _rms(acc, gain_ref[...])


def _combine(slot, y, x1, gain, tt):
    b, e, s = slot.shape
    cap, d = y.shape[2], y.shape[3]
    return pl.pallas_call(
        _combine_kernel,
        grid=(b, s // tt),
        in_specs=[pl.BlockSpec((1, e, tt), lambda i, j: (i, 0, j)),
                  pl.BlockSpec((e, 1, cap, d), lambda i, j: (0, i, 0, 0)),
                  pl.BlockSpec((1, tt, d), lambda i, j: (i, j, 0)),
                  pl.BlockSpec((1, d), lambda i, j: (0, 0))],
        out_specs=pl.BlockSpec((1, tt, d), lambda i, j: (i, j, 0)),
        out_shape=jax.ShapeDtypeStruct((b, s, d), F32),
        compiler_params=_params(("parallel", "parallel")),
        name="combine",
    )(slot, y, x1, gain)


def _layer(x, cos, sin, norm_mix, w_in, ret_decay_fwd, ret_decay_bwd, ret_norm,
           gla_gate_w_fwd, gla_gate_b_fwd, gla_gate_w_bwd, gla_gate_b_bwd, gla_norm,
           w_branch_ret, w_branch_gla, w_out, norm_ffn, w_router, w_gate, w_up, w_down, norm_out):
    b, s, d = x.shape
    t = b * s
    ret_qk, ret_v = RET_HEADS * RET_DK, RET_HEADS * RET_DV
    gla_qk, gla_v = GLA_HEADS * GLA_DK, GLA_HEADS * GLA_DV
    rank = GLA_GATE_RANK
    gla0 = 2 * ret_qk + 2 * ret_v
    ga0 = gla0 + 2 * gla_qk + 2 * gla_v
    assert w_in.shape == (d, ga0 + 2 * rank + 2 * d)
    assert s % CHUNK == 0 and 3 * GATE_COPY <= LANES and ga0 % LANES == 0

    w_in_t = w_in.T.astype(BF16)
    w_gl = w_in_t[ga0 + 2 * rank:, :]

    gw = jnp.zeros((GLA_HEADS, GATE_COPY, 2 * GLA_DK), F32)
    gw = gw.at[:, :rank, :GLA_DK].set(gla_gate_w_fwd.reshape(rank, GLA_HEADS, GLA_DK).transpose(1, 0, 2))
    gw = gw.at[:, rank:, GLA_DK:].set(gla_gate_w_bwd.reshape(rank, GLA_HEADS, GLA_DK).transpose(1, 0, 2))
    gw_hi = gw.astype(BF16)
    gw_lo = (gw - gw_hi.astype(F32)).astype(BF16)
    gate_w = jnp.concatenate([gw_hi, gw_lo, gw_hi, jnp.zeros_like(gw_hi)], axis=1)
    gate_b = jnp.concatenate([gla_gate_b_fwd.reshape(GLA_HEADS, 1, GLA_DK),
                              gla_gate_b_bwd.reshape(GLA_HEADS, 1, GLA_DK)], axis=2)

    dec = jnp.stack([ret_decay_fwd, ret_decay_bwd], axis=1)[:, :, None]
    dec = jnp.pad(jnp.broadcast_to(dec, (RET_HEADS, 2, RET_DV)), ((0, 0), (0, 6), (0, 0)))

    x2 = x.reshape(t, d)
    ret, h3 = _retention(x, norm_mix[None, :], w_in_t, 0, cos, sin, dec, ret_norm[None, :])
    gla = _gla(h3, w_in_t, gla0, gate_w, gate_b, gla_norm[None, :])
    h = h3.reshape(t, d)

    n_e = w_router.shape[1]
    w_r = jnp.pad(w_router, ((0, 0), (0, LANES - n_e)))
    wr_hi = w_r.astype(BF16)
    wr_lo = (w_r - wr_hi.astype(F32)).astype(BF16)
    x1, h2, aff_t = _merge(h, ret.reshape(t, ret_v), gla.reshape(t, gla_v), x2, w_gl,
                           w_branch_ret.astype(BF16), w_branch_gla.astype(BF16), w_out.astype(BF16),
                           norm_ffn[None, :], jnp.concatenate([wr_hi, wr_lo], axis=1), n_e, b,
                           _pick(s, (1024, 512, 256, 128)))

    capacity = EC_CAPACITY_FACTOR * s // n_e
    slot = _route(aff_t, capacity)
    xg, gate = _dispatch(slot, aff_t, h2.reshape(b, s, d), capacity)
    f = w_gate.shape[2]
    y = _ffn(xg.reshape(n_e, b * capacity, d), gate.reshape(n_e, b * capacity, LANES),
             w_gate, w_up, w_down, _pick(f, (256, 128)), _pick(b * capacity, (1024, 512, 256, 128)))
    return _combine(slot, y.reshape(n_e, b, capacity, d), x1.reshape(b, s, d), norm_out,
                    _pick(s, (1024, 512, 256, 128)))


def kernel(x, positions, norm_mix, w_in, ret_decay_fwd, ret_decay_bwd, ret_norm, gla_gate_w_fwd,
           gla_gate_b_fwd, gla_gate_w_bwd, gla_gate_b_bwd, gla_norm, w_branch_ret, w_branch_gla,
           w_out, norm_ffn, w_router, w_gate, w_up, w_down, norm_final):
    depth = norm_mix.shape[0]
    assert depth == 1, "the final RMSNorm is fused into the last layer's combine stage"
    cos, sin = _rope_table(positions, RET_DK)
    return _layer(x, cos, sin, norm_mix[0], w_in[0], ret_decay_fwd[0], ret_decay_bwd[0], ret_norm[0],
                  gla_gate_w_fwd[0], gla_gate_b_fwd[0], gla_gate_w_bwd[0], gla_gate_b_bwd[0], gla_norm[0],
                  w_branch_ret[0], w_branch_gla[0], w_out[0], norm_ffn[0], w_router[0],
                  w_gate[0], w_up[0], w_down[0], norm_final[None, :])
```

```python
import functools

import jax
import jax.numpy as jnp
from jax import lax
from jax.experimental import pallas as pl
from jax.experimental.pallas import tpu as pltpu

F32 = jnp.float32
BF16 = jnp.bfloat16

RET_HEADS = 4
RET_DK = 128
RET_DV = 256
GLA_HEADS = 4
GLA_DK = 128
GLA_DV = 256
GLA_GATE_RANK = 16
GLA_GATE_NORMALIZER = 16.0
CHUNK = 128
EC_CAPACITY_FACTOR = 2
ROPE_THETA = 10000.0
NORM_EPS = 1e-6
LOG2_E = 1.4426950408889634

CHUNK_UNROLL = 16
LANES = 128
VMEM_LIMIT = 56 << 20


def _params(sem, vmem=VMEM_LIMIT):
    return pltpu.CompilerParams(dimension_semantics=sem, vmem_limit_bytes=vmem)


def _resident(shape):
    return pl.BlockSpec(shape, lambda *_: (0,) * len(shape), pipeline_mode=pl.Buffered(1))


def _pick(n, prefs):
    for p in prefs:
        if n % p == 0:
            return p
    return n


def _sigmoid(x):
    return 1.0 / (1.0 + jnp.exp(-x))


def _silu(x):
    half = 0.5 * x
    return half + half * jnp.tanh(half)


def _log_sigmoid(x):
    return jnp.minimum(x, 0.0) - jnp.log(1.0 + jnp.exp(-jnp.abs(x)))


def _rms(x, gain):
    return x * lax.rsqrt(jnp.mean(x * x, axis=-1, keepdims=True) + NORM_EPS) * gain


def _dot(a, b):
    return jnp.dot(a, b, preferred_element_type=F32)


def _dot_nt(a, b):
    return lax.dot_general(a, b, (((1,), (1,)), ((), ())), preferred_element_type=F32)


def _dot_tn(a, b):
    return lax.dot_general(a, b, (((0,), (0,)), ((), ())), preferred_element_type=F32)


def _split2(x):
    hi = x.astype(BF16)
    lo = (x - hi.astype(F32)).astype(BF16)
    return hi, lo


def _rope_table_kernel(pos_ref, freq_ref, sign_ref, cos_ref, sin_ref):
    ang = pos_ref[0].astype(F32) * freq_ref[...]
    cos_ref[0] = jnp.cos(ang)
    sin_ref[0] = jnp.sin(ang) * sign_ref[...]


def _rope_table(positions, dk):
    b, s = positions.shape
    half = jnp.arange(0, dk, 2, dtype=F32) / dk
    inv_freq = ROPE_THETA ** (-half)
    freq = jnp.concatenate([inv_freq, inv_freq])[None, :]
    sign = jnp.concatenate([-jnp.ones(dk // 2, F32), jnp.ones(dk // 2, F32)])[None, :]
    out = jax.ShapeDtypeStruct((b, s, dk), F32)
    return pl.pallas_call(
        _rope_table_kernel,
        grid=(b,),
        in_specs=[pl.BlockSpec((1, s, 1), lambda i: (i, 0, 0)),
                  pl.BlockSpec((1, dk), lambda i: (0, 0)),
                  pl.BlockSpec((1, dk), lambda i: (0, 0))],
        out_specs=[pl.BlockSpec((1, s, dk), lambda i: (i, 0, 0))] * 2,
        out_shape=[out, out],
        compiler_params=_params(("parallel",)),
        name="rope_table",
    )(positions[:, :, None], freq, sign)


def _rope(t, cos, sin_signed):
    return t * cos + pltpu.roll(t, t.shape[-1] // 2, 1) * sin_signed


def _chunk_rows(n):
    return pl.ds(pl.multiple_of(n * CHUNK, CHUNK), CHUNK)


def _row_tile(s_len):
    return _pick(s_len, (512, 256, CHUNK))


def _per_tile(chunk_pattern, s_len):
    return jnp.concatenate([chunk_pattern] * (_row_tile(s_len) // CHUNK), axis=0)


def _for_row_tiles(s_len, body):
    tile = _row_tile(s_len)

    def step(i, carry):
        body(pl.ds(pl.multiple_of(i * tile, tile), tile))
        return carry

    lax.fori_loop(0, s_len // tile, step, 0, unroll=True)


def _retention_kernel(x_ref, mix_gain_ref, wq_ref, wk_ref, wv_ref, wg_ref, cos_ref, sin_ref, dec_ref,
                      gain_ref, o_ref, h_ref, qk_ref, qw_ref, kw_ref, v_ref, gg_ref, a_ref, st_ref):
    c = CHUNK
    s_len, dk, dv = qk_ref.shape[0], qk_ref.shape[1] // 2, v_ref.shape[1]
    n_chunks = s_len // c
    scale = dk ** -0.5

    lg_f = _log_sigmoid(dec_ref[0, 0:1, :])
    lg_b = _log_sigmoid(dec_ref[0, 1:2, :])
    lgf_k, lgb_k = lg_f[:, :dk], lg_b[:, :dk]
    pos = lax.broadcasted_iota(jnp.int32, (c, dk), 0).astype(F32)
    wq_f = _per_tile(jnp.exp((pos + 1.0) * lgf_k), s_len)
    wk_f = _per_tile(jnp.exp((c - 1.0 - pos) * lgf_k), s_len)
    wq_b = _per_tile(jnp.exp((c - pos) * lgb_k), s_len)
    wk_b = _per_tile(jnp.exp(pos * lgb_k), s_len)
    ri = lax.broadcasted_iota(jnp.int32, (c, c), 0)
    ci = lax.broadcasted_iota(jnp.int32, (c, c), 1)
    lower = ri >= ci
    rel = (ri - ci).astype(F32)
    decay_mask = jnp.where(lower,
                           jnp.exp(jnp.where(lower, rel, 0.0) * lg_f[:, :c]),
                           jnp.exp(jnp.where(lower, 0.0, -rel) * lg_b[:, :c]))
    chunk_decay_f = jnp.exp(c * lg_f)
    chunk_decay_b = jnp.exp(c * lg_b)

    @pl.when(pl.program_id(1) == 0)
    def _():
        def norm(rows):
            h_ref[0, rows, :] = _rms(x_ref[0, rows, :], mix_gain_ref[...]).astype(h_ref.dtype)
        _for_row_tiles(s_len, norm)

    w_qk = jnp.concatenate([wq_ref[...], wk_ref[...]], axis=0)
    w_v = wv_ref[...]
    w_g = wg_ref[...]

    def project(rows):
        h = h_ref[0, rows, :]
        qk = _dot_nt(h, w_qk)
        cos, sin = cos_ref[0, rows, :], sin_ref[0, rows, :]
        qr = _rope(qk[:, :dk], cos, sin) * scale
        kr = _rope(qk[:, dk:], cos, sin)
        qk_ref[rows, :] = jnp.concatenate([qr, kr], axis=1).astype(BF16)
        qw_ref[rows, :] = jnp.concatenate([qr * wq_f, qr * wq_b], axis=1).astype(BF16)
        kw_ref[rows, :] = jnp.concatenate([kr * wk_f, kr * wk_b], axis=1).astype(BF16)
        v_ref[rows, :] = _dot_nt(h, w_v).astype(BF16)
        gg_ref[rows, :] = _silu(_dot_nt(h, w_g)) * gain_ref[...]

    _for_row_tiles(s_len, project)

    kv = [_dot_tn(kw_ref[n * c:(n + 1) * c, :], v_ref[n * c:(n + 1) * c, :]) for n in range(n_chunks)]
    s_f = s_b = jnp.zeros((dk, dv), F32)
    for i in range(n_chunks):
        n = n_chunks - 1 - i
        st_ref[i, 0:dk, :] = s_f.astype(BF16)
        st_ref[n, dk:2 * dk, :] = s_b.astype(BF16)
        if i + 1 < n_chunks:
            s_f = chunk_decay_f * s_f + kv[i][0:dk, :]
            s_b = chunk_decay_b * s_b + kv[n][dk:2 * dk, :]

    def scores(n, carry):
        rows = _chunk_rows(n)
        qk = qk_ref[rows, :]
        a_ref[rows, :] = (_dot_nt(qk[:, :dk], qk[:, dk:]) * decay_mask).astype(BF16)
        return carry

    lax.fori_loop(0, n_chunks, scores, 0, unroll=CHUNK_UNROLL)

    def phase2(n, carry):
        rows = _chunk_rows(n)
        o = _dot(a_ref[rows, :], v_ref[rows, :]) + _dot(qw_ref[rows, :], st_ref[n])
        mu = jnp.mean(o, axis=-1, keepdims=True)
        d = o - mu
        var = jnp.mean(d * d, axis=-1, keepdims=True)
        o_ref[0, rows, :] = (d * lax.rsqrt(var + NORM_EPS) * gg_ref[rows, :]).astype(o_ref.dtype)
        return carry

    lax.fori_loop(0, n_chunks, phase2, 0, unroll=CHUNK_UNROLL)


def _retention(x, mix_gain, w_in, col0, cos, sin, dec, gain):
    b, s, d = x.shape
    h, dk, dv = RET_HEADS, RET_DK, RET_DV
    q0 = col0 // dk
    v0 = (col0 + 2 * h * dk) // dv
    n_chunks = s // CHUNK
    return pl.pallas_call(
        _retention_kernel,
        grid=(b, h),
        in_specs=[pl.BlockSpec((1, s, d), lambda i, j: (i, 0, 0)),
                  pl.BlockSpec((1, d), lambda i, j: (0, 0)),
                  pl.BlockSpec((dk, d), lambda i, j: (q0 + j, 0)),
                  pl.BlockSpec((dk, d), lambda i, j: (q0 + h + j, 0)),
                  pl.BlockSpec((dv, d), lambda i, j: (v0 + j, 0)),
                  pl.BlockSpec((dv, d), lambda i, j: (v0 + h + j, 0)),
                  pl.BlockSpec((1, s, dk), lambda i, j: (i, 0, 0)),
                  pl.BlockSpec((1, s, dk), lambda i, j: (i, 0, 0)),
                  pl.BlockSpec((1, 8, dv), lambda i, j: (j, 0, 0)),
                  pl.BlockSpec((1, dv), lambda i, j: (0, j))],
        out_specs=[pl.BlockSpec((1, s, dv), lambda i, j: (i, 0, j)),
                   pl.BlockSpec((1, s, d), lambda i, j: (i, 0, 0))],
        out_shape=[jax.ShapeDtypeStruct((b, s, h * dv), BF16),
                   jax.ShapeDtypeStruct((b, s, d), BF16)],
        scratch_shapes=[pltpu.VMEM((s, 2 * dk), BF16),
                        pltpu.VMEM((s, 2 * dk), BF16),
                        pltpu.VMEM((s, 2 * dk), BF16),
                        pltpu.VMEM((s, dv), BF16),
                        pltpu.VMEM((s, dv), F32),
                        pltpu.VMEM((s, CHUNK), BF16),
                        pltpu.VMEM((n_chunks, 2 * dk, dv), BF16)],
        compiler_params=_params(("parallel", "arbitrary")),
        name="retention",
    )(x, mix_gain, w_in, w_in, w_in, w_in, cos, sin, dec, gain)


GATE_COPY = 2 * GLA_GATE_RANK


def _cumsum_dot(tri2, x):
    hi, lo = _split2(x)
    return _dot(tri2, jnp.concatenate([hi, lo], axis=0))


def _gla_kernel(h_ref, wq_ref, wk_ref, wv_ref, wg_ref, wa_ref, gw_ref, gb_ref, gain_ref, o_ref,
                q_ref, k_ref, v_ref, gg_ref, ga_ref, la_ref, qs_ref, ks_ref, qw_ref, kw_ref, last_ref,
                a_ref, st_ref):
    c = CHUNK
    s_len, dk, dv = q_ref.shape[0], q_ref.shape[1], v_ref.shape[1]
    n_chunks = s_len // c
    scale = dk ** -0.5
    inv_norm = LOG2_E / GLA_GATE_NORMALIZER

    w_qk = jnp.concatenate([wq_ref[...], wk_ref[...]], axis=0)
    w_v = wv_ref[...]
    w_g = wg_ref[...]

    @pl.when(pl.program_id(1) == 0)
    def _():
        w_a = wa_ref[...]

        def gate_inputs(rows):
            x = _dot_nt(h_ref[0, rows, :], w_a)
            x_hi = x.astype(BF16).astype(F32)
            lane = lax.broadcasted_iota(jnp.int32, x.shape, 1)
            ga = jnp.where(lane < GATE_COPY, x_hi,
                           jnp.where(lane < 2 * GATE_COPY, pltpu.roll(x_hi, GATE_COPY, 1),
                                     jnp.where(lane < 3 * GATE_COPY, pltpu.roll(x - x_hi, 2 * GATE_COPY, 1),
                                               0.0)))
            ga_ref[rows, :] = ga.astype(BF16)

        _for_row_tiles(s_len, gate_inputs)

    def project(rows):
        h = h_ref[0, rows, :]
        qk = _dot_nt(h, w_qk)
        q_ref[rows, :] = qk[:, :dk] * scale
        k_ref[rows, :] = qk[:, dk:]
        v_ref[rows, :] = _dot_nt(h, w_v).astype(BF16)
        gg_ref[rows, :] = _silu(_dot_nt(h, w_g)) * gain_ref[...]
        la_ref[rows, :] = _log_sigmoid(_dot(ga_ref[rows, :], gw_ref[0]) + gb_ref[0]) * inv_norm

    _for_row_tiles(s_len, project)

    ri = lax.broadcasted_iota(jnp.int32, (c, c), 0)
    ci = lax.broadcasted_iota(jnp.int32, (c, c), 1)
    lower = ri >= ci
    tri_lower = jnp.where(lower, 1.0, 0.0).astype(BF16)
    tri2_lower = jnp.concatenate([tri_lower, tri_lower], axis=1)

    def cumulate(n, carry):
        rows = _chunk_rows(n)
        la = la_ref[rows, :]
        incl = _cumsum_dot(tri2_lower, la)
        cum_f = incl[:, 0:dk]
        cum_b = incl[c - 1:c, dk:2 * dk] - incl[:, dk:2 * dk] + la[:, dk:2 * dk]
        last_f, last_b = cum_f[c - 1:c, :], cum_b[0:1, :]
        ref_f, ref_b = cum_f[c // 2:c // 2 + 1, :], cum_b[c // 2 - 1:c // 2, :]
        q, k = q_ref[rows, :], k_ref[rows, :]
        q_f = q * jnp.exp2(cum_f - ref_f)
        q_b = q * jnp.exp2(cum_b - ref_b)
        qs_ref[rows, :] = jnp.concatenate([q_f, q_b], axis=1).astype(BF16)
        ks_ref[rows, :] = jnp.concatenate([k * jnp.exp2(ref_f - cum_f), k * jnp.exp2(ref_b - cum_b)],
                                          axis=1).astype(BF16)
        qw_ref[rows, :] = jnp.concatenate([q_f * jnp.exp2(ref_f), q_b * jnp.exp2(ref_b)], axis=1).astype(BF16)
        kw_ref[rows, :] = jnp.concatenate([k * jnp.exp2(last_f - cum_f), k * jnp.exp2(last_b - cum_b)],
                                          axis=1).astype(BF16)
        last_ref[n, 0:1, :] = last_f
        last_ref[n, 1:2, :] = last_b
        return carry

    lax.fori_loop(0, n_chunks, cumulate, 0, unroll=CHUNK_UNROLL)

    kv = [_dot_tn(v_ref[n * c:(n + 1) * c, :], kw_ref[n * c:(n + 1) * c, :]) for n in range(n_chunks)]
    s_f = s_b = jnp.zeros((dv, dk), F32)
    for i in range(n_chunks):
        n = n_chunks - 1 - i
        st_ref[i, :, 0:dk] = s_f.astype(BF16)
        st_ref[n, :, dk:2 * dk] = s_b.astype(BF16)
        if i + 1 < n_chunks:
            s_f = jnp.exp2(last_ref[i, 0:1, :]) * s_f + kv[i][:, 0:dk]
            s_b = jnp.exp2(last_ref[n, 1:2, :]) * s_b + kv[n][:, dk:2 * dk]

    def scores(n, carry):
        rows = _chunk_rows(n)
        qs, ks = qs_ref[rows, :], ks_ref[rows, :]
        a_ref[rows, :] = jnp.where(lower, _dot_nt(qs[:, :dk], ks[:, :dk]),
                                   _dot_nt(qs[:, dk:], ks[:, dk:])).astype(BF16)
        return carry

    lax.fori_loop(0, n_chunks, scores, 0, unroll=CHUNK_UNROLL)

    def phase2(n, carry):
        rows = _chunk_rows(n)
        o = _dot(a_ref[rows, :], v_ref[rows, :]) + _dot_nt(qw_ref[rows, :], st_ref[n])
        rms = lax.rsqrt(jnp.mean(o * o, axis=-1, keepdims=True) + NORM_EPS)
        o_ref[0, rows, :] = (o * rms * gg_ref[rows, :]).astype(o_ref.dtype)
        return carry

    lax.fori_loop(0, n_chunks, phase2, 0, unroll=CHUNK_UNROLL)


def _gla(h3, w_in, col0, gate_w, gate_b, gain):
    b, s, d = h3.shape
    h, dk, dv = GLA_HEADS, GLA_DK, GLA_DV
    q0 = col0 // dk
    v0 = (col0 + 2 * h * dk) // dv
    a0 = (col0 + 2 * h * dk + 2 * h * dv) // LANES
    n_chunks = s // CHUNK
    return pl.pallas_call(
        _gla_kernel,
        grid=(b, h),
        in_specs=[pl.BlockSpec((1, s, d), lambda i, j: (i, 0, 0)),
                  pl.BlockSpec((dk, d), lambda i, j: (q0 + j, 0)),
                  pl.BlockSpec((dk, d), lambda i, j: (q0 + h + j, 0)),
                  pl.BlockSpec((dv, d), lambda i, j: (v0 + j, 0)),
                  pl.BlockSpec((dv, d), lambda i, j: (v0 + h + j, 0)),
                  pl.BlockSpec((LANES, d), lambda i, j: (a0, 0)),
                  pl.BlockSpec((1, LANES, 2 * dk), lambda i, j: (j, 0, 0)),
                  pl.BlockSpec((1, 1, 2 * dk), lambda i, j: (j, 0, 0)),
                  pl.BlockSpec((1, dv), lambda i, j: (0, j))],
        out_specs=pl.BlockSpec((1, s, dv), lambda i, j: (i, 0, j)),
        out_shape=jax.ShapeDtypeStruct((b, s, h * dv), BF16),
        scratch_shapes=[pltpu.VMEM((s, dk), F32),
                        pltpu.VMEM((s, dk), F32),
                        pltpu.VMEM((s, dv), BF16),
                        pltpu.VMEM((s, dv), F32),
                        pltpu.VMEM((s, LANES), BF16),
                        pltpu.VMEM((s, 2 * dk), F32),
                        pltpu.VMEM((s, 2 * dk), BF16),
                        pltpu.VMEM((s, 2 * dk), BF16),
                        pltpu.VMEM((s, 2 * dk), BF16),
                        pltpu.VMEM((s, 2 * dk), BF16),
                        pltpu.VMEM((n_chunks, 8, dk), F32),
                        pltpu.VMEM((s, CHUNK), BF16),
                        pltpu.VMEM((n_chunks, dv, 2 * dk), BF16)],
        compiler_params=_params(("parallel", "arbitrary")),
        name="gla",
    )(h3, w_in, w_in, w_in, w_in, w_in, gate_w, gate_b, gain)


MERGE_COLS = 256


def _merge_kernel(n_experts, h_ref, ret_ref, gla_ref, x_ref, wgl_ref, wr_ref, wg_ref, wo_ref, gain_ref,
                  wr2_ref, x1_ref, h2_ref, aff_ref):
    d = x_ref.shape[1]
    h, ret, gla = h_ref[...], ret_ref[...], gla_ref[...]
    blocks = []
    for j in range(0, d, MERGE_COLS):
        cols = slice(j, j + MERGE_COLS)
        cols_gla = slice(d + j, d + j + MERGE_COLS)
        m = (_sigmoid(_dot_nt(h, wgl_ref[cols, :])) * _dot(ret, wr_ref[:, cols])
             + _sigmoid(_dot_nt(h, wgl_ref[cols_gla, :])) * _dot(gla, wg_ref[:, cols]))
        blocks.append(m.astype(BF16))
    x1 = x_ref[...] + _dot(jnp.concatenate(blocks, axis=1), wo_ref[...])
    x1_ref[...] = x1
    h2 = _rms(x1, gain_ref[...]).astype(BF16)
    h2_ref[...] = h2
    logits2 = _dot(h2, wr2_ref[...])
    logits = logits2[:, :LANES] + logits2[:, LANES:]
    lane = lax.broadcasted_iota(jnp.int32, logits.shape, 1)
    logits = jnp.where(lane < n_experts, logits, -jnp.inf)
    p = jnp.exp(logits - jnp.max(logits, axis=-1, keepdims=True))
    aff = p / jnp.sum(p, axis=-1, keepdims=True)
    aff_ref[0] = aff.T[0:n_experts, :]


def _merge(h, ret, gla, x2, w_gl, w_ret, w_gla, w_out, gain, w_router2, n_experts, batch, tm):
    t, d = x2.shape
    s = t // batch
    per_b = s // tm
    rows = lambda width: pl.BlockSpec((tm, width), lambda i: (i, 0))
    return pl.pallas_call(
        functools.partial(_merge_kernel, n_experts),
        grid=(t // tm,),
        in_specs=[rows(d), rows(ret.shape[1]), rows(gla.shape[1]), rows(d),
                  _resident(w_gl.shape), _resident(w_ret.shape), _resident(w_gla.shape),
                  _resident(w_out.shape), _resident((1, d)),
                  _resident(w_router2.shape)],
        out_specs=[rows(d), rows(d),
                   pl.BlockSpec((1, n_experts, tm), lambda i: (i // per_b, 0, i % per_b))],
        out_shape=[jax.ShapeDtypeStruct((t, d), F32),
                   jax.ShapeDtypeStruct((t, d), BF16),
                   jax.ShapeDtypeStruct((batch, n_experts, s), F32)],
        compiler_params=_params(("parallel",)),
        name="merge",
    )(h, ret, gla, x2, w_gl, w_ret, w_gla, w_out, gain, w_router2)


def _prefix_count(mask):
    s = mask.shape[1]
    ri = lax.broadcasted_iota(jnp.int32, (LANES, LANES), 0)
    ci = lax.broadcasted_iota(jnp.int32, (LANES, LANES), 1)
    tri = jnp.where(ri <= ci, 1.0, 0.0).astype(BF16)
    off = jnp.zeros((mask.shape[0], 1), F32)
    parts = []
    for j in range(s // LANES):
        p = _dot(mask[:, j * LANES:(j + 1) * LANES].astype(BF16), tri) + off
        parts.append(p)
        off = p[:, LANES - 1:LANES]
    return jnp.concatenate(parts, axis=1)


def _route_kernel(capacity, aff_ref, slot_ref):
    a = aff_ref[0]
    bits = lax.bitcast_convert_type(a, jnp.int32)
    n_e = a.shape[0]
    cap = float(capacity)

    def count(pred):
        return jnp.sum(jnp.where(pred, 1.0, 0.0), axis=1, keepdims=True)

    def search(_, c):
        lo, hi = c
        mid = lo + lax.shift_right_logical(hi - lo, 1)
        ok = count(bits >= mid) >= cap
        return jnp.where(ok, mid, lo), jnp.where(ok, hi, mid)

    lo0 = jnp.zeros((n_e, 1), jnp.int32)
    hi0 = jnp.full((n_e, 1), 0x7F800000, jnp.int32)
    thr_bits, _ = lax.fori_loop(0, 31, search, (lo0, hi0))
    thr0 = jnp.max(jnp.where(bits <= thr_bits, a, -1.0), axis=1, keepdims=True)

    def counts(v):
        return count(a >= v), count(a > v)

    def unsettled(state):
        _, c_ge, c_gt = state
        bad = jnp.where(c_ge < cap, 1.0, jnp.where(c_gt >= cap, 1.0, 0.0))
        return jnp.max(bad, axis=0, keepdims=True)[0, 0] > 0.0

    def step(state):
        v, c_ge, c_gt = state
        below = jnp.max(jnp.where(a < v, a, -1.0), axis=1, keepdims=True)
        above = jnp.min(jnp.where(a > v, a, 2.0), axis=1, keepdims=True)
        v = jnp.where(c_ge < cap, below, jnp.where(c_gt >= cap, above, v))
        return (v,) + counts(v)

    thr, _, n_gt = lax.while_loop(unsettled, step, (thr0,) + counts(thr0))

    gt = a > thr
    eq = a == thr
    need = cap - n_gt
    eq_rank = _prefix_count(jnp.where(eq, 1.0, 0.0))
    sel = jnp.where(gt, 1.0, jnp.where(eq, jnp.where(eq_rank <= need, 1.0, 0.0), 0.0))
    pos = _prefix_count(sel)
    slot_ref[0] = jnp.where(sel > 0.0, pos - 1.0, -1.0)


def _route(aff_t, capacity):
    b, e, s = aff_t.shape
    return pl.pallas_call(
        functools.partial(_route_kernel, capacity),
        grid=(1,),
        in_specs=[pl.BlockSpec((1, b * e, s), lambda i: (0, 0, 0))],
        out_specs=pl.BlockSpec((1, b * e, s), lambda i: (0, 0, 0)),
        out_shape=jax.ShapeDtypeStruct((1, b * e, s), F32),
        compiler_params=_params(("arbitrary",)),
        name="route",
    )(aff_t.reshape(1, b * e, s)).reshape(b, e, s)


def _dispatch_kernel(slot_ref, aff_ref, h_ref, xg_ref, gate_ref):
    group, cap = xg_ref.shape[0], xg_ref.shape[2]
    h = h_ref[0]
    row = lax.broadcasted_iota(jnp.int32, (cap, h.shape[0]), 0).astype(F32)
    for g in range(group):
        hit = row == slot_ref[0, g:g + 1, :]
        xg_ref[g, 0] = _dot(jnp.where(hit, 1.0, 0.0).astype(BF16), h).astype(xg_ref.dtype)
        gate = jnp.sum(jnp.where(hit, aff_ref[0, g:g + 1, :], 0.0), axis=1, keepdims=True)
        gate_ref[g, 0] = jnp.broadcast_to(gate, gate_ref.shape[2:])


def _dispatch(slot, aff_t, h2, capacity):
    b, e, s = slot.shape
    d = h2.shape[-1]
    group = _pick(e, (8,))
    return pl.pallas_call(
        _dispatch_kernel,
        grid=(b, e // group),
        in_specs=[pl.BlockSpec((1, group, s), lambda i, j: (i, j, 0)),
                  pl.BlockSpec((1, group, s), lambda i, j: (i, j, 0)),
                  pl.BlockSpec((1, s, d), lambda i, j: (i, 0, 0))],
        out_specs=[pl.BlockSpec((group, 1, capacity, d), lambda i, j: (j, i, 0, 0)),
                   pl.BlockSpec((group, 1, capacity, LANES), lambda i, j: (j, i, 0, 0))],
        out_shape=[jax.ShapeDtypeStruct((e, b, capacity, d), BF16),
                   jax.ShapeDtypeStruct((e, b, capacity, LANES), F32)],
        compiler_params=_params(("parallel", "parallel")),
        name="dispatch",
    )(slot, aff_t, h2)


def _ffn_kernel(tm, tf, x_ref, gate_ref, wg_hbm, wu_hbm, wd_hbm, y_ref, acc_ref, wg_buf, wu_buf, wd_buf, sems):
    expert, n_experts = pl.program_id(0), pl.num_programs(0)
    n_f = wg_hbm.shape[2] // tf
    m, d = x_ref.shape[1], x_ref.shape[2]

    def tile_copies(e, f, slot):
        cols = pl.ds(pl.multiple_of(f * tf, tf), tf)
        return (pltpu.make_async_copy(wg_hbm.at[e, :, cols], wg_buf.at[slot], sems.at[0, slot]),
                pltpu.make_async_copy(wu_hbm.at[e, :, cols], wu_buf.at[slot], sems.at[1, slot]),
                pltpu.make_async_copy(wd_hbm.at[e, cols, :], wd_buf.at[slot], sems.at[2, slot]))

    def start(e, f, slot):
        for copy in tile_copies(e, f, slot):
            copy.start()

    @pl.when(expert == 0)
    def _():
        start(0, 0, 0)

    def step(f, first, final):
        slot = lax.rem(expert * n_f + f, 2)
        if final:
            @pl.when(expert + 1 < n_experts)
            def _():
                start(expert + 1, 0, 1 - slot)
        else:
            start(expert, f + 1, 1 - slot)
        for copy in tile_copies(expert, f, slot):
            copy.wait()
        w_gate = wg_buf[slot].astype(BF16)
        w_up = wu_buf[slot].astype(BF16)
        w_down = wd_buf[slot].astype(BF16)
        for i in range(m // tm):
            rows = pl.ds(i * tm, tm)
            x = x_ref[0, rows, :]
            act = (_silu(_dot(x, w_gate)) * _dot(x, w_up)).astype(BF16)
            part = _dot(act, w_down)
            if not first:
                part = acc_ref[rows, :] + part
            if final:
                gate = jnp.concatenate([gate_ref[0, rows, :]] * (d // LANES), axis=1)
                y_ref[0, rows, :] = (part * gate).astype(y_ref.dtype)
            else:
                acc_ref[rows, :] = part

    if n_f == 1:
        step(0, True, True)
    else:
        step(0, True, False)
        if n_f > 2:
            def middle(f, carry):
                step(f, False, False)
                return carry
            lax.fori_loop(1, n_f - 1, middle, 0)
        step(n_f - 1, False, True)


def _ffn(xg, gate, w_gate, w_up, w_down, tf, tm):
    e, m, d = xg.shape
    assert w_gate.shape[2] % tf == 0
    return pl.pallas_call(
        functools.partial(_ffn_kernel, tm, tf),
        grid=(e,),
        in_specs=[pl.BlockSpec((1, m, d), lambda i: (i, 0, 0)),
                  pl.BlockSpec((1, m, LANES), lambda i: (i, 0, 0)),
                  pl.BlockSpec(memory_space=pl.ANY),
                  pl.BlockSpec(memory_space=pl.ANY),
                  pl.BlockSpec(memory_space=pl.ANY)],
        out_specs=pl.BlockSpec((1, m, d), lambda i: (i, 0, 0)),
        out_shape=jax.ShapeDtypeStruct((e, m, d), BF16),
        scratch_shapes=[pltpu.VMEM((m, d), F32),
                        pltpu.VMEM((2, d, tf), w_gate.dtype),
                        pltpu.VMEM((2, d, tf), w_up.dtype),
                        pltpu.VMEM((2, tf, d), w_down.dtype),
                        pltpu.SemaphoreType.DMA((3, 2))],
        compiler_params=_params(("arbitrary",)),
        name="ffn",
    )(xg, gate, w_gate, w_up, w_down)


def _combine_kernel(slot_ref, y_ref, x1_ref, gain_ref, o_ref):
    n_e, cap = y_ref.shape[0], y_ref.shape[2]
    tt = x1_ref.shape[1]
    acc = x1_ref[0]
    row = lax.broadcasted_iota(jnp.int32, (cap, tt), 0).astype(F32)
    for e in range(n_e):
        onehot = jnp.where(row == slot_ref[0, e:e + 1, :], 1.0, 0.0).astype(BF16)
        acc = acc + _dot_tn(onehot, y_ref[e, 0])
    o_ref[0] = _rms(acc, gain_ref[...])


def _combine(slot, y, x1, gain, tt):
    b, e, s = slot.shape
    cap, d = y.shape[2], y.shape[3]
    return pl.pallas_call(
        _combine_kernel,
        grid=(b, s // tt),
        in_specs=[pl.BlockSpec((1, e, tt), lambda i, j: (i, 0, j)),
                  pl.BlockSpec((e, 1, cap, d), lambda i, j: (0, i, 0, 0)),
                  pl.BlockSpec((1, tt, d), lambda i, j: (i, j, 0)),
                  pl.BlockSpec((1, d), lambda i, j: (0, 0))],
        out_specs=pl.BlockSpec((1, tt, d), lambda i, j: (i, j, 0)),
        out_shape=jax.ShapeDtypeStruct((b, s, d), F32),
        compiler_params=_params(("parallel", "parallel")),
        name="combine",
    )(slot, y, x1, gain)


def _layer(x, cos, sin, norm_mix, w_in, ret_decay_fwd, ret_decay_bwd, ret_norm,
           gla_gate_w_fwd, gla_gate_b_fwd, gla_gate_w_bwd, gla_gate_b_bwd, gla_norm,
           w_branch_ret, w_branch_gla, w_out, norm_ffn, w_router, w_gate, w_up, w_down, norm_out):
    b, s, d = x.shape
    t = b * s
    ret_qk, ret_v = RET_HEADS * RET_DK, RET_HEADS * RET_DV
    gla_qk, gla_v = GLA_HEADS * GLA_DK, GLA_HEADS * GLA_DV
    rank = GLA_GATE_RANK
    gla0 = 2 * ret_qk + 2 * ret_v
    ga0 = gla0 + 2 * gla_qk + 2 * gla_v
    assert w_in.shape == (d, ga0 + 2 * rank + 2 * d)
    assert s % CHUNK == 0 and 3 * GATE_COPY <= LANES and ga0 % LANES == 0

    w_in_t = w_in.T.astype(BF16)
    w_gl = w_in_t[ga0 + 2 * rank:, :]

    gw = jnp.zeros((GLA_HEADS, GATE_COPY, 2 * GLA_DK), F32)
    gw = gw.at[:, :rank, :GLA_DK].set(gla_gate_w_fwd.reshape(rank, GLA_HEADS, GLA_DK).transpose(1, 0, 2))
    gw = gw.at[:, rank:, GLA_DK:].set(gla_gate_w_bwd.reshape(rank, GLA_HEADS, GLA_DK).transpose(1, 0, 2))
    gw_hi = gw.astype(BF16)
    gw_lo = (gw - gw_hi.astype(F32)).astype(BF16)
    gate_w = jnp.concatenate([gw_hi, gw_lo, gw_hi, jnp.zeros_like(gw_hi)], axis=1)
    gate_b = jnp.concatenate([gla_gate_b_fwd.reshape(GLA_HEADS, 1, GLA_DK),
                              gla_gate_b_bwd.reshape(GLA_HEADS, 1, GLA_DK)], axis=2)

    dec = jnp.stack([ret_decay_fwd, ret_decay_bwd], axis=1)[:, :, None]
    dec = jnp.pad(jnp.broadcast_to(dec, (RET_HEADS, 2, RET_DV)), ((0, 0), (0, 6), (0, 0)))

    x2 = x.reshape(t, d)
    ret, h3 = _retention(x, norm_mix[None, :], w_in_t, 0, cos, sin, dec, ret_norm[None, :])
    gla = _gla(h3, w_in_t, gla0, gate_w, gate_b, gla_norm[None, :])
    h = h3.reshape(t, d)

    n_e = w_router.shape[1]
    w_r = jnp.pad(w_router, ((0, 0), (0, LANES - n_e)))
    wr_hi = w_r.astype(BF16)
    wr_lo = (w_r - wr_hi.astype(F32)).astype(BF16)
    x1, h2, aff_t = _merge(h, ret.reshape(t, ret_v), gla.reshape(t, gla_v), x2, w_gl,
                           w_branch_ret.astype(BF16), w_branch_gla.astype(BF16), w_out.astype(BF16),
                           norm_ffn[None, :], jnp.concatenate([wr_hi, wr_lo], axis=1), n_e, b,
                           _pick(s, (1024, 512, 256, 128)))

    capacity = EC_CAPACITY_FACTOR * s // n_e
    slot = _route(aff_t, capacity)
    xg, gate = _dispatch(slot, aff_t, h2.reshape(b, s, d), capacity)
    f = w_gate.shape[2]
    y = _ffn(xg.reshape(n_e, b * capacity, d), gate.reshape(n_e, b * capacity, LANES),
             w_gate, w_up, w_down, _pick(f, (256, 128)), _pick(b * capacity, (1024, 512, 256, 128)))
    return _combine(slot, y.reshape(n_e, b, capacity, d), x1.reshape(b, s, d), norm_out,
                    _pick(s, (1024, 512, 256, 128)))


def kernel(x, positions, norm_mix, w_in, ret_decay_fwd, ret_decay_bwd, ret_norm, gla_gate_w_fwd,
           gla_gate_b_fwd, gla_gate_w_bwd, gla_gate_b_bwd, gla_norm, w_branch_ret, w_branch_gla,
           w_out, norm_ffn, w_router, w_gate, w_up, w_down, norm_final):
    depth = norm_mix.shape[0]
    assert depth == 1, "the final RMSNorm is fused into the last layer's combine stage"
    cos, sin = _rope_table(positions, RET_DK)
    return _layer(x, cos, sin, norm_mix[0], w_in[0], ret_decay_fwd[0], ret_decay_bwd[0], ret_norm[0],
                  gla_gate_w_fwd[0], gla_gate_b_fwd[0], gla_gate_w_bwd[0], gla_gate_b_bwd[0], gla_norm[0],
                  w_branch_ret[0], w_branch_gla[0], w_out[0], norm_ffn[0], w_router[0],
                  w_gate[0], w_up[0], w_down[0], norm_final[None, :])
```

```python
import functools

import jax
import jax.numpy as jnp
from jax import lax
from jax.experimental import pallas as pl
from jax.experimental.pallas import tpu as pltpu

F32 = jnp.float32
BF16 = jnp.bfloat16

RET_HEADS = 4
RET_DK = 128
RET_DV = 256
GLA_HEADS = 4
GLA_DK = 128
GLA_DV = 256
GLA_GATE_RANK = 16
GLA_GATE_NORMALIZER = 16.0
CHUNK = 128
EC_CAPACITY_FACTOR = 2
ROPE_THETA = 10000.0
NORM_EPS = 1e-6
LOG2_E = 1.4426950408889634

CHUNK_UNROLL = 16
LANES = 128
VMEM_LIMIT = 56 << 20


def _params(sem, vmem=VMEM_LIMIT):
    return pltpu.CompilerParams(dimension_semantics=sem, vmem_limit_bytes=vmem)


def _resident(shape):
    return pl.BlockSpec(shape, lambda *_: (0,) * len(shape), pipeline_mode=pl.Buffered(1))


def _pick(n, prefs):
    for p in prefs:
        if n % p == 0:
            return p
    return n


def _sigmoid(x):
    return 1.0 / (1.0 + jnp.exp(-x))


def _silu(x):
    half = 0.5 * x
    return half + half * jnp.tanh(half)


def _log_sigmoid(x):
    return jnp.minimum(x, 0.0) - jnp.log(1.0 + jnp.exp(-jnp.abs(x)))


def _rms(x, gain):
    return x * lax.rsqrt(jnp.mean(x * x, axis=-1, keepdims=True) + NORM_EPS) * gain


def _dot(a, b):
    return jnp.dot(a, b, preferred_element_type=F32)


def _dot_nt(a, b):
    return lax.dot_general(a, b, (((1,), (1,)), ((), ())), preferred_element_type=F32)


def _dot_tn(a, b):
    return lax.dot_general(a, b, (((0,), (0,)), ((), ())), preferred_element_type=F32)


def _split2(x):
    hi = x.astype(BF16)
    lo = (x - hi.astype(F32)).astype(BF16)
    return hi, lo


def _rope_table_kernel(pos_a_ref, pos_b_ref, freq_ref, cos_ref, sin_ref):
    n, dk = pos_a_ref.shape[1], freq_ref.shape[1]
    lower = lax.broadcasted_iota(jnp.int32, (n, dk), 1) < dk // 2
    ang = jnp.where(lower, pos_a_ref[0].astype(F32), pos_b_ref[0].astype(F32)) * freq_ref[...]
    cos, sin = jnp.cos(ang), jnp.sin(ang)
    cos_x, sin_x = pltpu.roll(cos, dk // 2, 1), pltpu.roll(sin, dk // 2, 1)
    cos_ref[0, 0:n, :] = jnp.where(lower, cos, cos_x)
    cos_ref[0, n:2 * n, :] = jnp.where(lower, cos_x, cos)
    sin_ref[0, 0:n, :] = jnp.where(lower, -sin, sin_x)
    sin_ref[0, n:2 * n, :] = jnp.where(lower, -sin_x, sin)


def _rope_table(positions, dk):
    b, s = positions.shape
    assert s % 16 == 0
    half = jnp.arange(0, dk, 2, dtype=F32) / dk
    inv_freq = ROPE_THETA ** (-half)
    freq = jnp.concatenate([inv_freq, inv_freq])[None, :]
    out = jax.ShapeDtypeStruct((b, s, dk), F32)
    pos = positions[:, :, None]
    return pl.pallas_call(
        _rope_table_kernel,
        grid=(b,),
        in_specs=[pl.BlockSpec((1, s // 2, 1), lambda i: (i, 0, 0)),
                  pl.BlockSpec((1, s // 2, 1), lambda i: (i, 1, 0)),
                  pl.BlockSpec((1, dk), lambda i: (0, 0))],
        out_specs=[pl.BlockSpec((1, s, dk), lambda i: (i, 0, 0))] * 2,
        out_shape=[out, out],
        compiler_params=_params(("parallel",)),
        name="rope_table",
    )(pos, pos, freq)


def _rope(t, cos, sin_signed):
    return t * cos + pltpu.roll(t, t.shape[-1] // 2, 1) * sin_signed


def _chunk_rows(n):
    return pl.ds(pl.multiple_of(n * CHUNK, CHUNK), CHUNK)


def _row_tile(s_len):
    return _pick(s_len, (512, 256, CHUNK))


def _per_tile(chunk_pattern, s_len):
    return jnp.concatenate([chunk_pattern] * (_row_tile(s_len) // CHUNK), axis=0)


def _for_row_tiles(s_len, body):
    tile = _row_tile(s_len)

    def step(i, carry):
        body(pl.ds(pl.multiple_of(i * tile, tile), tile))
        return carry

    lax.fori_loop(0, s_len // tile, step, 0, unroll=True)


def _retention_kernel(x_ref, mix_gain_ref, wq_ref, wk_ref, wv_ref, wg_ref, cos_ref, sin_ref, dec_ref,
                      gain_ref, o_ref, h_ref, qk_ref, qw_ref, kw_ref, v_ref, gg_ref, a_ref, st_ref):
    c = CHUNK
    s_len, dk, dv = qk_ref.shape[0], qk_ref.shape[1] // 2, v_ref.shape[1]
    n_chunks = s_len // c
    scale = dk ** -0.5

    lg_f = _log_sigmoid(dec_ref[0, 0:1, :])
    lg_b = _log_sigmoid(dec_ref[0, 1:2, :])
    lgf_k, lgb_k = lg_f[:, :dk], lg_b[:, :dk]
    pos = lax.broadcasted_iota(jnp.int32, (c, dk), 0).astype(F32)
    wq_f = _per_tile(jnp.exp((pos + 1.0) * lgf_k), s_len)
    wk_f = _per_tile(jnp.exp((c - 1.0 - pos) * lgf_k), s_len)
    wq_b = _per_tile(jnp.exp((c - pos) * lgb_k), s_len)
    wk_b = _per_tile(jnp.exp(pos * lgb_k), s_len)
    ri = lax.broadcasted_iota(jnp.int32, (c, c), 0)
    ci = lax.broadcasted_iota(jnp.int32, (c, c), 1)
    lower = ri >= ci
    rel = (ri - ci).astype(F32)
    decay_mask = jnp.where(lower,
                           jnp.exp(jnp.where(lower, rel, 0.0) * lg_f[:, :c]),
                           jnp.exp(jnp.where(lower, 0.0, -rel) * lg_b[:, :c]))
    chunk_decay_f = jnp.exp(c * lg_f)
    chunk_decay_b = jnp.exp(c * lg_b)

    @pl.when(pl.program_id(1) == 0)
    def _():
        def norm(rows):
            h_ref[0, rows, :] = _rms(x_ref[0, rows, :], mix_gain_ref[...]).astype(h_ref.dtype)
        _for_row_tiles(s_len, norm)

    w_qk = jnp.concatenate([wq_ref[...], wk_ref[...]], axis=0)
    w_v = wv_ref[...]
    w_g = wg_ref[...]

    def project(rows):
        h = h_ref[0, rows, :]
        qk = _dot_nt(h, w_qk)
        cos, sin = cos_ref[0, rows, :], sin_ref[0, rows, :]
        qr = _rope(qk[:, :dk], cos, sin) * scale
        kr = _rope(qk[:, dk:], cos, sin)
        qk_ref[rows, :] = jnp.concatenate([qr, kr], axis=1).astype(BF16)
        qw_ref[rows, :] = jnp.concatenate([qr * wq_f, qr * wq_b], axis=1).astype(BF16)
        kw_ref[rows, :] = jnp.concatenate([kr * wk_f, kr * wk_b], axis=1).astype(BF16)
        v_ref[rows, :] = _dot_nt(h, w_v).astype(BF16)
        gg_ref[rows, :] = _silu(_dot_nt(h, w_g)) * gain_ref[...]

    _for_row_tiles(s_len, project)

    kv = [_dot_tn(kw_ref[n * c:(n + 1) * c, :], v_ref[n * c:(n + 1) * c, :]) for n in range(n_chunks)]
    s_f = s_b = jnp.zeros((dk, dv), F32)
    for i in range(n_chunks):
        n = n_chunks - 1 - i
        st_ref[i, 0:dk, :] = s_f.astype(BF16)
        st_ref[n, dk:2 * dk, :] = s_b.astype(BF16)
        if i + 1 < n_chunks:
            s_f = chunk_decay_f * s_f + kv[i][0:dk, :]
            s_b = chunk_decay_b * s_b + kv[n][dk:2 * dk, :]

    def scores(n, carry):
        rows = _chunk_rows(n)
        qk = qk_ref[rows, :]
        a_ref[rows, :] = (_dot_nt(qk[:, :dk], qk[:, dk:]) * decay_mask).astype(BF16)
        return carry

    lax.fori_loop(0, n_chunks, scores, 0, unroll=CHUNK_UNROLL)

    def phase2(n, carry):
        rows = _chunk_rows(n)
        o = _dot(a_ref[rows, :], v_ref[rows, :]) + _dot(qw_ref[rows, :], st_ref[n])
        mu = jnp.mean(o, axis=-1, keepdims=True)
        d = o - mu
        var = jnp.mean(d * d, axis=-1, keepdims=True)
        o_ref[0, rows, :] = (d * lax.rsqrt(var + NORM_EPS) * gg_ref[rows, :]).astype(o_ref.dtype)
        return carry

    lax.fori_loop(0, n_chunks, phase2, 0, unroll=CHUNK_UNROLL)


def _retention(x, mix_gain, w_in, col0, cos, sin, dec, gain):
    b, s, d = x.shape
    h, dk, dv = RET_HEADS, RET_DK, RET_DV
    q0 = col0 // dk
    v0 = (col0 + 2 * h * dk) // dv
    n_chunks = s // CHUNK
    return pl.pallas_call(
        _retention_kernel,
        grid=(b, h),
        in_specs=[pl.BlockSpec((1, s, d), lambda i, j: (i, 0, 0)),
                  pl.BlockSpec((1, d), lambda i, j: (0, 0)),
                  pl.BlockSpec((dk, d), lambda i, j: (q0 + j, 0)),
                  pl.BlockSpec((dk, d), lambda i, j: (q0 + h + j, 0)),
                  pl.BlockSpec((dv, d), lambda i, j: (v0 + j, 0)),
                  pl.BlockSpec((dv, d), lambda i, j: (v0 + h + j, 0)),
                  pl.BlockSpec((1, s, dk), lambda i, j: (i, 0, 0)),
                  pl.BlockSpec((1, s, dk), lambda i, j: (i, 0, 0)),
                  pl.BlockSpec((1, 8, dv), lambda i, j: (j, 0, 0)),
                  pl.BlockSpec((1, dv), lambda i, j: (0, j))],
        out_specs=[pl.BlockSpec((1, s, dv), lambda i, j: (i, 0, j)),
                   pl.BlockSpec((1, s, d), lambda i, j: (i, 0, 0))],
        out_shape=[jax.ShapeDtypeStruct((b, s, h * dv), BF16),
                   jax.ShapeDtypeStruct((b, s, d), BF16)],
        scratch_shapes=[pltpu.VMEM((s, 2 * dk), BF16),
                        pltpu.VMEM((s, 2 * dk), BF16),
                        pltpu.VMEM((s, 2 * dk), BF16),
                        pltpu.VMEM((s, dv), BF16),
                        pltpu.VMEM((s, dv), F32),
                        pltpu.VMEM((s, CHUNK), BF16),
                        pltpu.VMEM((n_chunks, 2 * dk, dv), BF16)],
        compiler_params=_params(("parallel", "arbitrary")),
        name="retention",
    )(x, mix_gain, w_in, w_in, w_in, w_in, cos, sin, dec, gain)


GATE_COPY = 2 * GLA_GATE_RANK


def _cumsum_dot(tri2, x):
    hi, lo = _split2(x)
    return _dot(tri2, jnp.concatenate([hi, lo], axis=0))


def _gla_kernel(h_ref, wq_ref, wk_ref, wv_ref, wg_ref, wa_ref, gw_ref, gb_ref, gain_ref, o_ref,
                q_ref, k_ref, v_ref, gg_ref, ga_ref, la_ref, qs_ref, ks_ref, qw_ref, kw_ref, last_ref,
                a_ref, st_ref):
    c = CHUNK
    s_len, dk, dv = q_ref.shape[0], q_ref.shape[1], v_ref.shape[1]
    n_chunks = s_len // c
    scale = dk ** -0.5
    inv_norm = LOG2_E / GLA_GATE_NORMALIZER

    w_qk = jnp.concatenate([wq_ref[...], wk_ref[...]], axis=0)
    w_v = wv_ref[...]
    w_g = wg_ref[...]

    @pl.when(pl.program_id(1) == 0)
    def _():
        w_a = wa_ref[...]

        def gate_inputs(rows):
            x = _dot_nt(h_ref[0, rows, :], w_a)
            x_hi = x.astype(BF16).astype(F32)
            lane = lax.broadcasted_iota(jnp.int32, x.shape, 1)
            ga = jnp.where(lane < GATE_COPY, x_hi,
                           jnp.where(lane < 2 * GATE_COPY, pltpu.roll(x_hi, GATE_COPY, 1),
                                     jnp.where(lane < 3 * GATE_COPY, pltpu.roll(x - x_hi, 2 * GATE_COPY, 1),
                                               0.0)))
            ga_ref[rows, :] = ga.astype(BF16)

        _for_row_tiles(s_len, gate_inputs)

    def project(rows):
        h = h_ref[0, rows, :]
        qk = _dot_nt(h, w_qk)
        q_ref[rows, :] = qk[:, :dk] * scale
        k_ref[rows, :] = qk[:, dk:]
        v_ref[rows, :] = _dot_nt(h, w_v).astype(BF16)
        gg_ref[rows, :] = _silu(_dot_nt(h, w_g)) * gain_ref[...]
        la_ref[rows, :] = _log_sigmoid(_dot(ga_ref[rows, :], gw_ref[0]) + gb_ref[0]) * inv_norm

    _for_row_tiles(s_len, project)

    ri = lax.broadcasted_iota(jnp.int32, (c, c), 0)
    ci = lax.broadcasted_iota(jnp.int32, (c, c), 1)
    lower = ri >= ci
    tri_lower = jnp.where(lower, 1.0, 0.0).astype(BF16)
    tri2_lower = jnp.concatenate([tri_lower, tri_lower], axis=1)

    def cumulate(n, carry):
        rows = _chunk_rows(n)
        la = la_ref[rows, :]
        incl = _cumsum_dot(tri2_lower, la)
        cum_f = incl[:, 0:dk]
        cum_b = incl[c - 1:c, dk:2 * dk] - incl[:, dk:2 * dk] + la[:, dk:2 * dk]
        last_f, last_b = cum_f[c - 1:c, :], cum_b[0:1, :]
        ref_f, ref_b = cum_f[c // 2:c // 2 + 1, :], cum_b[c // 2 - 1:c // 2, :]
        q, k = q_ref[rows, :], k_ref[rows, :]
        q_f = q * jnp.exp2(cum_f - ref_f)
        q_b = q * jnp.exp2(cum_b - ref_b)
        qs_ref[rows, :] = jnp.concatenate([q_f, q_b], axis=1).astype(BF16)
        ks_ref[rows, :] = jnp.concatenate([k * jnp.exp2(ref_f - cum_f), k * jnp.exp2(ref_b - cum_b)],
                                          axis=1).astype(BF16)
        qw_ref[rows, :] = jnp.concatenate([q_f * jnp.exp2(ref_f), q_b * jnp.exp2(ref_b)], axis=1).astype(BF16)
        kw_ref[rows, :] = jnp.concatenate([k * jnp.exp2(last_f - cum_f), k * jnp.exp2(last_b - cum_b)],
                                          axis=1).astype(BF16)
        last_ref[n, 0:1, :] = last_f
        last_ref[n, 1:2, :] = last_b
        return carry

    lax.fori_loop(0, n_chunks, cumulate, 0, unroll=CHUNK_UNROLL)

    kv = [_dot_tn(v_ref[n * c:(n + 1) * c, :], kw_ref[n * c:(n + 1) * c, :]) for n in range(n_chunks)]
    s_f = s_b = jnp.zeros((dv, dk), F32)
    for i in range(n_chunks):
        n = n_chunks - 1 - i
        st_ref[i, :, 0:dk] = s_f.astype(BF16)
        st_ref[n, :, dk:2 * dk] = s_b.astype(BF16)
        if i + 1 < n_chunks:
            s_f = jnp.exp2(last_ref[i, 0:1, :]) * s_f + kv[i][:, 0:dk]
            s_b = jnp.exp2(last_ref[n, 1:2, :]) * s_b + kv[n][:, dk:2 * dk]

    def scores(n, carry):
        rows = _chunk_rows(n)
        qs, ks = qs_ref[rows, :], ks_ref[rows, :]
        a_ref[rows, :] = jnp.where(lower, _dot_nt(qs[:, :dk], ks[:, :dk]),
                                   _dot_nt(qs[:, dk:], ks[:, dk:])).astype(BF16)
        return carry

    lax.fori_loop(0, n_chunks, scores, 0, unroll=CHUNK_UNROLL)

    def phase2(n, carry):
        rows = _chunk_rows(n)
        o = _dot(a_ref[rows, :], v_ref[rows, :]) + _dot_nt(qw_ref[rows, :], st_ref[n])
        rms = lax.rsqrt(jnp.mean(o * o, axis=-1, keepdims=True) + NORM_EPS)
        o_ref[0, rows, :] = (o * rms * gg_ref[rows, :]).astype(o_ref.dtype)
        return carry

    lax.fori_loop(0, n_chunks, phase2, 0, unroll=CHUNK_UNROLL)


def _gla(h3, w_in, col0, gate_w, gate_b, gain):
    b, s, d = h3.shape
    h, dk, dv = GLA_HEADS, GLA_DK, GLA_DV
    q0 = col0 // dk
    v0 = (col0 + 2 * h * dk) // dv
    a0 = (col0 + 2 * h * dk + 2 * h * dv) // LANES
    n_chunks = s // CHUNK
    return pl.pallas_call(
        _gla_kernel,
        grid=(b, h),
        in_specs=[pl.BlockSpec((1, s, d), lambda i, j: (i, 0, 0)),
                  pl.BlockSpec((dk, d), lambda i, j: (q0 + j, 0)),
                  pl.BlockSpec((dk, d), lambda i, j: (q0 + h + j, 0)),
                  pl.BlockSpec((dv, d), lambda i, j: (v0 + j, 0)),
                  pl.BlockSpec((dv, d), lambda i, j: (v0 + h + j, 0)),
                  pl.BlockSpec((LANES, d), lambda i, j: (a0, 0)),
                  pl.BlockSpec((1, LANES, 2 * dk), lambda i, j: (j, 0, 0)),
                  pl.BlockSpec((1, 1, 2 * dk), lambda i, j: (j, 0, 0)),
                  pl.BlockSpec((1, dv), lambda i, j: (0, j))],
        out_specs=pl.BlockSpec((1, s, dv), lambda i, j: (i, 0, j)),
        out_shape=jax.ShapeDtypeStruct((b, s, h * dv), BF16),
        scratch_shapes=[pltpu.VMEM((s, dk), F32),
                        pltpu.VMEM((s, dk), F32),
                        pltpu.VMEM((s, dv), BF16),
                        pltpu.VMEM((s, dv), F32),
                        pltpu.VMEM((s, LANES), BF16),
                        pltpu.VMEM((s, 2 * dk), F32),
                        pltpu.VMEM((s, 2 * dk), BF16),
                        pltpu.VMEM((s, 2 * dk), BF16),
                        pltpu.VMEM((s, 2 * dk), BF16),
                        pltpu.VMEM((s, 2 * dk), BF16),
                        pltpu.VMEM((n_chunks, 8, dk), F32),
                        pltpu.VMEM((s, CHUNK), BF16),
                        pltpu.VMEM((n_chunks, dv, 2 * dk), BF16)],
        compiler_params=_params(("parallel", "arbitrary")),
        name="gla",
    )(h3, w_in, w_in, w_in, w_in, w_in, gate_w, gate_b, gain)


MERGE_COLS = 256


def _merge_kernel(n_experts, h_ref, ret_ref, gla_ref, x_ref, wgl_ref, wr_ref, wg_ref, wo_ref, gain_ref,
                  wr2_ref, x1_ref, h2_ref, aff_ref):
    d = x_ref.shape[1]
    h, ret, gla = h_ref[...], ret_ref[...], gla_ref[...]
    blocks = []
    for j in range(0, d, MERGE_COLS):
        cols = slice(j, j + MERGE_COLS)
        cols_gla = slice(d + j, d + j + MERGE_COLS)
        m = (_sigmoid(_dot_nt(h, wgl_ref[cols, :])) * _dot(ret, wr_ref[:, cols])
             + _sigmoid(_dot_nt(h, wgl_ref[cols_gla, :])) * _dot(gla, wg_ref[:, cols]))
        blocks.append(m.astype(BF16))
    x1 = x_ref[...] + _dot(jnp.concatenate(blocks, axis=1), wo_ref[...])
    x1_ref[...] = x1
    h2 = _rms(x1, gain_ref[...]).astype(BF16)
    h2_ref[...] = h2
    logits2 = _dot(h2, wr2_ref[...])
    logits = logits2[:, :LANES] + logits2[:, LANES:]
    lane = lax.broadcasted_iota(jnp.int32, logits.shape, 1)
    logits = jnp.where(lane < n_experts, logits, -jnp.inf)
    p = jnp.exp(logits - jnp.max(logits, axis=-1, keepdims=True))
    aff = p / jnp.sum(p, axis=-1, keepdims=True)
    aff_ref[0] = aff.T[0:n_experts, :]


def _merge(h, ret, gla, x2, w_gl, w_ret, w_gla, w_out, gain, w_router2, n_experts, batch, tm):
    t, d = x2.shape
    s = t // batch
    per_b = s // tm
    rows = lambda width: pl.BlockSpec((tm, width), lambda i: (i, 0))
    return pl.pallas_call(
        functools.partial(_merge_kernel, n_experts),
        grid=(t // tm,),
        in_specs=[rows(d), rows(ret.shape[1]), rows(gla.shape[1]), rows(d),
                  _resident(w_gl.shape), _resident(w_ret.shape), _resident(w_gla.shape),
                  _resident(w_out.shape), _resident((1, d)),
                  _resident(w_router2.shape)],
        out_specs=[rows(d), rows(d),
                   pl.BlockSpec((1, n_experts, tm), lambda i: (i // per_b, 0, i % per_b))],
        out_shape=[jax.ShapeDtypeStruct((t, d), F32),
                   jax.ShapeDtypeStruct((t, d), BF16),
                   jax.ShapeDtypeStruct((batch, n_experts, s), F32)],
        compiler_params=_params(("parallel",)),
        name="merge",
    )(h, ret, gla, x2, w_gl, w_ret, w_gla, w_out, gain, w_router2)


def _prefix_count(mask):
    s = mask.shape[1]
    ri = lax.broadcasted_iota(jnp.int32, (LANES, LANES), 0)
    ci = lax.broadcasted_iota(jnp.int32, (LANES, LANES), 1)
    tri = jnp.where(ri <= ci, 1.0, 0.0).astype(BF16)
    off = jnp.zeros((mask.shape[0], 1), F32)
    parts = []
    for j in range(s // LANES):
        p = _dot(mask[:, j * LANES:(j + 1) * LANES].astype(BF16), tri) + off
        parts.append(p)
        off = p[:, LANES - 1:LANES]
    return jnp.concatenate(parts, axis=1)


def _route_kernel(capacity, aff_ref, slot_ref):
    a = aff_ref[0]
    bits = lax.bitcast_convert_type(a, jnp.int32)
    n_e = a.shape[0]
    cap = float(capacity)

    def count(pred):
        return jnp.sum(jnp.where(pred, 1.0, 0.0), axis=1, keepdims=True)

    def search(_, c):
        lo, hi = c
        mid = lo + lax.shift_right_logical(hi - lo, 1)
        ok = count(bits >= mid) >= cap
        return jnp.where(ok, mid, lo), jnp.where(ok, hi, mid)

    lo0 = jnp.zeros((n_e, 1), jnp.int32)
    hi0 = jnp.full((n_e, 1), 0x7F800000, jnp.int32)
    thr_bits, _ = lax.fori_loop(0, 31, search, (lo0, hi0))
    thr0 = jnp.max(jnp.where(bits <= thr_bits, a, -1.0), axis=1, keepdims=True)

    def counts(v):
        return count(a >= v), count(a > v)

    def unsettled(state):
        _, c_ge, c_gt = state
        bad = jnp.where(c_ge < cap, 1.0, jnp.where(c_gt >= cap, 1.0, 0.0))
        return jnp.max(bad, axis=0, keepdims=True)[0, 0] > 0.0

    def step(state):
        v, c_ge, c_gt = state
        below = jnp.max(jnp.where(a < v, a, -1.0), axis=1, keepdims=True)
        above = jnp.min(jnp.where(a > v, a, 2.0), axis=1, keepdims=True)
        v = jnp.where(c_ge < cap, below, jnp.where(c_gt >= cap, above, v))
        return (v,) + counts(v)

    thr, _, n_gt = lax.while_loop(unsettled, step, (thr0,) + counts(thr0))

    gt = a > thr
    eq = a == thr
    need = cap - n_gt
    eq_rank = _prefix_count(jnp.where(eq, 1.0, 0.0))
    sel = jnp.where(gt, 1.0, jnp.where(eq, jnp.where(eq_rank <= need, 1.0, 0.0), 0.0))
    pos = _prefix_count(sel)
    slot_ref[0] = jnp.where(sel > 0.0, pos - 1.0, -1.0)


def _route(aff_t, capacity):
    b, e, s = aff_t.shape
    return pl.pallas_call(
        functools.partial(_route_kernel, capacity),
        grid=(1,),
        in_specs=[pl.BlockSpec((1, b * e, s), lambda i: (0, 0, 0))],
        out_specs=pl.BlockSpec((1, b * e, s), lambda i: (0, 0, 0)),
        out_shape=jax.ShapeDtypeStruct((1, b * e, s), F32),
        compiler_params=_params(("arbitrary",)),
        name="route",
    )(aff_t.reshape(1, b * e, s)).reshape(b, e, s)


def _dispatch_kernel(slot_ref, aff_ref, h_ref, xg_ref, gate_ref):
    group, cap = xg_ref.shape[0], xg_ref.shape[2]
    h = h_ref[0]
    row = lax.broadcasted_iota(jnp.int32, (cap, h.shape[0]), 0).astype(F32)
    for g in range(group):
        hit = row == slot_ref[0, g:g + 1, :]
        xg_ref[g, 0] = _dot(jnp.where(hit, 1.0, 0.0).astype(BF16), h).astype(xg_ref.dtype)
        gate = jnp.sum(jnp.where(hit, aff_ref[0, g:g + 1, :], 0.0), axis=1, keepdims=True)
        gate_ref[g, 0] = jnp.broadcast_to(gate, gate_ref.shape[2:])


def _dispatch(slot, aff_t, h2, capacity):
    b, e, s = slot.shape
    d = h2.shape[-1]
    group = _pick(e, (8,))
    return pl.pallas_call(
        _dispatch_kernel,
        grid=(b, e // group),
        in_specs=[pl.BlockSpec((1, group, s), lambda i, j: (i, j, 0)),
                  pl.BlockSpec((1, group, s), lambda i, j: (i, j, 0)),
                  pl.BlockSpec((1, s, d), lambda i, j: (i, 0, 0))],
        out_specs=[pl.BlockSpec((group, 1, capacity, d), lambda i, j: (j, i, 0, 0)),
                   pl.BlockSpec((group, 1, capacity, LANES), lambda i, j: (j, i, 0, 0))],
        out_shape=[jax.ShapeDtypeStruct((e, b, capacity, d), BF16),
                   jax.ShapeDtypeStruct((e, b, capacity, LANES), F32)],
        compiler_params=_params(("parallel", "parallel")),
        name="dispatch",
    )(slot, aff_t, h2)


def _ffn_kernel(tm, tf, x_ref, gate_ref, wg_hbm, wu_hbm, wd_hbm, y_ref, acc_ref, wg_buf, wu_buf, wd_buf, sems):
    expert, n_experts = pl.program_id(0), pl.num_programs(0)
    n_f = wg_hbm.shape[2] // tf
    m, d = x_ref.shape[1], x_ref.shape[2]

    def tile_copies(e, f, slot):
        cols = pl.ds(pl.multiple_of(f * tf, tf), tf)
        return (pltpu.make_async_copy(wg_hbm.at[e, :, cols], wg_buf.at[slot], sems.at[0, slot]),
                pltpu.make_async_copy(wu_hbm.at[e, :, cols], wu_buf.at[slot], sems.at[1, slot]),
                pltpu.make_async_copy(wd_hbm.at[e, cols, :], wd_buf.at[slot], sems.at[2, slot]))

    def start(e, f, slot):
        for copy in tile_copies(e, f, slot):
            copy.start()

    @pl.when(expert == 0)
    def _():
        start(0, 0, 0)

    def step(f, first, final):
        slot = lax.rem(expert * n_f + f, 2)
        if final:
            @pl.when(expert + 1 < n_experts)
            def _():
                start(expert + 1, 0, 1 - slot)
        else:
            start(expert, f + 1, 1 - slot)
        for copy in tile_copies(expert, f, slot):
            copy.wait()
        w_gate = wg_buf[slot].astype(BF16)
        w_up = wu_buf[slot].astype(BF16)
        w_down = wd_buf[slot].astype(BF16)
        for i in range(m // tm):
            rows = pl.ds(i * tm, tm)
            x = x_ref[0, rows, :]
            act = (_silu(_dot(x, w_gate)) * _dot(x, w_up)).astype(BF16)
            part = _dot(act, w_down)
            if not first:
                part = acc_ref[rows, :] + part
            if final:
                gate = jnp.concatenate([gate_ref[0, rows, :]] * (d // LANES), axis=1)
                y_ref[0, rows, :] = (part * gate).astype(y_ref.dtype)
            else:
                acc_ref[rows, :] = part

    if n_f == 1:
        step(0, True, True)
    else:
        step(0, True, False)
        if n_f > 2:
            def middle(f, carry):
                step(f, False, False)
                return carry
            lax.fori_loop(1, n_f - 1, middle, 0)
        step(n_f - 1, False, True)


def _ffn(xg, gate, w_gate, w_up, w_down, tf, tm):
    e, m, d = xg.shape
    assert w_gate.shape[2] % tf == 0
    return pl.pallas_call(
        functools.partial(_ffn_kernel, tm, tf),
        grid=(e,),
        in_specs=[pl.BlockSpec((1, m, d), lambda i: (i, 0, 0)),
                  pl.BlockSpec((1, m, LANES), lambda i: (i, 0, 0)),
                  pl.BlockSpec(memory_space=pl.ANY),
                  pl.BlockSpec(memory_space=pl.ANY),
                  pl.BlockSpec(memory_space=pl.ANY)],
        out_specs=pl.BlockSpec((1, m, d), lambda i: (i, 0, 0)),
        out_shape=jax.ShapeDtypeStruct((e, m, d), BF16),
        scratch_shapes=[pltpu.VMEM((m, d), F32),
                        pltpu.VMEM((2, d, tf), w_gate.dtype),
                        pltpu.VMEM((2, d, tf), w_up.dtype),
                        pltpu.VMEM((2, tf, d), w_down.dtype),
                        pltpu.SemaphoreType.DMA((3, 2))],
        compiler_params=_params(("arbitrary",)),
        name="ffn",
    )(xg, gate, w_gate, w_up, w_down)


def _combine_kernel(slot_ref, y_ref, x1_ref, gain_ref, o_ref):
    n_e, cap = y_ref.shape[0], y_ref.shape[2]
    tt = x1_ref.shape[1]
    acc = x1_ref[0]
    row = lax.broadcasted_iota(jnp.int32, (cap, tt), 0).astype(F32)
    for e in range(n_e):
        onehot = jnp.where(row == slot_ref[0, e:e + 1, :], 1.0, 0.0).astype(BF16)
        acc = acc + _dot_tn(onehot, y_ref[e, 0])
    o_ref[0] = _rms(acc, gain_ref[...])


def _combine(slot, y, x1, gain, tt):
    b, e, s = slot.shape
    cap, d = y.shape[2], y.shape[3]
    return pl.pallas_call(
        _combine_kernel,
        grid=(b, s // tt),
        in_specs=[pl.BlockSpec((1, e, tt), lambda i, j: (i, 0, j)),
                  pl.BlockSpec((e, 1, cap, d), lambda i, j: (0, i, 0, 0)),
                  pl.BlockSpec((1, tt, d), lambda i, j: (i, j, 0)),
                  pl.BlockSpec((1, d), lambda i, j: (0, 0))],
        out_specs=pl.BlockSpec((1, tt, d), lambda i, j: (i, j, 0)),
        out_shape=jax.ShapeDtypeStruct((b, s, d), F32),
        compiler_params=_params(("parallel", "parallel")),
        name="combine",
    )(slot, y, x1, gain)


def _layer(x, cos, sin, norm_mix, w_in, ret_decay_fwd, ret_decay_bwd, ret_norm,
           gla_gate_w_fwd, gla_gate_b_fwd, gla_gate_w_bwd, gla_gate_b_bwd, gla_norm,
           w_branch_ret, w_branch_gla, w_out, norm_ffn, w_router, w_gate, w_up, w_down, norm_out):
    b, s, d = x.shape
    t = b * s
    ret_qk, ret_v = RET_HEADS * RET_DK, RET_HEADS * RET_DV
    gla_qk, gla_v = GLA_HEADS * GLA_DK, GLA_HEADS * GLA_DV
    rank = GLA_GATE_RANK
    gla0 = 2 * ret_qk + 2 * ret_v
    ga0 = gla0 + 2 * gla_qk + 2 * gla_v
    assert w_in.shape == (d, ga0 + 2 * rank + 2 * d)
    assert s % CHUNK == 0 and 3 * GATE_COPY <= LANES and ga0 % LANES == 0

    w_in_t = w_in.T.astype(BF16)
    w_gl = w_in_t[ga0 + 2 * rank:, :]

    gw = jnp.zeros((GLA_HEADS, GATE_COPY, 2 * GLA_DK), F32)
    gw = gw.at[:, :rank, :GLA_DK].set(gla_gate_w_fwd.reshape(rank, GLA_HEADS, GLA_DK).transpose(1, 0, 2))
    gw = gw.at[:, rank:, GLA_DK:].set(gla_gate_w_bwd.reshape(rank, GLA_HEADS, GLA_DK).transpose(1, 0, 2))
    gw_hi = gw.astype(BF16)
    gw_lo = (gw - gw_hi.astype(F32)).astype(BF16)
    gate_w = jnp.concatenate([gw_hi, gw_lo, gw_hi, jnp.zeros_like(gw_hi)], axis=1)
    gate_b = jnp.concatenate([gla_gate_b_fwd.reshape(GLA_HEADS, 1, GLA_DK),
                              gla_gate_b_bwd.reshape(GLA_HEADS, 1, GLA_DK)], axis=2)

    dec = jnp.stack([ret_decay_fwd, ret_decay_bwd], axis=1)[:, :, None]
    dec = jnp.pad(jnp.broadcast_to(dec, (RET_HEADS, 2, RET_DV)), ((0, 0), (0, 6), (0, 0)))

    x2 = x.reshape(t, d)
    ret, h3 = _retention(x, norm_mix[None, :], w_in_t, 0, cos, sin, dec, ret_norm[None, :])
    gla = _gla(h3, w_in_t, gla0, gate_w, gate_b, gla_norm[None, :])
    h = h3.reshape(t, d)

    n_e = w_router.shape[1]
    w_r = jnp.pad(w_router, ((0, 0), (0, LANES - n_e)))
    wr_hi = w_r.astype(BF16)
    wr_lo = (w_r - wr_hi.astype(F32)).astype(BF16)
    x1, h2, aff_t = _merge(h, ret.reshape(t, ret_v), gla.reshape(t, gla_v), x2, w_gl,
                           w_branch_ret.astype(BF16), w_branch_gla.astype(BF16), w_out.astype(BF16),
                           norm_ffn[None, :], jnp.concatenate([wr_hi, wr_lo], axis=1), n_e, b,
                           _pick(s, (1024, 512, 256, 128)))

    capacity = EC_CAPACITY_FACTOR * s // n_e
    slot = _route(aff_t, capacity)
    xg, gate = _dispatch(slot, aff_t, h2.reshape(b, s, d), capacity)
    f = w_gate.shape[2]
    y = _ffn(xg.reshape(n_e, b * capacity, d), gate.reshape(n_e, b * capacity, LANES),
             w_gate, w_up, w_down, _pick(f, (256, 128)), _pick(b * capacity, (1024, 512, 256, 128)))
    return _combine(slot, y.reshape(n_e, b, capacity, d), x1.reshape(b, s, d), norm_out,
                    _pick(s, (1024, 512, 256, 128)))


def kernel(x, positions, norm_mix, w_in, ret_decay_fwd, ret_decay_bwd, ret_norm, gla_gate_w_fwd,
           gla_gate_b_fwd, gla_gate_w_bwd, gla_gate_b_bwd, gla_norm, w_branch_ret, w_branch_gla,
           w_out, norm_ffn, w_router, w_gate, w_up, w_down, norm_final):
    depth = norm_mix.shape[0]
    assert depth == 1, "the final RMSNorm is fused into the last layer's combine stage"
    cos, sin = _rope_table(positions, RET_DK)
    return _layer(x, cos, sin, norm_mix[0], w_in[0], ret_decay_fwd[0], ret_decay_bwd[0], ret_norm[0],
                  gla_gate_w_fwd[0], gla_gate_b_fwd[0], gla_gate_w_bwd[0], gla_gate_b_bwd[0], gla_norm[0],
                  w_branch_ret[0], w_branch_gla[0], w_out[0], norm_ffn[0], w_router[0],
                  w_gate[0], w_up[0], w_down[0], norm_final[None, :])
```

```python
import functools

import jax
import jax.numpy as jnp
from jax import lax
from jax.experimental import pallas as pl
from jax.experimental.pallas import tpu as pltpu

F32 = jnp.float32
BF16 = jnp.bfloat16

RET_HEADS = 4
RET_DK = 128
RET_DV = 256
GLA_HEADS = 4
GLA_DK = 128
GLA_DV = 256
GLA_GATE_RANK = 16
GLA_GATE_NORMALIZER = 16.0
CHUNK = 128
EC_CAPACITY_FACTOR = 2
ROPE_THETA = 10000.0
NORM_EPS = 1e-6
LOG2_E = 1.4426950408889634

CHUNK_UNROLL = 16
LANES = 128
VMEM_LIMIT = 56 << 20


def _params(sem, vmem=VMEM_LIMIT):
    return pltpu.CompilerParams(dimension_semantics=sem, vmem_limit_bytes=vmem)


def _resident(shape):
    return pl.BlockSpec(shape, lambda *_: (0,) * len(shape), pipeline_mode=pl.Buffered(1))


def _pick(n, prefs):
    for p in prefs:
        if n % p == 0:
            return p
    return n


def _sigmoid(x):
    return 1.0 / (1.0 + jnp.exp(-x))


def _silu(x):
    half = 0.5 * x
    return half + half * jnp.tanh(half)


def _log_sigmoid(x):
    return jnp.minimum(x, 0.0) - jnp.log(1.0 + jnp.exp(-jnp.abs(x)))


def _rms(x, gain):
    return x * lax.rsqrt(jnp.mean(x * x, axis=-1, keepdims=True) + NORM_EPS) * gain


def _dot(a, b):
    return jnp.dot(a, b, preferred_element_type=F32)


def _dot_nt(a, b):
    return lax.dot_general(a, b, (((1,), (1,)), ((), ())), preferred_element_type=F32)


def _dot_tn(a, b):
    return lax.dot_general(a, b, (((0,), (0,)), ((), ())), preferred_element_type=F32)


def _split2(x):
    hi = x.astype(BF16)
    lo = (x - hi.astype(F32)).astype(BF16)
    return hi, lo


def _rope_table_kernel(pos_a_ref, pos_b_ref, freq_ref, cos_ref, sin_ref):
    n, dk = pos_a_ref.shape[1], freq_ref.shape[1]
    lower = lax.broadcasted_iota(jnp.int32, (n, dk), 1) < dk // 2
    ang = jnp.where(lower, pos_a_ref[0].astype(F32), pos_b_ref[0].astype(F32)) * freq_ref[...]
    cos, sin = jnp.cos(ang), jnp.sin(ang)
    cos_x, sin_x = pltpu.roll(cos, dk // 2, 1), pltpu.roll(sin, dk // 2, 1)
    cos_ref[0, 0:n, :] = jnp.where(lower, cos, cos_x)
    cos_ref[0, n:2 * n, :] = jnp.where(lower, cos_x, cos)
    sin_ref[0, 0:n, :] = jnp.where(lower, -sin, sin_x)
    sin_ref[0, n:2 * n, :] = jnp.where(lower, -sin_x, sin)


def _rope_table(positions, dk):
    b, s = positions.shape
    assert s % 16 == 0
    half = jnp.arange(0, dk, 2, dtype=F32) / dk
    inv_freq = ROPE_THETA ** (-half)
    freq = jnp.concatenate([inv_freq, inv_freq])[None, :]
    out = jax.ShapeDtypeStruct((b, s, dk), F32)
    pos = positions[:, :, None]
    return pl.pallas_call(
        _rope_table_kernel,
        grid=(b,),
        in_specs=[pl.BlockSpec((1, s // 2, 1), lambda i: (i, 0, 0)),
                  pl.BlockSpec((1, s // 2, 1), lambda i: (i, 1, 0)),
                  pl.BlockSpec((1, dk), lambda i: (0, 0))],
        out_specs=[pl.BlockSpec((1, s, dk), lambda i: (i, 0, 0))] * 2,
        out_shape=[out, out],
        compiler_params=_params(("parallel",)),
        name="rope_table",
    )(pos, pos, freq)


def _rope(t, cos, sin_signed):
    return t * cos + pltpu.roll(t, t.shape[-1] // 2, 1) * sin_signed


def _chunk_rows(n):
    return pl.ds(pl.multiple_of(n * CHUNK, CHUNK), CHUNK)


def _row_tile(s_len):
    return _pick(s_len, (512, 256, CHUNK))


def _per_tile(chunk_pattern, s_len):
    return jnp.concatenate([chunk_pattern] * (_row_tile(s_len) // CHUNK), axis=0)


def _for_row_tiles(s_len, body):
    tile = _row_tile(s_len)

    def step(i, carry):
        body(pl.ds(pl.multiple_of(i * tile, tile), tile))
        return carry

    lax.fori_loop(0, s_len // tile, step, 0, unroll=True)


def _retention_kernel(x_ref, mix_gain_ref, wq_ref, wk_ref, wv_ref, wg_ref, cos_ref, sin_ref, dec_ref,
                      gain_ref, o_ref, h_ref, qk_ref, qw_ref, kw_ref, v_ref, gg_ref, a_ref, st_ref):
    c = CHUNK
    s_len, dk, dv = qk_ref.shape[0], qk_ref.shape[1] // 2, v_ref.shape[1]
    n_chunks = s_len // c
    scale = dk ** -0.5

    lg_f = _log_sigmoid(dec_ref[0, 0:1, :])
    lg_b = _log_sigmoid(dec_ref[0, 1:2, :])
    lgf_k, lgb_k = lg_f[:, :dk], lg_b[:, :dk]
    pos = lax.broadcasted_iota(jnp.int32, (c, dk), 0).astype(F32)
    wq_f = _per_tile(jnp.exp((pos + 1.0) * lgf_k), s_len)
    wk_f = _per_tile(jnp.exp((c - 1.0 - pos) * lgf_k), s_len)
    wq_b = _per_tile(jnp.exp((c - pos) * lgb_k), s_len)
    wk_b = _per_tile(jnp.exp(pos * lgb_k), s_len)
    ri = lax.broadcasted_iota(jnp.int32, (c, c), 0)
    ci = lax.broadcasted_iota(jnp.int32, (c, c), 1)
    lower = ri >= ci
    rel = (ri - ci).astype(F32)
    decay_mask = jnp.where(lower,
                           jnp.exp(jnp.where(lower, rel, 0.0) * lg_f[:, :c]),
                           jnp.exp(jnp.where(lower, 0.0, -rel) * lg_b[:, :c]))
    chunk_decay_f = jnp.exp(c * lg_f)
    chunk_decay_b = jnp.exp(c * lg_b)

    @pl.when(pl.program_id(1) == 0)
    def _():
        def norm(rows):
            h_ref[0, rows, :] = _rms(x_ref[0, rows, :], mix_gain_ref[...]).astype(h_ref.dtype)
        _for_row_tiles(s_len, norm)

    w_qk = jnp.concatenate([wq_ref[...], wk_ref[...]], axis=0)
    w_v = wv_ref[...]
    w_g = wg_ref[...]

    def project(rows):
        h = h_ref[0, rows, :]
        qk = _dot_nt(h, w_qk)
        cos, sin = cos_ref[0, rows, :], sin_ref[0, rows, :]
        qr = _rope(qk[:, :dk], cos, sin) * scale
        kr = _rope(qk[:, dk:], cos, sin)
        qk_ref[rows, :] = jnp.concatenate([qr, kr], axis=1).astype(BF16)
        qw_ref[rows, :] = jnp.concatenate([qr * wq_f, qr * wq_b], axis=1).astype(BF16)
        kw_ref[rows, :] = jnp.concatenate([kr * wk_f, kr * wk_b], axis=1).astype(BF16)
        v_ref[rows, :] = _dot_nt(h, w_v).astype(BF16)
        gg_ref[rows, :] = _silu(_dot_nt(h, w_g)) * gain_ref[...]

    _for_row_tiles(s_len, project)

    kv = [_dot_tn(kw_ref[n * c:(n + 1) * c, :], v_ref[n * c:(n + 1) * c, :]) for n in range(n_chunks)]
    s_f = s_b = jnp.zeros((dk, dv), F32)
    for i in range(n_chunks):
        n = n_chunks - 1 - i
        st_ref[i, 0:dk, :] = s_f.astype(BF16)
        st_ref[n, dk:2 * dk, :] = s_b.astype(BF16)
        if i + 1 < n_chunks:
            s_f = chunk_decay_f * s_f + kv[i][0:dk, :]
            s_b = chunk_decay_b * s_b + kv[n][dk:2 * dk, :]

    def scores(n, carry):
        rows = _chunk_rows(n)
        qk = qk_ref[rows, :]
        a_ref[rows, :] = (_dot_nt(qk[:, :dk], qk[:, dk:]) * decay_mask).astype(BF16)
        return carry

    lax.fori_loop(0, n_chunks, scores, 0, unroll=CHUNK_UNROLL)

    def phase2(n, carry):
        rows = _chunk_rows(n)
        o = _dot(a_ref[rows, :], v_ref[rows, :]) + _dot(qw_ref[rows, :], st_ref[n])
        mu = jnp.mean(o, axis=-1, keepdims=True)
        d = o - mu
        var = jnp.mean(d * d, axis=-1, keepdims=True)
        o_ref[0, rows, :] = (d * lax.rsqrt(var + NORM_EPS) * gg_ref[rows, :]).astype(o_ref.dtype)
        return carry

    lax.fori_loop(0, n_chunks, phase2, 0, unroll=CHUNK_UNROLL)


def _retention(x, mix_gain, w_in, col0, cos, sin, dec, gain):
    b, s, d = x.shape
    h, dk, dv = RET_HEADS, RET_DK, RET_DV
    q0 = col0 // dk
    v0 = (col0 + 2 * h * dk) // dv
    n_chunks = s // CHUNK
    return pl.pallas_call(
        _retention_kernel,
        grid=(b, h),
        in_specs=[pl.BlockSpec((1, s, d), lambda i, j: (i, 0, 0)),
                  pl.BlockSpec((1, d), lambda i, j: (0, 0)),
                  pl.BlockSpec((dk, d), lambda i, j: (q0 + j, 0)),
                  pl.BlockSpec((dk, d), lambda i, j: (q0 + h + j, 0)),
                  pl.BlockSpec((dv, d), lambda i, j: (v0 + j, 0)),
                  pl.BlockSpec((dv, d), lambda i, j: (v0 + h + j, 0)),
                  pl.BlockSpec((1, s, dk), lambda i, j: (i, 0, 0)),
                  pl.BlockSpec((1, s, dk), lambda i, j: (i, 0, 0)),
                  pl.BlockSpec((1, 8, dv), lambda i, j: (j, 0, 0)),
                  pl.BlockSpec((1, dv), lambda i, j: (0, j))],
        out_specs=[pl.BlockSpec((1, s, dv), lambda i, j: (i, 0, j)),
                   pl.BlockSpec((1, s, d), lambda i, j: (i, 0, 0))],
        out_shape=[jax.ShapeDtypeStruct((b, s, h * dv), BF16),
                   jax.ShapeDtypeStruct((b, s, d), BF16)],
        scratch_shapes=[pltpu.VMEM((s, 2 * dk), BF16),
                        pltpu.VMEM((s, 2 * dk), BF16),
                        pltpu.VMEM((s, 2 * dk), BF16),
                        pltpu.VMEM((s, dv), BF16),
                        pltpu.VMEM((s, dv), F32),
                        pltpu.VMEM((s, CHUNK), BF16),
                        pltpu.VMEM((n_chunks, 2 * dk, dv), BF16)],
        compiler_params=_params(("parallel", "arbitrary")),
        name="retention",
    )(x, mix_gain, w_in, w_in, w_in, w_in, cos, sin, dec, gain)


GATE_COPY = 2 * GLA_GATE_RANK


def _cumsum_dot(tri2, x):
    hi, lo = _split2(x)
    return _dot(tri2, jnp.concatenate([hi, lo], axis=0))


def _gla_kernel(h_ref, wq_ref, wk_ref, wv_ref, wg_ref, wa_ref, gw_ref, gb_ref, gain_ref, o_ref,
                q_ref, k_ref, v_ref, gg_ref, ga_ref, la_ref, qs_ref, ks_ref, qw_ref, kw_ref, last_ref,
                a_ref, st_ref):
    c = CHUNK
    s_len, dk, dv = q_ref.shape[0], q_ref.shape[1], v_ref.shape[1]
    n_chunks = s_len // c
    scale = dk ** -0.5
    inv_norm = LOG2_E / GLA_GATE_NORMALIZER

    w_qk = jnp.concatenate([wq_ref[...], wk_ref[...]], axis=0)
    w_v = wv_ref[...]
    w_g = wg_ref[...]

    @pl.when(pl.program_id(1) == 0)
    def _():
        w_a = wa_ref[...]

        def gate_inputs(rows):
            x = _dot_nt(h_ref[0, rows, :], w_a)
            x_hi = x.astype(BF16).astype(F32)
            lane = lax.broadcasted_iota(jnp.int32, x.shape, 1)
            ga = jnp.where(lane < GATE_COPY, x_hi,
                           jnp.where(lane < 2 * GATE_COPY, pltpu.roll(x_hi, GATE_COPY, 1),
                                     jnp.where(lane < 3 * GATE_COPY, pltpu.roll(x - x_hi, 2 * GATE_COPY, 1),
                                               0.0)))
            ga_ref[rows, :] = ga.astype(BF16)

        _for_row_tiles(s_len, gate_inputs)

    def project(rows):
        h = h_ref[0, rows, :]
        qk = _dot_nt(h, w_qk)
        q_ref[rows, :] = qk[:, :dk] * scale
        k_ref[rows, :] = qk[:, dk:]
        v_ref[rows, :] = _dot_nt(h, w_v).astype(BF16)
        gg_ref[rows, :] = _silu(_dot_nt(h, w_g)) * gain_ref[...]
        la_ref[rows, :] = _log_sigmoid(_dot(ga_ref[rows, :], gw_ref[0]) + gb_ref[0]) * inv_norm

    _for_row_tiles(s_len, project)

    ri = lax.broadcasted_iota(jnp.int32, (c, c), 0)
    ci = lax.broadcasted_iota(jnp.int32, (c, c), 1)
    lower = ri >= ci
    tri_lower = jnp.where(lower, 1.0, 0.0).astype(BF16)
    tri2_lower = jnp.concatenate([tri_lower, tri_lower], axis=1)

    def cumulate(n, carry):
        rows = _chunk_rows(n)
        la = la_ref[rows, :]
        incl = _cumsum_dot(tri2_lower, la)
        cum_f = incl[:, 0:dk]
        cum_b = incl[c - 1:c, dk:2 * dk] - incl[:, dk:2 * dk] + la[:, dk:2 * dk]
        last_f, last_b = cum_f[c - 1:c, :], cum_b[0:1, :]
        ref_f, ref_b = cum_f[c // 2:c // 2 + 1, :], cum_b[c // 2 - 1:c // 2, :]
        q, k = q_ref[rows, :], k_ref[rows, :]
        q_f = q * jnp.exp2(cum_f - ref_f)
        q_b = q * jnp.exp2(cum_b - ref_b)
        qs_ref[rows, :] = jnp.concatenate([q_f, q_b], axis=1).astype(BF16)
        ks_ref[rows, :] = jnp.concatenate([k * jnp.exp2(ref_f - cum_f), k * jnp.exp2(ref_b - cum_b)],
                                          axis=1).astype(BF16)
        qw_ref[rows, :] = jnp.concatenate([q_f * jnp.exp2(ref_f), q_b * jnp.exp2(ref_b)], axis=1).astype(BF16)
        kw_ref[rows, :] = jnp.concatenate([k * jnp.exp2(last_f - cum_f), k * jnp.exp2(last_b - cum_b)],
                                          axis=1).astype(BF16)
        last_ref[n, 0:1, :] = last_f
        last_ref[n, 1:2, :] = last_b
        return carry

    lax.fori_loop(0, n_chunks, cumulate, 0, unroll=CHUNK_UNROLL)

    kv = [_dot_tn(v_ref[n * c:(n + 1) * c, :], kw_ref[n * c:(n + 1) * c, :]) for n in range(n_chunks)]
    s_f = s_b = jnp.zeros((dv, dk), F32)
    for i in range(n_chunks):
        n = n_chunks - 1 - i
        st_ref[i, :, 0:dk] = s_f.astype(BF16)
        st_ref[n, :, dk:2 * dk] = s_b.astype(BF16)
        if i + 1 < n_chunks:
            s_f = jnp.exp2(last_ref[i, 0:1, :]) * s_f + kv[i][:, 0:dk]
            s_b = jnp.exp2(last_ref[n, 1:2, :]) * s_b + kv[n][:, dk:2 * dk]

    def scores(n, carry):
        rows = _chunk_rows(n)
        qs, ks = qs_ref[rows, :], ks_ref[rows, :]
        a_ref[rows, :] = jnp.where(lower, _dot_nt(qs[:, :dk], ks[:, :dk]),
                                   _dot_nt(qs[:, dk:], ks[:, dk:])).astype(BF16)
        return carry

    lax.fori_loop(0, n_chunks, scores, 0, unroll=CHUNK_UNROLL)

    def phase2(n, carry):
        rows = _chunk_rows(n)
        o = _dot(a_ref[rows, :], v_ref[rows, :]) + _dot_nt(qw_ref[rows, :], st_ref[n])
        rms = lax.rsqrt(jnp.mean(o * o, axis=-1, keepdims=True) + NORM_EPS)
        o_ref[0, rows, :] = (o * rms * gg_ref[rows, :]).astype(o_ref.dtype)
        return carry

    lax.fori_loop(0, n_chunks, phase2, 0, unroll=CHUNK_UNROLL)


def _gla(h3, w_in, col0, gate_w, gate_b, gain):
    b, s, d = h3.shape
    h, dk, dv = GLA_HEADS, GLA_DK, GLA_DV
    q0 = col0 // dk
    v0 = (col0 + 2 * h * dk) // dv
    a0 = (col0 + 2 * h * dk + 2 * h * dv) // LANES
    n_chunks = s // CHUNK
    return pl.pallas_call(
        _gla_kernel,
        grid=(b, h),
        in_specs=[pl.BlockSpec((1, s, d), lambda i, j: (i, 0, 0)),
                  pl.BlockSpec((dk, d), lambda i, j: (q0 + j, 0)),
                  pl.BlockSpec((dk, d), lambda i, j: (q0 + h + j, 0)),
                  pl.BlockSpec((dv, d), lambda i, j: (v0 + j, 0)),
                  pl.BlockSpec((dv, d), lambda i, j: (v0 + h + j, 0)),
                  pl.BlockSpec((LANES, d), lambda i, j: (a0, 0)),
                  pl.BlockSpec((1, LANES, 2 * dk), lambda i, j: (j, 0, 0)),
                  pl.BlockSpec((1, 1, 2 * dk), lambda i, j: (j, 0, 0)),
                  pl.BlockSpec((1, dv), lambda i, j: (0, j))],
        out_specs=pl.BlockSpec((1, s, dv), lambda i, j: (i, 0, j)),
        out_shape=jax.ShapeDtypeStruct((b, s, h * dv), BF16),
        scratch_shapes=[pltpu.VMEM((s, dk), F32),
                        pltpu.VMEM((s, dk), F32),
                        pltpu.VMEM((s, dv), BF16),
                        pltpu.VMEM((s, dv), F32),
                        pltpu.VMEM((s, LANES), BF16),
                        pltpu.VMEM((s, 2 * dk), F32),
                        pltpu.VMEM((s, 2 * dk), BF16),
                        pltpu.VMEM((s, 2 * dk), BF16),
                        pltpu.VMEM((s, 2 * dk), BF16),
                        pltpu.VMEM((s, 2 * dk), BF16),
                        pltpu.VMEM((n_chunks, 8, dk), F32),
                        pltpu.VMEM((s, CHUNK), BF16),
                        pltpu.VMEM((n_chunks, dv, 2 * dk), BF16)],
        compiler_params=_params(("parallel", "arbitrary")),
        name="gla",
    )(h3, w_in, w_in, w_in, w_in, w_in, gate_w, gate_b, gain)


MERGE_COLS = 256


def _merge_kernel(n_experts, h_ref, ret_ref, gla_ref, x_ref, wgl_ref, wr_ref, wg_ref, wo_ref, gain_ref,
                  wr2_ref, x1_ref, h2_ref, aff_ref):
    d = x_ref.shape[1]
    h, ret, gla = h_ref[...], ret_ref[...], gla_ref[...]
    blocks = []
    for j in range(0, d, MERGE_COLS):
        cols = slice(j, j + MERGE_COLS)
        cols_gla = slice(d + j, d + j + MERGE_COLS)
        m = (_sigmoid(_dot_nt(h, wgl_ref[cols, :])) * _dot(ret, wr_ref[:, cols])
             + _sigmoid(_dot_nt(h, wgl_ref[cols_gla, :])) * _dot(gla, wg_ref[:, cols]))
        blocks.append(m.astype(BF16))
    x1 = x_ref[...] + _dot(jnp.concatenate(blocks, axis=1), wo_ref[...])
    x1_ref[...] = x1
    h2 = _rms(x1, gain_ref[...]).astype(BF16)
    h2_ref[...] = h2
    logits2 = _dot(h2, wr2_ref[...])
    logits = logits2[:, :LANES] + logits2[:, LANES:]
    lane = lax.broadcasted_iota(jnp.int32, logits.shape, 1)
    logits = jnp.where(lane < n_experts, logits, -jnp.inf)
    p = jnp.exp(logits - jnp.max(logits, axis=-1, keepdims=True))
    aff = p / jnp.sum(p, axis=-1, keepdims=True)
    aff_ref[0] = aff.T[0:n_experts, :]


def _merge(h, ret, gla, x2, w_gl, w_ret, w_gla, w_out, gain, w_router2, n_experts, batch, tm):
    t, d = x2.shape
    s = t // batch
    per_b = s // tm
    rows = lambda width: pl.BlockSpec((tm, width), lambda i: (i, 0))
    return pl.pallas_call(
        functools.partial(_merge_kernel, n_experts),
        grid=(t // tm,),
        in_specs=[rows(d), rows(ret.shape[1]), rows(gla.shape[1]), rows(d),
                  _resident(w_gl.shape), _resident(w_ret.shape), _resident(w_gla.shape),
                  _resident(w_out.shape), _resident((1, d)),
                  _resident(w_router2.shape)],
        out_specs=[rows(d), rows(d),
                   pl.BlockSpec((1, n_experts, tm), lambda i: (i // per_b, 0, i % per_b))],
        out_shape=[jax.ShapeDtypeStruct((t, d), F32),
                   jax.ShapeDtypeStruct((t, d), BF16),
                   jax.ShapeDtypeStruct((batch, n_experts, s), F32)],
        compiler_params=_params(("parallel",)),
        name="merge",
    )(h, ret, gla, x2, w_gl, w_ret, w_gla, w_out, gain, w_router2)


def _prefix_count(mask):
    s = mask.shape[1]
    ri = lax.broadcasted_iota(jnp.int32, (LANES, LANES), 0)
    ci = lax.broadcasted_iota(jnp.int32, (LANES, LANES), 1)
    tri = jnp.where(ri <= ci, 1.0, 0.0).astype(BF16)
    off = jnp.zeros((mask.shape[0], 1), F32)
    parts = []
    for j in range(s // LANES):
        p = _dot(mask[:, j * LANES:(j + 1) * LANES].astype(BF16), tri) + off
        parts.append(p)
        off = p[:, LANES - 1:LANES]
    return jnp.concatenate(parts, axis=1)


def _route_kernel(capacity, aff_ref, slot_ref):
    a = aff_ref[0]
    bits = lax.bitcast_convert_type(a, jnp.int32)
    n_e = a.shape[0]
    cap = float(capacity)

    def count(pred):
        return jnp.sum(jnp.where(pred, 1.0, 0.0), axis=1, keepdims=True)

    def search(_, c):
        lo, hi = c
        mid = lo + lax.shift_right_logical(hi - lo, 1)
        ok = count(bits >= mid) >= cap
        return jnp.where(ok, mid, lo), jnp.where(ok, hi, mid)

    lo0 = jnp.zeros((n_e, 1), jnp.int32)
    hi0 = jnp.full((n_e, 1), 0x7F800000, jnp.int32)
    thr_bits, _ = lax.fori_loop(0, 31, search, (lo0, hi0))
    thr0 = jnp.max(jnp.where(bits <= thr_bits, a, -1.0), axis=1, keepdims=True)

    def counts(v):
        return count(a >= v), count(a > v)

    def unsettled(state):
        _, c_ge, c_gt = state
        bad = jnp.where(c_ge < cap, 1.0, jnp.where(c_gt >= cap, 1.0, 0.0))
        return jnp.max(bad, axis=0, keepdims=True)[0, 0] > 0.0

    def step(state):
        v, c_ge, c_gt = state
        below = jnp.max(jnp.where(a < v, a, -1.0), axis=1, keepdims=True)
        above = jnp.min(jnp.where(a > v, a, 2.0), axis=1, keepdims=True)
        v = jnp.where(c_ge < cap, below, jnp.where(c_gt >= cap, above, v))
        return (v,) + counts(v)

    thr, _, n_gt = lax.while_loop(unsettled, step, (thr0,) + counts(thr0))

    gt = a > thr
    eq = a == thr
    need = cap - n_gt
    eq_rank = _prefix_count(jnp.where(eq, 1.0, 0.0))
    sel = jnp.where(gt, 1.0, jnp.where(eq, jnp.where(eq_rank <= need, 1.0, 0.0), 0.0))
    pos = _prefix_count(sel)
    slot_ref[0] = jnp.where(sel > 0.0, pos - 1.0, -1.0)


def _route(aff_t, capacity):
    b, e, s = aff_t.shape
    return pl.pallas_call(
        functools.partial(_route_kernel, capacity),
        grid=(1,),
        in_specs=[pl.BlockSpec((1, b * e, s), lambda i: (0, 0, 0))],
        out_specs=pl.BlockSpec((1, b * e, s), lambda i: (0, 0, 0)),
        out_shape=jax.ShapeDtypeStruct((1, b * e, s), F32),
        compiler_params=_params(("arbitrary",)),
        name="route",
    )(aff_t.reshape(1, b * e, s)).reshape(b, e, s)


def _dispatch_kernel(slot_ref, aff_ref, h_ref, xg_ref, gate_ref):
    group, cap = xg_ref.shape[0], xg_ref.shape[2]
    h = h_ref[0]
    row = lax.broadcasted_iota(jnp.int32, (cap, h.shape[0]), 0).astype(F32)
    for g in range(group):
        hit = row == slot_ref[0, g:g + 1, :]
        xg_ref[g, 0] = _dot(jnp.where(hit, 1.0, 0.0).astype(BF16), h).astype(xg_ref.dtype)
        gate = jnp.sum(jnp.where(hit, aff_ref[0, g:g + 1, :], 0.0), axis=1, keepdims=True)
        gate_ref[g, 0] = jnp.broadcast_to(gate, gate_ref.shape[2:])


def _dispatch(slot, aff_t, h2, capacity):
    b, e, s = slot.shape
    d = h2.shape[-1]
    group = _pick(e, (8,))
    return pl.pallas_call(
        _dispatch_kernel,
        grid=(b, e // group),
        in_specs=[pl.BlockSpec((1, group, s), lambda i, j: (i, j, 0)),
                  pl.BlockSpec((1, group, s), lambda i, j: (i, j, 0)),
                  pl.BlockSpec((1, s, d), lambda i, j: (i, 0, 0))],
        out_specs=[pl.BlockSpec((group, 1, capacity, d), lambda i, j: (j, i, 0, 0)),
                   pl.BlockSpec((group, 1, capacity, LANES), lambda i, j: (j, i, 0, 0))],
        out_shape=[jax.ShapeDtypeStruct((e, b, capacity, d), BF16),
                   jax.ShapeDtypeStruct((e, b, capacity, LANES), F32)],
        compiler_params=_params(("parallel", "parallel")),
        name="dispatch",
    )(slot, aff_t, h2)


FFN_TILE = 512


def _ffn_kernel(tm, tf, x_ref, gate_ref, wg_hbm, wu_hbm, wd_hbm, y_ref, acc_ref, wg_buf, wu_buf, wd_buf, sems):
    expert, n_experts = pl.program_id(0), pl.num_programs(0)
    n_full, rest = divmod(wg_hbm.shape[2], tf)
    n_f = n_full + (1 if rest else 0)
    width = lambda f: tf if f < n_full else rest
    m, d = x_ref.shape[1], x_ref.shape[2]

    def tile_copies(e, f, slot, w):
        cols = pl.ds(pl.multiple_of(f * tf, tf), w)
        return (pltpu.make_async_copy(wg_hbm.at[e, :, cols], wg_buf.at[slot, :, pl.ds(0, w)], sems.at[0, slot]),
                pltpu.make_async_copy(wu_hbm.at[e, :, cols], wu_buf.at[slot, :, pl.ds(0, w)], sems.at[1, slot]),
                pltpu.make_async_copy(wd_hbm.at[e, cols, :], wd_buf.at[slot, pl.ds(0, w), :], sems.at[2, slot]))

    def start(e, f, slot, w):
        for copy in tile_copies(e, f, slot, w):
            copy.start()

    @pl.when(expert == 0)
    def _():
        start(0, 0, 0, width(0))

    def step(f, w, w_next, first, final):
        slot = lax.rem(expert * n_f + f, 2)
        if final:
            @pl.when(expert + 1 < n_experts)
            def _():
                start(expert + 1, 0, 1 - slot, width(0))
        else:
            start(expert, f + 1, 1 - slot, w_next)
        for copy in tile_copies(expert, f, slot, w):
            copy.wait()
        w_gate = wg_buf[slot, :, 0:w].astype(BF16)
        w_up = wu_buf[slot, :, 0:w].astype(BF16)
        w_down = wd_buf[slot, 0:w, :].astype(BF16)
        for i in range(m // tm):
            rows = pl.ds(i * tm, tm)
            x = x_ref[0, rows, :]
            act = (_silu(_dot(x, w_gate)) * _dot(x, w_up)).astype(BF16)
            part = _dot(act, w_down)
            if not first:
                part = acc_ref[rows, :] + part
            if final:
                gate = jnp.concatenate([gate_ref[0, rows, :]] * (d // LANES), axis=1)
                y_ref[0, rows, :] = (part * gate).astype(y_ref.dtype)
            else:
                acc_ref[rows, :] = part

    uniform = [f for f in range(1, n_f - 1) if width(f) == tf and width(f + 1) == tf]
    f = 0
    while f < n_f:
        if uniform and f == uniform[0]:
            def middle(i, carry):
                step(i, tf, tf, False, False)
                return carry
            lax.fori_loop(uniform[0], uniform[-1] + 1, middle, 0)
            f = uniform[-1] + 1
        else:
            step(f, width(f), width(f + 1) if f + 1 < n_f else 0, f == 0, f == n_f - 1)
            f += 1


def _ffn(xg, gate, w_gate, w_up, w_down, tf, tm):
    e, m, d = xg.shape
    assert tf % LANES == 0 and w_gate.shape[2] % LANES == 0
    return pl.pallas_call(
        functools.partial(_ffn_kernel, tm, tf),
        grid=(e,),
        in_specs=[pl.BlockSpec((1, m, d), lambda i: (i, 0, 0)),
                  pl.BlockSpec((1, m, LANES), lambda i: (i, 0, 0)),
                  pl.BlockSpec(memory_space=pl.ANY),
                  pl.BlockSpec(memory_space=pl.ANY),
                  pl.BlockSpec(memory_space=pl.ANY)],
        out_specs=pl.BlockSpec((1, m, d), lambda i: (i, 0, 0)),
        out_shape=jax.ShapeDtypeStruct((e, m, d), BF16),
        scratch_shapes=[pltpu.VMEM((m, d), F32),
                        pltpu.VMEM((2, d, tf), w_gate.dtype),
                        pltpu.VMEM((2, d, tf), w_up.dtype),
                        pltpu.VMEM((2, tf, d), w_down.dtype),
                        pltpu.SemaphoreType.DMA((3, 2))],
        compiler_params=_params(("arbitrary",)),
        name="ffn",
    )(xg, gate, w_gate, w_up, w_down)


def _combine_kernel(slot_ref, y_ref, x1_ref, gain_ref, o_ref):
    n_e, cap = y_ref.shape[0], y_ref.shape[2]
    tt = x1_ref.shape[1]
    acc = x1_ref[0]
    row = lax.broadcasted_iota(jnp.int32, (cap, tt), 0).astype(F32)
    for e in range(n_e):
        onehot = jnp.where(row == slot_ref[0, e:e + 1, :], 1.0, 0.0).astype(BF16)
        acc = acc + _dot_tn(onehot, y_ref[e, 0])
    o_ref[0] = _rms(acc, gain_ref[...])


def _combine(slot, y, x1, gain, tt):
    b, e, s = slot.shape
    cap, d = y.shape[2], y.shape[3]
    return pl.pallas_call(
        _combine_kernel,
        grid=(b, s // tt),
        in_specs=[pl.BlockSpec((1, e, tt), lambda i, j: (i, 0, j)),
                  pl.BlockSpec((e, 1, cap, d), lambda i, j: (0, i, 0, 0)),
                  pl.BlockSpec((1, tt, d), lambda i, j: (i, j, 0)),
                  pl.BlockSpec((1, d), lambda i, j: (0, 0))],
        out_specs=pl.BlockSpec((1, tt, d), lambda i, j: (i, j, 0)),
        out_shape=jax.ShapeDtypeStruct((b, s, d), F32),
        compiler_params=_params(("parallel", "parallel")),
        name="combine",
    )(slot, y, x1, gain)


def _layer(x, cos, sin, norm_mix, w_in, ret_decay_fwd, ret_decay_bwd, ret_norm,
           gla_gate_w_fwd, gla_gate_b_fwd, gla_gate_w_bwd, gla_gate_b_bwd, gla_norm,
           w_branch_ret, w_branch_gla, w_out, norm_ffn, w_router, w_gate, w_up, w_down, norm_out):
    b, s, d = x.shape
    t = b * s
    ret_qk, ret_v = RET_HEADS * RET_DK, RET_HEADS * RET_DV
    gla_qk, gla_v = GLA_HEADS * GLA_DK, GLA_HEADS * GLA_DV
    rank = GLA_GATE_RANK
    gla0 = 2 * ret_qk + 2 * ret_v
    ga0 = gla0 + 2 * gla_qk + 2 * gla_v
    assert w_in.shape == (d, ga0 + 2 * rank + 2 * d)
    assert s % CHUNK == 0 and 3 * GATE_COPY <= LANES and ga0 % LANES == 0

    w_in_t = w_in.T.astype(BF16)
    w_gl = w_in_t[ga0 + 2 * rank:, :]

    gw = jnp.zeros((GLA_HEADS, GATE_COPY, 2 * GLA_DK), F32)
    gw = gw.at[:, :rank, :GLA_DK].set(gla_gate_w_fwd.reshape(rank, GLA_HEADS, GLA_DK).transpose(1, 0, 2))
    gw = gw.at[:, rank:, GLA_DK:].set(gla_gate_w_bwd.reshape(rank, GLA_HEADS, GLA_DK).transpose(1, 0, 2))
    gw_hi = gw.astype(BF16)
    gw_lo = (gw - gw_hi.astype(F32)).astype(BF16)
    gate_w = jnp.concatenate([gw_hi, gw_lo, gw_hi, jnp.zeros_like(gw_hi)], axis=1)
    gate_b = jnp.concatenate([gla_gate_b_fwd.reshape(GLA_HEADS, 1, GLA_DK),
                              gla_gate_b_bwd.reshape(GLA_HEADS, 1, GLA_DK)], axis=2)

    dec = jnp.stack([ret_decay_fwd, ret_decay_bwd], axis=1)[:, :, None]
    dec = jnp.pad(jnp.broadcast_to(dec, (RET_HEADS, 2, RET_DV)), ((0, 0), (0, 6), (0, 0)))

    x2 = x.reshape(t, d)
    ret, h3 = _retention(x, norm_mix[None, :], w_in_t, 0, cos, sin, dec, ret_norm[None, :])
    gla = _gla(h3, w_in_t, gla0, gate_w, gate_b, gla_norm[None, :])
    h = h3.reshape(t, d)

    n_e = w_router.shape[1]
    w_r = jnp.pad(w_router, ((0, 0), (0, LANES - n_e)))
    wr_hi = w_r.astype(BF16)
    wr_lo = (w_r - wr_hi.astype(F32)).astype(BF16)
    x1, h2, aff_t = _merge(h, ret.reshape(t, ret_v), gla.reshape(t, gla_v), x2, w_gl,
                           w_branch_ret.astype(BF16), w_branch_gla.astype(BF16), w_out.astype(BF16),
                           norm_ffn[None, :], jnp.concatenate([wr_hi, wr_lo], axis=1), n_e, b,
                           _pick(s, (1024, 512, 256, 128)))

    capacity = EC_CAPACITY_FACTOR * s // n_e
    slot = _route(aff_t, capacity)
    xg, gate = _dispatch(slot, aff_t, h2.reshape(b, s, d), capacity)
    f = w_gate.shape[2]
    y = _ffn(xg.reshape(n_e, b * capacity, d), gate.reshape(n_e, b * capacity, LANES),
             w_gate, w_up, w_down, min(FFN_TILE, f), _pick(b * capacity, (1024, 512, 256, 128)))
    return _combine(slot, y.reshape(n_e, b, capacity, d), x1.reshape(b, s, d), norm_out,
                    _pick(s, (1024, 512, 256, 128)))


def kernel(x, positions, norm_mix, w_in, ret_decay_fwd, ret_decay_bwd, ret_norm, gla_gate_w_fwd,
           gla_gate_b_fwd, gla_gate_w_bwd, gla_gate_b_bwd, gla_norm, w_branch_ret, w_branch_gla,
           w_out, norm_ffn, w_router, w_gate, w_up, w_down, norm_final):
    depth = norm_mix.shape[0]
    assert depth == 1, "the final RMSNorm is fused into the last layer's combine stage"
    cos, sin = _rope_table(positions, RET_DK)
    return _layer(x, cos, sin, norm_mix[0], w_in[0], ret_decay_fwd[0], ret_decay_bwd[0], ret_norm[0],
                  gla_gate_w_fwd[0], gla_gate_b_fwd[0], gla_gate_w_bwd[0], gla_gate_b_bwd[0], gla_norm[0],
                  w_branch_ret[0], w_branch_gla[0], w_out[0], norm_ffn[0], w_router[0],
                  w_gate[0], w_up[0], w_down[0], norm_final[None, :])
```

```python
import functools

import jax
import jax.numpy as jnp
from jax import lax
from jax.experimental import pallas as pl
from jax.experimental.pallas import tpu as pltpu

F32 = jnp.float32
BF16 = jnp.bfloat16

RET_HEADS = 4
RET_DK = 128
RET_DV = 256
GLA_HEADS = 4
GLA_DK = 128
GLA_DV = 256
GLA_GATE_RANK = 16
GLA_GATE_NORMALIZER = 16.0
CHUNK = 128
EC_CAPACITY_FACTOR = 2
ROPE_THETA = 10000.0
NORM_EPS = 1e-6
LOG2_E = 1.4426950408889634

CHUNK_UNROLL = 16
LANES = 128
VMEM_LIMIT = 56 << 20


def _params(sem, vmem=VMEM_LIMIT):
    return pltpu.CompilerParams(dimension_semantics=sem, vmem_limit_bytes=vmem)


def _resident(shape):
    return pl.BlockSpec(shape, lambda *_: (0,) * len(shape), pipeline_mode=pl.Buffered(1))


def _pick(n, prefs):
    for p in prefs:
        if n % p == 0:
            return p
    return n


def _sigmoid(x):
    return 1.0 / (1.0 + jnp.exp(-x))


def _silu(x):
    half = 0.5 * x
    return half + half * jnp.tanh(half)


def _log_sigmoid(x):
    return jnp.minimum(x, 0.0) - jnp.log(1.0 + jnp.exp(-jnp.abs(x)))


def _rms(x, gain):
    return x * lax.rsqrt(jnp.mean(x * x, axis=-1, keepdims=True) + NORM_EPS) * gain


def _dot(a, b):
    return jnp.dot(a, b, preferred_element_type=F32)


def _dot_nt(a, b):
    return lax.dot_general(a, b, (((1,), (1,)), ((), ())), preferred_element_type=F32)


def _dot_tn(a, b):
    return lax.dot_general(a, b, (((0,), (0,)), ((), ())), preferred_element_type=F32)


def _split2(x):
    hi = x.astype(BF16)
    lo = (x - hi.astype(F32)).astype(BF16)
    return hi, lo


def _rope_table_kernel(pos_a_ref, pos_b_ref, freq_ref, cos_ref, sin_ref):
    n, dk = pos_a_ref.shape[1], freq_ref.shape[1]
    lower = lax.broadcasted_iota(jnp.int32, (n, dk), 1) < dk // 2
    ang = jnp.where(lower, pos_a_ref[0].astype(F32), pos_b_ref[0].astype(F32)) * freq_ref[...]
    cos, sin = jnp.cos(ang), jnp.sin(ang)
    cos_x, sin_x = pltpu.roll(cos, dk // 2, 1), pltpu.roll(sin, dk // 2, 1)
    cos_ref[0, 0:n, :] = jnp.where(lower, cos, cos_x)
    cos_ref[0, n:2 * n, :] = jnp.where(lower, cos_x, cos)
    sin_ref[0, 0:n, :] = jnp.where(lower, -sin, sin_x)
    sin_ref[0, n:2 * n, :] = jnp.where(lower, -sin_x, sin)


def _rope_table(positions, dk):
    b, s = positions.shape
    assert s % 16 == 0
    half = jnp.arange(0, dk, 2, dtype=F32) / dk
    inv_freq = ROPE_THETA ** (-half)
    freq = jnp.concatenate([inv_freq, inv_freq])[None, :]
    out = jax.ShapeDtypeStruct((b, s, dk), F32)
    pos = positions[:, :, None]
    return pl.pallas_call(
        _rope_table_kernel,
        grid=(b,),
        in_specs=[pl.BlockSpec((1, s // 2, 1), lambda i: (i, 0, 0)),
                  pl.BlockSpec((1, s // 2, 1), lambda i: (i, 1, 0)),
                  pl.BlockSpec((1, dk), lambda i: (0, 0))],
        out_specs=[pl.BlockSpec((1, s, dk), lambda i: (i, 0, 0))] * 2,
        out_shape=[out, out],
        compiler_params=_params(("parallel",)),
        name="rope_table",
    )(pos, pos, freq)


def _rope(t, cos, sin_signed):
    return t * cos + pltpu.roll(t, t.shape[-1] // 2, 1) * sin_signed


def _chunk_rows(n):
    return pl.ds(pl.multiple_of(n * CHUNK, CHUNK), CHUNK)


def _row_tile(s_len):
    return _pick(s_len, (512, 256, CHUNK))


def _per_tile(chunk_pattern, s_len):
    return jnp.concatenate([chunk_pattern] * (_row_tile(s_len) // CHUNK), axis=0)


def _for_row_tiles(s_len, body):
    tile = _row_tile(s_len)

    def step(i, carry):
        body(pl.ds(pl.multiple_of(i * tile, tile), tile))
        return carry

    lax.fori_loop(0, s_len // tile, step, 0, unroll=True)


def _retention_kernel(x_ref, mix_gain_ref, wq_ref, wk_ref, wv_ref, wg_ref, cos_ref, sin_ref, dec_ref,
                      gain_ref, o_ref, h_ref, qk_ref, qw_ref, kw_ref, v_ref, gg_ref, a_ref, st_ref):
    c = CHUNK
    s_len, dk, dv = qk_ref.shape[0], qk_ref.shape[1] // 2, v_ref.shape[1]
    n_chunks = s_len // c
    scale = dk ** -0.5

    lg_f = _log_sigmoid(dec_ref[0, 0:1, :])
    lg_b = _log_sigmoid(dec_ref[0, 1:2, :])
    lgf_k, lgb_k = lg_f[:, :dk], lg_b[:, :dk]
    pos = lax.broadcasted_iota(jnp.int32, (c, dk), 0).astype(F32)
    wq_f = _per_tile(jnp.exp((pos + 1.0) * lgf_k), s_len)
    wk_f = _per_tile(jnp.exp((c - 1.0 - pos) * lgf_k), s_len)
    wq_b = _per_tile(jnp.exp((c - pos) * lgb_k), s_len)
    wk_b = _per_tile(jnp.exp(pos * lgb_k), s_len)
    ri = lax.broadcasted_iota(jnp.int32, (c, c), 0)
    ci = lax.broadcasted_iota(jnp.int32, (c, c), 1)
    lower = ri >= ci
    rel = (ri - ci).astype(F32)
    decay_mask = jnp.where(lower,
                           jnp.exp(jnp.where(lower, rel, 0.0) * lg_f[:, :c]),
                           jnp.exp(jnp.where(lower, 0.0, -rel) * lg_b[:, :c]))
    chunk_decay_f = jnp.exp(c * lg_f)
    chunk_decay_b = jnp.exp(c * lg_b)

    @pl.when(pl.program_id(1) == 0)
    def _():
        def norm(rows):
            h_ref[0, rows, :] = _rms(x_ref[0, rows, :], mix_gain_ref[...]).astype(h_ref.dtype)
        _for_row_tiles(s_len, norm)

    w_qk = jnp.concatenate([wq_ref[...], wk_ref[...]], axis=0)
    w_v = wv_ref[...]
    w_g = wg_ref[...]

    def project(rows):
        h = h_ref[0, rows, :]
        qk = _dot_nt(h, w_qk)
        cos, sin = cos_ref[0, rows, :], sin_ref[0, rows, :]
        qr = _rope(qk[:, :dk], cos, sin) * scale
        kr = _rope(qk[:, dk:], cos, sin)
        qk_ref[rows, :] = jnp.concatenate([qr, kr], axis=1).astype(BF16)
        qw_ref[rows, :] = jnp.concatenate([qr * wq_f, qr * wq_b], axis=1).astype(BF16)
        kw_ref[rows, :] = jnp.concatenate([kr * wk_f, kr * wk_b], axis=1).astype(BF16)
        v_ref[rows, :] = _dot_nt(h, w_v).astype(BF16)
        gg_ref[rows, :] = _silu(_dot_nt(h, w_g)) * gain_ref[...]

    _for_row_tiles(s_len, project)

    kv = [_dot_tn(kw_ref[n * c:(n + 1) * c, :], v_ref[n * c:(n + 1) * c, :]) for n in range(n_chunks)]
    s_f = s_b = jnp.zeros((dk, dv), F32)
    for i in range(n_chunks):
        n = n_chunks - 1 - i
        st_ref[i, 0:dk, :] = s_f.astype(BF16)
        st_ref[n, dk:2 * dk, :] = s_b.astype(BF16)
        if i + 1 < n_chunks:
            s_f = chunk_decay_f * s_f + kv[i][0:dk, :]
            s_b = chunk_decay_b * s_b + kv[n][dk:2 * dk, :]

    def scores(n, carry):
        rows = _chunk_rows(n)
        qk = qk_ref[rows, :]
        a_ref[rows, :] = (_dot_nt(qk[:, :dk], qk[:, dk:]) * decay_mask).astype(BF16)
        return carry

    lax.fori_loop(0, n_chunks, scores, 0, unroll=CHUNK_UNROLL)

    def phase2(n, carry):
        rows = _chunk_rows(n)
        o = _dot(a_ref[rows, :], v_ref[rows, :]) + _dot(qw_ref[rows, :], st_ref[n])
        mu = jnp.mean(o, axis=-1, keepdims=True)
        d = o - mu
        var = jnp.mean(d * d, axis=-1, keepdims=True)
        o_ref[0, rows, :] = (d * lax.rsqrt(var + NORM_EPS) * gg_ref[rows, :]).astype(o_ref.dtype)
        return carry

    lax.fori_loop(0, n_chunks, phase2, 0, unroll=CHUNK_UNROLL)


def _retention(x, mix_gain, w_in, col0, cos, sin, dec, gain):
    b, s, d = x.shape
    h, dk, dv = RET_HEADS, RET_DK, RET_DV
    q0 = col0 // dk
    v0 = (col0 + 2 * h * dk) // dv
    n_chunks = s // CHUNK
    return pl.pallas_call(
        _retention_kernel,
        grid=(b, h),
        in_specs=[pl.BlockSpec((1, s, d), lambda i, j: (i, 0, 0)),
                  pl.BlockSpec((1, d), lambda i, j: (0, 0)),
                  pl.BlockSpec((dk, d), lambda i, j: (q0 + j, 0)),
                  pl.BlockSpec((dk, d), lambda i, j: (q0 + h + j, 0)),
                  pl.BlockSpec((dv, d), lambda i, j: (v0 + j, 0)),
                  pl.BlockSpec((dv, d), lambda i, j: (v0 + h + j, 0)),
                  pl.BlockSpec((1, s, dk), lambda i, j: (i, 0, 0)),
                  pl.BlockSpec((1, s, dk), lambda i, j: (i, 0, 0)),
                  pl.BlockSpec((1, 8, dv), lambda i, j: (j, 0, 0)),
                  pl.BlockSpec((1, dv), lambda i, j: (0, j))],
        out_specs=[pl.BlockSpec((1, s, dv), lambda i, j: (i, 0, j)),
                   pl.BlockSpec((1, s, d), lambda i, j: (i, 0, 0))],
        out_shape=[jax.ShapeDtypeStruct((b, s, h * dv), BF16),
                   jax.ShapeDtypeStruct((b, s, d), BF16)],
        scratch_shapes=[pltpu.VMEM((s, 2 * dk), BF16),
                        pltpu.VMEM((s, 2 * dk), BF16),
                        pltpu.VMEM((s, 2 * dk), BF16),
                        pltpu.VMEM((s, dv), BF16),
                        pltpu.VMEM((s, dv), F32),
                        pltpu.VMEM((s, CHUNK), BF16),
                        pltpu.VMEM((n_chunks, 2 * dk, dv), BF16)],
        compiler_params=_params(("parallel", "arbitrary")),
        name="retention",
    )(x, mix_gain, w_in, w_in, w_in, w_in, cos, sin, dec, gain)


GATE_COPY = 2 * GLA_GATE_RANK


def _cumsum_dot(tri2, x):
    hi, lo = _split2(x)
    return _dot(tri2, jnp.concatenate([hi, lo], axis=0))


def _gla_kernel(h_ref, wq_ref, wk_ref, wv_ref, wg_ref, wa_ref, gw_ref, gb_ref, gain_ref, o_ref,
                q_ref, k_ref, v_ref, gg_ref, ga_ref, la_ref, qs_ref, ks_ref, qw_ref, kw_ref, last_ref,
                a_ref, st_ref):
    c = CHUNK
    s_len, dk, dv = q_ref.shape[0], q_ref.shape[1], v_ref.shape[1]
    n_chunks = s_len // c
    scale = dk ** -0.5
    inv_norm = LOG2_E / GLA_GATE_NORMALIZER

    w_qk = jnp.concatenate([wq_ref[...], wk_ref[...]], axis=0)
    w_v = wv_ref[...]
    w_g = wg_ref[...]

    @pl.when(pl.program_id(1) == 0)
    def _():
        w_a = wa_ref[...]

        def gate_inputs(rows):
            x = _dot_nt(h_ref[0, rows, :], w_a)
            x_hi = x.astype(BF16).astype(F32)
            lane = lax.broadcasted_iota(jnp.int32, x.shape, 1)
            ga = jnp.where(lane < GATE_COPY, x_hi,
                           jnp.where(lane < 2 * GATE_COPY, pltpu.roll(x_hi, GATE_COPY, 1),
                                     jnp.where(lane < 3 * GATE_COPY, pltpu.roll(x - x_hi, 2 * GATE_COPY, 1),
                                               0.0)))
            ga_ref[rows, :] = ga.astype(BF16)

        _for_row_tiles(s_len, gate_inputs)

    def project(rows):
        h = h_ref[0, rows, :]
        qk = _dot_nt(h, w_qk)
        q_ref[rows, :] = qk[:, :dk] * scale
        k_ref[rows, :] = qk[:, dk:]
        v_ref[rows, :] = _dot_nt(h, w_v).astype(BF16)
        gg_ref[rows, :] = _silu(_dot_nt(h, w_g)) * gain_ref[...]
        la_ref[rows, :] = _log_sigmoid(_dot(ga_ref[rows, :], gw_ref[0]) + gb_ref[0]) * inv_norm

    _for_row_tiles(s_len, project)

    ri = lax.broadcasted_iota(jnp.int32, (c, c), 0)
    ci = lax.broadcasted_iota(jnp.int32, (c, c), 1)
    lower = ri >= ci
    tri_lower = jnp.where(lower, 1.0, 0.0).astype(BF16)
    tri2_lower = jnp.concatenate([tri_lower, tri_lower], axis=1)

    def cumulate(n, carry):
        rows = _chunk_rows(n)
        la = la_ref[rows, :]
        incl = _cumsum_dot(tri2_lower, la)
        cum_f = incl[:, 0:dk]
        cum_b = incl[c - 1:c, dk:2 * dk] - incl[:, dk:2 * dk] + la[:, dk:2 * dk]
        last_f, last_b = cum_f[c - 1:c, :], cum_b[0:1, :]
        ref_f, ref_b = cum_f[c // 2:c // 2 + 1, :], cum_b[c // 2 - 1:c // 2, :]
        q, k = q_ref[rows, :], k_ref[rows, :]
        q_f = q * jnp.exp2(cum_f - ref_f)
        q_b = q * jnp.exp2(cum_b - ref_b)
        qs_ref[rows, :] = jnp.concatenate([q_f, q_b], axis=1).astype(BF16)
        ks_ref[rows, :] = jnp.concatenate([k * jnp.exp2(ref_f - cum_f), k * jnp.exp2(ref_b - cum_b)],
                                          axis=1).astype(BF16)
        qw_ref[rows, :] = jnp.concatenate([q_f * jnp.exp2(ref_f), q_b * jnp.exp2(ref_b)], axis=1).astype(BF16)
        kw_ref[rows, :] = jnp.concatenate([k * jnp.exp2(last_f - cum_f), k * jnp.exp2(last_b - cum_b)],
                                          axis=1).astype(BF16)
        last_ref[n, 0:1, :] = last_f
        last_ref[n, 1:2, :] = last_b
        return carry

    lax.fori_loop(0, n_chunks, cumulate, 0, unroll=CHUNK_UNROLL)

    kv = [_dot_tn(v_ref[n * c:(n + 1) * c, :], kw_ref[n * c:(n + 1) * c, :]) for n in range(n_chunks)]
    s_f = s_b = jnp.zeros((dv, dk), F32)
    for i in range(n_chunks):
        n = n_chunks - 1 - i
        st_ref[i, :, 0:dk] = s_f.astype(BF16)
        st_ref[n, :, dk:2 * dk] = s_b.astype(BF16)
        if i + 1 < n_chunks:
            s_f = jnp.exp2(last_ref[i, 0:1, :]) * s_f + kv[i][:, 0:dk]
            s_b = jnp.exp2(last_ref[n, 1:2, :]) * s_b + kv[n][:, dk:2 * dk]

    def scores(n, carry):
        rows = _chunk_rows(n)
        qs, ks = qs_ref[rows, :], ks_ref[rows, :]
        a_ref[rows, :] = jnp.where(lower, _dot_nt(qs[:, :dk], ks[:, :dk]),
                                   _dot_nt(qs[:, dk:], ks[:, dk:])).astype(BF16)
        return carry

    lax.fori_loop(0, n_chunks, scores, 0, unroll=CHUNK_UNROLL)

    def phase2(n, carry):
        rows = _chunk_rows(n)
        o = _dot(a_ref[rows, :], v_ref[rows, :]) + _dot_nt(qw_ref[rows, :], st_ref[n])
        rms = lax.rsqrt(jnp.mean(o * o, axis=-1, keepdims=True) + NORM_EPS)
        o_ref[0, rows, :] = (o * rms * gg_ref[rows, :]).astype(o_ref.dtype)
        return carry

    lax.fori_loop(0, n_chunks, phase2, 0, unroll=CHUNK_UNROLL)


def _gla(h3, w_in, col0, gate_w, gate_b, gain):
    b, s, d = h3.shape
    h, dk, dv = GLA_HEADS, GLA_DK, GLA_DV
    q0 = col0 // dk
    v0 = (col0 + 2 * h * dk) // dv
    a0 = (col0 + 2 * h * dk + 2 * h * dv) // LANES
    n_chunks = s // CHUNK
    return pl.pallas_call(
        _gla_kernel,
        grid=(b, h),
        in_specs=[pl.BlockSpec((1, s, d), lambda i, j: (i, 0, 0)),
                  pl.BlockSpec((dk, d), lambda i, j: (q0 + j, 0)),
                  pl.BlockSpec((dk, d), lambda i, j: (q0 + h + j, 0)),
                  pl.BlockSpec((dv, d), lambda i, j: (v0 + j, 0)),
                  pl.BlockSpec((dv, d), lambda i, j: (v0 + h + j, 0)),
                  pl.BlockSpec((LANES, d), lambda i, j: (a0, 0)),
                  pl.BlockSpec((1, LANES, 2 * dk), lambda i, j: (j, 0, 0)),
                  pl.BlockSpec((1, 1, 2 * dk), lambda i, j: (j, 0, 0)),
                  pl.BlockSpec((1, dv), lambda i, j: (0, j))],
        out_specs=pl.BlockSpec((1, s, dv), lambda i, j: (i, 0, j)),
        out_shape=jax.ShapeDtypeStruct((b, s, h * dv), BF16),
        scratch_shapes=[pltpu.VMEM((s, dk), F32),
                        pltpu.VMEM((s, dk), F32),
                        pltpu.VMEM((s, dv), BF16),
                        pltpu.VMEM((s, dv), F32),
                        pltpu.VMEM((s, LANES), BF16),
                        pltpu.VMEM((s, 2 * dk), F32),
                        pltpu.VMEM((s, 2 * dk), BF16),
                        pltpu.VMEM((s, 2 * dk), BF16),
                        pltpu.VMEM((s, 2 * dk), BF16),
                        pltpu.VMEM((s, 2 * dk), BF16),
                        pltpu.VMEM((n_chunks, 8, dk), F32),
                        pltpu.VMEM((s, CHUNK), BF16),
                        pltpu.VMEM((n_chunks, dv, 2 * dk), BF16)],
        compiler_params=_params(("parallel", "arbitrary")),
        name="gla",
    )(h3, w_in, w_in, w_in, w_in, w_in, gate_w, gate_b, gain)


MERGE_COLS = 256


def _merge_kernel(n_experts, h_ref, ret_ref, gla_ref, x_ref, wgl_ref, wr_ref, wg_ref, wo_ref, gain_ref,
                  wr2_ref, x1_ref, h2_ref, aff_ref):
    d = x_ref.shape[1]
    h, ret, gla = h_ref[...], ret_ref[...], gla_ref[...]
    blocks = []
    for j in range(0, d, MERGE_COLS):
        cols = slice(j, j + MERGE_COLS)
        cols_gla = slice(d + j, d + j + MERGE_COLS)
        m = (_sigmoid(_dot_nt(h, wgl_ref[cols, :])) * _dot(ret, wr_ref[:, cols])
             + _sigmoid(_dot_nt(h, wgl_ref[cols_gla, :])) * _dot(gla, wg_ref[:, cols]))
        blocks.append(m.astype(BF16))
    x1 = x_ref[...] + _dot(jnp.concatenate(blocks, axis=1), wo_ref[...])
    x1_ref[...] = x1
    h2 = _rms(x1, gain_ref[...]).astype(BF16)
    h2_ref[...] = h2
    logits2 = _dot(h2, wr2_ref[...])
    logits = logits2[:, :LANES] + logits2[:, LANES:]
    lane = lax.broadcasted_iota(jnp.int32, logits.shape, 1)
    logits = jnp.where(lane < n_experts, logits, -jnp.inf)
    p = jnp.exp(logits - jnp.max(logits, axis=-1, keepdims=True))
    aff = p / jnp.sum(p, axis=-1, keepdims=True)
    aff_ref[0] = aff.T[0:n_experts, :]


def _merge(h, ret, gla, x2, w_gl, w_ret, w_gla, w_out, gain, w_router2, n_experts, batch, tm):
    t, d = x2.shape
    s = t // batch
    per_b = s // tm
    rows = lambda width: pl.BlockSpec((tm, width), lambda i: (i, 0))
    return pl.pallas_call(
        functools.partial(_merge_kernel, n_experts),
        grid=(t // tm,),
        in_specs=[rows(d), rows(ret.shape[1]), rows(gla.shape[1]), rows(d),
                  _resident(w_gl.shape), _resident(w_ret.shape), _resident(w_gla.shape),
                  _resident(w_out.shape), _resident((1, d)),
                  _resident(w_router2.shape)],
        out_specs=[rows(d), rows(d),
                   pl.BlockSpec((1, n_experts, tm), lambda i: (i // per_b, 0, i % per_b))],
        out_shape=[jax.ShapeDtypeStruct((t, d), F32),
                   jax.ShapeDtypeStruct((t, d), BF16),
                   jax.ShapeDtypeStruct((batch, n_experts, s), F32)],
        compiler_params=_params(("parallel",)),
        name="merge",
    )(h, ret, gla, x2, w_gl, w_ret, w_gla, w_out, gain, w_router2)


def _prefix_count(mask):
    s = mask.shape[1]
    ri = lax.broadcasted_iota(jnp.int32, (LANES, LANES), 0)
    ci = lax.broadcasted_iota(jnp.int32, (LANES, LANES), 1)
    tri = jnp.where(ri <= ci, 1.0, 0.0).astype(BF16)
    off = jnp.zeros((mask.shape[0], 1), F32)
    parts = []
    for j in range(s // LANES):
        p = _dot(mask[:, j * LANES:(j + 1) * LANES].astype(BF16), tri) + off
        parts.append(p)
        off = p[:, LANES - 1:LANES]
    return jnp.concatenate(parts, axis=1)


def _route_kernel(capacity, aff_ref, slot_ref):
    a = aff_ref[0]
    bits = lax.bitcast_convert_type(a, jnp.int32)
    n_e = a.shape[0]
    cap = float(capacity)

    def count(pred):
        return jnp.sum(jnp.where(pred, 1.0, 0.0), axis=1, keepdims=True)

    def search(_, c):
        lo, hi = c
        mid = lo + lax.shift_right_logical(hi - lo, 1)
        ok = count(bits >= mid) >= cap
        return jnp.where(ok, mid, lo), jnp.where(ok, hi, mid)

    lo0 = jnp.zeros((n_e, 1), jnp.int32)
    hi0 = jnp.full((n_e, 1), 0x7F800000, jnp.int32)
    thr_bits, _ = lax.fori_loop(0, 31, search, (lo0, hi0))
    thr0 = jnp.max(jnp.where(bits <= thr_bits, a, -1.0), axis=1, keepdims=True)

    def counts(v):
        return count(a >= v), count(a > v)

    def unsettled(state):
        _, c_ge, c_gt = state
        bad = jnp.where(c_ge < cap, 1.0, jnp.where(c_gt >= cap, 1.0, 0.0))
        return jnp.max(bad, axis=0, keepdims=True)[0, 0] > 0.0

    def step(state):
        v, c_ge, c_gt = state
        below = jnp.max(jnp.where(a < v, a, -1.0), axis=1, keepdims=True)
        above = jnp.min(jnp.where(a > v, a, 2.0), axis=1, keepdims=True)
        v = jnp.where(c_ge < cap, below, jnp.where(c_gt >= cap, above, v))
        return (v,) + counts(v)

    thr, _, n_gt = lax.while_loop(unsettled, step, (thr0,) + counts(thr0))

    gt = a > thr
    eq = a == thr
    need = cap - n_gt
    eq_rank = _prefix_count(jnp.where(eq, 1.0, 0.0))
    sel = jnp.where(gt, 1.0, jnp.where(eq, jnp.where(eq_rank <= need, 1.0, 0.0), 0.0))
    pos = _prefix_count(sel)
    slot_ref[0] = jnp.where(sel > 0.0, pos - 1.0, -1.0)


def _route(aff_t, capacity):
    b, e, s = aff_t.shape
    return pl.pallas_call(
        functools.partial(_route_kernel, capacity),
        grid=(1,),
        in_specs=[pl.BlockSpec((1, b * e, s), lambda i: (0, 0, 0))],
        out_specs=pl.BlockSpec((1, b * e, s), lambda i: (0, 0, 0)),
        out_shape=jax.ShapeDtypeStruct((1, b * e, s), F32),
        compiler_params=_params(("arbitrary",)),
        name="route",
    )(aff_t.reshape(1, b * e, s)).reshape(b, e, s)


def _dispatch_kernel(slot_ref, aff_ref, h_ref, xg_ref, gate_ref):
    group, cap = xg_ref.shape[0], xg_ref.shape[2]
    h = h_ref[0]
    row = lax.broadcasted_iota(jnp.int32, (cap, h.shape[0]), 0).astype(F32)
    for g in range(group):
        hit = row == slot_ref[0, g:g + 1, :]
        xg_ref[g, 0] = _dot(jnp.where(hit, 1.0, 0.0).astype(BF16), h).astype(xg_ref.dtype)
        gate = jnp.sum(jnp.where(hit, aff_ref[0, g:g + 1, :], 0.0), axis=1, keepdims=True)
        gate_ref[g, 0] = jnp.broadcast_to(gate, gate_ref.shape[2:])


def _dispatch(slot, aff_t, h2, capacity):
    b, e, s = slot.shape
    d = h2.shape[-1]
    group = _pick(e, (8,))
    return pl.pallas_call(
        _dispatch_kernel,
        grid=(b, e // group),
        in_specs=[pl.BlockSpec((1, group, s), lambda i, j: (i, j, 0)),
                  pl.BlockSpec((1, group, s), lambda i, j: (i, j, 0)),
                  pl.BlockSpec((1, s, d), lambda i, j: (i, 0, 0))],
        out_specs=[pl.BlockSpec((group, 1, capacity, d), lambda i, j: (j, i, 0, 0)),
                   pl.BlockSpec((group, 1, capacity, LANES), lambda i, j: (j, i, 0, 0))],
        out_shape=[jax.ShapeDtypeStruct((e, b, capacity, d), BF16),
                   jax.ShapeDtypeStruct((e, b, capacity, LANES), F32)],
        compiler_params=_params(("parallel", "parallel")),
        name="dispatch",
    )(slot, aff_t, h2)


FFN_TILE = 768


def _ffn_kernel(tm, tf, x_ref, gate_ref, wg_hbm, wu_hbm, wd_hbm, y_ref, acc_ref, wg_buf, wu_buf, wd_buf, sems):
    expert, n_experts = pl.program_id(0), pl.num_programs(0)
    n_full, rest = divmod(wg_hbm.shape[2], tf)
    n_f = n_full + (1 if rest else 0)
    width = lambda f: tf if f < n_full else rest
    m, d = x_ref.shape[1], x_ref.shape[2]

    def tile_copies(e, f, slot, w):
        cols = pl.ds(pl.multiple_of(f * tf, tf), w)
        return (pltpu.make_async_copy(wg_hbm.at[e, :, cols], wg_buf.at[slot, :, pl.ds(0, w)], sems.at[0, slot]),
                pltpu.make_async_copy(wu_hbm.at[e, :, cols], wu_buf.at[slot, :, pl.ds(0, w)], sems.at[1, slot]),
                pltpu.make_async_copy(wd_hbm.at[e, cols, :], wd_buf.at[slot, pl.ds(0, w), :], sems.at[2, slot]))

    def start(e, f, slot, w):
        for copy in tile_copies(e, f, slot, w):
            copy.start()

    @pl.when(expert == 0)
    def _():
        start(0, 0, 0, width(0))

    def step(f, w, w_next, first, final):
        slot = lax.rem(expert * n_f + f, 2)
        if final:
            @pl.when(expert + 1 < n_experts)
            def _():
                start(expert + 1, 0, 1 - slot, width(0))
        else:
            start(expert, f + 1, 1 - slot, w_next)
        for copy in tile_copies(expert, f, slot, w):
            copy.wait()
        w_gate = wg_buf[slot, :, 0:w].astype(BF16)
        w_up = wu_buf[slot, :, 0:w].astype(BF16)
        w_down = wd_buf[slot, 0:w, :].astype(BF16)
        for i in range(m // tm):
            rows = pl.ds(i * tm, tm)
            x = x_ref[0, rows, :]
            act = (_silu(_dot(x, w_gate)) * _dot(x, w_up)).astype(BF16)
            part = _dot(act, w_down)
            if not first:
                part = acc_ref[rows, :] + part
            if final:
                gate = jnp.concatenate([gate_ref[0, rows, :]] * (d // LANES), axis=1)
                y_ref[0, rows, :] = (part * gate).astype(y_ref.dtype)
            else:
                acc_ref[rows, :] = part

    uniform = [f for f in range(1, n_f - 1) if width(f) == tf and width(f + 1) == tf]
    f = 0
    while f < n_f:
        if uniform and f == uniform[0]:
            def middle(i, carry):
                step(i, tf, tf, False, False)
                return carry
            lax.fori_loop(uniform[0], uniform[-1] + 1, middle, 0)
            f = uniform[-1] + 1
        else:
            step(f, width(f), width(f + 1) if f + 1 < n_f else 0, f == 0, f == n_f - 1)
            f += 1


def _ffn(xg, gate, w_gate, w_up, w_down, tf, tm):
    e, m, d = xg.shape
    assert tf % LANES == 0 and w_gate.shape[2] % LANES == 0
    return pl.pallas_call(
        functools.partial(_ffn_kernel, tm, tf),
        grid=(e,),
        in_specs=[pl.BlockSpec((1, m, d), lambda i: (i, 0, 0)),
                  pl.BlockSpec((1, m, LANES), lambda i: (i, 0, 0)),
                  pl.BlockSpec(memory_space=pl.ANY),
                  pl.BlockSpec(memory_space=pl.ANY),
                  pl.BlockSpec(memory_space=pl.ANY)],
        out_specs=pl.BlockSpec((1, m, d), lambda i: (i, 0, 0)),
        out_shape=jax.ShapeDtypeStruct((e, m, d), BF16),
        scratch_shapes=[pltpu.VMEM((m, d), F32),
                        pltpu.VMEM((2, d, tf), w_gate.dtype),
                        pltpu.VMEM((2, d, tf), w_up.dtype),
                        pltpu.VMEM((2, tf, d), w_down.dtype),
                        pltpu.SemaphoreType.DMA((3, 2))],
        compiler_params=_params(("arbitrary",)),
        name="ffn",
    )(xg, gate, w_gate, w_up, w_down)


def _combine_kernel(slot_ref, y_ref, x1_ref, gain_ref, o_ref):
    n_e, cap = y_ref.shape[0], y_ref.shape[2]
    tt = x1_ref.shape[1]
    acc = x1_ref[0]
    row = lax.broadcasted_iota(jnp.int32, (cap, tt), 0).astype(F32)
    for e in range(n_e):
        onehot = jnp.where(row == slot_ref[0, e:e + 1, :], 1.0, 0.0).astype(BF16)
        acc = acc + _dot_tn(onehot, y_ref[e, 0])
    o_ref[0] = _rms(acc, gain_ref[...])


def _combine(slot, y, x1, gain, tt):
    b, e, s = slot.shape
    cap, d = y.shape[2], y.shape[3]
    return pl.pallas_call(
        _combine_kernel,
        grid=(b, s // tt),
        in_specs=[pl.BlockSpec((1, e, tt), lambda i, j: (i, 0, j)),
                  pl.BlockSpec((e, 1, cap, d), lambda i, j: (0, i, 0, 0)),
                  pl.BlockSpec((1, tt, d), lambda i, j: (i, j, 0)),
                  pl.BlockSpec((1, d), lambda i, j: (0, 0))],
        out_specs=pl.BlockSpec((1, tt, d), lambda i, j: (i, j, 0)),
        out_shape=jax.ShapeDtypeStruct((b, s, d), F32),
        compiler_params=_params(("parallel", "parallel")),
        name="combine",
    )(slot, y, x1, gain)


def _layer(x, cos, sin, norm_mix, w_in, ret_decay_fwd, ret_decay_bwd, ret_norm,
           gla_gate_w_fwd, gla_gate_b_fwd, gla_gate_w_bwd, gla_gate_b_bwd, gla_norm,
           w_branch_ret, w_branch_gla, w_out, norm_ffn, w_router, w_gate, w_up, w_down, norm_out):
    b, s, d = x.shape
    t = b * s
    ret_qk, ret_v = RET_HEADS * RET_DK, RET_HEADS * RET_DV
    gla_qk, gla_v = GLA_HEADS * GLA_DK, GLA_HEADS * GLA_DV
    rank = GLA_GATE_RANK
    gla0 = 2 * ret_qk + 2 * ret_v
    ga0 = gla0 + 2 * gla_qk + 2 * gla_v
    assert w_in.shape == (d, ga0 + 2 * rank + 2 * d)
    assert s % CHUNK == 0 and 3 * GATE_COPY <= LANES and ga0 % LANES == 0

    w_in_t = w_in.T.astype(BF16)
    w_gl = w_in_t[ga0 + 2 * rank:, :]

    gw = jnp.zeros((GLA_HEADS, GATE_COPY, 2 * GLA_DK), F32)
    gw = gw.at[:, :rank, :GLA_DK].set(gla_gate_w_fwd.reshape(rank, GLA_HEADS, GLA_DK).transpose(1, 0, 2))
    gw = gw.at[:, rank:, GLA_DK:].set(gla_gate_w_bwd.reshape(rank, GLA_HEADS, GLA_DK).transpose(1, 0, 2))
    gw_hi = gw.astype(BF16)
    gw_lo = (gw - gw_hi.astype(F32)).astype(BF16)
    gate_w = jnp.concatenate([gw_hi, gw_lo, gw_hi, jnp.zeros_like(gw_hi)], axis=1)
    gate_b = jnp.concatenate([gla_gate_b_fwd.reshape(GLA_HEADS, 1, GLA_DK),
                              gla_gate_b_bwd.reshape(GLA_HEADS, 1, GLA_DK)], axis=2)

    dec = jnp.stack([ret_decay_fwd, ret_decay_bwd], axis=1)[:, :, None]
    dec = jnp.pad(jnp.broadcast_to(dec, (RET_HEADS, 2, RET_DV)), ((0, 0), (0, 6), (0, 0)))

    x2 = x.reshape(t, d)
    ret, h3 = _retention(x, norm_mix[None, :], w_in_t, 0, cos, sin, dec, ret_norm[None, :])
    gla = _gla(h3, w_in_t, gla0, gate_w, gate_b, gla_norm[None, :])
    h = h3.reshape(t, d)

    n_e = w_router.shape[1]
    w_r = jnp.pad(w_router, ((0, 0), (0, LANES - n_e)))
    wr_hi = w_r.astype(BF16)
    wr_lo = (w_r - wr_hi.astype(F32)).astype(BF16)
    x1, h2, aff_t = _merge(h, ret.reshape(t, ret_v), gla.reshape(t, gla_v), x2, w_gl,
                           w_branch_ret.astype(BF16), w_branch_gla.astype(BF16), w_out.astype(BF16),
                           norm_ffn[None, :], jnp.concatenate([wr_hi, wr_lo], axis=1), n_e, b,
                           _pick(s, (1024, 512, 256, 128)))

    capacity = EC_CAPACITY_FACTOR * s // n_e
    slot = _route(aff_t, capacity)
    xg, gate = _dispatch(slot, aff_t, h2.reshape(b, s, d), capacity)
    f = w_gate.shape[2]
    y = _ffn(xg.reshape(n_e, b * capacity, d), gate.reshape(n_e, b * capacity, LANES),
             w_gate, w_up, w_down, min(FFN_TILE, f), _pick(b * capacity, (1024, 512, 256, 128)))
    return _combine(slot, y.reshape(n_e, b, capacity, d), x1.reshape(b, s, d), norm_out,
                    _pick(s, (1024, 512, 256, 128)))


def kernel(x, positions, norm_mix, w_in, ret_decay_fwd, ret_decay_bwd, ret_norm, gla_gate_w_fwd,
           gla_gate_b_fwd, gla_gate_w_bwd, gla_gate_b_bwd, gla_norm, w_branch_ret, w_branch_gla,
           w_out, norm_ffn, w_router, w_gate, w_up, w_down, norm_final):
    depth = norm_mix.shape[0]
    assert depth == 1, "the final RMSNorm is fused into the last layer's combine stage"
    cos, sin = _rope_table(positions, RET_DK)
    return _layer(x, cos, sin, norm_mix[0], w_in[0], ret_decay_fwd[0], ret_decay_bwd[0], ret_norm[0],
                  gla_gate_w_fwd[0], gla_gate_b_fwd[0], gla_gate_w_bwd[0], gla_gate_b_bwd[0], gla_norm[0],
                  w_branch_ret[0], w_branch_gla[0], w_out[0], norm_ffn[0], w_router[0],
                  w_gate[0], w_up[0], w_down[0], norm_final[None, :])
```

```python
import functools

import jax
import jax.numpy as jnp
from jax import lax
from jax.experimental import pallas as pl
from jax.experimental.pallas import tpu as pltpu

F32 = jnp.float32
BF16 = jnp.bfloat16

RET_HEADS = 4
RET_DK = 128
RET_DV = 256
GLA_HEADS = 4
GLA_DK = 128
GLA_DV = 256
GLA_GATE_RANK = 16
GLA_GATE_NORMALIZER = 16.0
CHUNK = 128
EC_CAPACITY_FACTOR = 2
ROPE_THETA = 10000.0
NORM_EPS = 1e-6
LOG2_E = 1.4426950408889634

CHUNK_UNROLL = 16
LANES = 128
VMEM_LIMIT = 56 << 20


def _params(sem, vmem=VMEM_LIMIT):
    return pltpu.CompilerParams(dimension_semantics=sem, vmem_limit_bytes=vmem)


def _resident(shape):
    return pl.BlockSpec(shape, lambda *_: (0,) * len(shape), pipeline_mode=pl.Buffered(1))


def _pick(n, prefs):
    for p in prefs:
        if n % p == 0:
            return p
    return n


def _sigmoid(x):
    return 1.0 / (1.0 + jnp.exp(-x))


def _silu(x):
    half = 0.5 * x
    return half + half * jnp.tanh(half)


def _log_sigmoid(x):
    return jnp.minimum(x, 0.0) - jnp.log(1.0 + jnp.exp(-jnp.abs(x)))


def _rms(x, gain):
    return x * lax.rsqrt(jnp.mean(x * x, axis=-1, keepdims=True) + NORM_EPS) * gain


def _dot(a, b):
    return jnp.dot(a, b, preferred_element_type=F32)


def _dot_nt(a, b):
    return lax.dot_general(a, b, (((1,), (1,)), ((), ())), preferred_element_type=F32)


def _dot_tn(a, b):
    return lax.dot_general(a, b, (((0,), (0,)), ((), ())), preferred_element_type=F32)


def _split2(x):
    hi = x.astype(BF16)
    lo = (x - hi.astype(F32)).astype(BF16)
    return hi, lo


def _rope_table_kernel(pos_a_ref, pos_b_ref, freq_ref, cos_ref, sin_ref):
    n, dk = pos_a_ref.shape[1], freq_ref.shape[1]
    lower = lax.broadcasted_iota(jnp.int32, (n, dk), 1) < dk // 2
    ang = jnp.where(lower, pos_a_ref[0].astype(F32), pos_b_ref[0].astype(F32)) * freq_ref[...]
    cos, sin = jnp.cos(ang), jnp.sin(ang)
    cos_x, sin_x = pltpu.roll(cos, dk // 2, 1), pltpu.roll(sin, dk // 2, 1)
    cos_ref[0, 0:n, :] = jnp.where(lower, cos, cos_x)
    cos_ref[0, n:2 * n, :] = jnp.where(lower, cos_x, cos)
    sin_ref[0, 0:n, :] = jnp.where(lower, -sin, sin_x)
    sin_ref[0, n:2 * n, :] = jnp.where(lower, -sin_x, sin)


def _rope_table(positions, dk):
    b, s = positions.shape
    assert s % 16 == 0
    half = jnp.arange(0, dk, 2, dtype=F32) / dk
    inv_freq = ROPE_THETA ** (-half)
    freq = jnp.concatenate([inv_freq, inv_freq])[None, :]
    out = jax.ShapeDtypeStruct((b, s, dk), F32)
    pos = positions[:, :, None]
    return pl.pallas_call(
        _rope_table_kernel,
        grid=(b,),
        in_specs=[pl.BlockSpec((1, s // 2, 1), lambda i: (i, 0, 0)),
                  pl.BlockSpec((1, s // 2, 1), lambda i: (i, 1, 0)),
                  pl.BlockSpec((1, dk), lambda i: (0, 0))],
        out_specs=[pl.BlockSpec((1, s, dk), lambda i: (i, 0, 0))] * 2,
        out_shape=[out, out],
        compiler_params=_params(("parallel",)),
        name="rope_table",
    )(pos, pos, freq)


def _rope(t, cos, sin_signed):
    return t * cos + pltpu.roll(t, t.shape[-1] // 2, 1) * sin_signed


def _chunk_rows(n):
    return pl.ds(pl.multiple_of(n * CHUNK, CHUNK), CHUNK)


def _row_tile(s_len):
    return _pick(s_len, (512, 256, CHUNK))


def _per_tile(chunk_pattern, s_len):
    return jnp.concatenate([chunk_pattern] * (_row_tile(s_len) // CHUNK), axis=0)


def _for_row_tiles(s_len, body):
    tile = _row_tile(s_len)

    def step(i, carry):
        body(pl.ds(pl.multiple_of(i * tile, tile), tile))
        return carry

    lax.fori_loop(0, s_len // tile, step, 0, unroll=True)


def _retention_kernel(x_ref, mix_gain_ref, wq_ref, wk_ref, wv_ref, wg_ref, cos_ref, sin_ref, dec_ref,
                      gain_ref, o_ref, h_ref, qk_ref, qw_ref, kw_ref, v_ref, gg_ref, a_ref, st_ref):
    c = CHUNK
    s_len, dk, dv = qk_ref.shape[0], qk_ref.shape[1] // 2, v_ref.shape[1]
    n_chunks = s_len // c
    scale = dk ** -0.5

    lg_f = _log_sigmoid(dec_ref[0, 0:1, :])
    lg_b = _log_sigmoid(dec_ref[0, 1:2, :])
    lgf_k, lgb_k = lg_f[:, :dk], lg_b[:, :dk]
    pos = lax.broadcasted_iota(jnp.int32, (c, dk), 0).astype(F32)
    wq_f = _per_tile(jnp.exp((pos + 1.0) * lgf_k), s_len)
    wk_f = _per_tile(jnp.exp((c - 1.0 - pos) * lgf_k), s_len)
    wq_b = _per_tile(jnp.exp((c - pos) * lgb_k), s_len)
    wk_b = _per_tile(jnp.exp(pos * lgb_k), s_len)
    ri = lax.broadcasted_iota(jnp.int32, (c, c), 0)
    ci = lax.broadcasted_iota(jnp.int32, (c, c), 1)
    lower = ri >= ci
    rel = (ri - ci).astype(F32)
    decay_mask = jnp.where(lower,
                           jnp.exp(jnp.where(lower, rel, 0.0) * lg_f[:, :c]),
                           jnp.exp(jnp.where(lower, 0.0, -rel) * lg_b[:, :c]))
    chunk_decay_f = jnp.exp(c * lg_f)
    chunk_decay_b = jnp.exp(c * lg_b)

    @pl.when(pl.program_id(1) == 0)
    def _():
        def norm(rows):
            h_ref[0, rows, :] = _rms(x_ref[0, rows, :], mix_gain_ref[...]).astype(h_ref.dtype)
        _for_row_tiles(s_len, norm)

    w_qk = jnp.concatenate([wq_ref[...], wk_ref[...]], axis=0)
    w_v = wv_ref[...]
    w_g = wg_ref[...]

    def project(rows):
        h = h_ref[0, rows, :]
        qk = _dot_nt(h, w_qk)
        cos, sin = cos_ref[0, rows, :], sin_ref[0, rows, :]
        qr = _rope(qk[:, :dk], cos, sin) * scale
        kr = _rope(qk[:, dk:], cos, sin)
        qk_ref[rows, :] = jnp.concatenate([qr, kr], axis=1).astype(BF16)
        qw_ref[rows, :] = jnp.concatenate([qr * wq_f, qr * wq_b], axis=1).astype(BF16)
        kw_ref[rows, :] = jnp.concatenate([kr * wk_f, kr * wk_b], axis=1).astype(BF16)
        v_ref[rows, :] = _dot_nt(h, w_v).astype(BF16)
        gg_ref[rows, :] = _silu(_dot_nt(h, w_g)) * gain_ref[...]

    _for_row_tiles(s_len, project)

    kv = [_dot_tn(kw_ref[n * c:(n + 1) * c, :], v_ref[n * c:(n + 1) * c, :]) for n in range(n_chunks)]
    s_f = s_b = jnp.zeros((dk, dv), F32)
    for i in range(n_chunks):
        n = n_chunks - 1 - i
        st_ref[i, 0:dk, :] = s_f.astype(BF16)
        st_ref[n, dk:2 * dk, :] = s_b.astype(BF16)
        if i + 1 < n_chunks:
            s_f = chunk_decay_f * s_f + kv[i][0:dk, :]
            s_b = chunk_decay_b * s_b + kv[n][dk:2 * dk, :]

    def scores(n, carry):
        rows = _chunk_rows(n)
        qk = qk_ref[rows, :]
        a_ref[rows, :] = (_dot_nt(qk[:, :dk], qk[:, dk:]) * decay_mask).astype(BF16)
        return carry

    lax.fori_loop(0, n_chunks, scores, 0, unroll=CHUNK_UNROLL)

    def phase2(n, carry):
        rows = _chunk_rows(n)
        o = _dot(a_ref[rows, :], v_ref[rows, :]) + _dot(qw_ref[rows, :], st_ref[n])
        mu = jnp.mean(o, axis=-1, keepdims=True)
        d = o - mu
        var = jnp.mean(d * d, axis=-1, keepdims=True)
        o_ref[0, rows, :] = (d * lax.rsqrt(var + NORM_EPS) * gg_ref[rows, :]).astype(o_ref.dtype)
        return carry

    lax.fori_loop(0, n_chunks, phase2, 0, unroll=CHUNK_UNROLL)


def _retention(x, mix_gain, w_in, col0, cos, sin, dec, gain):
    b, s, d = x.shape
    h, dk, dv = RET_HEADS, RET_DK, RET_DV
    q0 = col0 // dk
    v0 = (col0 + 2 * h * dk) // dv
    n_chunks = s // CHUNK
    return pl.pallas_call(
        _retention_kernel,
        grid=(b, h),
        in_specs=[pl.BlockSpec((1, s, d), lambda i, j: (i, 0, 0)),
                  pl.BlockSpec((1, d), lambda i, j: (0, 0)),
                  pl.BlockSpec((dk, d), lambda i, j: (q0 + j, 0)),
                  pl.BlockSpec((dk, d), lambda i, j: (q0 + h + j, 0)),
                  pl.BlockSpec((dv, d), lambda i, j: (v0 + j, 0)),
                  pl.BlockSpec((dv, d), lambda i, j: (v0 + h + j, 0)),
                  pl.BlockSpec((1, s, dk), lambda i, j: (i, 0, 0)),
                  pl.BlockSpec((1, s, dk), lambda i, j: (i, 0, 0)),
                  pl.BlockSpec((1, 8, dv), lambda i, j: (j, 0, 0)),
                  pl.BlockSpec((1, dv), lambda i, j: (0, j))],
        out_specs=[pl.BlockSpec((1, s, dv), lambda i, j: (i, 0, j)),
                   pl.BlockSpec((1, s, d), lambda i, j: (i, 0, 0))],
        out_shape=[jax.ShapeDtypeStruct((b, s, h * dv), BF16),
                   jax.ShapeDtypeStruct((b, s, d), BF16)],
        scratch_shapes=[pltpu.VMEM((s, 2 * dk), BF16),
                        pltpu.VMEM((s, 2 * dk), BF16),
                        pltpu.VMEM((s, 2 * dk), BF16),
                        pltpu.VMEM((s, dv), BF16),
                        pltpu.VMEM((s, dv), F32),
                        pltpu.VMEM((s, CHUNK), BF16),
                        pltpu.VMEM((n_chunks, 2 * dk, dv), BF16)],
        compiler_params=_params(("parallel", "arbitrary")),
        name="retention",
    )(x, mix_gain, w_in, w_in, w_in, w_in, cos, sin, dec, gain)


GATE_COPY = 2 * GLA_GATE_RANK


def _cumsum_dot(tri2, x):
    hi, lo = _split2(x)
    return _dot(tri2, jnp.concatenate([hi, lo], axis=0))


def _gla_kernel(h_ref, wq_ref, wk_ref, wv_ref, wg_ref, wa_ref, gw_ref, gb_ref, gain_ref, o_ref,
                q_ref, k_ref, v_ref, gg_ref, ga_ref, la_ref, qs_ref, ks_ref, qw_ref, kw_ref, last_ref,
                a_ref, st_ref):
    c = CHUNK
    s_len, dk, dv = q_ref.shape[0], q_ref.shape[1], v_ref.shape[1]
    n_chunks = s_len // c
    scale = dk ** -0.5
    inv_norm = LOG2_E / GLA_GATE_NORMALIZER

    w_qk = jnp.concatenate([wq_ref[...], wk_ref[...]], axis=0)
    w_v = wv_ref[...]
    w_g = wg_ref[...]

    @pl.when(pl.program_id(1) == 0)
    def _():
        w_a = wa_ref[...]

        def gate_inputs(rows):
            x = _dot_nt(h_ref[0, rows, :], w_a)
            x_hi = x.astype(BF16).astype(F32)
            lane = lax.broadcasted_iota(jnp.int32, x.shape, 1)
            ga = jnp.where(lane < GATE_COPY, x_hi,
                           jnp.where(lane < 2 * GATE_COPY, pltpu.roll(x_hi, GATE_COPY, 1),
                                     jnp.where(lane < 3 * GATE_COPY, pltpu.roll(x - x_hi, 2 * GATE_COPY, 1),
                                               0.0)))
            ga_ref[rows, :] = ga.astype(BF16)

        _for_row_tiles(s_len, gate_inputs)

    def project(rows):
        h = h_ref[0, rows, :]
        qk = _dot_nt(h, w_qk)
        q_ref[rows, :] = qk[:, :dk] * scale
        k_ref[rows, :] = qk[:, dk:]
        v_ref[rows, :] = _dot_nt(h, w_v).astype(BF16)
        gg_ref[rows, :] = _silu(_dot_nt(h, w_g)) * gain_ref[...]
        la_ref[rows, :] = _log_sigmoid(_dot(ga_ref[rows, :], gw_ref[0]) + gb_ref[0]) * inv_norm

    _for_row_tiles(s_len, project)

    ri = lax.broadcasted_iota(jnp.int32, (c, c), 0)
    ci = lax.broadcasted_iota(jnp.int32, (c, c), 1)
    lower = ri >= ci
    tri_lower = jnp.where(lower, 1.0, 0.0).astype(BF16)
    tri2_lower = jnp.concatenate([tri_lower, tri_lower], axis=1)

    def cumulate(n, carry):
        rows = _chunk_rows(n)
        la = la_ref[rows, :]
        incl = _cumsum_dot(tri2_lower, la)
        cum_f = incl[:, 0:dk]
        cum_b = incl[c - 1:c, dk:2 * dk] - incl[:, dk:2 * dk] + la[:, dk:2 * dk]
        last_f, last_b = cum_f[c - 1:c, :], cum_b[0:1, :]
        ref_f, ref_b = cum_f[c // 2:c // 2 + 1, :], cum_b[c // 2 - 1:c // 2, :]
        q, k = q_ref[rows, :], k_ref[rows, :]
        q_f = q * jnp.exp2(cum_f - ref_f)
        q_b = q * jnp.exp2(cum_b - ref_b)
        qs_ref[rows, :] = jnp.concatenate([q_f, q_b], axis=1).astype(BF16)
        ks_ref[rows, :] = jnp.concatenate([k * jnp.exp2(ref_f - cum_f), k * jnp.exp2(ref_b - cum_b)],
                                          axis=1).astype(BF16)
        qw_ref[rows, :] = jnp.concatenate([q_f * jnp.exp2(ref_f), q_b * jnp.exp2(ref_b)], axis=1).astype(BF16)
        kw_ref[rows, :] = jnp.concatenate([k * jnp.exp2(last_f - cum_f), k * jnp.exp2(last_b - cum_b)],
                                          axis=1).astype(BF16)
        last_ref[n, 0:1, :] = last_f
        last_ref[n, 1:2, :] = last_b
        return carry

    lax.fori_loop(0, n_chunks, cumulate, 0, unroll=CHUNK_UNROLL)

    kv = [_dot_tn(v_ref[n * c:(n + 1) * c, :], kw_ref[n * c:(n + 1) * c, :]) for n in range(n_chunks)]
    s_f = s_b = jnp.zeros((dv, dk), F32)
    for i in range(n_chunks):
        n = n_chunks - 1 - i
        st_ref[i, :, 0:dk] = s_f.astype(BF16)
        st_ref[n, :, dk:2 * dk] = s_b.astype(BF16)
        if i + 1 < n_chunks:
            s_f = jnp.exp2(last_ref[i, 0:1, :]) * s_f + kv[i][:, 0:dk]
            s_b = jnp.exp2(last_ref[n, 1:2, :]) * s_b + kv[n][:, dk:2 * dk]

    def scores(n, carry):
        rows = _chunk_rows(n)
        qs, ks = qs_ref[rows, :], ks_ref[rows, :]
        a_ref[rows, :] = jnp.where(lower, _dot_nt(qs[:, :dk], ks[:, :dk]),
                                   _dot_nt(qs[:, dk:], ks[:, dk:])).astype(BF16)
        return carry

    lax.fori_loop(0, n_chunks, scores, 0, unroll=CHUNK_UNROLL)

    def phase2(n, carry):
        rows = _chunk_rows(n)
        o = _dot(a_ref[rows, :], v_ref[rows, :]) + _dot_nt(qw_ref[rows, :], st_ref[n])
        rms = lax.rsqrt(jnp.mean(o * o, axis=-1, keepdims=True) + NORM_EPS)
        o_ref[0, rows, :] = (o * rms * gg_ref[rows, :]).astype(o_ref.dtype)
        return carry

    lax.fori_loop(0, n_chunks, phase2, 0, unroll=CHUNK_UNROLL)


def _gla(h3, w_in, col0, gate_w, gate_b, gain):
    b, s, d = h3.shape
    h, dk, dv = GLA_HEADS, GLA_DK, GLA_DV
    q0 = col0 // dk
    v0 = (col0 + 2 * h * dk) // dv
    a0 = (col0 + 2 * h * dk + 2 * h * dv) // LANES
    n_chunks = s // CHUNK
    return pl.pallas_call(
        _gla_kernel,
        grid=(b, h),
        in_specs=[pl.BlockSpec((1, s, d), lambda i, j: (i, 0, 0)),
                  pl.BlockSpec((dk, d), lambda i, j: (q0 + j, 0)),
                  pl.BlockSpec((dk, d), lambda i, j: (q0 + h + j, 0)),
                  pl.BlockSpec((dv, d), lambda i, j: (v0 + j, 0)),
                  pl.BlockSpec((dv, d), lambda i, j: (v0 + h + j, 0)),
                  pl.BlockSpec((LANES, d), lambda i, j: (a0, 0)),
                  pl.BlockSpec((1, LANES, 2 * dk), lambda i, j: (j, 0, 0)),
                  pl.BlockSpec((1, 1, 2 * dk), lambda i, j: (j, 0, 0)),
                  pl.BlockSpec((1, dv), lambda i, j: (0, j))],
        out_specs=pl.BlockSpec((1, s, dv), lambda i, j: (i, 0, j)),
        out_shape=jax.ShapeDtypeStruct((b, s, h * dv), BF16),
        scratch_shapes=[pltpu.VMEM((s, dk), F32),
                        pltpu.VMEM((s, dk), F32),
                        pltpu.VMEM((s, dv), BF16),
                        pltpu.VMEM((s, dv), F32),
                        pltpu.VMEM((s, LANES), BF16),
                        pltpu.VMEM((s, 2 * dk), F32),
                        pltpu.VMEM((s, 2 * dk), BF16),
                        pltpu.VMEM((s, 2 * dk), BF16),
                        pltpu.VMEM((s, 2 * dk), BF16),
                        pltpu.VMEM((s, 2 * dk), BF16),
                        pltpu.VMEM((n_chunks, 8, dk), F32),
                        pltpu.VMEM((s, CHUNK), BF16),
                        pltpu.VMEM((n_chunks, dv, 2 * dk), BF16)],
        compiler_params=_params(("parallel", "arbitrary")),
        name="gla",
    )(h3, w_in, w_in, w_in, w_in, w_in, gate_w, gate_b, gain)


MERGE_COLS = 256


def _merge_kernel(n_experts, h_ref, ret_ref, gla_ref, x_ref, wgl_ref, wr_ref, wg_ref, wo_ref, gain_ref,
                  wr2_ref, x1_ref, h2_ref, aff_ref):
    d = x_ref.shape[1]
    h, ret, gla = h_ref[...], ret_ref[...], gla_ref[...]
    blocks = []
    for j in range(0, d, MERGE_COLS):
        cols = slice(j, j + MERGE_COLS)
        cols_gla = slice(d + j, d + j + MERGE_COLS)
        m = (_sigmoid(_dot_nt(h, wgl_ref[cols, :])) * _dot(ret, wr_ref[:, cols])
             + _sigmoid(_dot_nt(h, wgl_ref[cols_gla, :])) * _dot(gla, wg_ref[:, cols]))
        blocks.append(m.astype(BF16))
    x1 = x_ref[...] + _dot(jnp.concatenate(blocks, axis=1), wo_ref[...])
    x1_ref[...] = x1
    h2 = _rms(x1, gain_ref[...]).astype(BF16)
    h2_ref[...] = h2
    logits2 = _dot(h2, wr2_ref[...])
    logits = logits2[:, :LANES] + logits2[:, LANES:]
    lane = lax.broadcasted_iota(jnp.int32, logits.shape, 1)
    logits = jnp.where(lane < n_experts, logits, -jnp.inf)
    p = jnp.exp(logits - jnp.max(logits, axis=-1, keepdims=True))
    aff = p / jnp.sum(p, axis=-1, keepdims=True)
    aff_ref[0] = aff.T[0:n_experts, :]


def _merge(h, ret, gla, x2, w_gl, w_ret, w_gla, w_out, gain, w_router2, n_experts, batch, tm):
    t, d = x2.shape
    s = t // batch
    per_b = s // tm
    rows = lambda width: pl.BlockSpec((tm, width), lambda i: (i, 0))
    return pl.pallas_call(
        functools.partial(_merge_kernel, n_experts),
        grid=(t // tm,),
        in_specs=[rows(d), rows(ret.shape[1]), rows(gla.shape[1]), rows(d),
                  _resident(w_gl.shape), _resident(w_ret.shape), _resident(w_gla.shape),
                  _resident(w_out.shape), _resident((1, d)),
                  _resident(w_router2.shape)],
        out_specs=[rows(d), rows(d),
                   pl.BlockSpec((1, n_experts, tm), lambda i: (i // per_b, 0, i % per_b))],
        out_shape=[jax.ShapeDtypeStruct((t, d), F32),
                   jax.ShapeDtypeStruct((t, d), BF16),
                   jax.ShapeDtypeStruct((batch, n_experts, s), F32)],
        compiler_params=_params(("parallel",)),
        name="merge",
    )(h, ret, gla, x2, w_gl, w_ret, w_gla, w_out, gain, w_router2)


def _prefix_count(mask):
    s = mask.shape[1]
    ri = lax.broadcasted_iota(jnp.int32, (LANES, LANES), 0)
    ci = lax.broadcasted_iota(jnp.int32, (LANES, LANES), 1)
    tri = jnp.where(ri <= ci, 1.0, 0.0).astype(BF16)
    off = jnp.zeros((mask.shape[0], 1), F32)
    parts = []
    for j in range(s // LANES):
        p = _dot(mask[:, j * LANES:(j + 1) * LANES].astype(BF16), tri) + off
        parts.append(p)
        off = p[:, LANES - 1:LANES]
    return jnp.concatenate(parts, axis=1)


def _route_kernel(capacity, aff_ref, slot_ref):
    a = aff_ref[0]
    bits = lax.bitcast_convert_type(a, jnp.int32)
    n_e = a.shape[0]
    cap = float(capacity)

    def count(pred):
        return jnp.sum(jnp.where(pred, 1.0, 0.0), axis=1, keepdims=True)

    def search(_, c):
        lo, hi = c
        mid = lo + lax.shift_right_logical(hi - lo, 1)
        ok = count(bits >= mid) >= cap
        return jnp.where(ok, mid, lo), jnp.where(ok, hi, mid)

    lo0 = jnp.zeros((n_e, 1), jnp.int32)
    hi0 = jnp.full((n_e, 1), 0x7F800000, jnp.int32)
    thr_bits, _ = lax.fori_loop(0, 31, search, (lo0, hi0))
    thr0 = jnp.max(jnp.where(bits <= thr_bits, a, -1.0), axis=1, keepdims=True)

    def counts(v):
        return count(a >= v), count(a > v)

    def unsettled(state):
        _, c_ge, c_gt = state
        bad = jnp.where(c_ge < cap, 1.0, jnp.where(c_gt >= cap, 1.0, 0.0))
        return jnp.max(bad, axis=0, keepdims=True)[0, 0] > 0.0

    def step(state):
        v, c_ge, c_gt = state
        below = jnp.max(jnp.where(a < v, a, -1.0), axis=1, keepdims=True)
        above = jnp.min(jnp.where(a > v, a, 2.0), axis=1, keepdims=True)
        v = jnp.where(c_ge < cap, below, jnp.where(c_gt >= cap, above, v))
        return (v,) + counts(v)

    thr, _, n_gt = lax.while_loop(unsettled, step, (thr0,) + counts(thr0))

    gt = a > thr
    eq = a == thr
    need = cap - n_gt
    eq_rank = _prefix_count(jnp.where(eq, 1.0, 0.0))
    sel = jnp.where(gt, 1.0, jnp.where(eq, jnp.where(eq_rank <= need, 1.0, 0.0), 0.0))
    pos = _prefix_count(sel)
    slot_ref[0] = jnp.where(sel > 0.0, pos - 1.0, -1.0)


def _route(aff_t, capacity):
    b, e, s = aff_t.shape
    return pl.pallas_call(
        functools.partial(_route_kernel, capacity),
        grid=(1,),
        in_specs=[pl.BlockSpec((1, b * e, s), lambda i: (0, 0, 0))],
        out_specs=pl.BlockSpec((1, b * e, s), lambda i: (0, 0, 0)),
        out_shape=jax.ShapeDtypeStruct((1, b * e, s), F32),
        compiler_params=_params(("arbitrary",)),
        name="route",
    )(aff_t.reshape(1, b * e, s)).reshape(b, e, s)


def _dispatch_kernel(slot_ref, aff_ref, h_ref, xg_ref, gate_ref):
    group, cap = xg_ref.shape[0], xg_ref.shape[2]
    h = h_ref[0]
    row = lax.broadcasted_iota(jnp.int32, (cap, h.shape[0]), 0).astype(F32)
    for g in range(group):
        hit = row == slot_ref[0, g:g + 1, :]
        xg_ref[g, 0] = _dot(jnp.where(hit, 1.0, 0.0).astype(BF16), h).astype(xg_ref.dtype)
        gate = jnp.sum(jnp.where(hit, aff_ref[0, g:g + 1, :], 0.0), axis=1, keepdims=True)
        gate_ref[g, 0] = jnp.broadcast_to(gate, gate_ref.shape[2:])


def _dispatch(slot, aff_t, h2, capacity):
    b, e, s = slot.shape
    d = h2.shape[-1]
    group = _pick(e, (8,))
    return pl.pallas_call(
        _dispatch_kernel,
        grid=(b, e // group),
        in_specs=[pl.BlockSpec((1, group, s), lambda i, j: (i, j, 0)),
                  pl.BlockSpec((1, group, s), lambda i, j: (i, j, 0)),
                  pl.BlockSpec((1, s, d), lambda i, j: (i, 0, 0))],
        out_specs=[pl.BlockSpec((group, 1, capacity, d), lambda i, j: (j, i, 0, 0)),
                   pl.BlockSpec((group, 1, capacity, LANES), lambda i, j: (j, i, 0, 0))],
        out_shape=[jax.ShapeDtypeStruct((e, b, capacity, d), BF16),
                   jax.ShapeDtypeStruct((e, b, capacity, LANES), F32)],
        compiler_params=_params(("parallel", "parallel")),
        name="dispatch",
    )(slot, aff_t, h2)


FFN_TILE = 768


def _ffn_kernel(tm, tf, x_ref, gate_ref, wg_hbm, wu_hbm, wd_hbm, y_ref, acc_ref, wg_buf, wu_buf, wd_buf, sems):
    expert, n_experts = pl.program_id(0), pl.num_programs(0)
    n_full, rest = divmod(wg_hbm.shape[2], tf)
    n_f = n_full + (1 if rest else 0)
    width = lambda f: tf if f < n_full else rest
    m, d = x_ref.shape[1], x_ref.shape[2]

    def tile_copies(e, f, slot, w):
        cols = pl.ds(pl.multiple_of(f * tf, tf), w)
        return (pltpu.make_async_copy(wg_hbm.at[e, :, cols], wg_buf.at[slot, :, pl.ds(0, w)], sems.at[0, slot]),
                pltpu.make_async_copy(wu_hbm.at[e, :, cols], wu_buf.at[slot, :, pl.ds(0, w)], sems.at[1, slot]),
                pltpu.make_async_copy(wd_hbm.at[e, cols, :], wd_buf.at[slot, pl.ds(0, w), :], sems.at[2, slot]))

    def start(e, f, slot, w):
        for copy in tile_copies(e, f, slot, w):
            copy.start()

    @pl.when(expert == 0)
    def _():
        start(0, 0, 0, width(0))

    def step(f, w, w_next, first, final):
        slot = lax.rem(expert * n_f + f, 2)
        if final:
            @pl.when(expert + 1 < n_experts)
            def _():
                start(expert + 1, 0, 1 - slot, width(0))
        else:
            start(expert, f + 1, 1 - slot, w_next)
        for copy in tile_copies(expert, f, slot, w):
            copy.wait()
        w_gate = wg_buf[slot, :, 0:w].astype(BF16)
        w_up = wu_buf[slot, :, 0:w].astype(BF16)
        w_down = wd_buf[slot, 0:w, :].astype(BF16)
        for i in range(m // tm):
            rows = pl.ds(i * tm, tm)
            x = x_ref[0, rows, :]
            act = (_silu(_dot(x, w_gate)) * _dot(x, w_up)).astype(BF16)
            part = _dot(act, w_down)
            if not first:
                part = acc_ref[rows, :] + part
            if final:
                gate = jnp.concatenate([gate_ref[0, rows, :]] * (d // LANES), axis=1)
                y_ref[0, rows, :] = (part * gate).astype(y_ref.dtype)
            else:
                acc_ref[rows, :] = part

    uniform = [f for f in range(1, n_f - 1) if width(f) == tf and width(f + 1) == tf]
    f = 0
    while f < n_f:
        if uniform and f == uniform[0]:
            def middle(i, carry):
                step(i, tf, tf, False, False)
                return carry
            lax.fori_loop(uniform[0], uniform[-1] + 1, middle, 0)
            f = uniform[-1] + 1
        else:
            step(f, width(f), width(f + 1) if f + 1 < n_f else 0, f == 0, f == n_f - 1)
            f += 1


def _ffn(xg, gate, w_gate, w_up, w_down, tf, tm):
    e, m, d = xg.shape
    assert tf % LANES == 0 and w_gate.shape[2] % LANES == 0
    return pl.pallas_call(
        functools.partial(_ffn_kernel, tm, tf),
        grid=(e,),
        in_specs=[pl.BlockSpec((1, m, d), lambda i: (i, 0, 0)),
                  pl.BlockSpec((1, m, LANES), lambda i: (i, 0, 0)),
                  pl.BlockSpec(memory_space=pl.ANY),
                  pl.BlockSpec(memory_space=pl.ANY),
                  pl.BlockSpec(memory_space=pl.ANY)],
        out_specs=pl.BlockSpec((1, m, d), lambda i: (i, 0, 0)),
        out_shape=jax.ShapeDtypeStruct((e, m, d), BF16),
        scratch_shapes=[pltpu.VMEM((m, d), F32),
                        pltpu.VMEM((2, d, tf), w_gate.dtype),
                        pltpu.VMEM((2, d, tf), w_up.dtype),
                        pltpu.VMEM((2, tf, d), w_down.dtype),
                        pltpu.SemaphoreType.DMA((3, 2))],
        compiler_params=_params(("arbitrary",)),
        name="ffn",
    )(xg, gate, w_gate, w_up, w_down)


COMBINE_GROUP = 8


def _combine_kernel(slot_ref, y_ref, x1_ref, gain_ref, o_ref):
    n_e, cap = y_ref.shape[0], y_ref.shape[2]
    tt = x1_ref.shape[1]
    acc = x1_ref[0]
    row = lax.broadcasted_iota(jnp.int32, (cap, tt), 0).astype(F32)
    group = _pick(n_e, (COMBINE_GROUP,))
    for e0 in range(0, n_e, group):
        onehot = jnp.concatenate([jnp.where(row == slot_ref[0, e:e + 1, :], 1.0, 0.0).astype(BF16)
                                  for e in range(e0, e0 + group)], axis=0)
        acc = acc + _dot_tn(onehot, y_ref[e0:e0 + group, 0].reshape(group * cap, -1))
    o_ref[0] = _rms(acc, gain_ref[...])


def _combine(slot, y, x1, gain, tt):
    b, e, s = slot.shape
    cap, d = y.shape[2], y.shape[3]
    return pl.pallas_call(
        _combine_kernel,
        grid=(b, s // tt),
        in_specs=[pl.BlockSpec((1, e, tt), lambda i, j: (i, 0, j)),
                  pl.BlockSpec((e, 1, cap, d), lambda i, j: (0, i, 0, 0)),
                  pl.BlockSpec((1, tt, d), lambda i, j: (i, j, 0)),
                  pl.BlockSpec((1, d), lambda i, j: (0, 0))],
        out_specs=pl.BlockSpec((1, tt, d), lambda i, j: (i, j, 0)),
        out_shape=jax.ShapeDtypeStruct((b, s, d), F32),
        compiler_params=_params(("parallel", "parallel")),
        name="combine",
    )(slot, y, x1, gain)


def _layer(x, cos, sin, norm_mix, w_in, ret_decay_fwd, ret_decay_bwd, ret_norm,
           gla_gate_w_fwd, gla_gate_b_fwd, gla_gate_w_bwd, gla_gate_b_bwd, gla_norm,
           w_branch_ret, w_branch_gla, w_out, norm_ffn, w_router, w_gate, w_up, w_down, norm_out):
    b, s, d = x.shape
    t = b * s
    ret_qk, ret_v = RET_HEADS * RET_DK, RET_HEADS * RET_DV
    gla_qk, gla_v = GLA_HEADS * GLA_DK, GLA_HEADS * GLA_DV
    rank = GLA_GATE_RANK
    gla0 = 2 * ret_qk + 2 * ret_v
    ga0 = gla0 + 2 * gla_qk + 2 * gla_v
    assert w_in.shape == (d, ga0 + 2 * rank + 2 * d)
    assert s % CHUNK == 0 and 3 * GATE_COPY <= LANES and ga0 % LANES == 0

    w_in_t = w_in.T.astype(BF16)
    w_gl = w_in_t[ga0 + 2 * rank:, :]

    gw = jnp.zeros((GLA_HEADS, GATE_COPY, 2 * GLA_DK), F32)
    gw = gw.at[:, :rank, :GLA_DK].set(gla_gate_w_fwd.reshape(rank, GLA_HEADS, GLA_DK).transpose(1, 0, 2))
    gw = gw.at[:, rank:, GLA_DK:].set(gla_gate_w_bwd.reshape(rank, GLA_HEADS, GLA_DK).transpose(1, 0, 2))
    gw_hi = gw.astype(BF16)
    gw_lo = (gw - gw_hi.astype(F32)).astype(BF16)
    gate_w = jnp.concatenate([gw_hi, gw_lo, gw_hi, jnp.zeros_like(gw_hi)], axis=1)
    gate_b = jnp.concatenate([gla_gate_b_fwd.reshape(GLA_HEADS, 1, GLA_DK),
                              gla_gate_b_bwd.reshape(GLA_HEADS, 1, GLA_DK)], axis=2)

    dec = jnp.stack([ret_decay_fwd, ret_decay_bwd], axis=1)[:, :, None]
    dec = jnp.pad(jnp.broadcast_to(dec, (RET_HEADS, 2, RET_DV)), ((0, 0), (0, 6), (0, 0)))

    x2 = x.reshape(t, d)
    ret, h3 = _retention(x, norm_mix[None, :], w_in_t, 0, cos, sin, dec, ret_norm[None, :])
    gla = _gla(h3, w_in_t, gla0, gate_w, gate_b, gla_norm[None, :])
    h = h3.reshape(t, d)

    n_e = w_router.shape[1]
    w_r = jnp.pad(w_router, ((0, 0), (0, LANES - n_e)))
    wr_hi = w_r.astype(BF16)
    wr_lo = (w_r - wr_hi.astype(F32)).astype(BF16)
    x1, h2, aff_t = _merge(h, ret.reshape(t, ret_v), gla.reshape(t, gla_v), x2, w_gl,
                           w_branch_ret.astype(BF16), w_branch_gla.astype(BF16), w_out.astype(BF16),
                           norm_ffn[None, :], jnp.concatenate([wr_hi, wr_lo], axis=1), n_e, b,
                           _pick(s, (1024, 512, 256, 128)))

    capacity = EC_CAPACITY_FACTOR * s // n_e
    slot = _route(aff_t, capacity)
    xg, gate = _dispatch(slot, aff_t, h2.reshape(b, s, d), capacity)
    f = w_gate.shape[2]
    y = _ffn(xg.reshape(n_e, b * capacity, d), gate.reshape(n_e, b * capacity, LANES),
             w_gate, w_up, w_down, min(FFN_TILE, f), _pick(b * capacity, (1024, 512, 256, 128)))
    return _combine(slot, y.reshape(n_e, b, capacity, d), x1.reshape(b, s, d), norm_out,
                    _pick(s, (1024, 512, 256, 128)))


def kernel(x, positions, norm_mix, w_in, ret_decay_fwd, ret_decay_bwd, ret_norm, gla_gate_w_fwd,
           gla_gate_b_fwd, gla_gate_w_bwd, gla_gate_b_bwd, gla_norm, w_branch_ret, w_branch_gla,
           w_out, norm_ffn, w_router, w_gate, w_up, w_down, norm_final):
    depth = norm_mix.shape[0]
    assert depth == 1, "the final RMSNorm is fused into the last layer's combine stage"
    cos, sin = _rope_table(positions, RET_DK)
    return _layer(x, cos, sin, norm_mix[0], w_in[0], ret_decay_fwd[0], ret_decay_bwd[0], ret_norm[0],
                  gla_gate_w_fwd[0], gla_gate_b_fwd[0], gla_gate_w_bwd[0], gla_gate_b_bwd[0], gla_norm[0],
                  w_branch_ret[0], w_branch_gla[0], w_out[0], norm_ffn[0], w_router[0],
                  w_gate[0], w_up[0], w_down[0], norm_final[None, :])
```

```python
import functools

import jax
import jax.numpy as jnp
from jax import lax
from jax.experimental import pallas as pl
from jax.experimental.pallas import tpu as pltpu

F32 = jnp.float32
BF16 = jnp.bfloat16

RET_HEADS = 4
RET_DK = 128
RET_DV = 256
GLA_HEADS = 4
GLA_DK = 128
GLA_DV = 256
GLA_GATE_RANK = 16
GLA_GATE_NORMALIZER = 16.0
CHUNK = 128
EC_CAPACITY_FACTOR = 2
ROPE_THETA = 10000.0
NORM_EPS = 1e-6
LOG2_E = 1.4426950408889634

CHUNK_UNROLL = 16
LANES = 128
VMEM_LIMIT = 56 << 20


def _params(sem, vmem=VMEM_LIMIT):
    return pltpu.CompilerParams(dimension_semantics=sem, vmem_limit_bytes=vmem)


def _resident(shape):
    return pl.BlockSpec(shape, lambda *_: (0,) * len(shape), pipeline_mode=pl.Buffered(1))


def _pick(n, prefs):
    for p in prefs:
        if n % p == 0:
            return p
    return n


def _sigmoid(x):
    return 1.0 / (1.0 + jnp.exp(-x))


def _silu(x):
    half = 0.5 * x
    return half + half * jnp.tanh(half)


def _log_sigmoid(x):
    return jnp.minimum(x, 0.0) - jnp.log(1.0 + jnp.exp(-jnp.abs(x)))


def _rms(x, gain):
    return x * lax.rsqrt(jnp.mean(x * x, axis=-1, keepdims=True) + NORM_EPS) * gain


def _dot(a, b):
    return jnp.dot(a, b, preferred_element_type=F32)


def _dot_nt(a, b):
    return lax.dot_general(a, b, (((1,), (1,)), ((), ())), preferred_element_type=F32)


def _dot_tn(a, b):
    return lax.dot_general(a, b, (((0,), (0,)), ((), ())), preferred_element_type=F32)


def _split2(x):
    hi = x.astype(BF16)
    lo = (x - hi.astype(F32)).astype(BF16)
    return hi, lo


def _rope_table_kernel(pos_a_ref, pos_b_ref, freq_ref, cos_ref, sin_ref):
    n, dk = pos_a_ref.shape[1], freq_ref.shape[1]
    lower = lax.broadcasted_iota(jnp.int32, (n, dk), 1) < dk // 2
    ang = jnp.where(lower, pos_a_ref[0].astype(F32), pos_b_ref[0].astype(F32)) * freq_ref[...]
    cos, sin = jnp.cos(ang), jnp.sin(ang)
    cos_x, sin_x = pltpu.roll(cos, dk // 2, 1), pltpu.roll(sin, dk // 2, 1)
    cos_ref[0, 0:n, :] = jnp.where(lower, cos, cos_x)
    cos_ref[0, n:2 * n, :] = jnp.where(lower, cos_x, cos)
    sin_ref[0, 0:n, :] = jnp.where(lower, -sin, sin_x)
    sin_ref[0, n:2 * n, :] = jnp.where(lower, -sin_x, sin)


def _rope_table(positions, dk):
    b, s = positions.shape
    assert s % 16 == 0
    half = jnp.arange(0, dk, 2, dtype=F32) / dk
    inv_freq = ROPE_THETA ** (-half)
    freq = jnp.concatenate([inv_freq, inv_freq])[None, :]
    out = jax.ShapeDtypeStruct((b, s, dk), F32)
    pos = positions[:, :, None]
    return pl.pallas_call(
        _rope_table_kernel,
        grid=(b,),
        in_specs=[pl.BlockSpec((1, s // 2, 1), lambda i: (i, 0, 0)),
                  pl.BlockSpec((1, s // 2, 1), lambda i: (i, 1, 0)),
                  pl.BlockSpec((1, dk), lambda i: (0, 0))],
        out_specs=[pl.BlockSpec((1, s, dk), lambda i: (i, 0, 0))] * 2,
        out_shape=[out, out],
        compiler_params=_params(("parallel",)),
        name="rope_table",
    )(pos, pos, freq)


def _rope(t, cos, sin_signed):
    return t * cos + pltpu.roll(t, t.shape[-1] // 2, 1) * sin_signed


def _chunk_rows(n):
    return pl.ds(pl.multiple_of(n * CHUNK, CHUNK), CHUNK)


def _row_tile(s_len):
    return _pick(s_len, (512, 256, CHUNK))


def _per_tile(chunk_pattern, s_len):
    return jnp.concatenate([chunk_pattern] * (_row_tile(s_len) // CHUNK), axis=0)


def _for_row_tiles(s_len, body):
    tile = _row_tile(s_len)

    def step(i, carry):
        body(pl.ds(pl.multiple_of(i * tile, tile), tile))
        return carry

    lax.fori_loop(0, s_len // tile, step, 0, unroll=True)


def _retention_kernel(x_ref, mix_gain_ref, wq_ref, wk_ref, wv_ref, wg_ref, cos_ref, sin_ref, dec_ref,
                      gain_ref, o_ref, h_ref, qk_ref, qw_ref, kw_ref, v_ref, gg_ref, a_ref, st_ref):
    c = CHUNK
    s_len, dk, dv = qk_ref.shape[0], qk_ref.shape[1] // 2, v_ref.shape[1]
    n_chunks = s_len // c
    scale = dk ** -0.5

    lg_f = _log_sigmoid(dec_ref[0, 0:1, :])
    lg_b = _log_sigmoid(dec_ref[0, 1:2, :])
    lgf_k, lgb_k = lg_f[:, :dk], lg_b[:, :dk]
    pos = lax.broadcasted_iota(jnp.int32, (c, dk), 0).astype(F32)
    wq_f = _per_tile(jnp.exp((pos + 1.0) * lgf_k), s_len)
    wk_f = _per_tile(jnp.exp((c - 1.0 - pos) * lgf_k), s_len)
    wq_b = _per_tile(jnp.exp((c - pos) * lgb_k), s_len)
    wk_b = _per_tile(jnp.exp(pos * lgb_k), s_len)
    ri = lax.broadcasted_iota(jnp.int32, (c, c), 0)
    ci = lax.broadcasted_iota(jnp.int32, (c, c), 1)
    lower = ri >= ci
    rel = (ri - ci).astype(F32)
    decay_mask = jnp.where(lower,
                           jnp.exp(jnp.where(lower, rel, 0.0) * lg_f[:, :c]),
                           jnp.exp(jnp.where(lower, 0.0, -rel) * lg_b[:, :c]))
    chunk_decay_f = jnp.exp(c * lg_f)
    chunk_decay_b = jnp.exp(c * lg_b)

    @pl.when(pl.program_id(1) == 0)
    def _():
        def norm(rows):
            h_ref[0, rows, :] = _rms(x_ref[0, rows, :], mix_gain_ref[...]).astype(h_ref.dtype)
        _for_row_tiles(s_len, norm)

    w_qk = jnp.concatenate([wq_ref[...], wk_ref[...]], axis=0)
    w_v = wv_ref[...]
    w_g = wg_ref[...]

    def project(rows):
        h = h_ref[0, rows, :]
        qk = _dot_nt(h, w_qk)
        cos, sin = cos_ref[0, rows, :], sin_ref[0, rows, :]
        qr = _rope(qk[:, :dk], cos, sin) * scale
        kr = _rope(qk[:, dk:], cos, sin)
        qk_ref[rows, :] = jnp.concatenate([qr, kr], axis=1).astype(BF16)
        qw_ref[rows, :] = jnp.concatenate([qr * wq_f, qr * wq_b], axis=1).astype(BF16)
        kw_ref[rows, :] = jnp.concatenate([kr * wk_f, kr * wk_b], axis=1).astype(BF16)
        v_ref[rows, :] = _dot_nt(h, w_v).astype(BF16)
        gg_ref[rows, :] = _silu(_dot_nt(h, w_g)) * gain_ref[...]

    _for_row_tiles(s_len, project)

    kv = [_dot_tn(kw_ref[n * c:(n + 1) * c, :], v_ref[n * c:(n + 1) * c, :]) for n in range(n_chunks)]
    s_f = s_b = jnp.zeros((dk, dv), F32)
    for i in range(n_chunks):
        n = n_chunks - 1 - i
        st_ref[i, 0:dk, :] = s_f.astype(BF16)
        st_ref[n, dk:2 * dk, :] = s_b.astype(BF16)
        if i + 1 < n_chunks:
            s_f = chunk_decay_f * s_f + kv[i][0:dk, :]
            s_b = chunk_decay_b * s_b + kv[n][dk:2 * dk, :]

    def scores(n, carry):
        rows = _chunk_rows(n)
        qk = qk_ref[rows, :]
        a_ref[rows, :] = (_dot_nt(qk[:, :dk], qk[:, dk:]) * decay_mask).astype(BF16)
        return carry

    lax.fori_loop(0, n_chunks, scores, 0, unroll=CHUNK_UNROLL)

    def phase2(n, carry):
        rows = _chunk_rows(n)
        o = _dot(a_ref[rows, :], v_ref[rows, :]) + _dot(qw_ref[rows, :], st_ref[n])
        mu = jnp.mean(o, axis=-1, keepdims=True)
        d = o - mu
        var = jnp.mean(d * d, axis=-1, keepdims=True)
        o_ref[0, rows, :] = (d * lax.rsqrt(var + NORM_EPS) * gg_ref[rows, :]).astype(o_ref.dtype)
        return carry

    lax.fori_loop(0, n_chunks, phase2, 0, unroll=CHUNK_UNROLL)


def _retention(x, mix_gain, w_in, col0, cos, sin, dec, gain):
    b, s, d = x.shape
    h, dk, dv = RET_HEADS, RET_DK, RET_DV
    q0 = col0 // dk
    v0 = (col0 + 2 * h * dk) // dv
    n_chunks = s // CHUNK
    return pl.pallas_call(
        _retention_kernel,
        grid=(b, h),
        in_specs=[pl.BlockSpec((1, s, d), lambda i, j: (i, 0, 0)),
                  pl.BlockSpec((1, d), lambda i, j: (0, 0)),
                  pl.BlockSpec((dk, d), lambda i, j: (q0 + j, 0)),
                  pl.BlockSpec((dk, d), lambda i, j: (q0 + h + j, 0)),
                  pl.BlockSpec((dv, d), lambda i, j: (v0 + j, 0)),
                  pl.BlockSpec((dv, d), lambda i, j: (v0 + h + j, 0)),
                  pl.BlockSpec((1, s, dk), lambda i, j: (i, 0, 0)),
                  pl.BlockSpec((1, s, dk), lambda i, j: (i, 0, 0)),
                  pl.BlockSpec((1, 8, dv), lambda i, j: (j, 0, 0)),
                  pl.BlockSpec((1, dv), lambda i, j: (0, j))],
        out_specs=[pl.BlockSpec((1, s, dv), lambda i, j: (i, 0, j)),
                   pl.BlockSpec((1, s, d), lambda i, j: (i, 0, 0))],
        out_shape=[jax.ShapeDtypeStruct((b, s, h * dv), BF16),
                   jax.ShapeDtypeStruct((b, s, d), BF16)],
        scratch_shapes=[pltpu.VMEM((s, 2 * dk), BF16),
                        pltpu.VMEM((s, 2 * dk), BF16),
                        pltpu.VMEM((s, 2 * dk), BF16),
                        pltpu.VMEM((s, dv), BF16),
                        pltpu.VMEM((s, dv), F32),
                        pltpu.VMEM((s, CHUNK), BF16),
                        pltpu.VMEM((n_chunks, 2 * dk, dv), BF16)],
        compiler_params=_params(("parallel", "arbitrary")),
        name="retention",
    )(x, mix_gain, w_in, w_in, w_in, w_in, cos, sin, dec, gain)


GATE_COPY = 2 * GLA_GATE_RANK
GLA_HEADS_PER_STEP = 2


def _cumsum_dot(tri2, x):
    hi, lo = _split2(x)
    return _dot(tri2, jnp.concatenate([hi, lo], axis=0))


def _gla_kernel(heads, h_ref, wq_ref, wk_ref, wv_ref, wg_ref, wa_ref, gw_ref, gb_ref, gain_ref, o_ref,
                ga_ref, *head_scratch):
    s_len = ga_ref.shape[0]
    dk, dv = wq_ref.shape[0] // heads, wv_ref.shape[0] // heads

    @pl.when(pl.program_id(1) == 0)
    def _():
        w_a = wa_ref[...]

        def gate_inputs(rows):
            x = _dot_nt(h_ref[0, rows, :], w_a)
            x_hi = x.astype(BF16).astype(F32)
            lane = lax.broadcasted_iota(jnp.int32, x.shape, 1)
            ga = jnp.where(lane < GATE_COPY, x_hi,
                           jnp.where(lane < 2 * GATE_COPY, pltpu.roll(x_hi, GATE_COPY, 1),
                                     jnp.where(lane < 3 * GATE_COPY, pltpu.roll(x - x_hi, 2 * GATE_COPY, 1),
                                               0.0)))
            ga_ref[rows, :] = ga.astype(BF16)

        _for_row_tiles(s_len, gate_inputs)

    for k in range(heads):
        _gla_head(h_ref, wq_ref.at[pl.ds(k * dk, dk)], wk_ref.at[pl.ds(k * dk, dk)],
                  wv_ref.at[pl.ds(k * dv, dv)], wg_ref.at[pl.ds(k * dv, dv)],
                  gw_ref.at[pl.ds(k, 1)], gb_ref.at[pl.ds(k, 1)], gain_ref.at[:, pl.ds(k * dv, dv)],
                  o_ref.at[:, :, pl.ds(k * dv, dv)], ga_ref, *[ref.at[k] for ref in head_scratch])


def _gla_head(h_ref, wq_ref, wk_ref, wv_ref, wg_ref, gw_ref, gb_ref, gain_ref, o_ref, ga_ref,
              q_ref, k_ref, v_ref, gg_ref, la_ref, qs_ref, ks_ref, qw_ref, kw_ref, last_ref, a_ref, st_ref):
    c = CHUNK
    s_len, dk, dv = q_ref.shape[0], q_ref.shape[1], v_ref.shape[1]
    n_chunks = s_len // c
    scale = dk ** -0.5
    inv_norm = LOG2_E / GLA_GATE_NORMALIZER

    w_qk = jnp.concatenate([wq_ref[...], wk_ref[...]], axis=0)
    w_v = wv_ref[...]
    w_g = wg_ref[...]

    def project(rows):
        h = h_ref[0, rows, :]
        qk = _dot_nt(h, w_qk)
        q_ref[rows, :] = qk[:, :dk] * scale
        k_ref[rows, :] = qk[:, dk:]
        v_ref[rows, :] = _dot_nt(h, w_v).astype(BF16)
        gg_ref[rows, :] = _silu(_dot_nt(h, w_g)) * gain_ref[...]
        la_ref[rows, :] = _log_sigmoid(_dot(ga_ref[rows, :], gw_ref[0]) + gb_ref[0]) * inv_norm

    _for_row_tiles(s_len, project)

    ri = lax.broadcasted_iota(jnp.int32, (c, c), 0)
    ci = lax.broadcasted_iota(jnp.int32, (c, c), 1)
    lower = ri >= ci
    tri_lower = jnp.where(lower, 1.0, 0.0).astype(BF16)
    tri2_lower = jnp.concatenate([tri_lower, tri_lower], axis=1)

    def cumulate(n, carry):
        rows = _chunk_rows(n)
        la = la_ref[rows, :]
        incl = _cumsum_dot(tri2_lower, la)
        cum_f = incl[:, 0:dk]
        cum_b = incl[c - 1:c, dk:2 * dk] - incl[:, dk:2 * dk] + la[:, dk:2 * dk]
        last_f, last_b = cum_f[c - 1:c, :], cum_b[0:1, :]
        ref_f, ref_b = cum_f[c // 2:c // 2 + 1, :], cum_b[c // 2 - 1:c // 2, :]
        q, k = q_ref[rows, :], k_ref[rows, :]
        q_f = q * jnp.exp2(cum_f - ref_f)
        q_b = q * jnp.exp2(cum_b - ref_b)
        qs_ref[rows, :] = jnp.concatenate([q_f, q_b], axis=1).astype(BF16)
        ks_ref[rows, :] = jnp.concatenate([k * jnp.exp2(ref_f - cum_f), k * jnp.exp2(ref_b - cum_b)],
                                          axis=1).astype(BF16)
        qw_ref[rows, :] = jnp.concatenate([q_f * jnp.exp2(ref_f), q_b * jnp.exp2(ref_b)], axis=1).astype(BF16)
        kw_ref[rows, :] = jnp.concatenate([k * jnp.exp2(last_f - cum_f), k * jnp.exp2(last_b - cum_b)],
                                          axis=1).astype(BF16)
        last_ref[n, 0:1, :] = last_f
        last_ref[n, 1:2, :] = last_b
        return carry

    lax.fori_loop(0, n_chunks, cumulate, 0, unroll=CHUNK_UNROLL)

    kv = [_dot_tn(v_ref[n * c:(n + 1) * c, :], kw_ref[n * c:(n + 1) * c, :]) for n in range(n_chunks)]
    s_f = s_b = jnp.zeros((dv, dk), F32)
    for i in range(n_chunks):
        n = n_chunks - 1 - i
        st_ref[i, :, 0:dk] = s_f.astype(BF16)
        st_ref[n, :, dk:2 * dk] = s_b.astype(BF16)
        if i + 1 < n_chunks:
            s_f = jnp.exp2(last_ref[i, 0:1, :]) * s_f + kv[i][:, 0:dk]
            s_b = jnp.exp2(last_ref[n, 1:2, :]) * s_b + kv[n][:, dk:2 * dk]

    def scores(n, carry):
        rows = _chunk_rows(n)
        qs, ks = qs_ref[rows, :], ks_ref[rows, :]
        a_ref[rows, :] = jnp.where(lower, _dot_nt(qs[:, :dk], ks[:, :dk]),
                                   _dot_nt(qs[:, dk:], ks[:, dk:])).astype(BF16)
        return carry

    lax.fori_loop(0, n_chunks, scores, 0, unroll=CHUNK_UNROLL)

    def phase2(n, carry):
        rows = _chunk_rows(n)
        o = _dot(a_ref[rows, :], v_ref[rows, :]) + _dot_nt(qw_ref[rows, :], st_ref[n])
        rms = lax.rsqrt(jnp.mean(o * o, axis=-1, keepdims=True) + NORM_EPS)
        o_ref[0, rows, :] = (o * rms * gg_ref[rows, :]).astype(o_ref.dtype)
        return carry

    lax.fori_loop(0, n_chunks, phase2, 0, unroll=CHUNK_UNROLL)


def _gla(h3, w_in, col0, gate_w, gate_b, gain):
    b, s, d = h3.shape
    h, dk, dv = GLA_HEADS, GLA_DK, GLA_DV
    g = _pick(h, (GLA_HEADS_PER_STEP,))
    gk, gv = g * dk, g * dv
    assert col0 % gk == 0 and (col0 + 2 * h * dk) % gv == 0
    q0 = col0 // gk
    v0 = (col0 + 2 * h * dk) // gv
    a0 = (col0 + 2 * h * dk + 2 * h * dv) // LANES
    n_chunks = s // CHUNK
    per_head = lambda shape, dtype: pltpu.VMEM((g,) + shape, dtype)
    return pl.pallas_call(
        functools.partial(_gla_kernel, g),
        grid=(b, h // g),
        in_specs=[pl.BlockSpec((1, s, d), lambda i, j: (i, 0, 0)),
                  pl.BlockSpec((gk, d), lambda i, j: (q0 + j, 0)),
                  pl.BlockSpec((gk, d), lambda i, j: (q0 + h // g + j, 0)),
                  pl.BlockSpec((gv, d), lambda i, j: (v0 + j, 0)),
                  pl.BlockSpec((gv, d), lambda i, j: (v0 + h // g + j, 0)),
                  pl.BlockSpec((LANES, d), lambda i, j: (a0, 0)),
                  pl.BlockSpec((g, LANES, 2 * dk), lambda i, j: (j, 0, 0)),
                  pl.BlockSpec((g, 1, 2 * dk), lambda i, j: (j, 0, 0)),
                  pl.BlockSpec((1, gv), lambda i, j: (0, j))],
        out_specs=pl.BlockSpec((1, s, gv), lambda i, j: (i, 0, j)),
        out_shape=jax.ShapeDtypeStruct((b, s, h * dv), BF16),
        scratch_shapes=[pltpu.VMEM((s, LANES), BF16),
                        per_head((s, dk), F32),
                        per_head((s, dk), F32),
                        per_head((s, dv), BF16),
                        per_head((s, dv), F32),
                        per_head((s, 2 * dk), F32),
                        per_head((s, 2 * dk), BF16),
                        per_head((s, 2 * dk), BF16),
                        per_head((s, 2 * dk), BF16),
                        per_head((s, 2 * dk), BF16),
                        per_head((n_chunks, 8, dk), F32),
                        per_head((s, CHUNK), BF16),
                        per_head((n_chunks, dv, 2 * dk), BF16)],
        compiler_params=_params(("parallel", "arbitrary")),
        name="gla",
    )(h3, w_in, w_in, w_in, w_in, w_in, gate_w, gate_b, gain)


MERGE_COLS = 256


def _merge_kernel(n_experts, h_ref, ret_ref, gla_ref, x_ref, wgl_ref, wr_ref, wg_ref, wo_ref, gain_ref,
                  wr2_ref, x1_ref, h2_ref, aff_ref):
    d = x_ref.shape[1]
    h, ret, gla = h_ref[...], ret_ref[...], gla_ref[...]
    blocks = []
    for j in range(0, d, MERGE_COLS):
        cols = slice(j, j + MERGE_COLS)
        cols_gla = slice(d + j, d + j + MERGE_COLS)
        m = (_sigmoid(_dot_nt(h, wgl_ref[cols, :])) * _dot(ret, wr_ref[:, cols])
             + _sigmoid(_dot_nt(h, wgl_ref[cols_gla, :])) * _dot(gla, wg_ref[:, cols]))
        blocks.append(m.astype(BF16))
    x1 = x_ref[...] + _dot(jnp.concatenate(blocks, axis=1), wo_ref[...])
    x1_ref[...] = x1
    h2 = _rms(x1, gain_ref[...]).astype(BF16)
    h2_ref[...] = h2
    logits2 = _dot(h2, wr2_ref[...])
    logits = logits2[:, :LANES] + logits2[:, LANES:]
    lane = lax.broadcasted_iota(jnp.int32, logits.shape, 1)
    logits = jnp.where(lane < n_experts, logits, -jnp.inf)
    p = jnp.exp(logits - jnp.max(logits, axis=-1, keepdims=True))
    aff = p / jnp.sum(p, axis=-1, keepdims=True)
    aff_ref[0] = aff.T[0:n_experts, :]


def _merge(h, ret, gla, x2, w_gl, w_ret, w_gla, w_out, gain, w_router2, n_experts, batch, tm):
    t, d = x2.shape
    s = t // batch
    per_b = s // tm
    rows = lambda width: pl.BlockSpec((tm, width), lambda i: (i, 0))
    return pl.pallas_call(
        functools.partial(_merge_kernel, n_experts),
        grid=(t // tm,),
        in_specs=[rows(d), rows(ret.shape[1]), rows(gla.shape[1]), rows(d),
                  _resident(w_gl.shape), _resident(w_ret.shape), _resident(w_gla.shape),
                  _resident(w_out.shape), _resident((1, d)),
                  _resident(w_router2.shape)],
        out_specs=[rows(d), rows(d),
                   pl.BlockSpec((1, n_experts, tm), lambda i: (i // per_b, 0, i % per_b))],
        out_shape=[jax.ShapeDtypeStruct((t, d), F32),
                   jax.ShapeDtypeStruct((t, d), BF16),
                   jax.ShapeDtypeStruct((batch, n_experts, s), F32)],
        compiler_params=_params(("parallel",)),
        name="merge",
    )(h, ret, gla, x2, w_gl, w_ret, w_gla, w_out, gain, w_router2)


def _prefix_count(mask):
    s = mask.shape[1]
    ri = lax.broadcasted_iota(jnp.int32, (LANES, LANES), 0)
    ci = lax.broadcasted_iota(jnp.int32, (LANES, LANES), 1)
    tri = jnp.where(ri <= ci, 1.0, 0.0).astype(BF16)
    off = jnp.zeros((mask.shape[0], 1), F32)
    parts = []
    for j in range(s // LANES):
        p = _dot(mask[:, j * LANES:(j + 1) * LANES].astype(BF16), tri) + off
        parts.append(p)
        off = p[:, LANES - 1:LANES]
    return jnp.concatenate(parts, axis=1)


def _route_kernel(capacity, aff_ref, slot_ref):
    a = aff_ref[0]
    bits = lax.bitcast_convert_type(a, jnp.int32)
    n_e = a.shape[0]
    cap = float(capacity)

    def count(pred):
        return jnp.sum(jnp.where(pred, 1.0, 0.0), axis=1, keepdims=True)

    def search(_, c):
        lo, hi = c
        mid = lo + lax.shift_right_logical(hi - lo, 1)
        ok = count(bits >= mid) >= cap
        return jnp.where(ok, mid, lo), jnp.where(ok, hi, mid)

    lo0 = jnp.zeros((n_e, 1), jnp.int32)
    hi0 = jnp.full((n_e, 1), 0x7F800000, jnp.int32)
    thr_bits, _ = lax.fori_loop(0, 31, search, (lo0, hi0))
    thr0 = jnp.max(jnp.where(bits <= thr_bits, a, -1.0), axis=1, keepdims=True)

    def counts(v):
        return count(a >= v), count(a > v)

    def unsettled(state):
        _, c_ge, c_gt = state
        bad = jnp.where(c_ge < cap, 1.0, jnp.where(c_gt >= cap, 1.0, 0.0))
        return jnp.max(bad, axis=0, keepdims=True)[0, 0] > 0.0

    def step(state):
        v, c_ge, c_gt = state
        below = jnp.max(jnp.where(a < v, a, -1.0), axis=1, keepdims=True)
        above = jnp.min(jnp.where(a > v, a, 2.0), axis=1, keepdims=True)
        v = jnp.where(c_ge < cap, below, jnp.where(c_gt >= cap, above, v))
        return (v,) + counts(v)

    thr, _, n_gt = lax.while_loop(unsettled, step, (thr0,) + counts(thr0))

    gt = a > thr
    eq = a == thr
    need = cap - n_gt
    eq_rank = _prefix_count(jnp.where(eq, 1.0, 0.0))
    sel = jnp.where(gt, 1.0, jnp.where(eq, jnp.where(eq_rank <= need, 1.0, 0.0), 0.0))
    pos = _prefix_count(sel)
    slot_ref[0] = jnp.where(sel > 0.0, pos - 1.0, -1.0)


def _route(aff_t, capacity):
    b, e, s = aff_t.shape
    return pl.pallas_call(
        functools.partial(_route_kernel, capacity),
        grid=(1,),
        in_specs=[pl.BlockSpec((1, b * e, s), lambda i: (0, 0, 0))],
        out_specs=pl.BlockSpec((1, b * e, s), lambda i: (0, 0, 0)),
        out_shape=jax.ShapeDtypeStruct((1, b * e, s), F32),
        compiler_params=_params(("arbitrary",)),
        name="route",
    )(aff_t.reshape(1, b * e, s)).reshape(b, e, s)


def _dispatch_kernel(slot_ref, aff_ref, h_ref, xg_ref, gate_ref):
    group, cap = xg_ref.shape[0], xg_ref.shape[2]
    h = h_ref[0]
    row = lax.broadcasted_iota(jnp.int32, (cap, h.shape[0]), 0).astype(F32)
    for g in range(group):
        hit = row == slot_ref[0, g:g + 1, :]
        xg_ref[g, 0] = _dot(jnp.where(hit, 1.0, 0.0).astype(BF16), h).astype(xg_ref.dtype)
        gate = jnp.sum(jnp.where(hit, aff_ref[0, g:g + 1, :], 0.0), axis=1, keepdims=True)
        gate_ref[g, 0] = jnp.broadcast_to(gate, gate_ref.shape[2:])


def _dispatch(slot, aff_t, h2, capacity):
    b, e, s = slot.shape
    d = h2.shape[-1]
    group = _pick(e, (8,))
    return pl.pallas_call(
        _dispatch_kernel,
        grid=(b, e // group),
        in_specs=[pl.BlockSpec((1, group, s), lambda i, j: (i, j, 0)),
                  pl.BlockSpec((1, group, s), lambda i, j: (i, j, 0)),
                  pl.BlockSpec((1, s, d), lambda i, j: (i, 0, 0))],
        out_specs=[pl.BlockSpec((group, 1, capacity, d), lambda i, j: (j, i, 0, 0)),
                   pl.BlockSpec((group, 1, capacity, LANES), lambda i, j: (j, i, 0, 0))],
        out_shape=[jax.ShapeDtypeStruct((e, b, capacity, d), BF16),
                   jax.ShapeDtypeStruct((e, b, capacity, LANES), F32)],
        compiler_params=_params(("parallel", "parallel")),
        name="dispatch",
    )(slot, aff_t, h2)


FFN_TILE = 768


def _ffn_kernel(tm, tf, x_ref, gate_ref, wg_hbm, wu_hbm, wd_hbm, y_ref, acc_ref, wg_buf, wu_buf, wd_buf, sems):
    expert, n_experts = pl.program_id(0), pl.num_programs(0)
    n_full, rest = divmod(wg_hbm.shape[2], tf)
    n_f = n_full + (1 if rest else 0)
    width = lambda f: tf if f < n_full else rest
    m, d = x_ref.shape[1], x_ref.shape[2]

    def tile_copies(e, f, slot, w):
        cols = pl.ds(pl.multiple_of(f * tf, tf), w)
        return (pltpu.make_async_copy(wg_hbm.at[e, :, cols], wg_buf.at[slot, :, pl.ds(0, w)], sems.at[0, slot]),
                pltpu.make_async_copy(wu_hbm.at[e, :, cols], wu_buf.at[slot, :, pl.ds(0, w)], sems.at[1, slot]),
                pltpu.make_async_copy(wd_hbm.at[e, cols, :], wd_buf.at[slot, pl.ds(0, w), :], sems.at[2, slot]))

    def start(e, f, slot, w):
        for copy in tile_copies(e, f, slot, w):
            copy.start()

    @pl.when(expert == 0)
    def _():
        start(0, 0, 0, width(0))

    def step(f, w, w_next, first, final):
        slot = lax.rem(expert * n_f + f, 2)
        if final:
            @pl.when(expert + 1 < n_experts)
            def _():
                start(expert + 1, 0, 1 - slot, width(0))
        else:
            start(expert, f + 1, 1 - slot, w_next)
        for copy in tile_copies(expert, f, slot, w):
            copy.wait()
        w_gate = wg_buf[slot, :, 0:w].astype(BF16)
        w_up = wu_buf[slot, :, 0:w].astype(BF16)
        w_down = wd_buf[slot, 0:w, :].astype(BF16)
        for i in range(m // tm):
            rows = pl.ds(i * tm, tm)
            x = x_ref[0, rows, :]
            act = (_silu(_dot(x, w_gate)) * _dot(x, w_up)).astype(BF16)
            part = _dot(act, w_down)
            if not first:
                part = acc_ref[rows, :] + part
            if final:
                gate = jnp.concatenate([gate_ref[0, rows, :]] * (d // LANES), axis=1)
                y_ref[0, rows, :] = (part * gate).astype(y_ref.dtype)
            else:
                acc_ref[rows, :] = part

    uniform = [f for f in range(1, n_f - 1) if width(f) == tf and width(f + 1) == tf]
    f = 0
    while f < n_f:
        if uniform and f == uniform[0]:
            def middle(i, carry):
                step(i, tf, tf, False, False)
                return carry
            lax.fori_loop(uniform[0], uniform[-1] + 1, middle, 0)
            f = uniform[-1] + 1
        else:
            step(f, width(f), width(f + 1) if f + 1 < n_f else 0, f == 0, f == n_f - 1)
            f += 1


def _ffn(xg, gate, w_gate, w_up, w_down, tf, tm):
    e, m, d = xg.shape
    assert tf % LANES == 0 and w_gate.shape[2] % LANES == 0
    return pl.pallas_call(
        functools.partial(_ffn_kernel, tm, tf),
        grid=(e,),
        in_specs=[pl.BlockSpec((1, m, d), lambda i: (i, 0, 0)),
                  pl.BlockSpec((1, m, LANES), lambda i: (i, 0, 0)),
                  pl.BlockSpec(memory_space=pl.ANY),
                  pl.BlockSpec(memory_space=pl.ANY),
                  pl.BlockSpec(memory_space=pl.ANY)],
        out_specs=pl.BlockSpec((1, m, d), lambda i: (i, 0, 0)),
        out_shape=jax.ShapeDtypeStruct((e, m, d), BF16),
        scratch_shapes=[pltpu.VMEM((m, d), F32),
                        pltpu.VMEM((2, d, tf), w_gate.dtype),
                        pltpu.VMEM((2, d, tf), w_up.dtype),
                        pltpu.VMEM((2, tf, d), w_down.dtype),
                        pltpu.SemaphoreType.DMA((3, 2))],
        compiler_params=_params(("arbitrary",)),
        name="ffn",
    )(xg, gate, w_gate, w_up, w_down)


def _combine_kernel(slot_ref, y_ref, x1_ref, gain_ref, o_ref):
    n_e, cap = y_ref.shape[0], y_ref.shape[2]
    tt = x1_ref.shape[1]
    acc = x1_ref[0]
    row = lax.broadcasted_iota(jnp.int32, (cap, tt), 0).astype(F32)
    for e in range(n_e):
        onehot = jnp.where(row == slot_ref[0, e:e + 1, :], 1.0, 0.0).astype(BF16)
        acc = acc + _dot_tn(onehot, y_ref[e, 0])
    o_ref[0] = _rms(acc, gain_ref[...])


def _combine(slot, y, x1, gain, tt):
    b, e, s = slot.shape
    cap, d = y.shape[2], y.shape[3]
    return pl.pallas_call(
        _combine_kernel,
        grid=(b, s // tt),
        in_specs=[pl.BlockSpec((1, e, tt), lambda i, j: (i, 0, j)),
                  pl.BlockSpec((e, 1, cap, d), lambda i, j: (0, i, 0, 0)),
                  pl.BlockSpec((1, tt, d), lambda i, j: (i, j, 0)),
                  pl.BlockSpec((1, d), lambda i, j: (0, 0))],
        out_specs=pl.BlockSpec((1, tt, d), lambda i, j: (i, j, 0)),
        out_shape=jax.ShapeDtypeStruct((b, s, d), F32),
        compiler_params=_params(("parallel", "parallel")),
        name="combine",
    )(slot, y, x1, gain)


def _layer(x, cos, sin, norm_mix, w_in, ret_decay_fwd, ret_decay_bwd, ret_norm,
           gla_gate_w_fwd, gla_gate_b_fwd, gla_gate_w_bwd, gla_gate_b_bwd, gla_norm,
           w_branch_ret, w_branch_gla, w_out, norm_ffn, w_router, w_gate, w_up, w_down, norm_out):
    b, s, d = x.shape
    t = b * s
    ret_qk, ret_v = RET_HEADS * RET_DK, RET_HEADS * RET_DV
    gla_qk, gla_v = GLA_HEADS * GLA_DK, GLA_HEADS * GLA_DV
    rank = GLA_GATE_RANK
    gla0 = 2 * ret_qk + 2 * ret_v
    ga0 = gla0 + 2 * gla_qk + 2 * gla_v
    assert w_in.shape == (d, ga0 + 2 * rank + 2 * d)
    assert s % CHUNK == 0 and 3 * GATE_COPY <= LANES and ga0 % LANES == 0

    w_in_t = w_in.T.astype(BF16)
    w_gl = w_in_t[ga0 + 2 * rank:, :]

    gw = jnp.zeros((GLA_HEADS, GATE_COPY, 2 * GLA_DK), F32)
    gw = gw.at[:, :rank, :GLA_DK].set(gla_gate_w_fwd.reshape(rank, GLA_HEADS, GLA_DK).transpose(1, 0, 2))
    gw = gw.at[:, rank:, GLA_DK:].set(gla_gate_w_bwd.reshape(rank, GLA_HEADS, GLA_DK).transpose(1, 0, 2))
    gw_hi = gw.astype(BF16)
    gw_lo = (gw - gw_hi.astype(F32)).astype(BF16)
    gate_w = jnp.concatenate([gw_hi, gw_lo, gw_hi, jnp.zeros_like(gw_hi)], axis=1)
    gate_b = jnp.concatenate([gla_gate_b_fwd.reshape(GLA_HEADS, 1, GLA_DK),
                              gla_gate_b_bwd.reshape(GLA_HEADS, 1, GLA_DK)], axis=2)

    dec = jnp.stack([ret_decay_fwd, ret_decay_bwd], axis=1)[:, :, None]
    dec = jnp.pad(jnp.broadcast_to(dec, (RET_HEADS, 2, RET_DV)), ((0, 0), (0, 6), (0, 0)))

    x2 = x.reshape(t, d)
    ret, h3 = _retention(x, norm_mix[None, :], w_in_t, 0, cos, sin, dec, ret_norm[None, :])
    gla = _gla(h3, w_in_t, gla0, gate_w, gate_b, gla_norm[None, :])
    h = h3.reshape(t, d)

    n_e = w_router.shape[1]
    w_r = jnp.pad(w_router, ((0, 0), (0, LANES - n_e)))
    wr_hi = w_r.astype(BF16)
    wr_lo = (w_r - wr_hi.astype(F32)).astype(BF16)
    x1, h2, aff_t = _merge(h, ret.reshape(t, ret_v), gla.reshape(t, gla_v), x2, w_gl,
                           w_branch_ret.astype(BF16), w_branch_gla.astype(BF16), w_out.astype(BF16),
                           norm_ffn[None, :], jnp.concatenate([wr_hi, wr_lo], axis=1), n_e, b,
                           _pick(s, (1024, 512, 256, 128)))

    capacity = EC_CAPACITY_FACTOR * s // n_e
    slot = _route(aff_t, capacity)
    xg, gate = _dispatch(slot, aff_t, h2.reshape(b, s, d), capacity)
    f = w_gate.shape[2]
    y = _ffn(xg.reshape(n_e, b * capacity, d), gate.reshape(n_e, b * capacity, LANES),
             w_gate, w_up, w_down, min(FFN_TILE, f), _pick(b * capacity, (1024, 512, 256, 128)))
    return _combine(slot, y.reshape(n_e, b, capacity, d), x1.reshape(b, s, d), norm_out,
                    _pick(s, (1024, 512, 256, 128)))


def kernel(x, positions, norm_mix, w_in, ret_decay_fwd, ret_decay_bwd, ret_norm, gla_gate_w_fwd,
           gla_gate_b_fwd, gla_gate_w_bwd, gla_gate_b_bwd, gla_norm, w_branch_ret, w_branch_gla,
           w_out, norm_ffn, w_router, w_gate, w_up, w_down, norm_final):
    depth = norm_mix.shape[0]
    assert depth == 1, "the final RMSNorm is fused into the last layer's combine stage"
    cos, sin = _rope_table(positions, RET_DK)
    return _layer(x, cos, sin, norm_mix[0], w_in[0], ret_decay_fwd[0], ret_decay_bwd[0], ret_norm[0],
                  gla_gate_w_fwd[0], gla_gate_b_fwd[0], gla_gate_w_bwd[0], gla_gate_b_bwd[0], gla_norm[0],
                  w_branch_ret[0], w_branch_gla[0], w_out[0], norm_ffn[0], w_router[0],
                  w_gate[0], w_up[0], w_down[0], norm_final[None, :])
```

```python
import functools

import jax
import jax.numpy as jnp
from jax import lax
from jax.experimental import pallas as pl
from jax.experimental.pallas import tpu as pltpu

F32 = jnp.float32
BF16 = jnp.bfloat16

RET_HEADS = 4
RET_DK = 128
RET_DV = 256
GLA_HEADS = 4
GLA_DK = 128
GLA_DV = 256
GLA_GATE_RANK = 16
GLA_GATE_NORMALIZER = 16.0
CHUNK = 128
EC_CAPACITY_FACTOR = 2
ROPE_THETA = 10000.0
NORM_EPS = 1e-6
LOG2_E = 1.4426950408889634

CHUNK_UNROLL = 16
LANES = 128
VMEM_LIMIT = 56 << 20


def _params(sem, vmem=VMEM_LIMIT):
    return pltpu.CompilerParams(dimension_semantics=sem, vmem_limit_bytes=vmem)


def _resident(shape):
    return pl.BlockSpec(shape, lambda *_: (0,) * len(shape), pipeline_mode=pl.Buffered(1))


def _pick(n, prefs):
    for p in prefs:
        if n % p == 0:
            return p
    return n


def _sigmoid(x):
    return 1.0 / (1.0 + jnp.exp(-x))


def _silu(x):
    half = 0.5 * x
    return half + half * jnp.tanh(half)


def _log_sigmoid(x):
    return jnp.minimum(x, 0.0) - jnp.log(1.0 + jnp.exp(-jnp.abs(x)))


def _rms(x, gain):
    return x * lax.rsqrt(jnp.mean(x * x, axis=-1, keepdims=True) + NORM_EPS) * gain


def _dot(a, b):
    return jnp.dot(a, b, preferred_element_type=F32)


def _dot_nt(a, b):
    return lax.dot_general(a, b, (((1,), (1,)), ((), ())), preferred_element_type=F32)


def _dot_tn(a, b):
    return lax.dot_general(a, b, (((0,), (0,)), ((), ())), preferred_element_type=F32)


def _split2(x):
    hi = x.astype(BF16)
    lo = (x - hi.astype(F32)).astype(BF16)
    return hi, lo


def _rope_table_kernel(pos_a_ref, pos_b_ref, freq_ref, cos_ref, sin_ref):
    n, dk = pos_a_ref.shape[1], freq_ref.shape[1]
    lower = lax.broadcasted_iota(jnp.int32, (n, dk), 1) < dk // 2
    ang = jnp.where(lower, pos_a_ref[0].astype(F32), pos_b_ref[0].astype(F32)) * freq_ref[...]
    cos, sin = jnp.cos(ang), jnp.sin(ang)
    cos_x, sin_x = pltpu.roll(cos, dk // 2, 1), pltpu.roll(sin, dk // 2, 1)
    cos_ref[0, 0:n, :] = jnp.where(lower, cos, cos_x)
    cos_ref[0, n:2 * n, :] = jnp.where(lower, cos_x, cos)
    sin_ref[0, 0:n, :] = jnp.where(lower, -sin, sin_x)
    sin_ref[0, n:2 * n, :] = jnp.where(lower, -sin_x, sin)


def _rope_table(positions, dk):
    b, s = positions.shape
    assert s % 16 == 0
    half = jnp.arange(0, dk, 2, dtype=F32) / dk
    inv_freq = ROPE_THETA ** (-half)
    freq = jnp.concatenate([inv_freq, inv_freq])[None, :]
    out = jax.ShapeDtypeStruct((b, s, dk), F32)
    pos = positions[:, :, None]
    return pl.pallas_call(
        _rope_table_kernel,
        grid=(b,),
        in_specs=[pl.BlockSpec((1, s // 2, 1), lambda i: (i, 0, 0)),
                  pl.BlockSpec((1, s // 2, 1), lambda i: (i, 1, 0)),
                  pl.BlockSpec((1, dk), lambda i: (0, 0))],
        out_specs=[pl.BlockSpec((1, s, dk), lambda i: (i, 0, 0))] * 2,
        out_shape=[out, out],
        compiler_params=_params(("parallel",)),
        name="rope_table",
    )(pos, pos, freq)


def _rope(t, cos, sin_signed):
    return t * cos + pltpu.roll(t, t.shape[-1] // 2, 1) * sin_signed


def _chunk_rows(n, c=CHUNK):
    return pl.ds(pl.multiple_of(n * c, c), c)


def _row_tile(s_len):
    return _pick(s_len, (512, 256, CHUNK))


def _per_tile(chunk_pattern, s_len):
    return jnp.concatenate([chunk_pattern] * (_row_tile(s_len) // chunk_pattern.shape[0]), axis=0)


def _for_row_tiles(s_len, body):
    tile = _row_tile(s_len)

    def step(i, carry):
        body(pl.ds(pl.multiple_of(i * tile, tile), tile))
        return carry

    lax.fori_loop(0, s_len // tile, step, 0, unroll=True)


def _retention_kernel(x_ref, mix_gain_ref, wq_ref, wk_ref, wv_ref, wg_ref, cos_ref, sin_ref, dec_ref,
                      gain_ref, o_ref, h_ref, qk_ref, qw_ref, kw_ref, v_ref, gg_ref, a_ref, st_ref):
    c = a_ref.shape[1]
    s_len, dk, dv = qk_ref.shape[0], qk_ref.shape[1] // 2, v_ref.shape[1]
    n_chunks = s_len // c
    scale = dk ** -0.5

    lg_f = _log_sigmoid(dec_ref[0, 0:1, :])
    lg_b = _log_sigmoid(dec_ref[0, 1:2, :])
    lgf_k, lgb_k = lg_f[:, :dk], lg_b[:, :dk]
    pos = lax.broadcasted_iota(jnp.int32, (c, dk), 0).astype(F32)
    wq_f = _per_tile(jnp.exp((pos + 1.0) * lgf_k), s_len)
    wk_f = _per_tile(jnp.exp((c - 1.0 - pos) * lgf_k), s_len)
    wq_b = _per_tile(jnp.exp((c - pos) * lgb_k), s_len)
    wk_b = _per_tile(jnp.exp(pos * lgb_k), s_len)
    ri = lax.broadcasted_iota(jnp.int32, (c, c), 0)
    ci = lax.broadcasted_iota(jnp.int32, (c, c), 1)
    lower = ri >= ci
    rel = (ri - ci).astype(F32)
    decay_mask = jnp.where(lower,
                           jnp.exp(jnp.where(lower, rel, 0.0) * lg_f[:, :c]),
                           jnp.exp(jnp.where(lower, 0.0, -rel) * lg_b[:, :c]))
    chunk_decay_f = jnp.exp(c * lg_f)
    chunk_decay_b = jnp.exp(c * lg_b)

    @pl.when(pl.program_id(1) == 0)
    def _():
        def norm(rows):
            h_ref[0, rows, :] = _rms(x_ref[0, rows, :], mix_gain_ref[...]).astype(h_ref.dtype)
        _for_row_tiles(s_len, norm)

    w_qk = jnp.concatenate([wq_ref[...], wk_ref[...]], axis=0)
    w_v = wv_ref[...]
    w_g = wg_ref[...]

    def project(rows):
        h = h_ref[0, rows, :]
        qk = _dot_nt(h, w_qk)
        cos, sin = cos_ref[0, rows, :], sin_ref[0, rows, :]
        qr = _rope(qk[:, :dk], cos, sin) * scale
        kr = _rope(qk[:, dk:], cos, sin)
        qk_ref[rows, :] = jnp.concatenate([qr, kr], axis=1).astype(BF16)
        qw_ref[rows, :] = jnp.concatenate([qr * wq_f, qr * wq_b], axis=1).astype(BF16)
        kw_ref[rows, :] = jnp.concatenate([kr * wk_f, kr * wk_b], axis=1).astype(BF16)
        v_ref[rows, :] = _dot_nt(h, w_v).astype(BF16)
        gg_ref[rows, :] = _silu(_dot_nt(h, w_g)) * gain_ref[...]

    _for_row_tiles(s_len, project)

    kv = [_dot_tn(kw_ref[n * c:(n + 1) * c, :], v_ref[n * c:(n + 1) * c, :]) for n in range(n_chunks)]
    s_f = s_b = jnp.zeros((dk, dv), F32)
    for i in range(n_chunks):
        n = n_chunks - 1 - i
        st_ref[i, 0:dk, :] = s_f.astype(BF16)
        st_ref[n, dk:2 * dk, :] = s_b.astype(BF16)
        if i + 1 < n_chunks:
            s_f = chunk_decay_f * s_f + kv[i][0:dk, :]
            s_b = chunk_decay_b * s_b + kv[n][dk:2 * dk, :]

    def scores(n, carry):
        rows = _chunk_rows(n, c)
        qk = qk_ref[rows, :]
        a_ref[rows, :] = (_dot_nt(qk[:, :dk], qk[:, dk:]) * decay_mask).astype(BF16)
        return carry

    lax.fori_loop(0, n_chunks, scores, 0, unroll=CHUNK_UNROLL)

    def phase2(n, carry):
        rows = _chunk_rows(n, c)
        o = _dot(a_ref[rows, :], v_ref[rows, :]) + _dot(qw_ref[rows, :], st_ref[n])
        mu = jnp.mean(o, axis=-1, keepdims=True)
        d = o - mu
        var = jnp.mean(d * d, axis=-1, keepdims=True)
        o_ref[0, rows, :] = (d * lax.rsqrt(var + NORM_EPS) * gg_ref[rows, :]).astype(o_ref.dtype)
        return carry

    lax.fori_loop(0, n_chunks, phase2, 0, unroll=CHUNK_UNROLL)


def _retention(x, mix_gain, w_in, col0, cos, sin, dec, gain):
    b, s, d = x.shape
    h, dk, dv = RET_HEADS, RET_DK, RET_DV
    q0 = col0 // dk
    v0 = (col0 + 2 * h * dk) // dv
    chunk = _pick(s, (2 * CHUNK, CHUNK))
    n_chunks = s // chunk
    return pl.pallas_call(
        _retention_kernel,
        grid=(b, h),
        in_specs=[pl.BlockSpec((1, s, d), lambda i, j: (i, 0, 0)),
                  pl.BlockSpec((1, d), lambda i, j: (0, 0)),
                  pl.BlockSpec((dk, d), lambda i, j: (q0 + j, 0)),
                  pl.BlockSpec((dk, d), lambda i, j: (q0 + h + j, 0)),
                  pl.BlockSpec((dv, d), lambda i, j: (v0 + j, 0)),
                  pl.BlockSpec((dv, d), lambda i, j: (v0 + h + j, 0)),
                  pl.BlockSpec((1, s, dk), lambda i, j: (i, 0, 0)),
                  pl.BlockSpec((1, s, dk), lambda i, j: (i, 0, 0)),
                  pl.BlockSpec((1, 8, dv), lambda i, j: (j, 0, 0)),
                  pl.BlockSpec((1, dv), lambda i, j: (0, j))],
        out_specs=[pl.BlockSpec((1, s, dv), lambda i, j: (i, 0, j)),
                   pl.BlockSpec((1, s, d), lambda i, j: (i, 0, 0))],
        out_shape=[jax.ShapeDtypeStruct((b, s, h * dv), BF16),
                   jax.ShapeDtypeStruct((b, s, d), BF16)],
        scratch_shapes=[pltpu.VMEM((s, 2 * dk), BF16),
                        pltpu.VMEM((s, 2 * dk), BF16),
                        pltpu.VMEM((s, 2 * dk), BF16),
                        pltpu.VMEM((s, dv), BF16),
                        pltpu.VMEM((s, dv), F32),
                        pltpu.VMEM((s, chunk), BF16),
                        pltpu.VMEM((n_chunks, 2 * dk, dv), BF16)],
        compiler_params=_params(("parallel", "arbitrary")),
        name="retention",
    )(x, mix_gain, w_in, w_in, w_in, w_in, cos, sin, dec, gain)


GATE_COPY = 2 * GLA_GATE_RANK


def _cumsum_dot(tri2, x):
    hi, lo = _split2(x)
    return _dot(tri2, jnp.concatenate([hi, lo], axis=0))


def _gla_kernel(h_ref, wq_ref, wk_ref, wv_ref, wg_ref, wa_ref, gw_ref, gb_ref, gain_ref, o_ref,
                q_ref, k_ref, v_ref, gg_ref, ga_ref, la_ref, qs_ref, ks_ref, qw_ref, kw_ref, last_ref,
                a_ref, st_ref):
    c = CHUNK
    s_len, dk, dv = q_ref.shape[0], q_ref.shape[1], v_ref.shape[1]
    n_chunks = s_len // c
    scale = dk ** -0.5
    inv_norm = LOG2_E / GLA_GATE_NORMALIZER

    w_qk = jnp.concatenate([wq_ref[...], wk_ref[...]], axis=0)
    w_v = wv_ref[...]
    w_g = wg_ref[...]

    @pl.when(pl.program_id(1) == 0)
    def _():
        w_a = wa_ref[...]

        def gate_inputs(rows):
            x = _dot_nt(h_ref[0, rows, :], w_a)
            x_hi = x.astype(BF16).astype(F32)
            lane = lax.broadcasted_iota(jnp.int32, x.shape, 1)
            ga = jnp.where(lane < GATE_COPY, x_hi,
                           jnp.where(lane < 2 * GATE_COPY, pltpu.roll(x_hi, GATE_COPY, 1),
                                     jnp.where(lane < 3 * GATE_COPY, pltpu.roll(x - x_hi, 2 * GATE_COPY, 1),
                                               0.0)))
            ga_ref[rows, :] = ga.astype(BF16)

        _for_row_tiles(s_len, gate_inputs)

    def project(rows):
        h = h_ref[0, rows, :]
        qk = _dot_nt(h, w_qk)
        q_ref[rows, :] = qk[:, :dk] * scale
        k_ref[rows, :] = qk[:, dk:]
        v_ref[rows, :] = _dot_nt(h, w_v).astype(BF16)
        gg_ref[rows, :] = _silu(_dot_nt(h, w_g)) * gain_ref[...]
        la_ref[rows, :] = _log_sigmoid(_dot(ga_ref[rows, :], gw_ref[0]) + gb_ref[0]) * inv_norm

    _for_row_tiles(s_len, project)

    ri = lax.broadcasted_iota(jnp.int32, (c, c), 0)
    ci = lax.broadcasted_iota(jnp.int32, (c, c), 1)
    lower = ri >= ci
    tri_lower = jnp.where(lower, 1.0, 0.0).astype(BF16)
    tri2_lower = jnp.concatenate([tri_lower, tri_lower], axis=1)

    def cumulate(n, carry):
        rows = _chunk_rows(n)
        la = la_ref[rows, :]
        incl = _cumsum_dot(tri2_lower, la)
        cum_f = incl[:, 0:dk]
        cum_b = incl[c - 1:c, dk:2 * dk] - incl[:, dk:2 * dk] + la[:, dk:2 * dk]
        last_f, last_b = cum_f[c - 1:c, :], cum_b[0:1, :]
        ref_f, ref_b = cum_f[c // 2:c // 2 + 1, :], cum_b[c // 2 - 1:c // 2, :]
        q, k = q_ref[rows, :], k_ref[rows, :]
        q_f = q * jnp.exp2(cum_f - ref_f)
        q_b = q * jnp.exp2(cum_b - ref_b)
        qs_ref[rows, :] = jnp.concatenate([q_f, q_b], axis=1).astype(BF16)
        ks_ref[rows, :] = jnp.concatenate([k * jnp.exp2(ref_f - cum_f), k * jnp.exp2(ref_b - cum_b)],
                                          axis=1).astype(BF16)
        qw_ref[rows, :] = jnp.concatenate([q_f * jnp.exp2(ref_f), q_b * jnp.exp2(ref_b)], axis=1).astype(BF16)
        kw_ref[rows, :] = jnp.concatenate([k * jnp.exp2(last_f - cum_f), k * jnp.exp2(last_b - cum_b)],
                                          axis=1).astype(BF16)
        last_ref[n, 0:1, :] = last_f
        last_ref[n, 1:2, :] = last_b
        return carry

    lax.fori_loop(0, n_chunks, cumulate, 0, unroll=CHUNK_UNROLL)

    kv = [_dot_tn(v_ref[n * c:(n + 1) * c, :], kw_ref[n * c:(n + 1) * c, :]) for n in range(n_chunks)]
    s_f = s_b = jnp.zeros((dv, dk), F32)
    for i in range(n_chunks):
        n = n_chunks - 1 - i
        st_ref[i, :, 0:dk] = s_f.astype(BF16)
        st_ref[n, :, dk:2 * dk] = s_b.astype(BF16)
        if i + 1 < n_chunks:
            s_f = jnp.exp2(last_ref[i, 0:1, :]) * s_f + kv[i][:, 0:dk]
            s_b = jnp.exp2(last_ref[n, 1:2, :]) * s_b + kv[n][:, dk:2 * dk]

    def scores(n, carry):
        rows = _chunk_rows(n)
        qs, ks = qs_ref[rows, :], ks_ref[rows, :]
        a_ref[rows, :] = jnp.where(lower, _dot_nt(qs[:, :dk], ks[:, :dk]),
                                   _dot_nt(qs[:, dk:], ks[:, dk:])).astype(BF16)
        return carry

    lax.fori_loop(0, n_chunks, scores, 0, unroll=CHUNK_UNROLL)

    def phase2(n, carry):
        rows = _chunk_rows(n)
        o = _dot(a_ref[rows, :], v_ref[rows, :]) + _dot_nt(qw_ref[rows, :], st_ref[n])
        rms = lax.rsqrt(jnp.mean(o * o, axis=-1, keepdims=True) + NORM_EPS)
        o_ref[0, rows, :] = (o * rms * gg_ref[rows, :]).astype(o_ref.dtype)
        return carry

    lax.fori_loop(0, n_chunks, phase2, 0, unroll=CHUNK_UNROLL)


def _gla(h3, w_in, col0, gate_w, gate_b, gain):
    b, s, d = h3.shape
    h, dk, dv = GLA_HEADS, GLA_DK, GLA_DV
    q0 = col0 // dk
    v0 = (col0 + 2 * h * dk) // dv
    a0 = (col0 + 2 * h * dk + 2 * h * dv) // LANES
    n_chunks = s // CHUNK
    return pl.pallas_call(
        _gla_kernel,
        grid=(b, h),
        in_specs=[pl.BlockSpec((1, s, d), lambda i, j: (i, 0, 0)),
                  pl.BlockSpec((dk, d), lambda i, j: (q0 + j, 0)),
                  pl.BlockSpec((dk, d), lambda i, j: (q0 + h + j, 0)),
                  pl.BlockSpec((dv, d), lambda i, j: (v0 + j, 0)),
                  pl.BlockSpec((dv, d), lambda i, j: (v0 + h + j, 0)),
                  pl.BlockSpec((LANES, d), lambda i, j: (a0, 0)),
                  pl.BlockSpec((1, LANES, 2 * dk), lambda i, j: (j, 0, 0)),
                  pl.BlockSpec((1, 1, 2 * dk), lambda i, j: (j, 0, 0)),
                  pl.BlockSpec((1, dv), lambda i, j: (0, j))],
        out_specs=pl.BlockSpec((1, s, dv), lambda i, j: (i, 0, j)),
        out_shape=jax.ShapeDtypeStruct((b, s, h * dv), BF16),
        scratch_shapes=[pltpu.VMEM((s, dk), F32),
                        pltpu.VMEM((s, dk), F32),
                        pltpu.VMEM((s, dv), BF16),
                        pltpu.VMEM((s, dv), F32),
                        pltpu.VMEM((s, LANES), BF16),
                        pltpu.VMEM((s, 2 * dk), F32),
                        pltpu.VMEM((s, 2 * dk), BF16),
                        pltpu.VMEM((s, 2 * dk), BF16),
                        pltpu.VMEM((s, 2 * dk), BF16),
                        pltpu.VMEM((s, 2 * dk), BF16),
                        pltpu.VMEM((n_chunks, 8, dk), F32),
                        pltpu.VMEM((s, CHUNK), BF16),
                        pltpu.VMEM((n_chunks, dv, 2 * dk), BF16)],
        compiler_params=_params(("parallel", "arbitrary")),
        name="gla",
    )(h3, w_in, w_in, w_in, w_in, w_in, gate_w, gate_b, gain)


MERGE_COLS = 256


def _merge_kernel(n_experts, h_ref, ret_ref, gla_ref, x_ref, wgl_ref, wr_ref, wg_ref, wo_ref, gain_ref,
                  wr2_ref, x1_ref, h2_ref, aff_ref):
    d = x_ref.shape[1]
    h, ret, gla = h_ref[...], ret_ref[...], gla_ref[...]
    blocks = []
    for j in range(0, d, MERGE_COLS):
        cols = slice(j, j + MERGE_COLS)
        cols_gla = slice(d + j, d + j + MERGE_COLS)
        m = (_sigmoid(_dot_nt(h, wgl_ref[cols, :])) * _dot(ret, wr_ref[:, cols])
             + _sigmoid(_dot_nt(h, wgl_ref[cols_gla, :])) * _dot(gla, wg_ref[:, cols]))
        blocks.append(m.astype(BF16))
    x1 = x_ref[...] + _dot(jnp.concatenate(blocks, axis=1), wo_ref[...])
    x1_ref[...] = x1
    h2 = _rms(x1, gain_ref[...]).astype(BF16)
    h2_ref[...] = h2
    logits2 = _dot(h2, wr2_ref[...])
    logits = logits2[:, :LANES] + logits2[:, LANES:]
    lane = lax.broadcasted_iota(jnp.int32, logits.shape, 1)
    logits = jnp.where(lane < n_experts, logits, -jnp.inf)
    p = jnp.exp(logits - jnp.max(logits, axis=-1, keepdims=True))
    aff = p / jnp.sum(p, axis=-1, keepdims=True)
    aff_ref[0] = aff.T[0:n_experts, :]


def _merge(h, ret, gla, x2, w_gl, w_ret, w_gla, w_out, gain, w_router2, n_experts, batch, tm):
    t, d = x2.shape
    s = t // batch
    per_b = s // tm
    rows = lambda width: pl.BlockSpec((tm, width), lambda i: (i, 0))
    return pl.pallas_call(
        functools.partial(_merge_kernel, n_experts),
        grid=(t // tm,),
        in_specs=[rows(d), rows(ret.shape[1]), rows(gla.shape[1]), rows(d),
                  _resident(w_gl.shape), _resident(w_ret.shape), _resident(w_gla.shape),
                  _resident(w_out.shape), _resident((1, d)),
                  _resident(w_router2.shape)],
        out_specs=[rows(d), rows(d),
                   pl.BlockSpec((1, n_experts, tm), lambda i: (i // per_b, 0, i % per_b))],
        out_shape=[jax.ShapeDtypeStruct((t, d), F32),
                   jax.ShapeDtypeStruct((t, d), BF16),
                   jax.ShapeDtypeStruct((batch, n_experts, s), F32)],
        compiler_params=_params(("parallel",)),
        name="merge",
    )(h, ret, gla, x2, w_gl, w_ret, w_gla, w_out, gain, w_router2)


def _prefix_count(mask):
    s = mask.shape[1]
    ri = lax.broadcasted_iota(jnp.int32, (LANES, LANES), 0)
    ci = lax.broadcasted_iota(jnp.int32, (LANES, LANES), 1)
    tri = jnp.where(ri <= ci, 1.0, 0.0).astype(BF16)
    off = jnp.zeros((mask.shape[0], 1), F32)
    parts = []
    for j in range(s // LANES):
        p = _dot(mask[:, j * LANES:(j + 1) * LANES].astype(BF16), tri) + off
        parts.append(p)
        off = p[:, LANES - 1:LANES]
    return jnp.concatenate(parts, axis=1)


def _route_kernel(capacity, aff_ref, slot_ref):
    a = aff_ref[0]
    bits = lax.bitcast_convert_type(a, jnp.int32)
    n_e = a.shape[0]
    cap = float(capacity)

    def count(pred):
        return jnp.sum(jnp.where(pred, 1.0, 0.0), axis=1, keepdims=True)

    def search(_, c):
        lo, hi = c
        mid = lo + lax.shift_right_logical(hi - lo, 1)
        ok = count(bits >= mid) >= cap
        return jnp.where(ok, mid, lo), jnp.where(ok, hi, mid)

    lo0 = jnp.zeros((n_e, 1), jnp.int32)
    hi0 = jnp.full((n_e, 1), 0x7F800000, jnp.int32)
    thr_bits, _ = lax.fori_loop(0, 31, search, (lo0, hi0))
    thr0 = jnp.max(jnp.where(bits <= thr_bits, a, -1.0), axis=1, keepdims=True)

    def counts(v):
        return count(a >= v), count(a > v)

    def unsettled(state):
        _, c_ge, c_gt = state
        bad = jnp.where(c_ge < cap, 1.0, jnp.where(c_gt >= cap, 1.0, 0.0))
        return jnp.max(bad, axis=0, keepdims=True)[0, 0] > 0.0

    def step(state):
        v, c_ge, c_gt = state
        below = jnp.max(jnp.where(a < v, a, -1.0), axis=1, keepdims=True)
        above = jnp.min(jnp.where(a > v, a, 2.0), axis=1, keepdims=True)
        v = jnp.where(c_ge < cap, below, jnp.where(c_gt >= cap, above, v))
        return (v,) + counts(v)

    thr, _, n_gt = lax.while_loop(unsettled, step, (thr0,) + counts(thr0))

    gt = a > thr
    eq = a == thr
    need = cap - n_gt
    eq_rank = _prefix_count(jnp.where(eq, 1.0, 0.0))
    sel = jnp.where(gt, 1.0, jnp.where(eq, jnp.where(eq_rank <= need, 1.0, 0.0), 0.0))
    pos = _prefix_count(sel)
    slot_ref[0] = jnp.where(sel > 0.0, pos - 1.0, -1.0)


def _route(aff_t, capacity):
    b, e, s = aff_t.shape
    return pl.pallas_call(
        functools.partial(_route_kernel, capacity),
        grid=(1,),
        in_specs=[pl.BlockSpec((1, b * e, s), lambda i: (0, 0, 0))],
        out_specs=pl.BlockSpec((1, b * e, s), lambda i: (0, 0, 0)),
        out_shape=jax.ShapeDtypeStruct((1, b * e, s), F32),
        compiler_params=_params(("arbitrary",)),
        name="route",
    )(aff_t.reshape(1, b * e, s)).reshape(b, e, s)


def _dispatch_kernel(slot_ref, aff_ref, h_ref, xg_ref, gate_ref):
    group, cap = xg_ref.shape[0], xg_ref.shape[2]
    h = h_ref[0]
    row = lax.broadcasted_iota(jnp.int32, (cap, h.shape[0]), 0).astype(F32)
    for g in range(group):
        hit = row == slot_ref[0, g:g + 1, :]
        xg_ref[g, 0] = _dot(jnp.where(hit, 1.0, 0.0).astype(BF16), h).astype(xg_ref.dtype)
        gate = jnp.sum(jnp.where(hit, aff_ref[0, g:g + 1, :], 0.0), axis=1, keepdims=True)
        gate_ref[g, 0] = jnp.broadcast_to(gate, gate_ref.shape[2:])


def _dispatch(slot, aff_t, h2, capacity):
    b, e, s = slot.shape
    d = h2.shape[-1]
    group = _pick(e, (8,))
    return pl.pallas_call(
        _dispatch_kernel,
        grid=(b, e // group),
        in_specs=[pl.BlockSpec((1, group, s), lambda i, j: (i, j, 0)),
                  pl.BlockSpec((1, group, s), lambda i, j: (i, j, 0)),
                  pl.BlockSpec((1, s, d), lambda i, j: (i, 0, 0))],
        out_specs=[pl.BlockSpec((group, 1, capacity, d), lambda i, j: (j, i, 0, 0)),
                   pl.BlockSpec((group, 1, capacity, LANES), lambda i, j: (j, i, 0, 0))],
        out_shape=[jax.ShapeDtypeStruct((e, b, capacity, d), BF16),
                   jax.ShapeDtypeStruct((e, b, capacity, LANES), F32)],
        compiler_params=_params(("parallel", "parallel")),
        name="dispatch",
    )(slot, aff_t, h2)


FFN_TILE = 768


def _ffn_kernel(tm, tf, x_ref, gate_ref, wg_hbm, wu_hbm, wd_hbm, y_ref, acc_ref, wg_buf, wu_buf, wd_buf, sems):
    expert, n_experts = pl.program_id(0), pl.num_programs(0)
    n_full, rest = divmod(wg_hbm.shape[2], tf)
    n_f = n_full + (1 if rest else 0)
    width = lambda f: tf if f < n_full else rest
    m, d = x_ref.shape[1], x_ref.shape[2]

    def tile_copies(e, f, slot, w):
        cols = pl.ds(pl.multiple_of(f * tf, tf), w)
        return (pltpu.make_async_copy(wg_hbm.at[e, :, cols], wg_buf.at[slot, :, pl.ds(0, w)], sems.at[0, slot]),
                pltpu.make_async_copy(wu_hbm.at[e, :, cols], wu_buf.at[slot, :, pl.ds(0, w)], sems.at[1, slot]),
                pltpu.make_async_copy(wd_hbm.at[e, cols, :], wd_buf.at[slot, pl.ds(0, w), :], sems.at[2, slot]))

    def start(e, f, slot, w):
        for copy in tile_copies(e, f, slot, w):
            copy.start()

    @pl.when(expert == 0)
    def _():
        start(0, 0, 0, width(0))

    def step(f, w, w_next, first, final):
        slot = lax.rem(expert * n_f + f, 2)
        if final:
            @pl.when(expert + 1 < n_experts)
            def _():
                start(expert + 1, 0, 1 - slot, width(0))
        else:
            start(expert, f + 1, 1 - slot, w_next)
        for copy in tile_copies(expert, f, slot, w):
            copy.wait()
        w_gate = wg_buf[slot, :, 0:w].astype(BF16)
        w_up = wu_buf[slot, :, 0:w].astype(BF16)
        w_down = wd_buf[slot, 0:w, :].astype(BF16)
        for i in range(m // tm):
            rows = pl.ds(i * tm, tm)
            x = x_ref[0, rows, :]
            act = (_silu(_dot(x, w_gate)) * _dot(x, w_up)).astype(BF16)
            part = _dot(act, w_down)
            if not first:
                part = acc_ref[rows, :] + part
            if final:
                gate = jnp.concatenate([gate_ref[0, rows, :]] * (d // LANES), axis=1)
                y_ref[0, rows, :] = (part * gate).astype(y_ref.dtype)
            else:
                acc_ref[rows, :] = part

    uniform = [f for f in range(1, n_f - 1) if width(f) == tf and width(f + 1) == tf]
    f = 0
    while f < n_f:
        if uniform and f == uniform[0]:
            def middle(i, carry):
                step(i, tf, tf, False, False)
                return carry
            lax.fori_loop(uniform[0], uniform[-1] + 1, middle, 0)
            f = uniform[-1] + 1
        else:
            step(f, width(f), width(f + 1) if f + 1 < n_f else 0, f == 0, f == n_f - 1)
            f += 1


def _ffn(xg, gate, w_gate, w_up, w_down, tf, tm):
    e, m, d = xg.shape
    assert tf % LANES == 0 and w_gate.shape[2] % LANES == 0
    return pl.pallas_call(
        functools.partial(_ffn_kernel, tm, tf),
        grid=(e,),
        in_specs=[pl.BlockSpec((1, m, d), lambda i: (i, 0, 0)),
                  pl.BlockSpec((1, m, LANES), lambda i: (i, 0, 0)),
                  pl.BlockSpec(memory_space=pl.ANY),
                  pl.BlockSpec(memory_space=pl.ANY),
                  pl.BlockSpec(memory_space=pl.ANY)],
        out_specs=pl.BlockSpec((1, m, d), lambda i: (i, 0, 0)),
        out_shape=jax.ShapeDtypeStruct((e, m, d), BF16),
        scratch_shapes=[pltpu.VMEM((m, d), F32),
                        pltpu.VMEM((2, d, tf), w_gate.dtype),
                        pltpu.VMEM((2, d, tf), w_up.dtype),
                        pltpu.VMEM((2, tf, d), w_down.dtype),
                        pltpu.SemaphoreType.DMA((3, 2))],
        compiler_params=_params(("arbitrary",)),
        name="ffn",
    )(xg, gate, w_gate, w_up, w_down)


def _combine_kernel(slot_ref, y_ref, x1_ref, gain_ref, o_ref):
    n_e, cap = y_ref.shape[0], y_ref.shape[2]
    tt = x1_ref.shape[1]
    acc = x1_ref[0]
    row = lax.broadcasted_iota(jnp.int32, (cap, tt), 0).astype(F32)
    for e in range(n_e):
        onehot = jnp.where(row == slot_ref[0, e:e + 1, :], 1.0, 0.0).astype(BF16)
        acc = acc + _dot_tn(onehot, y_ref[e, 0])
    o_ref[0] = _rms(acc, gain_ref[...])


def _combine(slot, y, x1, gain, tt):
    b, e, s = slot.shape
    cap, d = y.shape[2], y.shape[3]
    return pl.pallas_call(
        _combine_kernel,
        grid=(b, s // tt),
        in_specs=[pl.BlockSpec((1, e, tt), lambda i, j: (i, 0, j)),
                  pl.BlockSpec((e, 1, cap, d), lambda i, j: (0, i, 0, 0)),
                  pl.BlockSpec((1, tt, d), lambda i, j: (i, j, 0)),
                  pl.BlockSpec((1, d), lambda i, j: (0, 0))],
        out_specs=pl.BlockSpec((1, tt, d), lambda i, j: (i, j, 0)),
        out_shape=jax.ShapeDtypeStruct((b, s, d), F32),
        compiler_params=_params(("parallel", "parallel")),
        name="combine",
    )(slot, y, x1, gain)


def _layer(x, cos, sin, norm_mix, w_in, ret_decay_fwd, ret_decay_bwd, ret_norm,
           gla_gate_w_fwd, gla_gate_b_fwd, gla_gate_w_bwd, gla_gate_b_bwd, gla_norm,
           w_branch_ret, w_branch_gla, w_out, norm_ffn, w_router, w_gate, w_up, w_down, norm_out):
    b, s, d = x.shape
    t = b * s
    ret_qk, ret_v = RET_HEADS * RET_DK, RET_HEADS * RET_DV
    gla_qk, gla_v = GLA_HEADS * GLA_DK, GLA_HEADS * GLA_DV
    rank = GLA_GATE_RANK
    gla0 = 2 * ret_qk + 2 * ret_v
    ga0 = gla0 + 2 * gla_qk + 2 * gla_v
    assert w_in.shape == (d, ga0 + 2 * rank + 2 * d)
    assert s % CHUNK == 0 and 3 * GATE_COPY <= LANES and ga0 % LANES == 0

    w_in_t = w_in.T.astype(BF16)
    w_gl = w_in_t[ga0 + 2 * rank:, :]

    gw = jnp.zeros((GLA_HEADS, GATE_COPY, 2 * GLA_DK), F32)
    gw = gw.at[:, :rank, :GLA_DK].set(gla_gate_w_fwd.reshape(rank, GLA_HEADS, GLA_DK).transpose(1, 0, 2))
    gw = gw.at[:, rank:, GLA_DK:].set(gla_gate_w_bwd.reshape(rank, GLA_HEADS, GLA_DK).transpose(1, 0, 2))
    gw_hi = gw.astype(BF16)
    gw_lo = (gw - gw_hi.astype(F32)).astype(BF16)
    gate_w = jnp.concatenate([gw_hi, gw_lo, gw_hi, jnp.zeros_like(gw_hi)], axis=1)
    gate_b = jnp.concatenate([gla_gate_b_fwd.reshape(GLA_HEADS, 1, GLA_DK),
                              gla_gate_b_bwd.reshape(GLA_HEADS, 1, GLA_DK)], axis=2)

    dec = jnp.stack([ret_decay_fwd, ret_decay_bwd], axis=1)[:, :, None]
    dec = jnp.pad(jnp.broadcast_to(dec, (RET_HEADS, 2, RET_DV)), ((0, 0), (0, 6), (0, 0)))

    x2 = x.reshape(t, d)
    ret, h3 = _retention(x, norm_mix[None, :], w_in_t, 0, cos, sin, dec, ret_norm[None, :])
    gla = _gla(h3, w_in_t, gla0, gate_w, gate_b, gla_norm[None, :])
    h = h3.reshape(t, d)

    n_e = w_router.shape[1]
    w_r = jnp.pad(w_router, ((0, 0), (0, LANES - n_e)))
    wr_hi = w_r.astype(BF16)
    wr_lo = (w_r - wr_hi.astype(F32)).astype(BF16)
    x1, h2, aff_t = _merge(h, ret.reshape(t, ret_v), gla.reshape(t, gla_v), x2, w_gl,
                           w_branch_ret.astype(BF16), w_branch_gla.astype(BF16), w_out.astype(BF16),
                           norm_ffn[None, :], jnp.concatenate([wr_hi, wr_lo], axis=1), n_e, b,
                           _pick(s, (1024, 512, 256, 128)))

    capacity = EC_CAPACITY_FACTOR * s // n_e
    slot = _route(aff_t, capacity)
    xg, gate = _dispatch(slot, aff_t, h2.reshape(b, s, d), capacity)
    f = w_gate.shape[2]
    y = _ffn(xg.reshape(n_e, b * capacity, d), gate.reshape(n_e, b * capacity, LANES),
             w_gate, w_up, w_down, min(FFN_TILE, f), _pick(b * capacity, (1024, 512, 256, 128)))
    return _combine(slot, y.reshape(n_e, b, capacity, d), x1.reshape(b, s, d), norm_out,
                    _pick(s, (1024, 512, 256, 128)))


def kernel(x, positions, norm_mix, w_in, ret_decay_fwd, ret_decay_bwd, ret_norm, gla_gate_w_fwd,
           gla_gate_b_fwd, gla_gate_w_bwd, gla_gate_b_bwd, gla_norm, w_branch_ret, w_branch_gla,
           w_out, norm_ffn, w_router, w_gate, w_up, w_down, norm_final):
    depth = norm_mix.shape[0]
    assert depth == 1, "the final RMSNorm is fused into the last layer's combine stage"
    cos, sin = _rope_table(positions, RET_DK)
    return _layer(x, cos, sin, norm_mix[0], w_in[0], ret_decay_fwd[0], ret_decay_bwd[0], ret_norm[0],
                  gla_gate_w_fwd[0], gla_gate_b_fwd[0], gla_gate_w_bwd[0], gla_gate_b_bwd[0], gla_norm[0],
                  w_branch_ret[0], w_branch_gla[0], w_out[0], norm_ffn[0], w_router[0],
                  w_gate[0], w_up[0], w_down[0], norm_final[None, :])
```

```python
import functools

import jax
import jax.numpy as jnp
from jax import lax
from jax.experimental import pallas as pl
from jax.experimental.pallas import tpu as pltpu

F32 = jnp.float32
BF16 = jnp.bfloat16

RET_HEADS = 4
RET_DK = 128
RET_DV = 256
GLA_HEADS = 4
GLA_DK = 128
GLA_DV = 256
GLA_GATE_RANK = 16
GLA_GATE_NORMALIZER = 16.0
CHUNK = 128
EC_CAPACITY_FACTOR = 2
ROPE_THETA = 10000.0
NORM_EPS = 1e-6
LOG2_E = 1.4426950408889634

CHUNK_UNROLL = 16
LANES = 128
VMEM_LIMIT = 56 << 20


def _params(sem, vmem=VMEM_LIMIT):
    return pltpu.CompilerParams(dimension_semantics=sem, vmem_limit_bytes=vmem)


def _resident(shape):
    return pl.BlockSpec(shape, lambda *_: (0,) * len(shape), pipeline_mode=pl.Buffered(1))


def _pick(n, prefs):
    for p in prefs:
        if n % p == 0:
            return p
    return n


def _sigmoid(x):
    return 1.0 / (1.0 + jnp.exp(-x))


def _silu(x):
    half = 0.5 * x
    return half + half * jnp.tanh(half)


def _log_sigmoid(x):
    return jnp.minimum(x, 0.0) - jnp.log(1.0 + jnp.exp(-jnp.abs(x)))


def _rms(x, gain):
    return x * lax.rsqrt(jnp.mean(x * x, axis=-1, keepdims=True) + NORM_EPS) * gain


def _dot(a, b):
    return jnp.dot(a, b, preferred_element_type=F32)


def _dot_nt(a, b):
    return lax.dot_general(a, b, (((1,), (1,)), ((), ())), preferred_element_type=F32)


def _dot_tn(a, b):
    return lax.dot_general(a, b, (((0,), (0,)), ((), ())), preferred_element_type=F32)


def _split2(x):
    hi = x.astype(BF16)
    lo = (x - hi.astype(F32)).astype(BF16)
    return hi, lo


def _rope_table_kernel(pos_a_ref, pos_b_ref, freq_ref, cos_ref, sin_ref):
    n, dk = pos_a_ref.shape[1], freq_ref.shape[1]
    lower = lax.broadcasted_iota(jnp.int32, (n, dk), 1) < dk // 2
    ang = jnp.where(lower, pos_a_ref[0].astype(F32), pos_b_ref[0].astype(F32)) * freq_ref[...]
    cos, sin = jnp.cos(ang), jnp.sin(ang)
    cos_x, sin_x = pltpu.roll(cos, dk // 2, 1), pltpu.roll(sin, dk // 2, 1)
    cos_ref[0, 0:n, :] = jnp.where(lower, cos, cos_x)
    cos_ref[0, n:2 * n, :] = jnp.where(lower, cos_x, cos)
    sin_ref[0, 0:n, :] = jnp.where(lower, -sin, sin_x)
    sin_ref[0, n:2 * n, :] = jnp.where(lower, -sin_x, sin)


def _rope_table(positions, dk):
    b, s = positions.shape
    assert s % 16 == 0
    half = jnp.arange(0, dk, 2, dtype=F32) / dk
    inv_freq = ROPE_THETA ** (-half)
    freq = jnp.concatenate([inv_freq, inv_freq])[None, :]
    out = jax.ShapeDtypeStruct((b, s, dk), F32)
    pos = positions[:, :, None]
    return pl.pallas_call(
        _rope_table_kernel,
        grid=(b,),
        in_specs=[pl.BlockSpec((1, s // 2, 1), lambda i: (i, 0, 0)),
                  pl.BlockSpec((1, s // 2, 1), lambda i: (i, 1, 0)),
                  pl.BlockSpec((1, dk), lambda i: (0, 0))],
        out_specs=[pl.BlockSpec((1, s, dk), lambda i: (i, 0, 0))] * 2,
        out_shape=[out, out],
        compiler_params=_params(("parallel",)),
        name="rope_table",
    )(pos, pos, freq)


def _rope(t, cos, sin_signed):
    return t * cos + pltpu.roll(t, t.shape[-1] // 2, 1) * sin_signed


def _chunk_rows(n, c=CHUNK):
    return pl.ds(pl.multiple_of(n * c, c), c)


def _row_tile(s_len):
    return _pick(s_len, (512, 256, CHUNK))


def _per_tile(chunk_pattern, s_len):
    return jnp.concatenate([chunk_pattern] * (_row_tile(s_len) // chunk_pattern.shape[0]), axis=0)


def _for_row_tiles(s_len, body):
    tile = _row_tile(s_len)

    def step(i, carry):
        body(pl.ds(pl.multiple_of(i * tile, tile), tile))
        return carry

    lax.fori_loop(0, s_len // tile, step, 0, unroll=True)


def _retention_kernel(x_ref, mix_gain_ref, wq_ref, wk_ref, wv_ref, wg_ref, cos_ref, sin_ref, dec_ref,
                      gain_ref, o_ref, h_ref, qk_ref, qw_ref, kw_ref, v_ref, gg_ref, a_ref, st_ref):
    c = a_ref.shape[1]
    s_len, dk, dv = qk_ref.shape[0], qk_ref.shape[1] // 2, v_ref.shape[1]
    n_chunks = s_len // c
    scale = dk ** -0.5

    lg_f = _log_sigmoid(dec_ref[0, 0:1, :])
    lg_b = _log_sigmoid(dec_ref[0, 1:2, :])
    lgf_k, lgb_k = lg_f[:, :dk], lg_b[:, :dk]
    pos = lax.broadcasted_iota(jnp.int32, (c, dk), 0).astype(F32)
    wq_f = _per_tile(jnp.exp((pos + 1.0) * lgf_k), s_len)
    wk_f = _per_tile(jnp.exp((c - 1.0 - pos) * lgf_k), s_len)
    wq_b = _per_tile(jnp.exp((c - pos) * lgb_k), s_len)
    wk_b = _per_tile(jnp.exp(pos * lgb_k), s_len)
    ri = lax.broadcasted_iota(jnp.int32, (c, c), 0)
    ci = lax.broadcasted_iota(jnp.int32, (c, c), 1)
    lower = ri >= ci
    rel = (ri - ci).astype(F32)
    decay_mask = jnp.where(lower,
                           jnp.exp(jnp.where(lower, rel, 0.0) * lg_f[:, :c]),
                           jnp.exp(jnp.where(lower, 0.0, -rel) * lg_b[:, :c]))
    chunk_decay_f = jnp.exp(c * lg_f)
    chunk_decay_b = jnp.exp(c * lg_b)

    @pl.when(pl.program_id(1) == 0)
    def _():
        def norm(rows):
            h_ref[0, rows, :] = _rms(x_ref[0, rows, :], mix_gain_ref[...]).astype(h_ref.dtype)
        _for_row_tiles(s_len, norm)

    w_qk = jnp.concatenate([wq_ref[...], wk_ref[...]], axis=0)
    w_v = wv_ref[...]
    w_g = wg_ref[...]

    def project(rows):
        h = h_ref[0, rows, :]
        qk = _dot_nt(h, w_qk)
        cos, sin = cos_ref[0, rows, :], sin_ref[0, rows, :]
        qr = _rope(qk[:, :dk], cos, sin) * scale
        kr = _rope(qk[:, dk:], cos, sin)
        qk_ref[rows, :] = jnp.concatenate([qr, kr], axis=1).astype(BF16)
        qw_ref[rows, :] = jnp.concatenate([qr * wq_f, qr * wq_b], axis=1).astype(BF16)
        kw_ref[rows, :] = jnp.concatenate([kr * wk_f, kr * wk_b], axis=1).astype(BF16)
        v_ref[rows, :] = _dot_nt(h, w_v).astype(BF16)
        gg_ref[rows, :] = _silu(_dot_nt(h, w_g)) * gain_ref[...]

    _for_row_tiles(s_len, project)

    kv = [_dot_tn(kw_ref[n * c:(n + 1) * c, :], v_ref[n * c:(n + 1) * c, :]) for n in range(n_chunks)]
    s_f = s_b = jnp.zeros((dk, dv), F32)
    for i in range(n_chunks):
        n = n_chunks - 1 - i
        st_ref[i, 0:dk, :] = s_f.astype(BF16)
        st_ref[n, dk:2 * dk, :] = s_b.astype(BF16)
        if i + 1 < n_chunks:
            s_f = chunk_decay_f * s_f + kv[i][0:dk, :]
            s_b = chunk_decay_b * s_b + kv[n][dk:2 * dk, :]

    def scores(n, carry):
        rows = _chunk_rows(n, c)
        qk = qk_ref[rows, :]
        a_ref[rows, :] = (_dot_nt(qk[:, :dk], qk[:, dk:]) * decay_mask).astype(BF16)
        return carry

    lax.fori_loop(0, n_chunks, scores, 0, unroll=CHUNK_UNROLL)

    def phase2(n, carry):
        rows = _chunk_rows(n, c)
        o = _dot(a_ref[rows, :], v_ref[rows, :]) + _dot(qw_ref[rows, :], st_ref[n])
        mu = jnp.mean(o, axis=-1, keepdims=True)
        d = o - mu
        var = jnp.mean(d * d, axis=-1, keepdims=True)
        o_ref[0, rows, :] = (d * lax.rsqrt(var + NORM_EPS) * gg_ref[rows, :]).astype(o_ref.dtype)
        return carry

    lax.fori_loop(0, n_chunks, phase2, 0, unroll=CHUNK_UNROLL)


def _retention(x, mix_gain, w_in, col0, cos, sin, dec, gain):
    b, s, d = x.shape
    h, dk, dv = RET_HEADS, RET_DK, RET_DV
    q0 = col0 // dk
    v0 = (col0 + 2 * h * dk) // dv
    chunk = _pick(s, (2 * CHUNK, CHUNK))
    n_chunks = s // chunk
    return pl.pallas_call(
        _retention_kernel,
        grid=(b, h),
        in_specs=[pl.BlockSpec((1, s, d), lambda i, j: (i, 0, 0)),
                  pl.BlockSpec((1, d), lambda i, j: (0, 0)),
                  pl.BlockSpec((dk, d), lambda i, j: (q0 + j, 0)),
                  pl.BlockSpec((dk, d), lambda i, j: (q0 + h + j, 0)),
                  pl.BlockSpec((dv, d), lambda i, j: (v0 + j, 0)),
                  pl.BlockSpec((dv, d), lambda i, j: (v0 + h + j, 0)),
                  pl.BlockSpec((1, s, dk), lambda i, j: (i, 0, 0)),
                  pl.BlockSpec((1, s, dk), lambda i, j: (i, 0, 0)),
                  pl.BlockSpec((1, 8, dv), lambda i, j: (j, 0, 0)),
                  pl.BlockSpec((1, dv), lambda i, j: (0, j))],
        out_specs=[pl.BlockSpec((1, s, dv), lambda i, j: (i, 0, j)),
                   pl.BlockSpec((1, s, d), lambda i, j: (i, 0, 0))],
        out_shape=[jax.ShapeDtypeStruct((b, s, h * dv), BF16),
                   jax.ShapeDtypeStruct((b, s, d), BF16)],
        scratch_shapes=[pltpu.VMEM((s, 2 * dk), BF16),
                        pltpu.VMEM((s, 2 * dk), BF16),
                        pltpu.VMEM((s, 2 * dk), BF16),
                        pltpu.VMEM((s, dv), BF16),
                        pltpu.VMEM((s, dv), F32),
                        pltpu.VMEM((s, chunk), BF16),
                        pltpu.VMEM((n_chunks, 2 * dk, dv), BF16)],
        compiler_params=_params(("parallel", "arbitrary")),
        name="retention",
    )(x, mix_gain, w_in, w_in, w_in, w_in, cos, sin, dec, gain)


GATE_COPY = 2 * GLA_GATE_RANK


def _cumsum_dot(tri2, x):
    hi, lo = _split2(x)
    return _dot(tri2, jnp.concatenate([hi, lo], axis=0))


def _gla_kernel(h_ref, wq_ref, wk_ref, wv_ref, wg_ref, wa_ref, gw_ref, gb_ref, gain_ref, o_ref,
                q_ref, k_ref, v_ref, gg_ref, ga_ref, la_ref, qs_ref, ks_ref, qw_ref, kw_ref, last_ref,
                a_ref, st_ref):
    c = CHUNK
    s_len, dk, dv = q_ref.shape[0], q_ref.shape[1], v_ref.shape[1]
    n_chunks = s_len // c
    scale = dk ** -0.5
    inv_norm = LOG2_E / GLA_GATE_NORMALIZER

    w_qk = jnp.concatenate([wq_ref[...], wk_ref[...]], axis=0)
    w_v = wv_ref[...]
    w_g = wg_ref[...]

    @pl.when(pl.program_id(1) == 0)
    def _():
        w_a = wa_ref[...]

        def gate_inputs(rows):
            x = _dot_nt(h_ref[0, rows, :], w_a)
            x_hi = x.astype(BF16).astype(F32)
            lane = lax.broadcasted_iota(jnp.int32, x.shape, 1)
            ga = jnp.where(lane < GATE_COPY, x_hi,
                           jnp.where(lane < 2 * GATE_COPY, pltpu.roll(x_hi, GATE_COPY, 1),
                                     jnp.where(lane < 3 * GATE_COPY, pltpu.roll(x - x_hi, 2 * GATE_COPY, 1),
                                               0.0)))
            ga_ref[rows, :] = ga.astype(BF16)

        _for_row_tiles(s_len, gate_inputs)

    def project(rows):
        h = h_ref[0, rows, :]
        qk = _dot_nt(h, w_qk)
        q_ref[rows, :] = qk[:, :dk] * scale
        k_ref[rows, :] = qk[:, dk:]
        v_ref[rows, :] = _dot_nt(h, w_v).astype(BF16)
        gg_ref[rows, :] = _silu(_dot_nt(h, w_g)) * gain_ref[...]
        la_ref[rows, :] = _log_sigmoid(_dot(ga_ref[rows, :], gw_ref[0]) + gb_ref[0]) * inv_norm

    _for_row_tiles(s_len, project)

    ri = lax.broadcasted_iota(jnp.int32, (c, c), 0)
    ci = lax.broadcasted_iota(jnp.int32, (c, c), 1)
    lower = ri >= ci
    tri_lower = jnp.where(lower, 1.0, 0.0).astype(BF16)
    tri2_lower = jnp.concatenate([tri_lower, tri_lower], axis=1)

    def cumulate(n, carry):
        rows = _chunk_rows(n)
        la = la_ref[rows, :]
        incl = _cumsum_dot(tri2_lower, la)
        cum_f = incl[:, 0:dk]
        cum_b = incl[c - 1:c, dk:2 * dk] - incl[:, dk:2 * dk] + la[:, dk:2 * dk]
        last_f, last_b = cum_f[c - 1:c, :], cum_b[0:1, :]
        ref_f, ref_b = cum_f[c // 2:c // 2 + 1, :], cum_b[c // 2 - 1:c // 2, :]
        q, k = q_ref[rows, :], k_ref[rows, :]
        q_f = q * jnp.exp2(cum_f - ref_f)
        q_b = q * jnp.exp2(cum_b - ref_b)
        qs_ref[rows, :] = jnp.concatenate([q_f, q_b], axis=1).astype(BF16)
        ks_ref[rows, :] = jnp.concatenate([k * jnp.exp2(ref_f - cum_f), k * jnp.exp2(ref_b - cum_b)],
                                          axis=1).astype(BF16)
        qw_ref[rows, :] = jnp.concatenate([q_f * jnp.exp2(ref_f), q_b * jnp.exp2(ref_b)], axis=1).astype(BF16)
        kw_ref[rows, :] = jnp.concatenate([k * jnp.exp2(last_f - cum_f), k * jnp.exp2(last_b - cum_b)],
                                          axis=1).astype(BF16)
        last_ref[n, 0:1, :] = last_f
        last_ref[n, 1:2, :] = last_b
        return carry

    lax.fori_loop(0, n_chunks, cumulate, 0, unroll=CHUNK_UNROLL)

    kv = [_dot_tn(v_ref[n * c:(n + 1) * c, :], kw_ref[n * c:(n + 1) * c, :]) for n in range(n_chunks)]
    s_f = s_b = jnp.zeros((dv, dk), F32)
    for i in range(n_chunks):
        n = n_chunks - 1 - i
        st_ref[i, :, 0:dk] = s_f.astype(BF16)
        st_ref[n, :, dk:2 * dk] = s_b.astype(BF16)
        if i + 1 < n_chunks:
            s_f = jnp.exp2(last_ref[i, 0:1, :]) * s_f + kv[i][:, 0:dk]
            s_b = jnp.exp2(last_ref[n, 1:2, :]) * s_b + kv[n][:, dk:2 * dk]

    def scores(n, carry):
        rows = _chunk_rows(n)
        qs, ks = qs_ref[rows, :], ks_ref[rows, :]
        a_ref[rows, :] = jnp.where(lower, _dot_nt(qs[:, :dk], ks[:, :dk]),
                                   _dot_nt(qs[:, dk:], ks[:, dk:])).astype(BF16)
        return carry

    lax.fori_loop(0, n_chunks, scores, 0, unroll=CHUNK_UNROLL)

    def phase2(n, carry):
        rows = _chunk_rows(n)
        o = _dot(a_ref[rows, :], v_ref[rows, :]) + _dot_nt(qw_ref[rows, :], st_ref[n])
        rms = lax.rsqrt(jnp.mean(o * o, axis=-1, keepdims=True) + NORM_EPS)
        o_ref[0, rows, :] = (o * rms * gg_ref[rows, :]).astype(o_ref.dtype)
        return carry

    lax.fori_loop(0, n_chunks, phase2, 0, unroll=CHUNK_UNROLL)


def _gla(h3, w_in, col0, gate_w, gate_b, gain):
    b, s, d = h3.shape
    h, dk, dv = GLA_HEADS, GLA_DK, GLA_DV
    q0 = col0 // dk
    v0 = (col0 + 2 * h * dk) // dv
    a0 = (col0 + 2 * h * dk + 2 * h * dv) // LANES
    n_chunks = s // CHUNK
    return pl.pallas_call(
        _gla_kernel,
        grid=(b, h),
        in_specs=[pl.BlockSpec((1, s, d), lambda i, j: (i, 0, 0)),
                  pl.BlockSpec((dk, d), lambda i, j: (q0 + j, 0)),
                  pl.BlockSpec((dk, d), lambda i, j: (q0 + h + j, 0)),
                  pl.BlockSpec((dv, d), lambda i, j: (v0 + j, 0)),
                  pl.BlockSpec((dv, d), lambda i, j: (v0 + h + j, 0)),
                  pl.BlockSpec((LANES, d), lambda i, j: (a0, 0)),
                  pl.BlockSpec((1, LANES, 2 * dk), lambda i, j: (j, 0, 0)),
                  pl.BlockSpec((1, 1, 2 * dk), lambda i, j: (j, 0, 0)),
                  pl.BlockSpec((1, dv), lambda i, j: (0, j))],
        out_specs=pl.BlockSpec((1, s, dv), lambda i, j: (i, 0, j)),
        out_shape=jax.ShapeDtypeStruct((b, s, h * dv), BF16),
        scratch_shapes=[pltpu.VMEM((s, dk), F32),
                        pltpu.VMEM((s, dk), F32),
                        pltpu.VMEM((s, dv), BF16),
                        pltpu.VMEM((s, dv), F32),
                        pltpu.VMEM((s, LANES), BF16),
                        pltpu.VMEM((s, 2 * dk), F32),
                        pltpu.VMEM((s, 2 * dk), BF16),
                        pltpu.VMEM((s, 2 * dk), BF16),
                        pltpu.VMEM((s, 2 * dk), BF16),
                        pltpu.VMEM((s, 2 * dk), BF16),
                        pltpu.VMEM((n_chunks, 8, dk), F32),
                        pltpu.VMEM((s, CHUNK), BF16),
                        pltpu.VMEM((n_chunks, dv, 2 * dk), BF16)],
        compiler_params=_params(("parallel", "arbitrary")),
        name="gla",
    )(h3, w_in, w_in, w_in, w_in, w_in, gate_w, gate_b, gain)


MERGE_COLS = 256


def _merge_kernel(n_experts, h_ref, ret_ref, gla_ref, x_ref, wgl_ref, wr_ref, wg_ref, wo_ref, gain_ref,
                  wr2_ref, x1_ref, h2_ref, aff_ref):
    d = x_ref.shape[1]
    h, ret, gla = h_ref[...], ret_ref[...], gla_ref[...]
    blocks = []
    for j in range(0, d, MERGE_COLS):
        cols = slice(j, j + MERGE_COLS)
        cols_gla = slice(d + j, d + j + MERGE_COLS)
        m = (_sigmoid(_dot_nt(h, wgl_ref[cols, :])) * _dot(ret, wr_ref[:, cols])
             + _sigmoid(_dot_nt(h, wgl_ref[cols_gla, :])) * _dot(gla, wg_ref[:, cols]))
        blocks.append(m.astype(BF16))
    x1 = x_ref[...] + _dot(jnp.concatenate(blocks, axis=1), wo_ref[...])
    x1_ref[...] = x1
    h2 = _rms(x1, gain_ref[...]).astype(BF16)
    h2_ref[...] = h2
    logits2 = _dot(h2, wr2_ref[...])
    logits = logits2[:, :LANES] + logits2[:, LANES:]
    lane = lax.broadcasted_iota(jnp.int32, logits.shape, 1)
    logits = jnp.where(lane < n_experts, logits, -jnp.inf)
    p = jnp.exp(logits - jnp.max(logits, axis=-1, keepdims=True))
    aff = p / jnp.sum(p, axis=-1, keepdims=True)
    aff_ref[0] = aff.T[0:n_experts, :]


def _merge(h, ret, gla, x2, w_gl, w_ret, w_gla, w_out, gain, w_router2, n_experts, batch, tm):
    t, d = x2.shape
    s = t // batch
    per_b = s // tm
    rows = lambda width: pl.BlockSpec((tm, width), lambda i: (i, 0))
    return pl.pallas_call(
        functools.partial(_merge_kernel, n_experts),
        grid=(t // tm,),
        in_specs=[rows(d), rows(ret.shape[1]), rows(gla.shape[1]), rows(d),
                  _resident(w_gl.shape), _resident(w_ret.shape), _resident(w_gla.shape),
                  _resident(w_out.shape), _resident((1, d)),
                  _resident(w_router2.shape)],
        out_specs=[rows(d), rows(d),
                   pl.BlockSpec((1, n_experts, tm), lambda i: (i // per_b, 0, i % per_b))],
        out_shape=[jax.ShapeDtypeStruct((t, d), F32),
                   jax.ShapeDtypeStruct((t, d), BF16),
                   jax.ShapeDtypeStruct((batch, n_experts, s), F32)],
        compiler_params=_params(("parallel",)),
        name="merge",
    )(h, ret, gla, x2, w_gl, w_ret, w_gla, w_out, gain, w_router2)


def _prefix_count(mask):
    s = mask.shape[1]
    ri = lax.broadcasted_iota(jnp.int32, (LANES, LANES), 0)
    ci = lax.broadcasted_iota(jnp.int32, (LANES, LANES), 1)
    tri = jnp.where(ri <= ci, 1.0, 0.0).astype(BF16)
    off = jnp.zeros((mask.shape[0], 1), F32)
    parts = []
    for j in range(s // LANES):
        p = _dot(mask[:, j * LANES:(j + 1) * LANES].astype(BF16), tri) + off
        parts.append(p)
        off = p[:, LANES - 1:LANES]
    return jnp.concatenate(parts, axis=1)


def _route_kernel(capacity, aff_ref, slot_ref):
    a = aff_ref[0]
    bits = lax.bitcast_convert_type(a, jnp.int32)
    n_e = a.shape[0]
    cap = float(capacity)

    def count(pred):
        return jnp.sum(jnp.where(pred, 1.0, 0.0), axis=1, keepdims=True)

    def search(_, c):
        lo, hi = c
        mid = lo + lax.shift_right_logical(hi - lo, 1)
        ok = count(bits >= mid) >= cap
        return jnp.where(ok, mid, lo), jnp.where(ok, hi, mid)

    lo0 = jnp.zeros((n_e, 1), jnp.int32)
    hi0 = jnp.full((n_e, 1), 0x7F800000, jnp.int32)
    thr_bits, _ = lax.fori_loop(0, 31, search, (lo0, hi0))
    thr0 = jnp.max(jnp.where(bits <= thr_bits, a, -1.0), axis=1, keepdims=True)

    def counts(v):
        return count(a >= v), count(a > v)

    def unsettled(state):
        _, c_ge, c_gt = state
        bad = jnp.where(c_ge < cap, 1.0, jnp.where(c_gt >= cap, 1.0, 0.0))
        return jnp.max(bad, axis=0, keepdims=True)[0, 0] > 0.0

    def step(state):
        v, c_ge, c_gt = state
        below = jnp.max(jnp.where(a < v, a, -1.0), axis=1, keepdims=True)
        above = jnp.min(jnp.where(a > v, a, 2.0), axis=1, keepdims=True)
        v = jnp.where(c_ge < cap, below, jnp.where(c_gt >= cap, above, v))
        return (v,) + counts(v)

    thr, _, n_gt = lax.while_loop(unsettled, step, (thr0,) + counts(thr0))

    gt = a > thr
    eq = a == thr
    need = cap - n_gt
    eq_rank = _prefix_count(jnp.where(eq, 1.0, 0.0))
    sel = jnp.where(gt, 1.0, jnp.where(eq, jnp.where(eq_rank <= need, 1.0, 0.0), 0.0))
    pos = _prefix_count(sel)
    slot_ref[0] = jnp.where(sel > 0.0, pos - 1.0, -1.0)


def _route(aff_t, capacity):
    b, e, s = aff_t.shape
    return pl.pallas_call(
        functools.partial(_route_kernel, capacity),
        grid=(1,),
        in_specs=[pl.BlockSpec((1, b * e, s), lambda i: (0, 0, 0))],
        out_specs=pl.BlockSpec((1, b * e, s), lambda i: (0, 0, 0)),
        out_shape=jax.ShapeDtypeStruct((1, b * e, s), F32),
        compiler_params=_params(("arbitrary",)),
        name="route",
    )(aff_t.reshape(1, b * e, s)).reshape(b, e, s)


def _dispatch_kernel(slot_ref, aff_ref, h_ref, xg_ref, gate_ref):
    group, cap = xg_ref.shape[0], xg_ref.shape[2]
    h = h_ref[0]
    row = lax.broadcasted_iota(jnp.int32, (cap, h.shape[0]), 0).astype(F32)
    for g in range(group):
        hit = row == slot_ref[0, g:g + 1, :]
        xg_ref[g, 0] = _dot(jnp.where(hit, 1.0, 0.0).astype(BF16), h).astype(xg_ref.dtype)
        gate = jnp.sum(jnp.where(hit, aff_ref[0, g:g + 1, :], 0.0), axis=1, keepdims=True)
        gate_ref[g, 0] = jnp.broadcast_to(gate, gate_ref.shape[2:])


def _dispatch(slot, aff_t, h2, capacity):
    b, e, s = slot.shape
    d = h2.shape[-1]
    group = _pick(e, (8,))
    return pl.pallas_call(
        _dispatch_kernel,
        grid=(b, e // group),
        in_specs=[pl.BlockSpec((1, group, s), lambda i, j: (i, j, 0)),
                  pl.BlockSpec((1, group, s), lambda i, j: (i, j, 0)),
                  pl.BlockSpec((1, s, d), lambda i, j: (i, 0, 0))],
        out_specs=[pl.BlockSpec((group, 1, capacity, d), lambda i, j: (j, i, 0, 0)),
                   pl.BlockSpec((group, 1, capacity, LANES), lambda i, j: (j, i, 0, 0))],
        out_shape=[jax.ShapeDtypeStruct((e, b, capacity, d), BF16),
                   jax.ShapeDtypeStruct((e, b, capacity, LANES), F32)],
        compiler_params=_params(("parallel", "parallel")),
        name="dispatch",
    )(slot, aff_t, h2)


FFN_TILE = 768


def _ffn_kernel(tm, tf, x_ref, gate_ref, wg_hbm, wu_hbm, wd_hbm, y_ref, acc_ref, wg_buf, wu_buf, wd_buf, sems):
    expert, n_experts = pl.program_id(0), pl.num_programs(0)
    n_full, rest = divmod(wg_hbm.shape[2], tf)
    n_f = n_full + (1 if rest else 0)
    width = lambda f: tf if f < n_full else rest
    m, d = x_ref.shape[1], x_ref.shape[2]

    def tile_copies(e, f, slot, w):
        cols = pl.ds(pl.multiple_of(f * tf, tf), w)
        return (pltpu.make_async_copy(wg_hbm.at[e, :, cols], wg_buf.at[slot, :, pl.ds(0, w)], sems.at[0, slot]),
                pltpu.make_async_copy(wu_hbm.at[e, :, cols], wu_buf.at[slot, :, pl.ds(0, w)], sems.at[1, slot]),
                pltpu.make_async_copy(wd_hbm.at[e, cols, :], wd_buf.at[slot, pl.ds(0, w), :], sems.at[2, slot]))

    def start(e, f, slot, w):
        for copy in tile_copies(e, f, slot, w):
            copy.start()

    @pl.when(expert == 0)
    def _():
        start(0, 0, 0, width(0))

    def step(f, w, w_next, first, final):
        slot = lax.rem(expert * n_f + f, 2)
        for copy in tile_copies(expert, f, slot, w):
            copy.wait()
        w_gate = wg_buf[slot, :, 0:w].astype(BF16)
        w_up = wu_buf[slot, :, 0:w].astype(BF16)
        w_down = wd_buf[slot, 0:w, :].astype(BF16)
        if final:
            @pl.when(expert + 1 < n_experts)
            def _():
                start(expert + 1, 0, 1 - slot, width(0))
        else:
            start(expert, f + 1, 1 - slot, w_next)
        for i in range(m // tm):
            rows = pl.ds(i * tm, tm)
            x = x_ref[0, rows, :]
            act = (_silu(_dot(x, w_gate)) * _dot(x, w_up)).astype(BF16)
            part = _dot(act, w_down)
            if not first:
                part = acc_ref[rows, :] + part
            if final:
                gate = jnp.concatenate([gate_ref[0, rows, :]] * (d // LANES), axis=1)
                y_ref[0, rows, :] = (part * gate).astype(y_ref.dtype)
            else:
                acc_ref[rows, :] = part

    uniform = [f for f in range(1, n_f - 1) if width(f) == tf and width(f + 1) == tf]
    f = 0
    while f < n_f:
        if uniform and f == uniform[0]:
            def middle(i, carry):
                step(i, tf, tf, False, False)
                return carry
            lax.fori_loop(uniform[0], uniform[-1] + 1, middle, 0)
            f = uniform[-1] + 1
        else:
            step(f, width(f), width(f + 1) if f + 1 < n_f else 0, f == 0, f == n_f - 1)
            f += 1


def _ffn(xg, gate, w_gate, w_up, w_down, tf, tm):
    e, m, d = xg.shape
    assert tf % LANES == 0 and w_gate.shape[2] % LANES == 0
    return pl.pallas_call(
        functools.partial(_ffn_kernel, tm, tf),
        grid=(e,),
        in_specs=[pl.BlockSpec((1, m, d), lambda i: (i, 0, 0)),
                  pl.BlockSpec((1, m, LANES), lambda i: (i, 0, 0)),
                  pl.BlockSpec(memory_space=pl.ANY),
                  pl.BlockSpec(memory_space=pl.ANY),
                  pl.BlockSpec(memory_space=pl.ANY)],
        out_specs=pl.BlockSpec((1, m, d), lambda i: (i, 0, 0)),
        out_shape=jax.ShapeDtypeStruct((e, m, d), BF16),
        scratch_shapes=[pltpu.VMEM((m, d), F32),
                        pltpu.VMEM((2, d, tf), w_gate.dtype),
                        pltpu.VMEM((2, d, tf), w_up.dtype),
                        pltpu.VMEM((2, tf, d), w_down.dtype),
                        pltpu.SemaphoreType.DMA((3, 2))],
        compiler_params=_params(("arbitrary",)),
        name="ffn",
    )(xg, gate, w_gate, w_up, w_down)


def _combine_kernel(slot_ref, y_ref, x1_ref, gain_ref, o_ref):
    n_e, cap = y_ref.shape[0], y_ref.shape[2]
    tt = x1_ref.shape[1]
    acc = x1_ref[0]
    row = lax.broadcasted_iota(jnp.int32, (cap, tt), 0).astype(F32)
    for e in range(n_e):
        onehot = jnp.where(row == slot_ref[0, e:e + 1, :], 1.0, 0.0).astype(BF16)
        acc = acc + _dot_tn(onehot, y_ref[e, 0])
    o_ref[0] = _rms(acc, gain_ref[...])


def _combine(slot, y, x1, gain, tt):
    b, e, s = slot.shape
    cap, d = y.shape[2], y.shape[3]
    return pl.pallas_call(
        _combine_kernel,
        grid=(b, s // tt),
        in_specs=[pl.BlockSpec((1, e, tt), lambda i, j: (i, 0, j)),
                  pl.BlockSpec((e, 1, cap, d), lambda i, j: (0, i, 0, 0)),
                  pl.BlockSpec((1, tt, d), lambda i, j: (i, j, 0)),
                  pl.BlockSpec((1, d), lambda i, j: (0, 0))],
        out_specs=pl.BlockSpec((1, tt, d), lambda i, j: (i, j, 0)),
        out_shape=jax.ShapeDtypeStruct((b, s, d), F32),
        compiler_params=_params(("parallel", "parallel")),
        name="combine",
    )(slot, y, x1, gain)


def _layer(x, cos, sin, norm_mix, w_in, ret_decay_fwd, ret_decay_bwd, ret_norm,
           gla_gate_w_fwd, gla_gate_b_fwd, gla_gate_w_bwd, gla_gate_b_bwd, gla_norm,
           w_branch_ret, w_branch_gla, w_out, norm_ffn, w_router, w_gate, w_up, w_down, norm_out):
    b, s, d = x.shape
    t = b * s
    ret_qk, ret_v = RET_HEADS * RET_DK, RET_HEADS * RET_DV
    gla_qk, gla_v = GLA_HEADS * GLA_DK, GLA_HEADS * GLA_DV
    rank = GLA_GATE_RANK
    gla0 = 2 * ret_qk + 2 * ret_v
    ga0 = gla0 + 2 * gla_qk + 2 * gla_v
    assert w_in.shape == (d, ga0 + 2 * rank + 2 * d)
    assert s % CHUNK == 0 and 3 * GATE_COPY <= LANES and ga0 % LANES == 0

    w_in_t = w_in.T.astype(BF16)
    w_gl = w_in_t[ga0 + 2 * rank:, :]

    gw = jnp.zeros((GLA_HEADS, GATE_COPY, 2 * GLA_DK), F32)
    gw = gw.at[:, :rank, :GLA_DK].set(gla_gate_w_fwd.reshape(rank, GLA_HEADS, GLA_DK).transpose(1, 0, 2))
    gw = gw.at[:, rank:, GLA_DK:].set(gla_gate_w_bwd.reshape(rank, GLA_HEADS, GLA_DK).transpose(1, 0, 2))
    gw_hi = gw.astype(BF16)
    gw_lo = (gw - gw_hi.astype(F32)).astype(BF16)
    gate_w = jnp.concatenate([gw_hi, gw_lo, gw_hi, jnp.zeros_like(gw_hi)], axis=1)
    gate_b = jnp.concatenate([gla_gate_b_fwd.reshape(GLA_HEADS, 1, GLA_DK),
                              gla_gate_b_bwd.reshape(GLA_HEADS, 1, GLA_DK)], axis=2)

    dec = jnp.stack([ret_decay_fwd, ret_decay_bwd], axis=1)[:, :, None]
    dec = jnp.pad(jnp.broadcast_to(dec, (RET_HEADS, 2, RET_DV)), ((0, 0), (0, 6), (0, 0)))

    x2 = x.reshape(t, d)
    ret, h3 = _retention(x, norm_mix[None, :], w_in_t, 0, cos, sin, dec, ret_norm[None, :])
    gla = _gla(h3, w_in_t, gla0, gate_w, gate_b, gla_norm[None, :])
    h = h3.reshape(t, d)

    n_e = w_router.shape[1]
    w_r = jnp.pad(w_router, ((0, 0), (0, LANES - n_e)))
    wr_hi = w_r.astype(BF16)
    wr_lo = (w_r - wr_hi.astype(F32)).astype(BF16)
    x1, h2, aff_t = _merge(h, ret.reshape(t, ret_v), gla.reshape(t, gla_v), x2, w_gl,
                           w_branch_ret.astype(BF16), w_branch_gla.astype(BF16), w_out.astype(BF16),
                           norm_ffn[None, :], jnp.concatenate([wr_hi, wr_lo], axis=1), n_e, b,
                           _pick(s, (1024, 512, 256, 128)))

    capacity = EC_CAPACITY_FACTOR * s // n_e
    slot = _route(aff_t, capacity)
    xg, gate = _dispatch(slot, aff_t, h2.reshape(b, s, d), capacity)
    f = w_gate.shape[2]
    y = _ffn(xg.reshape(n_e, b * capacity, d), gate.reshape(n_e, b * capacity, LANES),
             w_gate, w_up, w_down, min(FFN_TILE, f), _pick(b * capacity, (1024, 512, 256, 128)))
    return _combine(slot, y.reshape(n_e, b, capacity, d), x1.reshape(b, s, d), norm_out,
                    _pick(s, (1024, 512, 256, 128)))


def kernel(x, positions, norm_mix, w_in, ret_decay_fwd, ret_decay_bwd, ret_norm, gla_gate_w_fwd,
           gla_gate_b_fwd, gla_gate_w_bwd, gla_gate_b_bwd, gla_norm, w_branch_ret, w_branch_gla,
           w_out, norm_ffn, w_router, w_gate, w_up, w_down, norm_final):
    depth = norm_mix.shape[0]
    assert depth == 1, "the final RMSNorm is fused into the last layer's combine stage"
    cos, sin = _rope_table(positions, RET_DK)
    return _layer(x, cos, sin, norm_mix[0], w_in[0], ret_decay_fwd[0], ret_decay_bwd[0], ret_norm[0],
                  gla_gate_w_fwd[0], gla_gate_b_fwd[0], gla_gate_w_bwd[0], gla_gate_b_bwd[0], gla_norm[0],
                  w_branch_ret[0], w_branch_gla[0], w_out[0], norm_ffn[0], w_router[0],
                  w_gate[0], w_up[0], w_down[0], norm_final[None, :])
```
